```python
import math
import jax
import jax.numpy as jnp
from jax import lax
import numpy as np

D_MODEL = 1024
BATCH = 16
SEQ = 256
DEPTH = 2
DEC_BATCH = 8
DEC_SEQ = 4096
PAST_LEN = 256

GRID_W = 64
EPS = 1e-6
F32 = jnp.float32

HY_WIDTH = D_MODEL // 2
HY_IN = 3 * HY_WIDTH
HY_BANDS = 16
HY_EMB = 2 * HY_BANDS + 1
HY_FFN = 64
HY_TARGET = 1e-2
HY_SHORT_FRAC = 0.3
HY_LONG_FRAC = 1.5
HY_MAX_DECAY = math.log(HY_TARGET) / HY_SHORT_FRAC
HY_MIN_DECAY = math.log(HY_TARGET) / HY_LONG_FRAC
DA_HEADS = 4
DA_DH = 64
DA_VD = 2 * DA_DH
DA_WIDTH = DA_HEADS * DA_VD
DA_IN = 3 * DA_HEADS * 2 * DA_DH
ROPE_NF = DA_DH // 4
ROPE_BASE = 10000.0
Q_BLOCK = 128
GLA_HEADS = 4
GLA_DK = D_MODEL // 2 // GLA_HEADS
GLA_DV = D_MODEL // GLA_HEADS
GLA_RANK = 16
GLA_GATE_NORM = 16.0
GLA_CHUNK = 64
GLA_IN = GLA_HEADS * (2 * GLA_DK + 2 * GLA_DV) + 2 * GLA_RANK
N_EXPERTS = 16
N_GROUPS = 4
EXPERTS_PER_GROUP = N_EXPERTS // N_GROUPS
TOP_K = 2
D_FF = 512

kernel_name = 'hybrid_hyena_diffattn_gla_moe_step'


def rms_norm(x, w):
    xf = x.astype(F32)
    y = xf * lax.rsqrt(jnp.mean(xf * xf, axis=-1, keepdims=True) + EPS)
    return (y * w.astype(F32)).astype(x.dtype)


def short_conv(x, w, b):
    L = x.shape[1]
    xp = jnp.pad(x, ((0, 0), (1, 1), (0, 0)))
    return xp[:, :L] * w[0] + xp[:, 1:L + 1] * w[1] + xp[:, 2:L + 2] * w[2] + b


def hyena_filters(L, f_w1, f_b1, f_freq, f_w2, f_b2, f_w3):
    pos = jnp.arange(L, dtype=F32)[:, None]
    t = pos / (L - 1)
    bands = jnp.linspace(1e-4, HY_BANDS - 1, HY_BANDS, dtype=F32)
    ang = 2.0 * math.pi * pos / L * bands
    feats = jnp.concatenate([t, jnp.cos(ang), -jnp.sin(ang)], axis=-1)
    hid = jnp.sin(f_freq[0].astype(F32) * (feats @ f_w1.astype(F32) + f_b1.astype(F32)))
    hid = jnp.sin(f_freq[1].astype(F32) * (hid @ f_w2.astype(F32) + f_b2.astype(F32)))
    filt = (hid @ f_w3.astype(F32)).reshape(L, 4, HY_WIDTH)
    decay = jnp.abs(jnp.linspace(HY_MIN_DECAY, HY_MAX_DECAY, HY_WIDTH, dtype=F32))
    return filt * jnp.exp(-t * decay)[:, None, :]


def fft_long_conv(u, h_back, h_ahead, d_skip):
    L = u.shape[1]
    taps = jnp.concatenate([h_back, jnp.zeros_like(h_back[:1]), h_ahead[:0:-1]], axis=0)
    taps = taps / (jnp.sum(jnp.abs(taps), axis=0, keepdims=True) + EPS)
    u_f = jnp.fft.rfft(u.astype(F32), n=2 * L, axis=1)
    t_f = jnp.fft.rfft(taps, n=2 * L, axis=0)
    y = jnp.fft.irfft(u_f * t_f[None], n=2 * L, axis=1)[:, :L]
    return (y + u.astype(F32) * d_skip.astype(F32)).astype(u.dtype)


def hyena_operator(z, conv_w, conv_b, filt_params, hy_skip):
    L = z.shape[1]
    z = short_conv(z, conv_w, conv_b)
    v, x1, x2 = jnp.split(z, 3, axis=-1)
    h = hyena_filters(L, *filt_params)
    y = x1 * fft_long_conv(v, h[:, 0], h[:, 1], hy_skip[0])
    return x2 * fft_long_conv(y, h[:, 2], h[:, 3], hy_skip[1])


def axial_rope_tables(L):
    t = jnp.arange(L, dtype=jnp.int32)
    pos = jnp.stack([t // GRID_W, t % GRID_W], axis=-1).astype(F32)
    inv = ROPE_BASE ** (-jnp.arange(ROPE_NF, dtype=F32) / ROPE_NF)
    ang = pos[:, :, None] * inv
    return jnp.cos(ang), jnp.sin(ang)


def apply_axial_rope(x, cos, sin):
    shp = x.shape
    xr = x.reshape(*shp[:-1], 2, 2, ROPE_NF)
    x1, x2 = xr[..., 0, :], xr[..., 1, :]
    c = cos[:, None].astype(x.dtype)
    s = sin[:, None].astype(x.dtype)
    return jnp.stack([x1 * c - x2 * s, x2 * c + x1 * s], axis=-2).reshape(shp)


def diff_attention(q, k, v, lam):
    B, H, L = q.shape[:3]
    qb = jnp.moveaxis(q.reshape(B, H, L // Q_BLOCK, Q_BLOCK, 2, DA_DH), 2, 0)

    def one_block(qq):
        s = jnp.einsum('bhqmd,bhkmd->bmhqk', qq, k).astype(F32) * DA_DH ** -0.5
        p = jax.nn.softmax(s, axis=-1)
        a = (p[:, 0] - lam * p[:, 1]).astype(v.dtype)
        return jnp.einsum('bhqk,bhkd->bhqd', a, v)

    o = lax.map(one_block, qb)
    return jnp.moveaxis(o, 0, 2).reshape(B, H, L, DA_VD)


def mixer_ab(h, ab, layer_idx, ctx_kv):
    (w_in, conv_w, conv_b, f_w1, f_b1, f_freq, f_w2, f_b2, f_w3, hy_skip,
     qn_w, kn_w, lam_p, subln_w, w_out) = ab
    B, L, _ = h.shape
    proj = h @ w_in
    y_hy = hyena_operator(proj[..., :HY_IN], conv_w, conv_b,
                          (f_w1, f_b1, f_freq, f_w2, f_b2, f_w3), hy_skip)
    q, k, v = jnp.split(proj[..., HY_IN:], 3, axis=-1)
    q = rms_norm(q.reshape(B, L, DA_HEADS, 2, DA_DH), qn_w).transpose(0, 2, 1, 3, 4)
    k = rms_norm(k.reshape(B, L, DA_HEADS, 2, DA_DH), kn_w).transpose(0, 2, 1, 3, 4)
    v = v.reshape(B, L, DA_HEADS, DA_VD).transpose(0, 2, 1, 3)
    lam_init = 0.8 - 0.6 * math.exp(-0.3 * layer_idx)
    lp = lam_p.astype(F32)
    lam = jnp.exp(jnp.sum(lp[0] * lp[1])) - jnp.exp(jnp.sum(lp[2] * lp[3])) + lam_init
    if ctx_kv is None:
        o = diff_attention(q, k, v, lam)
        aux = (k.reshape(B, DA_HEADS, L, 2 * DA_DH), v)
    else:
        ctx_k, ctx_v = ctx_kv
        cos, sin = axial_rope_tables(L)
        q = apply_axial_rope(q, cos, sin)
        k = apply_axial_rope(k, cos, sin)
        k_all = jnp.concatenate([ctx_k.reshape(B, DA_HEADS, ctx_k.shape[2], 2, DA_DH), k], axis=2)
        v_all = jnp.concatenate([ctx_v, v], axis=2)
        o = diff_attention(q, k_all, v_all, lam)
        aux = None
    o = rms_norm(o, subln_w) * (1.0 - lam_init)
    o = o.transpose(0, 2, 1, 3).reshape(B, L, DA_WIDTH)
    return jnp.concatenate([y_hy, o], axis=-1) @ w_out, aux


def gla_chunk_scan(q, k, v, g, s0):
    B, H, L, _ = q.shape
    n = L // GLA_CHUNK
    mask = jnp.tril(jnp.ones((GLA_CHUNK, GLA_CHUNK), dtype=bool))

    def chunks(x):
        return jnp.moveaxis(x.astype(F32).reshape(B, H, n, GLA_CHUNK, x.shape[-1]), 2, 0)

    def step(S, inp):
        qc, kc, vc, gc = inp
        b = jnp.cumsum(gc, axis=-2)
        b_last = b[..., -1:, :]
        q_dec = qc * jnp.exp(b)
        att = jnp.einsum('bhcd,bhsd->bhcs', q_dec, kc * jnp.exp(-b))
        att = jnp.where(mask, att, 0.0)
        o = jnp.einsum('bhcs,bhse->bhce', att, vc) + jnp.einsum('bhcd,bhde->bhce', q_dec, S)
        S = jnp.exp(b_last[..., 0, :])[..., None] * S + jnp.einsum('bhsd,bhse->bhde', kc * jnp.exp(b_last - b), vc)
        return S, o

    S, o = lax.scan(step, s0.astype(F32), (chunks(q), chunks(k), chunks(v), chunks(g)))
    o = jnp.moveaxis(o, 0, 2).reshape(B, H, L, v.shape[-1])
    return o.astype(v.dtype), S.astype(v.dtype)


def mixer_c(h, cp, s0_f, s0_b):
    w_in, gate_up_w, gate_up_b, out_norm_w, w_out = cp
    B, L, _ = h.shape
    dk, dv = GLA_HEADS * GLA_DK, GLA_HEADS * GLA_DV
    p = h @ w_in
    q, k, v, gout, lr_f, lr_b = jnp.split(
        p, [dk, 2 * dk, 2 * dk + dv, 2 * dk + 2 * dv, 2 * dk + 2 * dv + GLA_RANK], axis=-1)

    def heads(x):
        return x.reshape(B, L, GLA_HEADS, -1).transpose(0, 2, 1, 3)

    def log_gate(lr, i):
        return jax.nn.log_sigmoid((lr @ gate_up_w[i] + gate_up_b[i]).astype(F32)) / GLA_GATE_NORM

    def flip(x):
        return jnp.flip(x, axis=2)

    qh = heads(q) * GLA_DK ** -0.5
    kh, vh = heads(k), heads(v)
    o_f, s_f = gla_chunk_scan(qh, kh, vh, heads(log_gate(lr_f, 0)), s0_f)
    o_b, s_b = gla_chunk_scan(flip(qh), flip(kh), flip(vh), flip(heads(log_gate(lr_b, 1))), s0_b)
    o = rms_norm(o_f + flip(o_b), out_norm_w)
    o = o.transpose(0, 2, 1, 3).reshape(B, L, dv) * jax.nn.silu(gout)
    return o @ w_out, (s_f, s_b)


def grouped_moe(h, router_w, router_b, w_gate, w_up, w_down):
    B, L, D = h.shape
    t = h.reshape(-1, D)
    s = jax.nn.sigmoid((t @ router_w).astype(F32))
    s_sel = s + router_b.astype(F32)
    grp = s_sel.reshape(-1, N_GROUPS, EXPERTS_PER_GROUP)
    g_idx = jnp.argmax(lax.top_k(grp, TOP_K)[0].sum(-1), axis=-1)
    in_grp = jnp.take_along_axis(grp, g_idx[:, None, None], axis=1)[:, 0]
    _, loc = lax.top_k(in_grp, TOP_K)
    e_idx = g_idx[:, None] * EXPERTS_PER_GROUP + loc
    w_sel = jnp.take_along_axis(s, e_idx, axis=-1)
    w_sel = w_sel / jnp.sum(w_sel, axis=-1, keepdims=True)
    combine = jnp.einsum('tk,tke->te', w_sel, jax.nn.one_hot(e_idx, N_EXPERTS, dtype=F32)).astype(t.dtype)
    out = jnp.zeros_like(t)
    for e in range(N_EXPERTS):
        y = (jax.nn.silu(t @ w_gate[e]) * (t @ w_up[e])) @ w_down[e]
        out = out + combine[:, e:e + 1] * y
    return out.reshape(B, L, D)


def residual_block(x, cond, w_mod_l, b_mod_l, norm_w_l, mixer, moe_p):
    m = (jax.nn.silu(cond) @ w_mod_l + b_mod_l)[..., None, :]
    sh1, sc1, g1, sh2, sc2, g2 = jnp.split(m, 6, axis=-1)
    out, aux = mixer(rms_norm(x, norm_w_l[0]) * (1.0 + sc1) + sh1)
    x = x + g1 * out
    x = x + g2 * grouped_moe(rms_norm(x, norm_w_l[1]) * (1.0 + sc2) + sh2, *moe_p)
    return x, aux


def setup_inputs(seed: int = 0) -> dict:
    key = jax.random.key(seed)
    ks = iter(jax.random.split(key, 48))
    D = D_MODEL

    def nrm(shape, scale=1.0):
        return scale * jax.random.normal(next(ks), shape, F32)

    def gain(shape):
        return 1.0 + nrm(shape, 0.02)

    return {
        'x_prompt': nrm((BATCH, SEQ, D)),
        'x_sample': nrm((DEC_BATCH, DEC_SEQ, D)),
        'cache_l0_k': nrm((DEC_BATCH, DA_HEADS, PAST_LEN, 2 * DA_DH)),
        'cache_l0_v': nrm((DEC_BATCH, DA_HEADS, PAST_LEN, DA_VD)),
        'state_l1_fwd': nrm((DEC_BATCH, GLA_HEADS, GLA_DK, GLA_DV)),
        'state_l1_bwd': nrm((DEC_BATCH, GLA_HEADS, GLA_DK, GLA_DV)),
        'c': nrm((DEC_BATCH, D)),
        'c_ctx': nrm((D,)),
        'w_mod': nrm((DEPTH, D, 6 * D), 0.5 * D ** -0.5),
        'b_mod': nrm((DEPTH, 6 * D), 0.02),
        'norm_w': gain((DEPTH, 2, D)),
        'l0_w_in': nrm((D, HY_IN + DA_IN), D ** -0.5),
        'l0_conv_w': nrm((3, HY_IN), 3 ** -0.5),
        'l0_conv_b': nrm((HY_IN,), 0.02),
        'l0_filt_w1': nrm((HY_EMB, HY_FFN), HY_EMB ** -0.5),
        'l0_filt_b1': nrm((HY_FFN,), 0.02),
        'l0_filt_freq': gain((2, HY_FFN)),
        'l0_filt_w2': nrm((HY_FFN, HY_FFN), HY_FFN ** -0.5),
        'l0_filt_b2': nrm((HY_FFN,), 0.02),
        'l0_filt_w3': nrm((HY_FFN, 4 * HY_WIDTH), HY_FFN ** -0.5),
        'l0_hy_skip': nrm((2, HY_WIDTH), 0.5),
        'l0_qn_w': gain((DA_DH,)),
        'l0_kn_w': gain((DA_DH,)),
        'l0_lambda': nrm((4, DA_DH), 0.1),
        'l0_subln_w': gain((DA_VD,)),
        'l0_w_out': nrm((HY_WIDTH + DA_WIDTH, D), D ** -0.5),
        'l1_w_in': nrm((D, GLA_IN), D ** -0.5),
        'l1_gate_up_w': nrm((2, GLA_RANK, GLA_HEADS * GLA_DK), GLA_RANK ** -0.5),
        'l1_gate_up_b': nrm((2, GLA_HEADS * GLA_DK), 0.1),
        'l1_out_norm_w': gain((GLA_DV,)),
        'l1_w_out': nrm((GLA_HEADS * GLA_DV, D), D ** -0.5),
        'moe_router_w': nrm((D, N_EXPERTS), D ** -0.5),
        'moe_router_b': nrm((N_EXPERTS,), 0.01),
        'moe_w_gate': nrm((DEPTH, N_EXPERTS, D, D_FF), D ** -0.5),
        'moe_w_up': nrm((DEPTH, N_EXPERTS, D, D_FF), D ** -0.5),
        'moe_w_down': nrm((DEPTH, N_EXPERTS, D_FF, D), D_FF ** -0.5),
    }


def reference(x_prompt, x_sample, cache_l0_k, cache_l0_v, state_l1_fwd, state_l1_bwd, c, c_ctx,
              w_mod, b_mod, norm_w,
              l0_w_in, l0_conv_w, l0_conv_b, l0_filt_w1, l0_filt_b1, l0_filt_freq, l0_filt_w2,
              l0_filt_b2, l0_filt_w3, l0_hy_skip, l0_qn_w, l0_kn_w, l0_lambda, l0_subln_w, l0_w_out,
              l1_w_in, l1_gate_up_w, l1_gate_up_b, l1_out_norm_w, l1_w_out,
              moe_router_w, moe_router_b, moe_w_gate, moe_w_up, moe_w_down):
    ab = (l0_w_in, l0_conv_w, l0_conv_b, l0_filt_w1, l0_filt_b1, l0_filt_freq, l0_filt_w2,
          l0_filt_b2, l0_filt_w3, l0_hy_skip, l0_qn_w, l0_kn_w, l0_lambda, l0_subln_w, l0_w_out)
    cp = (l1_w_in, l1_gate_up_w, l1_gate_up_b, l1_out_norm_w, l1_w_out)

    y_prompt = x_prompt
    zero_state = jnp.zeros((x_prompt.shape[0], GLA_HEADS, GLA_DK, GLA_DV), x_prompt.dtype)
    for l in range(DEPTH):
        moe_p = (moe_router_w, moe_router_b, moe_w_gate[l], moe_w_up[l], moe_w_down[l])
        if l % 2 == 0:
            y_prompt, (new_k, new_v) = residual_block(
                y_prompt, c_ctx, w_mod[l], b_mod[l], norm_w[l],
                lambda hh: mixer_ab(hh, ab, l, None), moe_p)
        else:
            y_prompt, (new_sf, new_sb) = residual_block(
                y_prompt, c_ctx, w_mod[l], b_mod[l], norm_w[l],
                lambda hh: mixer_c(hh, cp, zero_state, zero_state), moe_p)

    y_sample = x_sample
    for l in range(DEPTH):
        moe_p = (moe_router_w, moe_router_b, moe_w_gate[l], moe_w_up[l], moe_w_down[l])
        if l % 2 == 0:
            y_sample, _ = residual_block(
                y_sample, c, w_mod[l], b_mod[l], norm_w[l],
                lambda hh: mixer_ab(hh, ab, l, (cache_l0_k, cache_l0_v)), moe_p)
        else:
            y_sample, _ = residual_block(
                y_sample, c, w_mod[l], b_mod[l], norm_w[l],
                lambda hh: mixer_c(hh, cp, state_l1_fwd, state_l1_bwd), moe_p)

    return (y_prompt, y_sample, new_k, new_v, new_sf, new_sb)
```

```python
import cmath
import functools
import math

import numpy as np
import jax
import jax.numpy as jnp
from jax import lax
from jax.experimental import pallas as pl
from jax.experimental.pallas import tpu as pltpu

F32 = jnp.float32
BF16 = jnp.bfloat16
HI = lax.Precision.HIGHEST
EPS = 1e-6

LANES = 128
SUBLANES = 8
VMEM_LIMIT = 56 << 20

DA_HEADS = 4
DA_DH = 64
DA_VD = 2 * DA_DH
ROPE_NF = DA_DH // 4
ROPE_BASE = 10000.0
GRID_W = 64
HY_BANDS = 16
HY_TARGET = 1e-2
HY_MAX_DECAY = math.log(HY_TARGET) / 0.3
HY_MIN_DECAY = math.log(HY_TARGET) / 1.5
GLA_HEADS = 4
GLA_RANK = 16
GLA_GATE_NORM = 16.0
GLA_CHUNK = 64
N_EXPERTS = 16
N_GROUPS = 4
EPG = N_EXPERTS // N_GROUPS
MOE_TILE = 128
NT = (((1,), (1,)), ((), ()))
TN = (((0,), (0,)), ((), ()))


def _params(sem):
    return pltpu.CompilerParams(dimension_semantics=sem, vmem_limit_bytes=VMEM_LIMIT)


def _const_spec(shape):
    nd = len(shape)
    return pl.BlockSpec(shape, lambda *_: (0,) * nd)


def _silu(x):
    return x * (1.0 / (1.0 + jnp.exp(-x)))


def _norm_mod(x, nw, sc, sh):
    ms = jnp.mean(x * x, axis=-1, keepdims=True)
    return x * lax.rsqrt(ms + EPS) * nw * (1.0 + sc) + sh


def _mod_kernel(c_ref, w_ref, b_ref, o_ref):
    s = _silu(c_ref[...])
    o_ref[...] = jnp.dot(s, w_ref[...], precision=HI, preferred_element_type=F32) + b_ref[...]


def _modulation(cond, w_mod, b_mod):
    depth, d, d6 = w_mod.shape
    r = cond.shape[0]
    tn = 1536
    return pl.pallas_call(
        _mod_kernel,
        grid=(depth, d6 // tn),
        in_specs=[
            pl.BlockSpec((r, d), lambda l, j: (0, 0)),
            pl.BlockSpec((None, d, tn), lambda l, j: (l, 0, j)),
            pl.BlockSpec((None, 1, tn), lambda l, j: (l, 0, j)),
        ],
        out_specs=pl.BlockSpec((None, r, tn), lambda l, j: (l, 0, j)),
        out_shape=jax.ShapeDtypeStruct((depth, r, d6), F32),
        compiler_params=_params(("parallel", "parallel")),
        name="modulation",
    )(cond, w_mod, b_mod.reshape(depth, 1, d6))


def _proj0_kernel(x_ref, sc_ref, sh_ref, nw_ref, wqkv_ref, whyt_ref, qkw_ref, ones_ref, cos_ref, sin_ref,
                  hyt_ref, q_ref, k_ref, v_ref, *cache_refs, rope):
    h = _norm_mod(x_ref[...], nw_ref[...], sc_ref[...], sh_ref[...]).astype(BF16)
    hyt_ref[...] = lax.dot_general(whyt_ref[...], h, NT, preferred_element_type=F32).astype(hyt_ref.dtype)
    p = jnp.dot(h, wqkv_ref[...], preferred_element_type=F32)
    w = DA_HEADS * DA_VD
    ones = ones_ref[...]

    def segnorm(z, gain):
        z2 = z * z
        hi = z2.astype(BF16)
        lo = (z2 - hi.astype(F32)).astype(BF16)
        ss = jnp.dot(hi, ones, preferred_element_type=F32) + jnp.dot(lo, ones, preferred_element_type=F32)
        return z * lax.rsqrt(ss * (1.0 / DA_DH) + EPS) * gain

    q = segnorm(p[:, :w], qkw_ref[0:1, :])
    k = segnorm(p[:, w:2 * w], qkw_ref[1:2, :])
    v = p[:, 2 * w:]
    if cache_refs:
        kn_ref, vn_ref = cache_refs
        for hh in range(DA_HEADS):
            kn_ref[hh] = k[:, hh * DA_VD:(hh + 1) * DA_VD]
            vn_ref[hh] = v[:, hh * DA_VD:(hh + 1) * DA_VD]
    if rope:
        cos = jnp.concatenate([cos_ref[...]] * (w // LANES), axis=1)
        sin = jnp.concatenate([sin_ref[...]] * (w // LANES), axis=1)
        lane = lax.broadcasted_iota(jnp.int32, q.shape, 1)
        first = (lane % (2 * ROPE_NF)) < ROPE_NF

        def rot(z):
            partner = jnp.where(first, pltpu.roll(z, w - ROPE_NF, 1), pltpu.roll(z, ROPE_NF, 1))
            return z * cos + partner * sin

        q = rot(q)
        k = rot(k)
    q_ref[...] = (q * DA_DH ** -0.5).astype(q_ref.dtype)
    k_ref[...] = k.astype(k_ref.dtype)
    v_ref[...] = v.astype(v_ref.dtype)


def _rope_tables(seq):
    t = np.arange(seq)
    lane = np.arange(LANES)
    d = lane % DA_DH
    axis = d // (2 * ROPE_NF)
    part = (d % (2 * ROPE_NF)) // ROPE_NF
    f = d % ROPE_NF
    pos = jnp.where(axis[None, :] == 0, (t // GRID_W)[:, None], (t % GRID_W)[:, None]).astype(F32)
    inv = ROPE_BASE ** (-jnp.arange(ROPE_NF, dtype=F32) / ROPE_NF)
    ang = pos * inv[f][None, :]
    sign = jnp.asarray(np.where(part == 0, -1.0, 1.0), F32)[None, :]
    return jnp.cos(ang), jnp.sin(ang) * sign


def _proj0(x, sc, sh, nw, wqkv, whyt, qkw, ones, batch, seq, row_fn, rope, emit_cache):
    t, d = x.shape
    tm = min(seq, 512)
    per_seq = seq // tm
    w = DA_HEADS * DA_VD
    hy_in = whyt.shape[0]
    if rope:
        cos, sin = _rope_tables(seq)
    else:
        cos = jnp.ones((seq, LANES), F32)
        sin = jnp.zeros((seq, LANES), F32)
    mod_spec = pl.BlockSpec((None, 1, d), lambda i: (row_fn(i // per_seq), 0, 0))
    tab_spec = pl.BlockSpec((tm, LANES), lambda i: (i % per_seq, 0))
    tok = lambda n: pl.BlockSpec((tm, n), lambda i: (i, 0))
    out_specs = [pl.BlockSpec((None, hy_in, tm), lambda i: (i // per_seq, 0, i % per_seq)), tok(w), tok(w), tok(w)]
    out_shape = [jax.ShapeDtypeStruct((batch, hy_in, seq), BF16)] + [jax.ShapeDtypeStruct((t, w), BF16)] * 3
    if emit_cache:
        cspec = pl.BlockSpec((None, DA_HEADS, tm, DA_VD), lambda i: (i // per_seq, 0, i % per_seq, 0))
        out_specs += [cspec, cspec]
        out_shape += [jax.ShapeDtypeStruct((batch, DA_HEADS, seq, DA_VD), F32)] * 2
    return pl.pallas_call(
        functools.partial(_proj0_kernel, rope=rope),
        grid=(t // tm,),
        in_specs=[tok(d), mod_spec, mod_spec, _const_spec((1, d)), _const_spec(wqkv.shape), _const_spec(whyt.shape),
                  _const_spec(qkw.shape), _const_spec(ones.shape), tab_spec, tab_spec],
        out_specs=out_specs,
        out_shape=out_shape,
        compiler_params=_params(("parallel",)),
        name="proj0",
    )(x, sc, sh, nw, wqkv, whyt, qkw, ones, cos, sin)


def _c_add(a, b, sign=1.0):
    if b is None:
        return a
    if a is None:
        return b if sign > 0 else tuple(None if p is None else -p for p in b)
    out = []
    for pa, pb in zip(a, b):
        if pb is None:
            out.append(pa)
        elif pa is None:
            out.append(pb if sign > 0 else -pb)
        else:
            out.append(pa + pb if sign > 0 else pa - pb)
    return tuple(out)


def _c_mul_const(w, a):
    if a is None:
        return None
    wr = 0.0 if abs(w.real) < 1e-12 else w.real
    wi = 0.0 if abs(w.imag) < 1e-12 else w.imag
    ar, ai = a

    def scaled(c, p):
        if p is None or c == 0.0:
            return None
        if c == 1.0:
            return p
        if c == -1.0:
            return -p
        return c * p

    re = _c_add((scaled(wr, ar),), (scaled(wi, ai),), -1.0)[0]
    im = _c_add((scaled(wr, ai),), (scaled(wi, ar),), 1.0)[0]
    return (re, im)


def _fft_list(vals, sign, first_half_only=False):
    n = len(vals)
    if n == 1:
        return list(vals)
    ev = _fft_list(vals[0::2], sign)
    od = _fft_list(vals[1::2], sign)
    out = [None] * n
    for k in range(n // 2):
        tw = _c_mul_const(cmath.exp(sign * 2j * math.pi * k / n), od[k])
        out[k] = _c_add(ev[k], tw, 1.0)
        if not first_half_only:
            out[k + n // 2] = _c_add(ev[k], tw, -1.0)
    return out


def _dft_mats(n2, total):
    k = np.arange(n2)
    ang = 2.0 * np.pi * np.outer(k, k) / n2
    fr, fi = np.cos(ang), -np.sin(ang)
    fwd = np.block([[fr, fi], [-fi, fr]])
    inv = np.block([[fr, -fi], [fi, fr]]) / total
    return jnp.asarray(fwd, BF16), jnp.asarray(inv, BF16)


def _twiddles(n1, n2):
    ang = 2.0 * np.pi * np.outer(np.arange(n1), np.arange(n2)) / (n1 * n2)
    return jnp.asarray(np.cos(ang), F32), jnp.asarray(-np.sin(ang), F32)


def _fft_split(seq):
    n = 2 * seq
    n2 = min(512, n // 2)
    return n // n2, n2


def _across_fwd(load, n_in, z_ref, twr_ref, twi_ref, n1, n2, rows):
    per_row = n2 // LANES

    def body(it, carry):
        r0 = pl.multiple_of((it // per_row) * SUBLANES, SUBLANES)
        l0 = pl.multiple_of((it % per_row) * LANES, LANES)
        vals = [load(j, r0, l0) for j in range(n_in)] + [None] * (n1 - n_in)
        outs = _fft_list(vals, -1.0)
        for k1 in range(n1):
            twr = twr_ref[pl.ds(k1, 1), pl.ds(l0, LANES)]
            twi = twi_ref[pl.ds(k1, 1), pl.ds(l0, LANES)]
            re, im = outs[k1]
            zero = jnp.zeros((SUBLANES, LANES), F32)
            re = zero if re is None else re
            im = zero if im is None else im
            z_ref[k1, pl.ds(r0, SUBLANES), pl.ds(l0, LANES)] = re * twr - im * twi
            z_ref[k1, pl.ds(r0, SUBLANES), pl.ds(n2 + l0, LANES)] = re * twi + im * twr
        return carry

    lax.fori_loop(0, (rows // SUBLANES) * per_row, body, 0)


def _hy_hidden_kernel(w1_ref, b1_ref, freq_ref, w2_ref, b2_ref, bands_ref, o_ref, *, seq):
    n = 2 * seq
    k = lax.broadcasted_iota(jnp.int32, (n, 1), 0)
    pos = jnp.where(k < seq, k, n - k).astype(F32)
    t = pos / (seq - 1)
    ang = 2.0 * math.pi * pos / seq * bands_ref[...]
    z = (t * w1_ref[0:1, :]
         + jnp.dot(jnp.cos(ang), w1_ref[1:1 + HY_BANDS, :], precision=HI, preferred_element_type=F32)
         + jnp.dot(-jnp.sin(ang), w1_ref[1 + HY_BANDS:, :], precision=HI, preferred_element_type=F32)
         + b1_ref[...])
    hid = jnp.sin(freq_ref[0:1, :] * z)
    hid = jnp.sin(freq_ref[1:2, :] * (jnp.dot(hid, w2_ref[...], precision=HI, preferred_element_type=F32) + b2_ref[...]))
    o_ref[...] = hid


def _hy_hidden(seq, f_w1, f_b1, f_freq, f_w2, f_b2):
    ffn = f_w1.shape[1]
    bands = jnp.linspace(1e-4, HY_BANDS - 1, HY_BANDS, dtype=F32).reshape(1, HY_BANDS)
    args = (f_w1, f_b1.reshape(1, ffn), f_freq, f_w2, f_b2.reshape(1, ffn), bands)
    return pl.pallas_call(
        functools.partial(_hy_hidden_kernel, seq=seq),
        grid=(1,),
        in_specs=[_const_spec(a.shape) for a in args],
        out_specs=_const_spec((2 * seq, ffn)),
        out_shape=jax.ShapeDtypeStruct((2 * seq, ffn), F32),
        compiler_params=_params(("arbitrary",)),
        name="hyena_filter_hidden",
    )(*args)


def _hy_spec_kernel(hid_ref, wb_ref, wa_ref, dec_ref, twr_ref, twi_ref, gf_ref, o_ref, taps_ref, z_ref, *, seq, n1, n2):
    ct = wb_ref.shape[0]
    hid = hid_ref[...]
    back = lax.dot_general(wb_ref[...], hid[:seq], NT, precision=HI, preferred_element_type=F32)
    ahead = lax.dot_general(wa_ref[...], hid[seq:], NT, precision=HI, preferred_element_type=F32)
    lane = lax.broadcasted_iota(jnp.int32, (1, seq), 1)
    dec = dec_ref[...]
    t_back = lane.astype(F32) / (seq - 1)
    t_ahead = (seq - lane).astype(F32) / (seq - 1)
    back = back * jnp.exp(-t_back * dec)
    ahead = jnp.where(lane == 0, 0.0, ahead * jnp.exp(-t_ahead * dec))
    norm = (jnp.sum(jnp.abs(back), axis=1, keepdims=True) + jnp.sum(jnp.abs(ahead), axis=1, keepdims=True)) + EPS
    taps_ref[:, :seq] = back / norm
    taps_ref[:, seq:] = ahead / norm

    def load(j, r0, l0):
        return (taps_ref[pl.ds(r0, SUBLANES), pl.ds(j * n2 + l0, LANES)], None)

    _across_fwd(load, n1, z_ref, twr_ref, twi_ref, n1, n2, ct)
    for k1 in range(n1):
        o_ref[k1] = jnp.dot(z_ref[k1].astype(BF16), gf_ref[...], preferred_element_type=F32)


def _hy_filter_spectra(hid, f_w3, seq, width, ct):
    n1, n2 = _fft_split(seq)
    ffn = f_w3.shape[0]
    w3t = f_w3.T.reshape(4, width, ffn)
    decay = jnp.abs(jnp.linspace(HY_MIN_DECAY, HY_MAX_DECAY, width, dtype=F32)).reshape(width, 1)
    twr, twi = _twiddles(n1, n2)
    gf, _ = _dft_mats(n2, 2 * seq)
    return pl.pallas_call(
        functools.partial(_hy_spec_kernel, seq=seq, n1=n1, n2=n2),
        grid=(2, width // ct),
        in_specs=[
            _const_spec(hid.shape),
            pl.BlockSpec((None, ct, ffn), lambda i, c: (2 * i, c, 0)),
            pl.BlockSpec((None, ct, ffn), lambda i, c: (2 * i + 1, c, 0)),
            pl.BlockSpec((ct, 1), lambda i, c: (c, 0)),
            _const_spec(twr.shape), _const_spec(twi.shape), _const_spec(gf.shape),
        ],
        out_specs=pl.BlockSpec((None, n1, ct, 2 * n2), lambda i, c: (i, 0, c, 0)),
        out_shape=jax.ShapeDtypeStruct((2, n1, width, 2 * n2), F32),
        scratch_shapes=[pltpu.VMEM((ct, 2 * seq), F32), pltpu.VMEM((n1, ct, 2 * n2), F32)],
        compiler_params=_params(("parallel", "parallel")),
        name="hyena_filter_spectra",
    )(hid, w3t, w3t, decay, twr, twi, gf)


def _short_conv_rows(ref, cw, r0, nrows, seq):
    x = ref[pl.ds(r0, nrows), :].astype(F32)
    lane = lax.broadcasted_iota(jnp.int32, x.shape, 1)
    left = jnp.where(lane == 0, 0.0, pltpu.roll(x, 1, 1))
    right = jnp.where(lane == seq - 1, 0.0, pltpu.roll(x, seq - 1, 1))
    return left * cw[:, 0:1] + x * cw[:, 1:2] + right * cw[:, 2:3] + cw[:, 3:4]


def _hy_conv_kernel(uin_ref, gin_ref, cwu_ref, cwg_ref, d_ref, h_ref, twr_ref, twi_ref, gf_ref, gi_ref,
                    o_ref, u_ref, z_ref, *, seq, n1, n2, conv_u):
    ct = uin_ref.shape[1]
    rows16 = 2 * SUBLANES

    def prep(i, carry):
        r0 = pl.multiple_of(i * rows16, rows16)
        cwg = cwg_ref[pl.ds(r0, rows16), :]
        cwu = cwu_ref[pl.ds(r0, rows16), :]
        for e in range(2):
            o_ref[e, pl.ds(r0, rows16), :] = _short_conv_rows(gin_ref.at[e], cwg, r0, rows16, seq)
            if conv_u:
                u_ref[e, pl.ds(r0, rows16), :] = _short_conv_rows(uin_ref.at[e], cwu, r0, rows16, seq)
            else:
                u_ref[e, pl.ds(r0, rows16), :] = uin_ref[e, pl.ds(r0, rows16), :].astype(F32)
        return carry

    lax.fori_loop(0, ct // rows16, prep, 0)

    def load(j, r0, l0):
        return (u_ref[0, pl.ds(r0, SUBLANES), pl.ds(j * n2 + l0, LANES)],
                u_ref[1, pl.ds(r0, SUBLANES), pl.ds(j * n2 + l0, LANES)])

    _across_fwd(load, n1 // 2, z_ref, twr_ref, twi_ref, n1, n2, ct)

    def within(k1, carry):
        y = jnp.dot(z_ref[k1].astype(BF16), gf_ref[...], preferred_element_type=F32)
        hh = h_ref[k1]
        yr, yi = y[:, :n2], y[:, n2:]
        hr, hi = hh[:, :n2], hh[:, n2:]
        p = jnp.concatenate([yr * hr - yi * hi, yr * hi + yi * hr], axis=1)
        z_ref[k1] = jnp.dot(p.astype(BF16), gi_ref[...], preferred_element_type=F32)
        return carry

    lax.fori_loop(0, n1, within, 0)

    per_row = n2 // LANES

    def finish(it, carry):
        r0 = pl.multiple_of((it // per_row) * SUBLANES, SUBLANES)
        l0 = pl.multiple_of((it % per_row) * LANES, LANES)
        vals = []
        for k1 in range(n1):
            twr = twr_ref[pl.ds(k1, 1), pl.ds(l0, LANES)]
            twi = twi_ref[pl.ds(k1, 1), pl.ds(l0, LANES)]
            re = z_ref[k1, pl.ds(r0, SUBLANES), pl.ds(l0, LANES)]
            im = z_ref[k1, pl.ds(r0, SUBLANES), pl.ds(n2 + l0, LANES)]
            vals.append((re * twr + im * twi, im * twr - re * twi))
        outs = _fft_list(vals, 1.0, first_half_only=True)
        d = d_ref[pl.ds(r0, SUBLANES), :]
        for j in range(n1 // 2):
            sl = (pl.ds(r0, SUBLANES), pl.ds(j * n2 + l0, LANES))
            for e in range(2):
                o_ref[(e,) + sl] = o_ref[(e,) + sl] * (outs[j][e] + u_ref[(e,) + sl] * d)
        return carry

    lax.fori_loop(0, (ct // SUBLANES) * per_row, finish, 0)


def _hy_conv(u, u_off, g, g_off, cw, spectra, conv_idx, d_skip, seq, width, ct, conv_u):
    batch = u.shape[0]
    n1, n2 = _fft_split(seq)
    twr, twi = _twiddles(n1, n2)
    gf, gi = _dft_mats(n2, 2 * seq)
    uo, go = u_off // ct, g_off // ct
    return pl.pallas_call(
        functools.partial(_hy_conv_kernel, seq=seq, n1=n1, n2=n2, conv_u=conv_u),
        grid=(width // ct, batch // 2),
        in_specs=[
            pl.BlockSpec((2, ct, seq), lambda c, p: (p, uo + c, 0)),
            pl.BlockSpec((2, ct, seq), lambda c, p: (p, go + c, 0)),
            pl.BlockSpec((ct, 4), lambda c, p: (uo + c, 0)),
            pl.BlockSpec((ct, 4), lambda c, p: (go + c, 0)),
            pl.BlockSpec((ct, 1), lambda c, p: (c, 0)),
            pl.BlockSpec((None, n1, ct, 2 * n2), lambda c, p: (conv_idx, 0, c, 0), pipeline_mode=pl.Buffered(1)),
            _const_spec(twr.shape), _const_spec(twi.shape), _const_spec(gf.shape), _const_spec(gi.shape),
        ],
        out_specs=pl.BlockSpec((2, ct, seq), lambda c, p: (p, c, 0)),
        out_shape=jax.ShapeDtypeStruct((batch, width, seq), F32),
        scratch_shapes=[pltpu.VMEM((2, ct, seq), F32), pltpu.VMEM((n1, ct, 2 * n2), F32)],
        compiler_params=_params(("arbitrary", "arbitrary")),
        name="hyena_conv",
    )(u, g, cw, cw, d_skip, spectra, twr, twi, gf, gi)


def _hyena(hyt, conv_w, conv_b, filt, hy_skip, seq, width):
    f_w1, f_b1, f_freq, f_w2, f_b2, f_w3 = filt
    ct = LANES
    hid = _hy_hidden(seq, f_w1, f_b1, f_freq, f_w2, f_b2)
    spectra = _hy_filter_spectra(hid, f_w3, seq, width, ct)
    cw = jnp.concatenate([conv_w.T, conv_b[:, None]], axis=1)
    y1 = _hy_conv(hyt, 0, hyt, width, cw, spectra, 0, hy_skip[0].reshape(width, 1), seq, width, ct, True)
    return _hy_conv(y1, 0, hyt, 2 * width, cw, spectra, 1, hy_skip[1].reshape(width, 1), seq, width, ct, False)


def _attn_kernel(q_ref, k_ref, v_ref, *rest, lam_init, with_cache):
    if with_cache:
        kc_ref, vc_ref, lam_ref, sw_ref, o_ref = rest
    else:
        lam_ref, sw_ref, o_ref = rest
    lp = lam_ref[...]
    lam = (jnp.exp(jnp.sum(lp[0:1] * lp[1:2], axis=1, keepdims=True))
           - jnp.exp(jnp.sum(lp[2:3] * lp[3:4], axis=1, keepdims=True)) + lam_init)
    q = q_ref[...]
    lane = lax.broadcasted_iota(jnp.int32, q.shape, 1)
    zero = jnp.zeros_like(q)
    qs = (jnp.where(lane < DA_DH, q, zero), jnp.where(lane >= DA_DH, q, zero))
    keys = [(k_ref[...], v_ref[...])]
    if with_cache:
        keys.append((kc_ref[...].astype(BF16), vc_ref[...].astype(BF16)))
    probs = []
    for qm in qs:
        s = [lax.dot_general(qm, kk, NT, preferred_element_type=F32) for kk, _ in keys]
        m = functools.reduce(jnp.maximum, [jnp.max(si, axis=1, keepdims=True) for si in s])
        p = [jnp.exp(si - m) for si in s]
        l = functools.reduce(lambda a, b: a + b, [jnp.sum(pi, axis=1, keepdims=True) for pi in p])
        probs.append((p, 1.0 / l))
    (p1, r1), (p2, r2) = probs
    r2 = r2 * lam
    o = None
    for i, (_, vv) in enumerate(keys):
        a = (p1[i] * r1 - p2[i] * r2).astype(BF16)
        part = jnp.dot(a, vv, preferred_element_type=F32)
        o = part if o is None else o + part
    ms = jnp.mean(o * o, axis=1, keepdims=True)
    o_ref[...] = (o * lax.rsqrt(ms + EPS) * sw_ref[...] * (1.0 - lam_init)).astype(o_ref.dtype)


def _diff_attention(q, k, v, cache, lam_p, subln_w, batch, seq, lam_init):
    tq = min(seq, 256)
    nq = seq // tq
    grid = (batch, DA_HEADS, nq)
    q_spec = pl.BlockSpec((tq, DA_VD), lambda b, h, i: (b * nq + i, h))
    kv_spec = pl.BlockSpec((seq, DA_VD), lambda b, h, i: (b, h))
    in_specs = [q_spec, kv_spec, kv_spec]
    args = [q, k, v]
    if cache is not None:
        past = cache[0].shape[2]
        c_spec = pl.BlockSpec((None, None, past, DA_VD), lambda b, h, i: (b, h, 0, 0))
        in_specs += [c_spec, c_spec]
        args += list(cache)
    in_specs += [_const_spec(lam_p.shape), _const_spec((1, DA_VD))]
    args += [lam_p, subln_w.reshape(1, DA_VD)]
    return pl.pallas_call(
        functools.partial(_attn_kernel, lam_init=lam_init, with_cache=cache is not None),
        grid=grid,
        in_specs=in_specs,
        out_specs=q_spec,
        out_shape=jax.ShapeDtypeStruct(q.shape, BF16),
        compiler_params=_params(("parallel", "parallel", "parallel")),
        name="diff_attention",
    )(*args)


def _outproj0_kernel(x_ref, yt_ref, o_ref, g_ref, wy_ref, wo_ref, out_ref):
    mix = lax.dot_general(yt_ref[...].astype(BF16), wy_ref[...], TN, preferred_element_type=F32)
    mix = mix + jnp.dot(o_ref[...], wo_ref[...], preferred_element_type=F32)
    out_ref[...] = x_ref[...] + g_ref[...] * mix


def _outproj0(x, yt, o, g, w_out, seq, row_fn):
    t, d = x.shape
    width = yt.shape[1]
    tm = min(seq, 512)
    per_seq = seq // tm
    wy = w_out[:width].astype(BF16)
    wo = w_out[width:].astype(BF16)
    tok = lambda n: pl.BlockSpec((tm, n), lambda i: (i, 0))
    return pl.pallas_call(
        _outproj0_kernel,
        grid=(t // tm,),
        in_specs=[tok(d), pl.BlockSpec((None, width, tm), lambda i: (i // per_seq, 0, i % per_seq)), tok(o.shape[1]),
                  pl.BlockSpec((None, 1, d), lambda i: (row_fn(i // per_seq), 0, 0)),
                  _const_spec(wy.shape), _const_spec(wo.shape)],
        out_specs=tok(d),
        out_shape=jax.ShapeDtypeStruct((t, d), F32),
        compiler_params=_params(("parallel",)),
        name="outproj0",
    )(x, yt, o, g, wy, wo)


def _top2_of_rows(rows):
    n = len(rows)
    v1 = functools.reduce(jnp.maximum, rows)
    i1 = jnp.full(rows[0].shape, n - 1, jnp.int32)
    for j in range(n - 2, -1, -1):
        i1 = jnp.where(rows[j] == v1, j, i1)
    masked = [jnp.where(i1 == j, -jnp.inf, rows[j]) for j in range(n)]
    v2 = functools.reduce(jnp.maximum, masked)
    i2 = jnp.full(rows[0].shape, n - 1, jnp.int32)
    for j in range(n - 2, -1, -1):
        i2 = jnp.where(masked[j] == v2, j, i2)
    return v1, i1, v2, i2


def _select_rows(idx, rows):
    out = rows[-1]
    for j in range(len(rows) - 2, -1, -1):
        out = jnp.where(idx == j, rows[j], out)
    return out


def _moe_kernel(x_ref, sc_ref, sh_ref, g_ref, nw_ref, rwt_ref, rb_ref, upper_ref, wg_ref, wu_ref, wd_ref,
                out_ref, hb_ref, rank_ref, comb_ref, acc_ref, cnt_ref):
    e = pl.program_id(1)
    tb = x_ref.shape[0]

    @pl.when(e == 0)
    def _route():
        h = _norm_mod(x_ref[...], nw_ref[...], sc_ref[...], sh_ref[...])
        hb_ref[...] = h.astype(BF16)
        logits = lax.dot_general(rwt_ref[...], h, NT, precision=HI, preferred_element_type=F32)
        s = 1.0 / (1.0 + jnp.exp(-logits))
        s_sel = s + rb_ref[...]
        s_rows = [s[j:j + 1, :] for j in range(N_EXPERTS)]
        sel_rows = [s_sel[j:j + 1, :] for j in range(N_EXPERTS)]
        tops = [_top2_of_rows(sel_rows[g * EPG:(g + 1) * EPG]) for g in range(N_GROUPS)]
        scores = [tp[0] + tp[2] for tp in tops]
        best = functools.reduce(jnp.maximum, scores)
        g_idx = jnp.full(best.shape, N_GROUPS - 1, jnp.int32)
        for g in range(N_GROUPS - 2, -1, -1):
            g_idx = jnp.where(scores[g] == best, g, g_idx)
        e1 = g_idx * EPG + _select_rows(g_idx, [tp[1] for tp in tops])
        e2 = g_idx * EPG + _select_rows(g_idx, [tp[3] for tp in tops])
        w1 = _select_rows(e1, s_rows)
        w2 = _select_rows(e2, s_rows)
        tot = w1 + w2
        w1, w2 = w1 / tot, w2 / tot
        masks = [(e1 == j) | (e2 == j) for j in range(N_EXPERTS)]
        comb_ref[...] = jnp.concatenate(
            [jnp.where(e1 == j, w1, 0.0) + jnp.where(e2 == j, w2, 0.0) for j in range(N_EXPERTS)], axis=0)
        mask = jnp.concatenate([jnp.where(m, 1.0, 0.0) for m in masks], axis=0)
        rank = jnp.dot(mask.astype(BF16), upper_ref[...], preferred_element_type=F32)
        rank_ref[...] = jnp.where(mask > 0.0, rank, -1.0)
        for j in range(N_EXPERTS):
            cnt_ref[j] = jnp.sum(mask[j:j + 1, :]).astype(jnp.int32)
        acc_ref[...] = jnp.zeros_like(acc_ref)

    rank = rank_ref[pl.ds(e, 1), :]
    comb = comb_ref[pl.ds(e, 1), :]
    n_tiles = (cnt_ref[e] + MOE_TILE - 1) // MOE_TILE

    def tile(j, carry):
        slot = (lax.broadcasted_iota(jnp.int32, (MOE_TILE, tb), 0) + j * MOE_TILE).astype(F32)
        hit = rank == slot
        onehot = jnp.where(hit, 1.0, 0.0).astype(BF16)
        xe = jnp.dot(onehot, hb_ref[...], preferred_element_type=F32).astype(BF16)
        gate = jnp.dot(xe, wg_ref[...], preferred_element_type=F32)
        up = jnp.dot(xe, wu_ref[...], preferred_element_type=F32)
        y = jnp.dot((_silu(gate) * up).astype(BF16), wd_ref[...], preferred_element_type=F32)
        w_slot = jnp.sum(jnp.where(hit, comb, 0.0), axis=1, keepdims=True)
        acc_ref[...] += lax.dot_general(onehot, (y * w_slot).astype(BF16), TN, preferred_element_type=F32)
        return carry

    lax.fori_loop(0, n_tiles, tile, 0)

    @pl.when(e == N_EXPERTS - 1)
    def _finish():
        out_ref[...] = x_ref[...] + g_ref[...] * acc_ref[...]


def _moe(x, sc, sh, g, nw, router_w, router_b, w_gate, w_up, w_down, seq, row_fn):
    t, d = x.shape
    tb = min(seq, 1024)
    per_seq = seq // tb
    dff = w_gate.shape[-1]
    upper = jnp.asarray(np.triu(np.ones((tb, tb), np.float32), 1), BF16)
    mod_spec = pl.BlockSpec((None, 1, d), lambda i, e: (row_fn(i // per_seq), 0, 0))
    tok = pl.BlockSpec((tb, d), lambda i, e: (i, 0))
    return pl.pallas_call(
        _moe_kernel,
        grid=(t // tb, N_EXPERTS),
        in_specs=[tok, mod_spec, mod_spec, mod_spec, _const_spec((1, d)), _const_spec((N_EXPERTS, d)),
                  _const_spec((N_EXPERTS, 1)), _const_spec(upper.shape),
                  pl.BlockSpec((None, d, dff), lambda i, e: (e, 0, 0)),
                  pl.BlockSpec((None, d, dff), lambda i, e: (e, 0, 0)),
                  pl.BlockSpec((None, dff, d), lambda i, e: (e, 0, 0))],
        out_specs=tok,
        out_shape=jax.ShapeDtypeStruct((t, d), F32),
        scratch_shapes=[pltpu.VMEM((tb, d), BF16), pltpu.VMEM((N_EXPERTS, tb), F32), pltpu.VMEM((N_EXPERTS, tb), F32),
                        pltpu.VMEM((tb, d), F32), pltpu.SMEM((N_EXPERTS,), jnp.int32)],
        compiler_params=_params(("parallel", "arbitrary")),
        name="moe",
    )(x, sc, sh, g, nw, router_w.T, router_b.reshape(N_EXPERTS, 1), upper, w_gate, w_up, w_down)


def _proj1_kernel(x_ref, sc_ref, sh_ref, nw_ref, w_ref, wlr_ref, p_ref, lr_ref):
    h = _norm_mod(x_ref[...], nw_ref[...], sc_ref[...], sh_ref[...]).astype(BF16)
    p_ref[...] = jnp.dot(h, w_ref[...], preferred_element_type=F32).astype(p_ref.dtype)
    lr_ref[...] = jnp.dot(h, wlr_ref[...], preferred_element_type=F32)


def _proj1(x, sc, sh, nw, w_main, w_lr, seq, row_fn):
    t, d = x.shape
    tm = min(seq, 512)
    per_seq = seq // tm
    mod_spec = pl.BlockSpec((None, 1, d), lambda i: (row_fn(i // per_seq), 0, 0))
    tok = lambda n: pl.BlockSpec((tm, n), lambda i: (i, 0))
    return pl.pallas_call(
        _proj1_kernel,
        grid=(t // tm,),
        in_specs=[tok(d), mod_spec, mod_spec, _const_spec((1, d)), _const_spec(w_main.shape), _const_spec(w_lr.shape)],
        out_specs=[tok(w_main.shape[1]), tok(LANES)],
        out_shape=[jax.ShapeDtypeStruct((t, w_main.shape[1]), BF16), jax.ShapeDtypeStruct((t, LANES), F32)],
        compiler_params=_params(("parallel",)),
        name="proj1",
    )(x, sc, sh, nw, w_main, w_lr)


def _log_sigmoid(x):
    return jnp.minimum(x, 0.0) - jnp.log(1.0 + jnp.exp(-jnp.abs(x)))


def _gla_chunk(q, k, v, g, state, tri, ones_col, reverse):
    b = jnp.dot(tri, g, precision=HI, preferred_element_type=F32)
    b_end = b[0:1, :] if reverse else b[GLA_CHUNK - 1:GLA_CHUNK, :]
    q_dec = (q * jnp.exp(b)).astype(BF16)
    k_dec = (k * jnp.exp(-b)).astype(BF16)
    att = lax.dot_general(q_dec, k_dec, NT, preferred_element_type=F32)
    row = lax.broadcasted_iota(jnp.int32, att.shape, 0)
    col = lax.broadcasted_iota(jnp.int32, att.shape, 1)
    att = jnp.where((col >= row) if reverse else (col <= row), att, 0.0)
    o = jnp.dot(att.astype(BF16), v, preferred_element_type=F32)
    o = o + jnp.dot(q_dec, state.astype(BF16), preferred_element_type=F32)
    tot = lax.dot_general(g, ones_col, TN, precision=HI, preferred_element_type=F32)
    dv = state.shape[1]
    tot = jnp.concatenate([tot] * (dv // LANES), axis=1)
    k_rem = (k * jnp.exp(b_end - b)).astype(BF16)
    state = jnp.exp(tot) * state + lax.dot_general(k_rem, v, TN, preferred_element_type=F32)
    return o, state


def _gla_kernel(qf_ref, kf_ref, vf_ref, lf_ref, qb_ref, kb_ref, vb_ref, lb_ref, wgf_ref, wgb_ref, bg_ref,
                s0f_ref, s0b_ref, of_ref, ob_ref, sf_ref, sb_ref, stf_ref, stb_ref, *, dk, dv):
    i = pl.program_id(1)
    nb = pl.num_programs(1)
    tc = qf_ref.shape[0]

    @pl.when(i == 0)
    def _init():
        stf_ref[...] = s0f_ref[...].astype(F32)
        stb_ref[...] = s0b_ref[...].astype(F32)

    r = lax.broadcasted_iota(jnp.int32, (GLA_CHUNK, GLA_CHUNK), 0)
    c = lax.broadcasted_iota(jnp.int32, (GLA_CHUNK, GLA_CHUNK), 1)
    tri_f = jnp.where(c <= r, 1.0, 0.0)
    tri_b = jnp.where(c >= r, 1.0, 0.0)
    ones_col = jnp.ones((GLA_CHUNK, LANES), F32)
    scale = dk ** -0.5
    gate_f = _log_sigmoid(jnp.dot(lf_ref[...], wgf_ref[...], precision=HI, preferred_element_type=F32)
                          + bg_ref[0:1, :]) / GLA_GATE_NORM
    gate_b = _log_sigmoid(jnp.dot(lb_ref[...], wgb_ref[...], precision=HI, preferred_element_type=F32)
                          + bg_ref[1:2, :]) / GLA_GATE_NORM
    n_chunks = tc // GLA_CHUNK
    for h in range(GLA_HEADS):
        ks, vs = slice(h * dk, (h + 1) * dk), slice(h * dv, (h + 1) * dv)
        state = stf_ref[h]
        for ci in range(n_chunks):
            rs = slice(ci * GLA_CHUNK, (ci + 1) * GLA_CHUNK)
            o, state = _gla_chunk(qf_ref[rs, ks].astype(F32) * scale, kf_ref[rs, ks].astype(F32), vf_ref[rs, vs],
                                  gate_f[rs, ks], state, tri_f, ones_col, False)
            of_ref[rs, vs] = o.astype(of_ref.dtype)
        stf_ref[h] = state
        state = stb_ref[h]
        for ci in range(n_chunks - 1, -1, -1):
            rs = slice(ci * GLA_CHUNK, (ci + 1) * GLA_CHUNK)
            o, state = _gla_chunk(qb_ref[rs, ks].astype(F32) * scale, kb_ref[rs, ks].astype(F32), vb_ref[rs, vs],
                                  gate_b[rs, ks], state, tri_b, ones_col, True)
            ob_ref[rs, vs] = o.astype(ob_ref.dtype)
        stb_ref[h] = state

    @pl.when(i == nb - 1)
    def _emit():
        sf_ref[...] = stf_ref[...]
        sb_ref[...] = stb_ref[...]


def _gla(p, lr, wgf, wgb, bg, s0f, s0b, batch, seq, dk, dv):
    tc = min(seq, 256)
    nb = seq // tc
    hk, hv = GLA_HEADS * dk, GLA_HEADS * dv
    fwd = lambda width, col: pl.BlockSpec((tc, width), lambda b, i: (b * nb + i, col))
    bwd = lambda width, col: pl.BlockSpec((tc, width), lambda b, i: (b * nb + nb - 1 - i, col))
    st_spec = pl.BlockSpec((None, GLA_HEADS, dk, dv), lambda b, i: (b, 0, 0, 0))
    v_col = 2 * hk // hv
    return pl.pallas_call(
        functools.partial(_gla_kernel, dk=dk, dv=dv),
        grid=(batch, nb),
        in_specs=[fwd(hk, 0), fwd(hk, 1), fwd(hv, v_col), fwd(LANES, 0),
                  bwd(hk, 0), bwd(hk, 1), bwd(hv, v_col), bwd(LANES, 0),
                  _const_spec(wgf.shape), _const_spec(wgb.shape), _const_spec(bg.shape), st_spec, st_spec],
        out_specs=[fwd(hv, 0), bwd(hv, 0), st_spec, st_spec],
        out_shape=[jax.ShapeDtypeStruct((batch * seq, hv), BF16)] * 2
        + [jax.ShapeDtypeStruct((batch, GLA_HEADS, dk, dv), F32)] * 2,
        scratch_shapes=[pltpu.VMEM((GLA_HEADS, dk, dv), F32)] * 2,
        compiler_params=_params(("parallel", "arbitrary")),
        name="gla_scan",
    )(p, p, p, lr, p, p, p, lr, wgf, wgb, bg, s0f, s0b)


def _outproj1_kernel(x_ref, of_ref, ob_ref, go_ref, g_ref, nw_ref, w_ref, out_ref, *, dv):
    o = of_ref[...].astype(F32) + ob_ref[...].astype(F32)
    parts = []
    for h in range(GLA_HEADS):
        oh = o[:, h * dv:(h + 1) * dv]
        ms = jnp.mean(oh * oh, axis=1, keepdims=True)
        parts.append(oh * lax.rsqrt(ms + EPS))
    o = jnp.concatenate(parts, axis=1) * nw_ref[...] * _silu(go_ref[...].astype(F32))
    out_ref[...] = x_ref[...] + g_ref[...] * jnp.dot(o.astype(BF16), w_ref[...], preferred_element_type=F32)


def _outproj1(x, o_f, o_b, p, g, out_norm_w, w_out, seq, row_fn, dv):
    t, d = x.shape
    hv = GLA_HEADS * dv
    tm = min(seq, 512)
    per_seq = seq // tm
    tok = lambda n, col=0: pl.BlockSpec((tm, n), lambda i: (i, col))
    go_col = (p.shape[1] - hv) // hv
    nw = jnp.tile(out_norm_w, GLA_HEADS).reshape(1, hv)
    wb = w_out.astype(BF16)
    return pl.pallas_call(
        functools.partial(_outproj1_kernel, dv=dv),
        grid=(t // tm,),
        in_specs=[tok(d), tok(hv), tok(hv), tok(hv, go_col),
                  pl.BlockSpec((None, 1, d), lambda i: (row_fn(i // per_seq), 0, 0)),
                  _const_spec((1, hv)), _const_spec(wb.shape)],
        out_specs=tok(d),
        out_shape=jax.ShapeDtypeStruct((t, d), F32),
        compiler_params=_params(("parallel",)),
        name="outproj1",
    )(x, o_f, o_b, p, g, nw, wb)


def _mod_rows(mod_l, d):
    return [mod_l[:, j * d:(j + 1) * d].reshape(mod_l.shape[0], 1, d) for j in range(6)]


def _trunk(x, batch, seq, mods, row_fn, norm_w, ab, cp, moe_w, cache, s0, emit_cache):
    d = x.shape[-1]
    t = batch * seq
    x = x.reshape(t, d)
    (w_in0, conv_w, conv_b, filt, hy_skip, qn_w, kn_w, lam_p, subln_w, w_out0) = ab
    (w_in1, gate_up_w, gate_up_b, out_norm_w, w_out1) = cp
    router_w, router_b, wg, wu, wd = moe_w
    width = d // 2
    hy_in = 3 * width

    sh1, sc1, g1, sh2, sc2, g2 = mods[0]
    nseg = DA_HEADS * 2
    qkw = jnp.stack([jnp.tile(qn_w, nseg), jnp.tile(kn_w, nseg)])
    seg = np.arange(DA_HEADS * DA_VD) // DA_DH
    ones = jnp.asarray(seg[:, None] == seg[None, :], BF16)
    outs = _proj0(x, sc1, sh1, norm_w[0, 0].reshape(1, d), w_in0[:, hy_in:].astype(BF16),
                  w_in0[:, :hy_in].T.astype(BF16), qkw, ones, batch, seq, row_fn,
                  rope=cache is not None, emit_cache=emit_cache)
    hyt, q, k, v = outs[:4]
    y_hy = _hyena(hyt, conv_w, conv_b, filt, hy_skip, seq, width)
    lam_init = 0.8 - 0.6 * math.exp(-0.3 * 0)
    o = _diff_attention(q, k, v, cache, lam_p, subln_w, batch, seq, lam_init)
    x = _outproj0(x, y_hy, o, g1, w_out0, seq, row_fn)
    x = _moe(x, sc2, sh2, g2, norm_w[0, 1].reshape(1, d), router_w, router_b, wg[0], wu[0], wd[0], seq, row_fn)

    sh1, sc1, g1, sh2, sc2, g2 = mods[1]
    dk = d // 2 // GLA_HEADS
    dv = d // GLA_HEADS
    n_main = GLA_HEADS * (2 * dk + 2 * dv)
    w_lr = jnp.pad(w_in1[:, n_main:], ((0, 0), (0, LANES - 2 * GLA_RANK))).astype(BF16)
    p, lr = _proj1(x, sc1, sh1, norm_w[1, 0].reshape(1, d), w_in1[:, :n_main].astype(BF16), w_lr, seq, row_fn)
    wgf = jnp.pad(gate_up_w[0], ((0, LANES - GLA_RANK), (0, 0)))
    wgb = jnp.pad(gate_up_w[1], ((GLA_RANK, LANES - 2 * GLA_RANK), (0, 0)))
    o_f, o_b, s_f, s_b = _gla(p, lr, wgf, wgb, gate_up_b, s0[0], s0[1], batch, seq, dk, dv)
    x = _outproj1(x, o_f, o_b, p, g1, out_norm_w, w_out1, seq, row_fn, dv)
    x = _moe(x, sc2, sh2, g2, norm_w[1, 1].reshape(1, d), router_w, router_b, wg[1], wu[1], wd[1], seq, row_fn)
    return x.reshape(batch, seq, d), outs[4:], (s_f, s_b)


def kernel(x_prompt, x_sample, cache_l0_k, cache_l0_v, state_l1_fwd, state_l1_bwd, c, c_ctx, w_mod, b_mod, norm_w,
           l0_w_in, l0_conv_w, l0_conv_b, l0_filt_w1, l0_filt_b1, l0_filt_freq, l0_filt_w2, l0_filt_b2, l0_filt_w3,
           l0_hy_skip, l0_qn_w, l0_kn_w, l0_lambda, l0_subln_w, l0_w_out, l1_w_in, l1_gate_up_w, l1_gate_up_b,
           l1_out_norm_w, l1_w_out, moe_router_w, moe_router_b, moe_w_gate, moe_w_up, moe_w_down):
    d = x_prompt.shape[-1]
    n_lat = c.shape[0]
    ctx_row = n_lat
    rows = 2 * SUBLANES
    cond = jnp.zeros((rows, d), F32).at[:n_lat].set(c).at[ctx_row].set(c_ctx)
    mod = _modulation(cond, w_mod, b_mod)
    mods = [_mod_rows(mod[l], d) for l in range(mod.shape[0])]
    filt = (l0_filt_w1, l0_filt_b1, l0_filt_freq, l0_filt_w2, l0_filt_b2, l0_filt_w3)
    ab = (l0_w_in, l0_conv_w, l0_conv_b, filt, l0_hy_skip, l0_qn_w, l0_kn_w, l0_lambda, l0_subln_w, l0_w_out)
    cp = (l1_w_in, l1_gate_up_w, l1_gate_up_b, l1_out_norm_w, l1_w_out)
    moe_w = (moe_router_w, moe_router_b, moe_w_gate.astype(BF16), moe_w_up.astype(BF16), moe_w_down.astype(BF16))

    b_ctx, l_ctx = x_prompt.shape[:2]
    zero_state = jnp.zeros((b_ctx,) + state_l1_fwd.shape[1:], F32)
    y_prompt, (new_k, new_v), (new_sf, new_sb) = _trunk(
        x_prompt, b_ctx, l_ctx, mods, lambda b: ctx_row, norm_w, ab, cp, moe_w, None, (zero_state, zero_state), True)
    b_lat, l_lat = x_sample.shape[:2]
    y_sample, _, _ = _trunk(
        x_sample, b_lat, l_lat, mods, lambda b: b, norm_w, ab, cp, moe_w, (cache_l0_k, cache_l0_v),
        (state_l1_fwd, state_l1_bwd), False)
    return (y_prompt, y_sample, new_k, new_v, new_sf, new_sb)
```

```python
import cmath
import functools
import math

import numpy as np
import jax
import jax.numpy as jnp
from jax import lax
from jax.experimental import pallas as pl
from jax.experimental.pallas import tpu as pltpu

F32 = jnp.float32
BF16 = jnp.bfloat16
HI = lax.Precision.HIGHEST
EPS = 1e-6

LANES = 128
SUBLANES = 8
VMEM_LIMIT = 56 << 20

DA_HEADS = 4
DA_DH = 64
DA_VD = 2 * DA_DH
ROPE_NF = DA_DH // 4
ROPE_BASE = 10000.0
GRID_W = 64
HY_BANDS = 16
HY_TARGET = 1e-2
HY_MAX_DECAY = math.log(HY_TARGET) / 0.3
HY_MIN_DECAY = math.log(HY_TARGET) / 1.5
GLA_HEADS = 4
GLA_RANK = 16
GLA_GATE_NORM = 16.0
GLA_CHUNK = 64
N_EXPERTS = 16
N_GROUPS = 4
EPG = N_EXPERTS // N_GROUPS
MOE_TILE = 160
NT = (((1,), (1,)), ((), ()))
TN = (((0,), (0,)), ((), ()))


def _params(sem):
    return pltpu.CompilerParams(dimension_semantics=sem, vmem_limit_bytes=VMEM_LIMIT)


def _const_spec(shape):
    nd = len(shape)
    return pl.BlockSpec(shape, lambda *_: (0,) * nd)


def _silu(x):
    return x * (1.0 / (1.0 + jnp.exp(-x)))


def _norm_mod(x, nw, sc, sh):
    ms = jnp.mean(x * x, axis=-1, keepdims=True)
    return x * lax.rsqrt(ms + EPS) * nw * (1.0 + sc) + sh


def _mod_kernel(c_ref, w_ref, b_ref, o_ref):
    s = _silu(c_ref[...])
    o_ref[...] = jnp.dot(s, w_ref[...], precision=HI, preferred_element_type=F32) + b_ref[...]


def _modulation(cond, w_mod, b_mod):
    depth, d, d6 = w_mod.shape
    r = cond.shape[0]
    tn = 1536
    return pl.pallas_call(
        _mod_kernel,
        grid=(depth, d6 // tn),
        in_specs=[
            pl.BlockSpec((r, d), lambda l, j: (0, 0)),
            pl.BlockSpec((None, d, tn), lambda l, j: (l, 0, j)),
            pl.BlockSpec((None, 1, tn), lambda l, j: (l, 0, j)),
        ],
        out_specs=pl.BlockSpec((None, r, tn), lambda l, j: (l, 0, j)),
        out_shape=jax.ShapeDtypeStruct((depth, r, d6), F32),
        compiler_params=_params(("parallel", "parallel")),
        name="modulation",
    )(cond, w_mod, b_mod.reshape(depth, 1, d6))


def _proj0_kernel(x_ref, sc_ref, sh_ref, nw_ref, wtok_ref, wchan_ref, qkw_ref, ones_ref, cos_ref, sin_ref,
                  chan_ref, q_ref, k_ref, *cache_refs, rope):
    h = _norm_mod(x_ref[...], nw_ref[...], sc_ref[...], sh_ref[...]).astype(BF16)
    chan_ref[...] = lax.dot_general(wchan_ref[...], h, NT, preferred_element_type=F32).astype(chan_ref.dtype)
    p = jnp.dot(h, wtok_ref[...], preferred_element_type=F32)
    w = DA_HEADS * DA_VD
    ones = ones_ref[...]

    def segnorm(z, gain):
        z2 = z * z
        hi = z2.astype(BF16)
        lo = (z2 - hi.astype(F32)).astype(BF16)
        ss = jnp.dot(hi, ones, preferred_element_type=F32) + jnp.dot(lo, ones, preferred_element_type=F32)
        return z * lax.rsqrt(ss * (1.0 / DA_DH) + EPS) * gain

    q = segnorm(p[:, :w], qkw_ref[0:1, :])
    k = segnorm(p[:, w:2 * w], qkw_ref[1:2, :])
    if cache_refs:
        kn_ref, vn_ref = cache_refs
        v = p[:, 2 * w:]
        for hh in range(DA_HEADS):
            kn_ref[hh] = k[:, hh * DA_VD:(hh + 1) * DA_VD]
            vn_ref[hh] = v[:, hh * DA_VD:(hh + 1) * DA_VD]
    if rope:
        cos = jnp.concatenate([cos_ref[...]] * (w // LANES), axis=1)
        sin = jnp.concatenate([sin_ref[...]] * (w // LANES), axis=1)
        lane = lax.broadcasted_iota(jnp.int32, q.shape, 1)
        first = (lane % (2 * ROPE_NF)) < ROPE_NF

        def rot(z):
            partner = jnp.where(first, pltpu.roll(z, w - ROPE_NF, 1), pltpu.roll(z, ROPE_NF, 1))
            return z * cos + partner * sin

        q = rot(q)
        k = rot(k)
    q_ref[...] = (q * (DA_DH ** -0.5 * math.log2(math.e))).astype(q_ref.dtype)
    k_ref[...] = k.astype(k_ref.dtype)


def _rope_tables(seq):
    t = np.arange(seq)
    lane = np.arange(LANES)
    d = lane % DA_DH
    axis = d // (2 * ROPE_NF)
    part = (d % (2 * ROPE_NF)) // ROPE_NF
    f = d % ROPE_NF
    pos = jnp.where(axis[None, :] == 0, (t // GRID_W)[:, None], (t % GRID_W)[:, None]).astype(F32)
    inv = ROPE_BASE ** (-jnp.arange(ROPE_NF, dtype=F32) / ROPE_NF)
    ang = pos * inv[f][None, :]
    sign = jnp.asarray(np.where(part == 0, -1.0, 1.0), F32)[None, :]
    return jnp.cos(ang), jnp.sin(ang) * sign


def _proj0(x, sc, sh, nw, wqkv, whyt, qkw, ones, batch, seq, row_fn, rope, emit_cache):
    t, d = x.shape
    tm = min(seq, 512)
    per_seq = seq // tm
    w = DA_HEADS * DA_VD
    hy_in = whyt.shape[0]
    if not emit_cache:
        wqkv = wqkv[:, :2 * w]
    if rope:
        cos, sin = _rope_tables(seq)
    else:
        cos = jnp.ones((seq, LANES), F32)
        sin = jnp.zeros((seq, LANES), F32)
    mod_spec = pl.BlockSpec((None, 1, d), lambda i: (row_fn(i // per_seq), 0, 0))
    tab_spec = pl.BlockSpec((tm, LANES), lambda i: (i % per_seq, 0))
    tok = lambda n: pl.BlockSpec((tm, n), lambda i: (i, 0))
    out_specs = [pl.BlockSpec((None, hy_in, tm), lambda i: (i // per_seq, 0, i % per_seq)), tok(w), tok(w)]
    out_shape = [jax.ShapeDtypeStruct((batch, hy_in, seq), BF16)] + [jax.ShapeDtypeStruct((t, w), BF16)] * 2
    if emit_cache:
        cspec = pl.BlockSpec((None, DA_HEADS, tm, DA_VD), lambda i: (i // per_seq, 0, i % per_seq, 0))
        out_specs += [cspec, cspec]
        out_shape += [jax.ShapeDtypeStruct((batch, DA_HEADS, seq, DA_VD), F32)] * 2
    return pl.pallas_call(
        functools.partial(_proj0_kernel, rope=rope),
        grid=(t // tm,),
        in_specs=[tok(d), mod_spec, mod_spec, _const_spec((1, d)), _const_spec(wqkv.shape), _const_spec(whyt.shape),
                  _const_spec(qkw.shape), _const_spec(ones.shape), tab_spec, tab_spec],
        out_specs=out_specs,
        out_shape=out_shape,
        compiler_params=_params(("parallel",)),
        name="proj0",
    )(x, sc, sh, nw, wqkv, whyt, qkw, ones, cos, sin)


def _c_add(a, b, sign=1.0):
    if b is None:
        return a
    if a is None:
        return b if sign > 0 else tuple(None if p is None else -p for p in b)
    out = []
    for pa, pb in zip(a, b):
        if pb is None:
            out.append(pa)
        elif pa is None:
            out.append(pb if sign > 0 else -pb)
        else:
            out.append(pa + pb if sign > 0 else pa - pb)
    return tuple(out)


def _c_mul_const(w, a):
    if a is None:
        return None
    wr = 0.0 if abs(w.real) < 1e-12 else w.real
    wi = 0.0 if abs(w.imag) < 1e-12 else w.imag
    ar, ai = a

    def scaled(c, p):
        if p is None or c == 0.0:
            return None
        if c == 1.0:
            return p
        if c == -1.0:
            return -p
        return c * p

    re = _c_add((scaled(wr, ar),), (scaled(wi, ai),), -1.0)[0]
    im = _c_add((scaled(wr, ai),), (scaled(wi, ar),), 1.0)[0]
    return (re, im)


def _fft_list(vals, sign, first_half_only=False):
    n = len(vals)
    if n == 1:
        return list(vals)
    ev = _fft_list(vals[0::2], sign)
    od = _fft_list(vals[1::2], sign)
    out = [None] * n
    for k in range(n // 2):
        tw = _c_mul_const(cmath.exp(sign * 2j * math.pi * k / n), od[k])
        out[k] = _c_add(ev[k], tw, 1.0)
        if not first_half_only:
            out[k + n // 2] = _c_add(ev[k], tw, -1.0)
    return out


def _dft_mats(n2, total):
    k = np.arange(n2)
    ang = 2.0 * np.pi * np.outer(k, k) / n2
    fr, fi = np.cos(ang), -np.sin(ang)
    fwd = np.block([[fr, fi], [-fi, fr]])
    inv = np.block([[fr, -fi], [fi, fr]]) / total
    return jnp.asarray(fwd, BF16), jnp.asarray(inv, BF16)


def _twiddles(n1, n2):
    ang = 2.0 * np.pi * np.outer(np.arange(n1), np.arange(n2)) / (n1 * n2)
    return jnp.asarray(np.cos(ang), F32), jnp.asarray(-np.sin(ang), F32)


def _fft_split(seq):
    n = 2 * seq
    n2 = min(512, n // 2)
    return n // n2, n2


def _across_fwd(load, n_in, z_ref, twr_ref, twi_ref, n1, n2, rows):
    per_row = n2 // LANES

    def body(it, carry):
        r0 = pl.multiple_of((it // per_row) * SUBLANES, SUBLANES)
        l0 = pl.multiple_of((it % per_row) * LANES, LANES)
        vals = [load(j, r0, l0) for j in range(n_in)] + [None] * (n1 - n_in)
        outs = _fft_list(vals, -1.0)
        for k1 in range(n1):
            twr = twr_ref[pl.ds(k1, 1), pl.ds(l0, LANES)]
            twi = twi_ref[pl.ds(k1, 1), pl.ds(l0, LANES)]
            re, im = outs[k1]
            zero = jnp.zeros((SUBLANES, LANES), F32)
            re = zero if re is None else re
            im = zero if im is None else im
            z_ref[k1, pl.ds(r0, SUBLANES), pl.ds(l0, LANES)] = re * twr - im * twi
            z_ref[k1, pl.ds(r0, SUBLANES), pl.ds(n2 + l0, LANES)] = re * twi + im * twr
        return carry

    lax.fori_loop(0, (rows // SUBLANES) * per_row, body, 0)


def _hy_hidden_kernel(w1_ref, b1_ref, freq_ref, w2_ref, b2_ref, bands_ref, o_ref, *, seq):
    n = 2 * seq
    k = lax.broadcasted_iota(jnp.int32, (n, 1), 0)
    pos = jnp.where(k < seq, k, n - k).astype(F32)
    t = pos / (seq - 1)
    ang = 2.0 * math.pi * pos / seq * bands_ref[...]
    z = (t * w1_ref[0:1, :]
         + jnp.dot(jnp.cos(ang), w1_ref[1:1 + HY_BANDS, :], precision=HI, preferred_element_type=F32)
         + jnp.dot(-jnp.sin(ang), w1_ref[1 + HY_BANDS:, :], precision=HI, preferred_element_type=F32)
         + b1_ref[...])
    hid = jnp.sin(freq_ref[0:1, :] * z)
    hid = jnp.sin(freq_ref[1:2, :] * (jnp.dot(hid, w2_ref[...], precision=HI, preferred_element_type=F32) + b2_ref[...]))
    o_ref[...] = hid


def _hy_hidden(seq, f_w1, f_b1, f_freq, f_w2, f_b2):
    ffn = f_w1.shape[1]
    bands = jnp.linspace(1e-4, HY_BANDS - 1, HY_BANDS, dtype=F32).reshape(1, HY_BANDS)
    args = (f_w1, f_b1.reshape(1, ffn), f_freq, f_w2, f_b2.reshape(1, ffn), bands)
    return pl.pallas_call(
        functools.partial(_hy_hidden_kernel, seq=seq),
        grid=(1,),
        in_specs=[_const_spec(a.shape) for a in args],
        out_specs=_const_spec((2 * seq, ffn)),
        out_shape=jax.ShapeDtypeStruct((2 * seq, ffn), F32),
        compiler_params=_params(("arbitrary",)),
        name="hyena_filter_hidden",
    )(*args)


def _hy_spec_kernel(hid_ref, wb_ref, wa_ref, dec_ref, twr_ref, twi_ref, gf_ref, o_ref, taps_ref, z_ref, *, seq, n1, n2):
    ct = wb_ref.shape[0]
    hid = hid_ref[...]
    back = lax.dot_general(wb_ref[...], hid[:seq], NT, precision=HI, preferred_element_type=F32)
    ahead = lax.dot_general(wa_ref[...], hid[seq:], NT, precision=HI, preferred_element_type=F32)
    lane = lax.broadcasted_iota(jnp.int32, (1, seq), 1)
    dec = dec_ref[...]
    t_back = lane.astype(F32) / (seq - 1)
    t_ahead = (seq - lane).astype(F32) / (seq - 1)
    back = back * jnp.exp(-t_back * dec)
    ahead = jnp.where(lane == 0, 0.0, ahead * jnp.exp(-t_ahead * dec))
    norm = (jnp.sum(jnp.abs(back), axis=1, keepdims=True) + jnp.sum(jnp.abs(ahead), axis=1, keepdims=True)) + EPS
    taps_ref[:, :seq] = back / norm
    taps_ref[:, seq:] = ahead / norm

    def load(j, r0, l0):
        return (taps_ref[pl.ds(r0, SUBLANES), pl.ds(j * n2 + l0, LANES)], None)

    _across_fwd(load, n1, z_ref, twr_ref, twi_ref, n1, n2, ct)
    for k1 in range(n1):
        o_ref[k1] = jnp.dot(z_ref[k1].astype(BF16), gf_ref[...], preferred_element_type=F32)


def _hy_filter_spectra(hid, f_w3, seq, width, ct):
    n1, n2 = _fft_split(seq)
    ffn = f_w3.shape[0]
    w3t = f_w3.T.reshape(4, width, ffn)
    decay = jnp.abs(jnp.linspace(HY_MIN_DECAY, HY_MAX_DECAY, width, dtype=F32)).reshape(width, 1)
    twr, twi = _twiddles(n1, n2)
    gf, _ = _dft_mats(n2, 2 * seq)
    return pl.pallas_call(
        functools.partial(_hy_spec_kernel, seq=seq, n1=n1, n2=n2),
        grid=(2, width // ct),
        in_specs=[
            _const_spec(hid.shape),
            pl.BlockSpec((None, ct, ffn), lambda i, c: (2 * i, c, 0)),
            pl.BlockSpec((None, ct, ffn), lambda i, c: (2 * i + 1, c, 0)),
            pl.BlockSpec((ct, 1), lambda i, c: (c, 0)),
            _const_spec(twr.shape), _const_spec(twi.shape), _const_spec(gf.shape),
        ],
        out_specs=pl.BlockSpec((None, n1, ct, 2 * n2), lambda i, c: (i, 0, c, 0)),
        out_shape=jax.ShapeDtypeStruct((2, n1, width, 2 * n2), F32),
        scratch_shapes=[pltpu.VMEM((ct, 2 * seq), F32), pltpu.VMEM((n1, ct, 2 * n2), F32)],
        compiler_params=_params(("parallel", "parallel")),
        name="hyena_filter_spectra",
    )(hid, w3t, w3t, decay, twr, twi, gf)


def _short_conv_rows(ref, cw, r0, nrows, seq):
    x = ref[pl.ds(r0, nrows), :].astype(F32)
    lane = lax.broadcasted_iota(jnp.int32, x.shape, 1)
    left = jnp.where(lane == 0, 0.0, pltpu.roll(x, 1, 1))
    right = jnp.where(lane == seq - 1, 0.0, pltpu.roll(x, seq - 1, 1))
    return left * cw[:, 0:1] + x * cw[:, 1:2] + right * cw[:, 2:3] + cw[:, 3:4]


def _hy_conv_kernel(uin_ref, gin_ref, cwu_ref, cwg_ref, d_ref, h_ref, twr_ref, twi_ref, gf_ref, gi_ref,
                    o_ref, u_ref, z_ref, *, seq, n1, n2, conv_u):
    ct = uin_ref.shape[1]
    rows16 = 2 * SUBLANES

    def prep(i, carry):
        r0 = pl.multiple_of(i * rows16, rows16)
        cwg = cwg_ref[pl.ds(r0, rows16), :]
        cwu = cwu_ref[pl.ds(r0, rows16), :]
        for e in range(2):
            o_ref[e, pl.ds(r0, rows16), :] = _short_conv_rows(gin_ref.at[e], cwg, r0, rows16, seq)
            if conv_u:
                u_ref[e, pl.ds(r0, rows16), :] = _short_conv_rows(uin_ref.at[e], cwu, r0, rows16, seq)
            else:
                u_ref[e, pl.ds(r0, rows16), :] = uin_ref[e, pl.ds(r0, rows16), :].astype(F32)
        return carry

    lax.fori_loop(0, ct // rows16, prep, 0)

    def load(j, r0, l0):
        return (u_ref[0, pl.ds(r0, SUBLANES), pl.ds(j * n2 + l0, LANES)],
                u_ref[1, pl.ds(r0, SUBLANES), pl.ds(j * n2 + l0, LANES)])

    _across_fwd(load, n1 // 2, z_ref, twr_ref, twi_ref, n1, n2, ct)

    def within(k1, carry):
        y = jnp.dot(z_ref[k1].astype(BF16), gf_ref[...], preferred_element_type=F32)
        hh = h_ref[k1]
        yr, yi = y[:, :n2], y[:, n2:]
        hr, hi = hh[:, :n2], hh[:, n2:]
        p = jnp.concatenate([yr * hr - yi * hi, yr * hi + yi * hr], axis=1)
        z_ref[k1] = jnp.dot(p.astype(BF16), gi_ref[...], preferred_element_type=F32)
        return carry

    lax.fori_loop(0, n1, within, 0)

    per_row = n2 // LANES

    def finish(it, carry):
        r0 = pl.multiple_of((it // per_row) * SUBLANES, SUBLANES)
        l0 = pl.multiple_of((it % per_row) * LANES, LANES)
        vals = []
        for k1 in range(n1):
            twr = twr_ref[pl.ds(k1, 1), pl.ds(l0, LANES)]
            twi = twi_ref[pl.ds(k1, 1), pl.ds(l0, LANES)]
            re = z_ref[k1, pl.ds(r0, SUBLANES), pl.ds(l0, LANES)]
            im = z_ref[k1, pl.ds(r0, SUBLANES), pl.ds(n2 + l0, LANES)]
            vals.append((re * twr + im * twi, im * twr - re * twi))
        outs = _fft_list(vals, 1.0, first_half_only=True)
        d = d_ref[pl.ds(r0, SUBLANES), :]
        for j in range(n1 // 2):
            sl = (pl.ds(r0, SUBLANES), pl.ds(j * n2 + l0, LANES))
            for e in range(2):
                o_ref[(e,) + sl] = o_ref[(e,) + sl] * (outs[j][e] + u_ref[(e,) + sl] * d)
        return carry

    lax.fori_loop(0, (ct // SUBLANES) * per_row, finish, 0)


def _hy_conv(u, u_off, g, g_off, cw, spectra, conv_idx, d_skip, seq, width, ct, conv_u):
    batch = u.shape[0]
    n1, n2 = _fft_split(seq)
    twr, twi = _twiddles(n1, n2)
    gf, gi = _dft_mats(n2, 2 * seq)
    uo, go = u_off // ct, g_off // ct
    return pl.pallas_call(
        functools.partial(_hy_conv_kernel, seq=seq, n1=n1, n2=n2, conv_u=conv_u),
        grid=(width // ct, batch // 2),
        in_specs=[
            pl.BlockSpec((2, ct, seq), lambda c, p: (p, uo + c, 0)),
            pl.BlockSpec((2, ct, seq), lambda c, p: (p, go + c, 0)),
            pl.BlockSpec((ct, 4), lambda c, p: (uo + c, 0)),
            pl.BlockSpec((ct, 4), lambda c, p: (go + c, 0)),
            pl.BlockSpec((ct, 1), lambda c, p: (c, 0)),
            pl.BlockSpec((None, n1, ct, 2 * n2), lambda c, p: (conv_idx, 0, c, 0), pipeline_mode=pl.Buffered(1)),
            _const_spec(twr.shape), _const_spec(twi.shape), _const_spec(gf.shape), _const_spec(gi.shape),
        ],
        out_specs=pl.BlockSpec((2, ct, seq), lambda c, p: (p, c, 0)),
        out_shape=jax.ShapeDtypeStruct((batch, width, seq), F32),
        scratch_shapes=[pltpu.VMEM((2, ct, seq), F32), pltpu.VMEM((n1, ct, 2 * n2), F32)],
        compiler_params=_params(("arbitrary", "arbitrary")),
        name="hyena_conv",
    )(u, g, cw, cw, d_skip, spectra, twr, twi, gf, gi)


def _hyena(hyt, conv_w, conv_b, filt, hy_skip, seq, width):
    f_w1, f_b1, f_freq, f_w2, f_b2, f_w3 = filt
    ct = LANES
    hid = _hy_hidden(seq, f_w1, f_b1, f_freq, f_w2, f_b2)
    spectra = _hy_filter_spectra(hid, f_w3, seq, width, ct)
    cw = jnp.concatenate([conv_w.T, conv_b[:, None]], axis=1)
    y1 = _hy_conv(hyt, 0, hyt, width, cw, spectra, 0, hy_skip[0].reshape(width, 1), seq, width, ct, True)
    return _hy_conv(y1, 0, hyt, 2 * width, cw, spectra, 1, hy_skip[1].reshape(width, 1), seq, width, ct, False)


def _attn_kernel(q_ref, k_ref, vt_ref, *rest, lam_init, with_cache, kc):
    if with_cache:
        kc_ref, vct_ref, lam_ref, sw_ref, o_ref = rest
    else:
        lam_ref, sw_ref, o_ref = rest
    lp = lam_ref[...]
    lam = (jnp.exp(jnp.sum(lp[0:1] * lp[1:2], axis=1, keepdims=True))
           - jnp.exp(jnp.sum(lp[2:3] * lp[3:4], axis=1, keepdims=True)) + lam_init)
    q = q_ref[...]
    lane = lax.broadcasted_iota(jnp.int32, q.shape, 1)
    zero = jnp.zeros_like(q)
    qs = (jnp.where(lane < DA_DH, q, zero), jnp.where(lane >= DA_DH, q, zero))
    seq = k_ref.shape[0]
    chunks = [(k_ref[c * kc:(c + 1) * kc, :], vt_ref[:, c * kc:(c + 1) * kc]) for c in range(seq // kc)]
    if with_cache:
        chunks.append((kc_ref[...].astype(BF16), vct_ref[...].astype(BF16)))
    state = [None, None]

    def scores(ci):
        return [lax.dot_general(chunks[ci][0], qm, NT, preferred_element_type=F32) for qm in qs]

    s_next = scores(0)
    for ci, (kk, vv) in enumerate(chunks):
        s_cur = s_next
        if ci + 1 < len(chunks):
            s_next = scores(ci + 1)
        for mi in range(2):
            s = s_cur[mi]
            cmax = jnp.max(s, axis=0, keepdims=True)
            if state[mi] is None:
                m = cmax
                p = jnp.exp2(s - m)
                l = jnp.sum(p, axis=0, keepdims=True)
                acc = jnp.dot(vv, p.astype(BF16), preferred_element_type=F32)
            else:
                m_old, l, acc = state[mi]
                m = jnp.maximum(m_old, cmax)
                alpha = jnp.exp2(m_old - m)
                p = jnp.exp2(s - m)
                l = l * alpha + jnp.sum(p, axis=0, keepdims=True)
                acc = acc * alpha + jnp.dot(vv, p.astype(BF16), preferred_element_type=F32)
            state[mi] = (m, l, acc)
    (_, l1, acc1), (_, l2, acc2) = state
    o = acc1 * (1.0 / l1) - acc2 * (lam / l2)
    ms = jnp.mean(o * o, axis=0, keepdims=True)
    o_ref[...] = (o * lax.rsqrt(ms + EPS) * sw_ref[...] * (1.0 - lam_init)).astype(o_ref.dtype)


def _diff_attention(q, k, chan, v_row0, cache, lam_p, subln_w, batch, seq, lam_init):
    tq = min(seq, 256)
    kc = min(seq, 512)
    nq = seq // tq
    vb = v_row0 // DA_VD
    grid = (batch, DA_HEADS, nq)
    in_specs = [pl.BlockSpec((tq, DA_VD), lambda b, h, i: (b * nq + i, h)),
                pl.BlockSpec((seq, DA_VD), lambda b, h, i: (b, h)),
                pl.BlockSpec((None, DA_VD, seq), lambda b, h, i: (b, vb + h, 0))]
    args = [q, k, chan]
    if cache is not None:
        past = cache[0].shape[2]
        in_specs += [pl.BlockSpec((None, None, past, DA_VD), lambda b, h, i: (b, h, 0, 0)),
                     pl.BlockSpec((None, None, DA_VD, past), lambda b, h, i: (b, h, 0, 0))]
        args += [cache[0], jnp.swapaxes(cache[1], 2, 3)]
    in_specs += [_const_spec(lam_p.shape), _const_spec((DA_VD, 1))]
    args += [lam_p, subln_w.reshape(DA_VD, 1)]
    return pl.pallas_call(
        functools.partial(_attn_kernel, lam_init=lam_init, with_cache=cache is not None, kc=kc),
        grid=grid,
        in_specs=in_specs,
        out_specs=pl.BlockSpec((None, DA_VD, tq), lambda b, h, i: (b, h, i)),
        out_shape=jax.ShapeDtypeStruct((batch, DA_HEADS * DA_VD, seq), BF16),
        compiler_params=_params(("parallel", "parallel", "parallel")),
        name="diff_attention",
    )(*args)


def _outproj0_kernel(x_ref, yt_ref, ot_ref, g_ref, wy_ref, wo_ref, out_ref):
    mix = lax.dot_general(yt_ref[...].astype(BF16), wy_ref[...], TN, preferred_element_type=F32)
    mix = mix + lax.dot_general(ot_ref[...], wo_ref[...], TN, preferred_element_type=F32)
    out_ref[...] = x_ref[...] + g_ref[...] * mix


def _outproj0(x, yt, ot, g, w_out, seq, row_fn):
    t, d = x.shape
    width = yt.shape[1]
    tm = min(seq, 512)
    per_seq = seq // tm
    wy = w_out[:width].astype(BF16)
    wo = w_out[width:].astype(BF16)
    tok = pl.BlockSpec((tm, d), lambda i: (i, 0))
    chan = lambda n: pl.BlockSpec((None, n, tm), lambda i: (i // per_seq, 0, i % per_seq))
    return pl.pallas_call(
        _outproj0_kernel,
        grid=(t // tm,),
        in_specs=[tok, chan(width), chan(ot.shape[1]),
                  pl.BlockSpec((None, 1, d), lambda i: (row_fn(i // per_seq), 0, 0)),
                  _const_spec(wy.shape), _const_spec(wo.shape)],
        out_specs=tok,
        out_shape=jax.ShapeDtypeStruct((t, d), F32),
        compiler_params=_params(("parallel",)),
        name="outproj0",
    )(x, yt, ot, g, wy, wo)


def _top2_of_rows(rows):
    n = len(rows)
    v1 = functools.reduce(jnp.maximum, rows)
    i1 = jnp.full(rows[0].shape, n - 1, jnp.int32)
    for j in range(n - 2, -1, -1):
        i1 = jnp.where(rows[j] == v1, j, i1)
    masked = [jnp.where(i1 == j, -jnp.inf, rows[j]) for j in range(n)]
    v2 = functools.reduce(jnp.maximum, masked)
    i2 = jnp.full(rows[0].shape, n - 1, jnp.int32)
    for j in range(n - 2, -1, -1):
        i2 = jnp.where(masked[j] == v2, j, i2)
    return v1, i1, v2, i2


def _select_rows(idx, rows):
    out = rows[-1]
    for j in range(len(rows) - 2, -1, -1):
        out = jnp.where(idx == j, rows[j], out)
    return out


def _moe_kernel(x_ref, sc_ref, sh_ref, g_ref, nw_ref, rwt_ref, rb_ref, upper_ref, wg_ref, wu_ref, wd_ref,
                out_ref, hb_ref, rank_ref, comb_ref, acc_ref, cnt_ref):
    e = pl.program_id(1)
    tb = x_ref.shape[0]

    @pl.when(e == 0)
    def _route():
        h = _norm_mod(x_ref[...], nw_ref[...], sc_ref[...], sh_ref[...])
        hb_ref[...] = h.astype(BF16)
        logits = lax.dot_general(rwt_ref[...], h, NT, precision=HI, preferred_element_type=F32)
        s = 1.0 / (1.0 + jnp.exp(-logits))
        s_sel = s + rb_ref[...]
        s_rows = [s[j:j + 1, :] for j in range(N_EXPERTS)]
        sel_rows = [s_sel[j:j + 1, :] for j in range(N_EXPERTS)]
        tops = [_top2_of_rows(sel_rows[g * EPG:(g + 1) * EPG]) for g in range(N_GROUPS)]
        scores = [tp[0] + tp[2] for tp in tops]
        best = functools.reduce(jnp.maximum, scores)
        g_idx = jnp.full(best.shape, N_GROUPS - 1, jnp.int32)
        for g in range(N_GROUPS - 2, -1, -1):
            g_idx = jnp.where(scores[g] == best, g, g_idx)
        e1 = g_idx * EPG + _select_rows(g_idx, [tp[1] for tp in tops])
        e2 = g_idx * EPG + _select_rows(g_idx, [tp[3] for tp in tops])
        w1 = _select_rows(e1, s_rows)
        w2 = _select_rows(e2, s_rows)
        tot = w1 + w2
        w1, w2 = w1 / tot, w2 / tot
        masks = [(e1 == j) | (e2 == j) for j in range(N_EXPERTS)]
        comb_ref[...] = jnp.concatenate(
            [jnp.where(e1 == j, w1, 0.0) + jnp.where(e2 == j, w2, 0.0) for j in range(N_EXPERTS)], axis=0)
        mask = jnp.concatenate([jnp.where(m, 1.0, 0.0) for m in masks], axis=0)
        rank = jnp.dot(mask.astype(BF16), upper_ref[...], preferred_element_type=F32)
        rank_ref[...] = jnp.where(mask > 0.0, rank, -1.0)
        for j in range(N_EXPERTS):
            cnt_ref[j] = jnp.sum(mask[j:j + 1, :]).astype(jnp.int32)
        acc_ref[...] = jnp.zeros_like(acc_ref)

    rank = rank_ref[pl.ds(e, 1), :]
    comb = comb_ref[pl.ds(e, 1), :]
    n_tiles = (cnt_ref[e] + MOE_TILE - 1) // MOE_TILE

    def tile(j, carry):
        slot = (lax.broadcasted_iota(jnp.int32, (MOE_TILE, tb), 0) + j * MOE_TILE).astype(F32)
        hit = rank == slot
        onehot = jnp.where(hit, 1.0, 0.0).astype(BF16)
        xe = jnp.dot(onehot, hb_ref[...], preferred_element_type=F32).astype(BF16)
        gate = jnp.dot(xe, wg_ref[...], preferred_element_type=F32)
        up = jnp.dot(xe, wu_ref[...], preferred_element_type=F32)
        y = jnp.dot((_silu(gate) * up).astype(BF16), wd_ref[...], preferred_element_type=F32)
        w_slot = jnp.sum(jnp.where(hit, comb, 0.0), axis=1, keepdims=True)
        acc_ref[...] += lax.dot_general(onehot, (y * w_slot).astype(BF16), TN, preferred_element_type=F32)
        return carry

    lax.fori_loop(0, n_tiles, tile, 0)

    @pl.when(e == N_EXPERTS - 1)
    def _finish():
        out_ref[...] = x_ref[...] + g_ref[...] * acc_ref[...]


def _moe(x, sc, sh, g, nw, router_w, router_b, w_gate, w_up, w_down, seq, row_fn):
    t, d = x.shape
    tb = min(seq, 1024)
    per_seq = seq // tb
    dff = w_gate.shape[-1]
    upper = jnp.asarray(np.triu(np.ones((tb, tb), np.float32), 1), BF16)
    mod_spec = pl.BlockSpec((None, 1, d), lambda i, e: (row_fn(i // per_seq), 0, 0))
    tok = pl.BlockSpec((tb, d), lambda i, e: (i, 0))
    return pl.pallas_call(
        _moe_kernel,
        grid=(t // tb, N_EXPERTS),
        in_specs=[tok, mod_spec, mod_spec, mod_spec, _const_spec((1, d)), _const_spec((N_EXPERTS, d)),
                  _const_spec((N_EXPERTS, 1)), _const_spec(upper.shape),
                  pl.BlockSpec((None, d, dff), lambda i, e: (e, 0, 0)),
                  pl.BlockSpec((None, d, dff), lambda i, e: (e, 0, 0)),
                  pl.BlockSpec((None, dff, d), lambda i, e: (e, 0, 0))],
        out_specs=tok,
        out_shape=jax.ShapeDtypeStruct((t, d), F32),
        scratch_shapes=[pltpu.VMEM((tb, d), BF16), pltpu.VMEM((N_EXPERTS, tb), F32), pltpu.VMEM((N_EXPERTS, tb), F32),
                        pltpu.VMEM((tb, d), F32), pltpu.SMEM((N_EXPERTS,), jnp.int32)],
        compiler_params=_params(("parallel", "arbitrary")),
        name="moe",
    )(x, sc, sh, g, nw, router_w.T, router_b.reshape(N_EXPERTS, 1), upper, w_gate, w_up, w_down)


def _proj1_kernel(x_ref, sc_ref, sh_ref, nw_ref, w_ref, wlr_ref, p_ref, lr_ref):
    h = _norm_mod(x_ref[...], nw_ref[...], sc_ref[...], sh_ref[...]).astype(BF16)
    p_ref[...] = jnp.dot(h, w_ref[...], preferred_element_type=F32).astype(p_ref.dtype)
    lr_ref[...] = jnp.dot(h, wlr_ref[...], preferred_element_type=F32)


def _proj1(x, sc, sh, nw, w_main, w_lr, seq, row_fn):
    t, d = x.shape
    tm = min(seq, 512)
    per_seq = seq // tm
    mod_spec = pl.BlockSpec((None, 1, d), lambda i: (row_fn(i // per_seq), 0, 0))
    tok = lambda n: pl.BlockSpec((tm, n), lambda i: (i, 0))
    return pl.pallas_call(
        _proj1_kernel,
        grid=(t // tm,),
        in_specs=[tok(d), mod_spec, mod_spec, _const_spec((1, d)), _const_spec(w_main.shape), _const_spec(w_lr.shape)],
        out_specs=[tok(w_main.shape[1]), tok(LANES)],
        out_shape=[jax.ShapeDtypeStruct((t, w_main.shape[1]), BF16), jax.ShapeDtypeStruct((t, LANES), F32)],
        compiler_params=_params(("parallel",)),
        name="proj1",
    )(x, sc, sh, nw, w_main, w_lr)


def _log_sigmoid(x):
    return jnp.minimum(x, 0.0) - jnp.log(1.0 + jnp.exp(-jnp.abs(x)))


def _gla_chunk(q, k, v, b, state_t, keep, reverse):
    b_end = b[0:1, :] if reverse else b[GLA_CHUNK - 1:GLA_CHUNK, :]
    q_dec = (q * jnp.exp(b)).astype(BF16)
    k_dec = k * jnp.exp(-b)
    att = lax.dot_general(q_dec, k_dec.astype(BF16), NT, preferred_element_type=F32)
    att = jnp.where(keep, att, 0.0)
    o = jnp.dot(att.astype(BF16), v, preferred_element_type=F32)
    o = o + lax.dot_general(q_dec, state_t.astype(BF16), NT, preferred_element_type=F32)
    e_end = jnp.exp(b_end)
    k_rem = (k_dec * e_end).astype(BF16)
    state_t = state_t * e_end + lax.dot_general(v, k_rem, TN, preferred_element_type=F32)
    return o, state_t


def _gla_kernel(qf_ref, kf_ref, vf_ref, lf_ref, qb_ref, kb_ref, vb_ref, lb_ref, wgf_ref, wgb_ref, bg_ref,
                trif_ref, trib_ref, s0f_ref, s0b_ref, of_ref, ob_ref, sf_ref, sb_ref, stf_ref, stb_ref, *, dk, dv):
    i = pl.program_id(1)
    nb = pl.num_programs(1)
    tc = qf_ref.shape[0]

    @pl.when(i == 0)
    def _init():
        stf_ref[...] = s0f_ref[...].astype(F32)
        stb_ref[...] = s0b_ref[...].astype(F32)

    r = lax.broadcasted_iota(jnp.int32, (GLA_CHUNK, GLA_CHUNK), 0)
    c = lax.broadcasted_iota(jnp.int32, (GLA_CHUNK, GLA_CHUNK), 1)
    scale = dk ** -0.5

    def decay_sums(l_ref, wg_ref, bias, tri_ref):
        gate = _log_sigmoid(jnp.dot(l_ref[...], wg_ref[...], precision=HI, preferred_element_type=F32)
                            + bias) / GLA_GATE_NORM
        total = None
        rest = gate
        for _ in range(3):
            part = rest.astype(BF16)
            rest = rest - part.astype(F32)
            term = jnp.dot(tri_ref[...], part, preferred_element_type=F32)
            total = term if total is None else total + term
        return total

    b_f = decay_sums(lf_ref, wgf_ref, bg_ref[0:1, :], trif_ref)
    b_b = decay_sums(lb_ref, wgb_ref, bg_ref[1:2, :], trib_ref)
    n_chunks = tc // GLA_CHUNK
    states = [[stf_ref[h], stb_ref[h]] for h in range(GLA_HEADS)]
    for step in range(n_chunks):
        for h in range(GLA_HEADS):
            ks, vs = slice(h * dk, (h + 1) * dk), slice(h * dv, (h + 1) * dv)
            rs = slice(step * GLA_CHUNK, (step + 1) * GLA_CHUNK)
            o, states[h][0] = _gla_chunk(qf_ref[rs, ks].astype(F32) * scale, kf_ref[rs, ks].astype(F32),
                                         vf_ref[rs, vs], b_f[rs, ks], states[h][0], c <= r, False)
            of_ref[rs, vs] = o.astype(of_ref.dtype)
            ci = n_chunks - 1 - step
            rs = slice(ci * GLA_CHUNK, (ci + 1) * GLA_CHUNK)
            o, states[h][1] = _gla_chunk(qb_ref[rs, ks].astype(F32) * scale, kb_ref[rs, ks].astype(F32),
                                         vb_ref[rs, vs], b_b[rs, ks], states[h][1], c >= r, True)
            ob_ref[rs, vs] = o.astype(ob_ref.dtype)
    for h in range(GLA_HEADS):
        stf_ref[h] = states[h][0]
        stb_ref[h] = states[h][1]

    @pl.when(i == nb - 1)
    def _emit():
        sf_ref[...] = stf_ref[...]
        sb_ref[...] = stb_ref[...]


def _gla(p, lr, wgf, wgb, bg, s0f, s0b, batch, seq, dk, dv):
    tc = min(seq, 256)
    nb = seq // tc
    hk, hv = GLA_HEADS * dk, GLA_HEADS * dv
    fwd = lambda width, col: pl.BlockSpec((tc, width), lambda b, i: (b * nb + i, col))
    bwd = lambda width, col: pl.BlockSpec((tc, width), lambda b, i: (b * nb + nb - 1 - i, col))
    st_spec = pl.BlockSpec((None, GLA_HEADS, dv, dk), lambda b, i: (b, 0, 0, 0))
    v_col = 2 * hk // hv
    idx = np.arange(tc)
    same = (idx[:, None] // GLA_CHUNK) == (idx[None, :] // GLA_CHUNK)
    tri_f = jnp.asarray(same & (idx[None, :] <= idx[:, None]), BF16)
    tri_b = jnp.asarray(same & (idx[None, :] >= idx[:, None]), BF16)
    return pl.pallas_call(
        functools.partial(_gla_kernel, dk=dk, dv=dv),
        grid=(batch, nb),
        in_specs=[fwd(hk, 0), fwd(hk, 1), fwd(hv, v_col), fwd(LANES, 0),
                  bwd(hk, 0), bwd(hk, 1), bwd(hv, v_col), bwd(LANES, 0),
                  _const_spec(wgf.shape), _const_spec(wgb.shape), _const_spec(bg.shape),
                  _const_spec(tri_f.shape), _const_spec(tri_b.shape), st_spec, st_spec],
        out_specs=[fwd(hv, 0), bwd(hv, 0), st_spec, st_spec],
        out_shape=[jax.ShapeDtypeStruct((batch * seq, hv), BF16)] * 2
        + [jax.ShapeDtypeStruct((batch, GLA_HEADS, dv, dk), F32)] * 2,
        scratch_shapes=[pltpu.VMEM((GLA_HEADS, dv, dk), F32)] * 2,
        compiler_params=_params(("parallel", "arbitrary")),
        name="gla_scan",
    )(p, p, p, lr, p, p, p, lr, wgf, wgb, bg, tri_f, tri_b, s0f, s0b)


def _outproj1_kernel(x_ref, of_ref, ob_ref, go_ref, g_ref, nw_ref, w_ref, out_ref, *, dv):
    o = of_ref[...].astype(F32) + ob_ref[...].astype(F32)
    parts = []
    for h in range(GLA_HEADS):
        oh = o[:, h * dv:(h + 1) * dv]
        ms = jnp.mean(oh * oh, axis=1, keepdims=True)
        parts.append(oh * lax.rsqrt(ms + EPS))
    o = jnp.concatenate(parts, axis=1) * nw_ref[...] * _silu(go_ref[...].astype(F32))
    out_ref[...] = x_ref[...] + g_ref[...] * jnp.dot(o.astype(BF16), w_ref[...], preferred_element_type=F32)


def _outproj1(x, o_f, o_b, p, g, out_norm_w, w_out, seq, row_fn, dv):
    t, d = x.shape
    hv = GLA_HEADS * dv
    tm = min(seq, 512)
    per_seq = seq // tm
    tok = lambda n, col=0: pl.BlockSpec((tm, n), lambda i: (i, col))
    go_col = (p.shape[1] - hv) // hv
    nw = jnp.tile(out_norm_w, GLA_HEADS).reshape(1, hv)
    wb = w_out.astype(BF16)
    return pl.pallas_call(
        functools.partial(_outproj1_kernel, dv=dv),
        grid=(t // tm,),
        in_specs=[tok(d), tok(hv), tok(hv), tok(hv, go_col),
                  pl.BlockSpec((None, 1, d), lambda i: (row_fn(i // per_seq), 0, 0)),
                  _const_spec((1, hv)), _const_spec(wb.shape)],
        out_specs=tok(d),
        out_shape=jax.ShapeDtypeStruct((t, d), F32),
        compiler_params=_params(("parallel",)),
        name="outproj1",
    )(x, o_f, o_b, p, g, nw, wb)


def _mod_rows(mod_l, d):
    return [mod_l[:, j * d:(j + 1) * d].reshape(mod_l.shape[0], 1, d) for j in range(6)]


def _trunk(x, batch, seq, mods, row_fn, norm_w, ab, cp, moe_w, cache, s0, emit_cache):
    d = x.shape[-1]
    t = batch * seq
    x = x.reshape(t, d)
    (w_in0, conv_w, conv_b, filt, hy_skip, qn_w, kn_w, lam_p, subln_w, w_out0) = ab
    (w_in1, gate_up_w, gate_up_b, out_norm_w, w_out1) = cp
    router_w, router_b, wg, wu, wd = moe_w
    width = d // 2
    hy_in = 3 * width

    sh1, sc1, g1, sh2, sc2, g2 = mods[0]
    nseg = DA_HEADS * 2
    qkw = jnp.stack([jnp.tile(qn_w, nseg), jnp.tile(kn_w, nseg)])
    seg = np.arange(DA_HEADS * DA_VD) // DA_DH
    ones = jnp.asarray(seg[:, None] == seg[None, :], BF16)
    v_cols = w_in0[:, hy_in + 2 * DA_HEADS * DA_VD:]
    w_chan = jnp.concatenate([w_in0[:, :hy_in], v_cols], axis=1).T.astype(BF16)
    outs = _proj0(x, sc1, sh1, norm_w[0, 0].reshape(1, d), w_in0[:, hy_in:].astype(BF16), w_chan, qkw, ones,
                  batch, seq, row_fn, rope=cache is not None, emit_cache=emit_cache)
    chan, q, k = outs[:3]
    y_hy = _hyena(chan, conv_w, conv_b, filt, hy_skip, seq, width)
    lam_init = 0.8 - 0.6 * math.exp(-0.3 * 0)
    o = _diff_attention(q, k, chan, hy_in, cache, lam_p, subln_w, batch, seq, lam_init)
    x = _outproj0(x, y_hy, o, g1, w_out0, seq, row_fn)
    x = _moe(x, sc2, sh2, g2, norm_w[0, 1].reshape(1, d), router_w, router_b, wg[0], wu[0], wd[0], seq, row_fn)

    sh1, sc1, g1, sh2, sc2, g2 = mods[1]
    dk = d // 2 // GLA_HEADS
    dv = d // GLA_HEADS
    n_main = GLA_HEADS * (2 * dk + 2 * dv)
    w_lr = jnp.pad(w_in1[:, n_main:], ((0, 0), (0, LANES - 2 * GLA_RANK))).astype(BF16)
    p, lr = _proj1(x, sc1, sh1, norm_w[1, 0].reshape(1, d), w_in1[:, :n_main].astype(BF16), w_lr, seq, row_fn)
    wgf = jnp.pad(gate_up_w[0], ((0, LANES - GLA_RANK), (0, 0)))
    wgb = jnp.pad(gate_up_w[1], ((GLA_RANK, LANES - 2 * GLA_RANK), (0, 0)))
    o_f, o_b, s_f, s_b = _gla(p, lr, wgf, wgb, gate_up_b, jnp.swapaxes(s0[0], 2, 3), jnp.swapaxes(s0[1], 2, 3),
                              batch, seq, dk, dv)
    s_f, s_b = jnp.swapaxes(s_f, 2, 3), jnp.swapaxes(s_b, 2, 3)
    x = _outproj1(x, o_f, o_b, p, g1, out_norm_w, w_out1, seq, row_fn, dv)
    x = _moe(x, sc2, sh2, g2, norm_w[1, 1].reshape(1, d), router_w, router_b, wg[1], wu[1], wd[1], seq, row_fn)
    return x.reshape(batch, seq, d), outs[3:], (s_f, s_b)


def kernel(x_prompt, x_sample, cache_l0_k, cache_l0_v, state_l1_fwd, state_l1_bwd, c, c_ctx, w_mod, b_mod, norm_w,
           l0_w_in, l0_conv_w, l0_conv_b, l0_filt_w1, l0_filt_b1, l0_filt_freq, l0_filt_w2, l0_filt_b2, l0_filt_w3,
           l0_hy_skip, l0_qn_w, l0_kn_w, l0_lambda, l0_subln_w, l0_w_out, l1_w_in, l1_gate_up_w, l1_gate_up_b,
           l1_out_norm_w, l1_w_out, moe_router_w, moe_router_b, moe_w_gate, moe_w_up, moe_w_down):
    d = x_prompt.shape[-1]
    n_lat = c.shape[0]
    ctx_row = n_lat
    rows = 2 * SUBLANES
    cond = jnp.zeros((rows, d), F32).at[:n_lat].set(c).at[ctx_row].set(c_ctx)
    mod = _modulation(cond, w_mod, b_mod)
    mods = [_mod_rows(mod[l], d) for l in range(mod.shape[0])]
    filt = (l0_filt_w1, l0_filt_b1, l0_filt_freq, l0_filt_w2, l0_filt_b2, l0_filt_w3)
    ab = (l0_w_in, l0_conv_w, l0_conv_b, filt, l0_hy_skip, l0_qn_w, l0_kn_w, l0_lambda, l0_subln_w, l0_w_out)
    cp = (l1_w_in, l1_gate_up_w, l1_gate_up_b, l1_out_norm_w, l1_w_out)
    moe_w = (moe_router_w, moe_router_b, moe_w_gate.astype(BF16), moe_w_up.astype(BF16), moe_w_down.astype(BF16))

    b_ctx, l_ctx = x_prompt.shape[:2]
    zero_state = jnp.zeros((b_ctx,) + state_l1_fwd.shape[1:], F32)
    y_prompt, (new_k, new_v), (new_sf, new_sb) = _trunk(
        x_prompt, b_ctx, l_ctx, mods, lambda b: ctx_row, norm_w, ab, cp, moe_w, None, (zero_state, zero_state), True)
    b_lat, l_lat = x_sample.shape[:2]
    y_sample, _, _ = _trunk(
        x_sample, b_lat, l_lat, mods, lambda b: b, norm_w, ab, cp, moe_w, (cache_l0_k, cache_l0_v),
        (state_l1_fwd, state_l1_bwd), False)
    return (y_prompt, y_sample, new_k, new_v, new_sf, new_sb)
```

```python
import cmath
import functools
import math

import numpy as np
import jax
import jax.numpy as jnp
from jax import lax
from jax.experimental import pallas as pl
from jax.experimental.pallas import tpu as pltpu

F32 = jnp.float32
BF16 = jnp.bfloat16
HI = lax.Precision.HIGHEST
EPS = 1e-6

LANES = 128
SUBLANES = 8
VMEM_LIMIT = 56 << 20

DA_HEADS = 4
DA_DH = 64
DA_VD = 2 * DA_DH
ROPE_NF = DA_DH // 4
ROPE_BASE = 10000.0
GRID_W = 64
HY_BANDS = 16
HY_TARGET = 1e-2
HY_MAX_DECAY = math.log(HY_TARGET) / 0.3
HY_MIN_DECAY = math.log(HY_TARGET) / 1.5
GLA_HEADS = 4
GLA_RANK = 16
GLA_GATE_NORM = 16.0
GLA_CHUNK = 64
N_EXPERTS = 16
N_GROUPS = 4
EPG = N_EXPERTS // N_GROUPS
MOE_TILE = 160
NT = (((1,), (1,)), ((), ()))
TN = (((0,), (0,)), ((), ()))


def _params(sem):
    return pltpu.CompilerParams(dimension_semantics=sem, vmem_limit_bytes=VMEM_LIMIT)


def _const_spec(shape):
    nd = len(shape)
    return pl.BlockSpec(shape, lambda *_: (0,) * nd)


def _silu(x):
    return x * (1.0 / (1.0 + jnp.exp(-x)))


def _norm_mod(x, nw, sc, sh):
    ms = jnp.mean(x * x, axis=-1, keepdims=True)
    return x * lax.rsqrt(ms + EPS) * nw * (1.0 + sc) + sh


def _mod_kernel(c_ref, w_ref, b_ref, o_ref):
    s = _silu(c_ref[...])
    o_ref[...] = jnp.dot(s, w_ref[...], precision=HI, preferred_element_type=F32) + b_ref[...]


def _modulation(cond, w_mod, b_mod):
    depth, d, d6 = w_mod.shape
    r = cond.shape[0]
    tn = 1536
    return pl.pallas_call(
        _mod_kernel,
        grid=(depth, d6 // tn),
        in_specs=[
            pl.BlockSpec((r, d), lambda l, j: (0, 0)),
            pl.BlockSpec((None, d, tn), lambda l, j: (l, 0, j)),
            pl.BlockSpec((None, 1, tn), lambda l, j: (l, 0, j)),
        ],
        out_specs=pl.BlockSpec((None, r, tn), lambda l, j: (l, 0, j)),
        out_shape=jax.ShapeDtypeStruct((depth, r, d6), F32),
        compiler_params=_params(("parallel", "parallel")),
        name="modulation",
    )(cond, w_mod, b_mod.reshape(depth, 1, d6))


def _proj0_kernel(x_ref, sc_ref, sh_ref, nw_ref, wtok_ref, wchan_ref, qkw_ref, ones_ref, cos_ref, sin_ref,
                  chan_ref, q_ref, k_ref, *cache_refs, rope):
    h = _norm_mod(x_ref[...], nw_ref[...], sc_ref[...], sh_ref[...]).astype(BF16)
    chan_ref[...] = lax.dot_general(wchan_ref[...], h, NT, preferred_element_type=F32).astype(chan_ref.dtype)
    p = jnp.dot(h, wtok_ref[...], preferred_element_type=F32)
    w = DA_HEADS * DA_VD
    ones = ones_ref[...]

    def segnorm(z, gain):
        ss = jnp.dot((z * z).astype(BF16), ones, preferred_element_type=F32)
        return z * lax.rsqrt(ss * (1.0 / DA_DH) + EPS) * gain

    q = segnorm(p[:, :w], qkw_ref[0:1, :])
    k = segnorm(p[:, w:2 * w], qkw_ref[1:2, :])
    if cache_refs:
        kn_ref, vn_ref = cache_refs
        v = p[:, 2 * w:]
        for hh in range(DA_HEADS):
            kn_ref[hh] = k[:, hh * DA_VD:(hh + 1) * DA_VD]
            vn_ref[hh] = v[:, hh * DA_VD:(hh + 1) * DA_VD]
    if rope:
        cos = jnp.concatenate([cos_ref[...]] * (w // LANES), axis=1)
        sin = jnp.concatenate([sin_ref[...]] * (w // LANES), axis=1)
        lane = lax.broadcasted_iota(jnp.int32, q.shape, 1)
        first = (lane % (2 * ROPE_NF)) < ROPE_NF

        def rot(z):
            partner = jnp.where(first, pltpu.roll(z, w - ROPE_NF, 1), pltpu.roll(z, ROPE_NF, 1))
            return z * cos + partner * sin

        q = rot(q)
        k = rot(k)
    q_ref[...] = (q * (DA_DH ** -0.5 * math.log2(math.e))).astype(q_ref.dtype)
    k_ref[...] = k.astype(k_ref.dtype)


def _rope_tables(seq):
    t = np.arange(seq)
    lane = np.arange(LANES)
    d = lane % DA_DH
    axis = d // (2 * ROPE_NF)
    part = (d % (2 * ROPE_NF)) // ROPE_NF
    f = d % ROPE_NF
    pos = jnp.where(axis[None, :] == 0, (t // GRID_W)[:, None], (t % GRID_W)[:, None]).astype(F32)
    inv = ROPE_BASE ** (-jnp.arange(ROPE_NF, dtype=F32) / ROPE_NF)
    ang = pos * inv[f][None, :]
    sign = jnp.asarray(np.where(part == 0, -1.0, 1.0), F32)[None, :]
    return jnp.cos(ang), jnp.sin(ang) * sign


def _proj0(x, sc, sh, nw, wqkv, whyt, qkw, ones, batch, seq, row_fn, rope, emit_cache):
    t, d = x.shape
    tm = min(seq, 512)
    per_seq = seq // tm
    w = DA_HEADS * DA_VD
    hy_in = whyt.shape[0]
    if not emit_cache:
        wqkv = wqkv[:, :2 * w]
    if rope:
        cos, sin = _rope_tables(seq)
    else:
        cos = jnp.ones((seq, LANES), F32)
        sin = jnp.zeros((seq, LANES), F32)
    mod_spec = pl.BlockSpec((None, 1, d), lambda i: (row_fn(i // per_seq), 0, 0))
    tab_spec = pl.BlockSpec((tm, LANES), lambda i: (i % per_seq, 0))
    tok = lambda n: pl.BlockSpec((tm, n), lambda i: (i, 0))
    out_specs = [pl.BlockSpec((None, hy_in, tm), lambda i: (i // per_seq, 0, i % per_seq)), tok(w), tok(w)]
    out_shape = [jax.ShapeDtypeStruct((batch, hy_in, seq), BF16)] + [jax.ShapeDtypeStruct((t, w), BF16)] * 2
    if emit_cache:
        cspec = pl.BlockSpec((None, DA_HEADS, tm, DA_VD), lambda i: (i // per_seq, 0, i % per_seq, 0))
        out_specs += [cspec, cspec]
        out_shape += [jax.ShapeDtypeStruct((batch, DA_HEADS, seq, DA_VD), F32)] * 2
    return pl.pallas_call(
        functools.partial(_proj0_kernel, rope=rope),
        grid=(t // tm,),
        in_specs=[tok(d), mod_spec, mod_spec, _const_spec((1, d)), _const_spec(wqkv.shape), _const_spec(whyt.shape),
                  _const_spec(qkw.shape), _const_spec(ones.shape), tab_spec, tab_spec],
        out_specs=out_specs,
        out_shape=out_shape,
        compiler_params=_params(("parallel",)),
        name="proj0",
    )(x, sc, sh, nw, wqkv, whyt, qkw, ones, cos, sin)


def _c_add(a, b, sign=1.0):
    if b is None:
        return a
    if a is None:
        return b if sign > 0 else tuple(None if p is None else -p for p in b)
    out = []
    for pa, pb in zip(a, b):
        if pb is None:
            out.append(pa)
        elif pa is None:
            out.append(pb if sign > 0 else -pb)
        else:
            out.append(pa + pb if sign > 0 else pa - pb)
    return tuple(out)


def _c_mul_const(w, a):
    if a is None:
        return None
    wr = 0.0 if abs(w.real) < 1e-12 else w.real
    wi = 0.0 if abs(w.imag) < 1e-12 else w.imag
    ar, ai = a

    def scaled(c, p):
        if p is None or c == 0.0:
            return None
        if c == 1.0:
            return p
        if c == -1.0:
            return -p
        return c * p

    re = _c_add((scaled(wr, ar),), (scaled(wi, ai),), -1.0)[0]
    im = _c_add((scaled(wr, ai),), (scaled(wi, ar),), 1.0)[0]
    return (re, im)


def _fft_list(vals, sign, first_half_only=False):
    n = len(vals)
    if n == 1:
        return list(vals)
    ev = _fft_list(vals[0::2], sign)
    od = _fft_list(vals[1::2], sign)
    out = [None] * n
    for k in range(n // 2):
        tw = _c_mul_const(cmath.exp(sign * 2j * math.pi * k / n), od[k])
        out[k] = _c_add(ev[k], tw, 1.0)
        if not first_half_only:
            out[k + n // 2] = _c_add(ev[k], tw, -1.0)
    return out


def _dft_mats(n2, total):
    k = np.arange(n2)
    ang = 2.0 * np.pi * np.outer(k, k) / n2
    fr, fi = np.cos(ang), -np.sin(ang)
    fwd = np.block([[fr, fi], [-fi, fr]])
    inv = np.block([[fr, -fi], [fi, fr]]) / total
    return jnp.asarray(fwd, BF16), jnp.asarray(inv, BF16)


def _twiddles(n1, n2):
    ang = 2.0 * np.pi * np.outer(np.arange(n1), np.arange(n2)) / (n1 * n2)
    return jnp.asarray(np.cos(ang), F32), jnp.asarray(-np.sin(ang), F32)


def _fft_split(seq):
    n = 2 * seq
    n2 = min(512, n // 2)
    return n // n2, n2


def _across_fwd(load, n_in, z_ref, twr_ref, twi_ref, n1, n2, rows):
    per_row = n2 // LANES

    def body(it, carry):
        r0 = pl.multiple_of((it // per_row) * SUBLANES, SUBLANES)
        l0 = pl.multiple_of((it % per_row) * LANES, LANES)
        vals = [load(j, r0, l0) for j in range(n_in)] + [None] * (n1 - n_in)
        outs = _fft_list(vals, -1.0)
        for k1 in range(n1):
            twr = twr_ref[pl.ds(k1, 1), pl.ds(l0, LANES)]
            twi = twi_ref[pl.ds(k1, 1), pl.ds(l0, LANES)]
            re, im = outs[k1]
            zero = jnp.zeros((SUBLANES, LANES), F32)
            re = zero if re is None else re
            im = zero if im is None else im
            z_ref[k1, pl.ds(r0, SUBLANES), pl.ds(l0, LANES)] = re * twr - im * twi
            z_ref[k1, pl.ds(r0, SUBLANES), pl.ds(n2 + l0, LANES)] = re * twi + im * twr
        return carry

    lax.fori_loop(0, (rows // SUBLANES) * per_row, body, 0)


def _hy_hidden_kernel(w1_ref, b1_ref, freq_ref, w2_ref, b2_ref, bands_ref, o_ref, *, seq):
    n = 2 * seq
    k = lax.broadcasted_iota(jnp.int32, (n, 1), 0)
    pos = jnp.where(k < seq, k, n - k).astype(F32)
    t = pos / (seq - 1)
    ang = 2.0 * math.pi * pos / seq * bands_ref[...]
    z = (t * w1_ref[0:1, :]
         + jnp.dot(jnp.cos(ang), w1_ref[1:1 + HY_BANDS, :], precision=HI, preferred_element_type=F32)
         + jnp.dot(-jnp.sin(ang), w1_ref[1 + HY_BANDS:, :], precision=HI, preferred_element_type=F32)
         + b1_ref[...])
    hid = jnp.sin(freq_ref[0:1, :] * z)
    hid = jnp.sin(freq_ref[1:2, :] * (jnp.dot(hid, w2_ref[...], precision=HI, preferred_element_type=F32) + b2_ref[...]))
    o_ref[...] = hid


def _hy_hidden(seq, f_w1, f_b1, f_freq, f_w2, f_b2):
    ffn = f_w1.shape[1]
    bands = jnp.linspace(1e-4, HY_BANDS - 1, HY_BANDS, dtype=F32).reshape(1, HY_BANDS)
    args = (f_w1, f_b1.reshape(1, ffn), f_freq, f_w2, f_b2.reshape(1, ffn), bands)
    return pl.pallas_call(
        functools.partial(_hy_hidden_kernel, seq=seq),
        grid=(1,),
        in_specs=[_const_spec(a.shape) for a in args],
        out_specs=_const_spec((2 * seq, ffn)),
        out_shape=jax.ShapeDtypeStruct((2 * seq, ffn), F32),
        compiler_params=_params(("arbitrary",)),
        name="hyena_filter_hidden",
    )(*args)


def _hy_spec_kernel(hid_ref, wb_ref, wa_ref, dec_ref, twr_ref, twi_ref, gf_ref, o_ref, taps_ref, z_ref, *, seq, n1, n2):
    ct = wb_ref.shape[0]
    hid = hid_ref[...]
    back = lax.dot_general(wb_ref[...], hid[:seq], NT, precision=HI, preferred_element_type=F32)
    ahead = lax.dot_general(wa_ref[...], hid[seq:], NT, precision=HI, preferred_element_type=F32)
    lane = lax.broadcasted_iota(jnp.int32, (1, seq), 1)
    dec = dec_ref[...]
    t_back = lane.astype(F32) / (seq - 1)
    t_ahead = (seq - lane).astype(F32) / (seq - 1)
    back = back * jnp.exp(-t_back * dec)
    ahead = jnp.where(lane == 0, 0.0, ahead * jnp.exp(-t_ahead * dec))
    norm = (jnp.sum(jnp.abs(back), axis=1, keepdims=True) + jnp.sum(jnp.abs(ahead), axis=1, keepdims=True)) + EPS
    taps_ref[:, :seq] = back / norm
    taps_ref[:, seq:] = ahead / norm

    def load(j, r0, l0):
        return (taps_ref[pl.ds(r0, SUBLANES), pl.ds(j * n2 + l0, LANES)], None)

    _across_fwd(load, n1, z_ref, twr_ref, twi_ref, n1, n2, ct)
    for k1 in range(n1):
        o_ref[k1] = jnp.dot(z_ref[k1].astype(BF16), gf_ref[...], preferred_element_type=F32)


def _hy_filter_spectra(hid, f_w3, seq, width, ct):
    n1, n2 = _fft_split(seq)
    ffn = f_w3.shape[0]
    w3t = f_w3.T.reshape(4, width, ffn)
    decay = jnp.abs(jnp.linspace(HY_MIN_DECAY, HY_MAX_DECAY, width, dtype=F32)).reshape(width, 1)
    twr, twi = _twiddles(n1, n2)
    gf, _ = _dft_mats(n2, 2 * seq)
    return pl.pallas_call(
        functools.partial(_hy_spec_kernel, seq=seq, n1=n1, n2=n2),
        grid=(2, width // ct),
        in_specs=[
            _const_spec(hid.shape),
            pl.BlockSpec((None, ct, ffn), lambda i, c: (2 * i, c, 0)),
            pl.BlockSpec((None, ct, ffn), lambda i, c: (2 * i + 1, c, 0)),
            pl.BlockSpec((ct, 1), lambda i, c: (c, 0)),
            _const_spec(twr.shape), _const_spec(twi.shape), _const_spec(gf.shape),
        ],
        out_specs=pl.BlockSpec((None, n1, ct, 2 * n2), lambda i, c: (i, 0, c, 0)),
        out_shape=jax.ShapeDtypeStruct((2, n1, width, 2 * n2), F32),
        scratch_shapes=[pltpu.VMEM((ct, 2 * seq), F32), pltpu.VMEM((n1, ct, 2 * n2), F32)],
        compiler_params=_params(("parallel", "parallel")),
        name="hyena_filter_spectra",
    )(hid, w3t, w3t, decay, twr, twi, gf)


def _short_conv_rows(ref, cw, r0, nrows, seq):
    x = ref[pl.ds(r0, nrows), :].astype(F32)
    lane = lax.broadcasted_iota(jnp.int32, x.shape, 1)
    left = jnp.where(lane == 0, 0.0, pltpu.roll(x, 1, 1))
    right = jnp.where(lane == seq - 1, 0.0, pltpu.roll(x, seq - 1, 1))
    return left * cw[:, 0:1] + x * cw[:, 1:2] + right * cw[:, 2:3] + cw[:, 3:4]


def _hy_conv_kernel(uin_ref, gin_ref, cwu_ref, cwg_ref, d_ref, h_ref, twr_ref, twi_ref, gf_ref, gi_ref,
                    o_ref, u_ref, z_ref, *, seq, n1, n2, conv_u):
    ct = uin_ref.shape[1]
    rows16 = 2 * SUBLANES

    def prep(i, carry):
        r0 = pl.multiple_of(i * rows16, rows16)
        cwg = cwg_ref[pl.ds(r0, rows16), :]
        cwu = cwu_ref[pl.ds(r0, rows16), :]
        for e in range(2):
            o_ref[e, pl.ds(r0, rows16), :] = _short_conv_rows(gin_ref.at[e], cwg, r0, rows16, seq)
            if conv_u:
                u_ref[e, pl.ds(r0, rows16), :] = _short_conv_rows(uin_ref.at[e], cwu, r0, rows16, seq)
            else:
                u_ref[e, pl.ds(r0, rows16), :] = uin_ref[e, pl.ds(r0, rows16), :].astype(F32)
        return carry

    lax.fori_loop(0, ct // rows16, prep, 0)

    def load(j, r0, l0):
        return (u_ref[0, pl.ds(r0, SUBLANES), pl.ds(j * n2 + l0, LANES)],
                u_ref[1, pl.ds(r0, SUBLANES), pl.ds(j * n2 + l0, LANES)])

    _across_fwd(load, n1 // 2, z_ref, twr_ref, twi_ref, n1, n2, ct)

    def within(k1, carry):
        y = jnp.dot(z_ref[k1].astype(BF16), gf_ref[...], preferred_element_type=F32)
        hh = h_ref[k1]
        yr, yi = y[:, :n2], y[:, n2:]
        hr, hi = hh[:, :n2], hh[:, n2:]
        p = jnp.concatenate([yr * hr - yi * hi, yr * hi + yi * hr], axis=1)
        z_ref[k1] = jnp.dot(p.astype(BF16), gi_ref[...], preferred_element_type=F32)
        return carry

    lax.fori_loop(0, n1, within, 0)

    per_row = n2 // LANES

    def finish(it, carry):
        r0 = pl.multiple_of((it // per_row) * SUBLANES, SUBLANES)
        l0 = pl.multiple_of((it % per_row) * LANES, LANES)
        vals = []
        for k1 in range(n1):
            twr = twr_ref[pl.ds(k1, 1), pl.ds(l0, LANES)]
            twi = twi_ref[pl.ds(k1, 1), pl.ds(l0, LANES)]
            re = z_ref[k1, pl.ds(r0, SUBLANES), pl.ds(l0, LANES)]
            im = z_ref[k1, pl.ds(r0, SUBLANES), pl.ds(n2 + l0, LANES)]
            vals.append((re * twr + im * twi, im * twr - re * twi))
        outs = _fft_list(vals, 1.0, first_half_only=True)
        d = d_ref[pl.ds(r0, SUBLANES), :]
        for j in range(n1 // 2):
            sl = (pl.ds(r0, SUBLANES), pl.ds(j * n2 + l0, LANES))
            for e in range(2):
                o_ref[(e,) + sl] = o_ref[(e,) + sl] * (outs[j][e] + u_ref[(e,) + sl] * d)
        return carry

    lax.fori_loop(0, (ct // SUBLANES) * per_row, finish, 0)


def _hy_conv(u, u_off, g, g_off, cw, spectra, conv_idx, d_skip, seq, width, ct, conv_u):
    batch = u.shape[0]
    n1, n2 = _fft_split(seq)
    twr, twi = _twiddles(n1, n2)
    gf, gi = _dft_mats(n2, 2 * seq)
    uo, go = u_off // ct, g_off // ct
    return pl.pallas_call(
        functools.partial(_hy_conv_kernel, seq=seq, n1=n1, n2=n2, conv_u=conv_u),
        grid=(width // ct, batch // 2),
        in_specs=[
            pl.BlockSpec((2, ct, seq), lambda c, p: (p, uo + c, 0)),
            pl.BlockSpec((2, ct, seq), lambda c, p: (p, go + c, 0)),
            pl.BlockSpec((ct, 4), lambda c, p: (uo + c, 0)),
            pl.BlockSpec((ct, 4), lambda c, p: (go + c, 0)),
            pl.BlockSpec((ct, 1), lambda c, p: (c, 0)),
            pl.BlockSpec((None, n1, ct, 2 * n2), lambda c, p: (conv_idx, 0, c, 0), pipeline_mode=pl.Buffered(1)),
            _const_spec(twr.shape), _const_spec(twi.shape), _const_spec(gf.shape), _const_spec(gi.shape),
        ],
        out_specs=pl.BlockSpec((2, ct, seq), lambda c, p: (p, c, 0)),
        out_shape=jax.ShapeDtypeStruct((batch, width, seq), F32),
        scratch_shapes=[pltpu.VMEM((2, ct, seq), F32), pltpu.VMEM((n1, ct, 2 * n2), F32)],
        compiler_params=_params(("arbitrary", "arbitrary")),
        name="hyena_conv",
    )(u, g, cw, cw, d_skip, spectra, twr, twi, gf, gi)


def _hyena(hyt, conv_w, conv_b, filt, hy_skip, seq, width):
    f_w1, f_b1, f_freq, f_w2, f_b2, f_w3 = filt
    ct = LANES
    hid = _hy_hidden(seq, f_w1, f_b1, f_freq, f_w2, f_b2)
    spectra = _hy_filter_spectra(hid, f_w3, seq, width, ct)
    cw = jnp.concatenate([conv_w.T, conv_b[:, None]], axis=1)
    y1 = _hy_conv(hyt, 0, hyt, width, cw, spectra, 0, hy_skip[0].reshape(width, 1), seq, width, ct, True)
    return _hy_conv(y1, 0, hyt, 2 * width, cw, spectra, 1, hy_skip[1].reshape(width, 1), seq, width, ct, False)


def _attn_kernel(q_ref, k_ref, vt_ref, *rest, lam_init, with_cache, kc):
    if with_cache:
        kc_ref, vct_ref, lam_ref, sw_ref, o_ref, kmax_ref = rest
    else:
        lam_ref, sw_ref, o_ref, kmax_ref = rest
    lp = lam_ref[...]
    lam = (jnp.exp(jnp.sum(lp[0:1] * lp[1:2], axis=1, keepdims=True))
           - jnp.exp(jnp.sum(lp[2:3] * lp[3:4], axis=1, keepdims=True)) + lam_init)
    q = q_ref[...]
    lane = lax.broadcasted_iota(jnp.int32, q.shape, 1)
    zero = jnp.zeros_like(q)
    qs = (jnp.where(lane < DA_DH, q, zero), jnp.where(lane >= DA_DH, q, zero))
    seq = k_ref.shape[0]
    chunks = [(k_ref[c * kc:(c + 1) * kc, :], vt_ref[:, c * kc:(c + 1) * kc]) for c in range(seq // kc)]
    if with_cache:
        chunks.append((kc_ref[...].astype(BF16), vct_ref[...].astype(BF16)))
    tq = q.shape[0]
    sub = lax.broadcasted_iota(jnp.int32, (SUBLANES, DA_VD), 0)
    ln = lax.broadcasted_iota(jnp.int32, (SUBLANES, DA_VD), 1)
    pick = jnp.where((ln // DA_DH) == sub, 1.0, 0.0).astype(BF16)

    def sq_norms(x):
        xf = x.astype(F32)
        return lax.dot_general(pick, (xf * xf).astype(BF16), NT, preferred_element_type=F32)

    @pl.when(pl.program_id(2) == 0)
    def _key_norms():
        best = None
        for kk, _ in chunks:
            cur = jnp.max(sq_norms(kk), axis=1, keepdims=True)
            best = cur if best is None else jnp.maximum(best, cur)
        kmax_ref[...] = jnp.broadcast_to(best, kmax_ref.shape)

    def scores(ci):
        return [lax.dot_general(chunks[ci][0], qm, NT, preferred_element_type=F32) for qm in qs]

    def finish(acc1, l1, acc2, l2):
        o = acc1 * (1.0 / l1) - acc2 * (lam / l2)
        ms = jnp.mean(o * o, axis=0, keepdims=True)
        o_ref[...] = (o * lax.rsqrt(ms + EPS) * sw_ref[...] * (1.0 - lam_init)).astype(o_ref.dtype)

    bound = jnp.sqrt(sq_norms(q) * kmax_ref[:, 0:1]) * 1.02
    lsum = [jnp.zeros((SUBLANES, tq), F32), jnp.zeros((SUBLANES, tq), F32)]
    acc = [None, None]
    s_next = scores(0)
    for ci, (kk, vv) in enumerate(chunks):
        s_cur = s_next
        if ci + 1 < len(chunks):
            s_next = scores(ci + 1)
        for mi in range(2):
            p = jnp.exp2(s_cur[mi] - bound[mi:mi + 1, :])
            lsum[mi] = lsum[mi] + jnp.sum(p.reshape(-1, SUBLANES, tq), axis=0)
            part = jnp.dot(vv, p.astype(BF16), preferred_element_type=F32)
            acc[mi] = part if acc[mi] is None else acc[mi] + part
    l1 = jnp.sum(lsum[0], axis=0, keepdims=True)
    l2 = jnp.sum(lsum[1], axis=0, keepdims=True)
    finish(acc[0], l1, acc[1], l2)
    safe = jnp.min(jnp.minimum(l1, l2)) > 2.0 ** -60

    @pl.when(jnp.logical_not(safe))
    def _online():
        state = [None, None]
        for ci, (kk, vv) in enumerate(chunks):
            s_cur = scores(ci)
            for mi in range(2):
                s = s_cur[mi]
                cmax = jnp.max(s, axis=0, keepdims=True)
                if state[mi] is None:
                    m = cmax
                    p = jnp.exp2(s - m)
                    l = jnp.sum(p, axis=0, keepdims=True)
                    a = jnp.dot(vv, p.astype(BF16), preferred_element_type=F32)
                else:
                    m_old, l, a = state[mi]
                    m = jnp.maximum(m_old, cmax)
                    alpha = jnp.exp2(m_old - m)
                    p = jnp.exp2(s - m)
                    l = l * alpha + jnp.sum(p, axis=0, keepdims=True)
                    a = a * alpha + jnp.dot(vv, p.astype(BF16), preferred_element_type=F32)
                state[mi] = (m, l, a)
        finish(state[0][2], state[0][1], state[1][2], state[1][1])


def _diff_attention(q, k, chan, v_row0, cache, lam_p, subln_w, batch, seq, lam_init):
    tq = min(seq, 256)
    kc = min(seq, 512)
    nq = seq // tq
    vb = v_row0 // DA_VD
    grid = (batch, DA_HEADS, nq)
    in_specs = [pl.BlockSpec((tq, DA_VD), lambda b, h, i: (b * nq + i, h)),
                pl.BlockSpec((seq, DA_VD), lambda b, h, i: (b, h)),
                pl.BlockSpec((None, DA_VD, seq), lambda b, h, i: (b, vb + h, 0))]
    args = [q, k, chan]
    if cache is not None:
        past = cache[0].shape[2]
        in_specs += [pl.BlockSpec((None, None, past, DA_VD), lambda b, h, i: (b, h, 0, 0)),
                     pl.BlockSpec((None, None, DA_VD, past), lambda b, h, i: (b, h, 0, 0))]
        args += [cache[0], jnp.swapaxes(cache[1], 2, 3)]
    in_specs += [_const_spec(lam_p.shape), _const_spec((DA_VD, 1))]
    args += [lam_p, subln_w.reshape(DA_VD, 1)]
    return pl.pallas_call(
        functools.partial(_attn_kernel, lam_init=lam_init, with_cache=cache is not None, kc=kc),
        grid=grid,
        in_specs=in_specs,
        out_specs=pl.BlockSpec((None, DA_VD, tq), lambda b, h, i: (b, h, i)),
        out_shape=jax.ShapeDtypeStruct((batch, DA_HEADS * DA_VD, seq), BF16),
        scratch_shapes=[pltpu.VMEM((SUBLANES, LANES), F32)],
        compiler_params=_params(("parallel", "parallel", "arbitrary")),
        name="diff_attention",
    )(*args)


def _outproj0_kernel(x_ref, yt_ref, ot_ref, g_ref, wy_ref, wo_ref, out_ref):
    mix = lax.dot_general(yt_ref[...].astype(BF16), wy_ref[...], TN, preferred_element_type=F32)
    mix = mix + lax.dot_general(ot_ref[...], wo_ref[...], TN, preferred_element_type=F32)
    out_ref[...] = x_ref[...] + g_ref[...] * mix


def _outproj0(x, yt, ot, g, w_out, seq, row_fn):
    t, d = x.shape
    width = yt.shape[1]
    tm = min(seq, 512)
    per_seq = seq // tm
    wy = w_out[:width].astype(BF16)
    wo = w_out[width:].astype(BF16)
    tok = pl.BlockSpec((tm, d), lambda i: (i, 0))
    chan = lambda n: pl.BlockSpec((None, n, tm), lambda i: (i // per_seq, 0, i % per_seq))
    return pl.pallas_call(
        _outproj0_kernel,
        grid=(t // tm,),
        in_specs=[tok, chan(width), chan(ot.shape[1]),
                  pl.BlockSpec((None, 1, d), lambda i: (row_fn(i // per_seq), 0, 0)),
                  _const_spec(wy.shape), _const_spec(wo.shape)],
        out_specs=tok,
        out_shape=jax.ShapeDtypeStruct((t, d), F32),
        compiler_params=_params(("parallel",)),
        name="outproj0",
    )(x, yt, ot, g, wy, wo)


def _top2_of_rows(rows):
    n = len(rows)
    v1 = functools.reduce(jnp.maximum, rows)
    i1 = jnp.full(rows[0].shape, n - 1, jnp.int32)
    for j in range(n - 2, -1, -1):
        i1 = jnp.where(rows[j] == v1, j, i1)
    masked = [jnp.where(i1 == j, -jnp.inf, rows[j]) for j in range(n)]
    v2 = functools.reduce(jnp.maximum, masked)
    i2 = jnp.full(rows[0].shape, n - 1, jnp.int32)
    for j in range(n - 2, -1, -1):
        i2 = jnp.where(masked[j] == v2, j, i2)
    return v1, i1, v2, i2


def _select_rows(idx, rows):
    out = rows[-1]
    for j in range(len(rows) - 2, -1, -1):
        out = jnp.where(idx == j, rows[j], out)
    return out


def _moe_kernel(x_ref, sc_ref, sh_ref, g_ref, nw_ref, rwt_ref, rb_ref, upper_ref, wg_ref, wu_ref, wd_ref,
                out_ref, hb_ref, rank_ref, comb_ref, acc_ref, cnt_ref):
    e = pl.program_id(1)
    tb = x_ref.shape[0]

    @pl.when(e == 0)
    def _route():
        h = _norm_mod(x_ref[...], nw_ref[...], sc_ref[...], sh_ref[...])
        hb_ref[...] = h.astype(BF16)
        logits = lax.dot_general(rwt_ref[...], h, NT, precision=HI, preferred_element_type=F32)
        s = 1.0 / (1.0 + jnp.exp(-logits))
        s_sel = s + rb_ref[...]
        s_rows = [s[j:j + 1, :] for j in range(N_EXPERTS)]
        sel_rows = [s_sel[j:j + 1, :] for j in range(N_EXPERTS)]
        tops = [_top2_of_rows(sel_rows[g * EPG:(g + 1) * EPG]) for g in range(N_GROUPS)]
        scores = [tp[0] + tp[2] for tp in tops]
        best = functools.reduce(jnp.maximum, scores)
        g_idx = jnp.full(best.shape, N_GROUPS - 1, jnp.int32)
        for g in range(N_GROUPS - 2, -1, -1):
            g_idx = jnp.where(scores[g] == best, g, g_idx)
        e1 = g_idx * EPG + _select_rows(g_idx, [tp[1] for tp in tops])
        e2 = g_idx * EPG + _select_rows(g_idx, [tp[3] for tp in tops])
        w1 = _select_rows(e1, s_rows)
        w2 = _select_rows(e2, s_rows)
        tot = w1 + w2
        w1, w2 = w1 / tot, w2 / tot
        masks = [(e1 == j) | (e2 == j) for j in range(N_EXPERTS)]
        comb_ref[...] = jnp.concatenate(
            [jnp.where(e1 == j, w1, 0.0) + jnp.where(e2 == j, w2, 0.0) for j in range(N_EXPERTS)], axis=0)
        mask = jnp.concatenate([jnp.where(m, 1.0, 0.0) for m in masks], axis=0)
        rank = jnp.dot(mask.astype(BF16), upper_ref[...], preferred_element_type=F32)
        rank_ref[...] = jnp.where(mask > 0.0, rank, -1.0)
        for j in range(N_EXPERTS):
            cnt_ref[j] = jnp.sum(mask[j:j + 1, :]).astype(jnp.int32)
        acc_ref[...] = jnp.zeros_like(acc_ref)

    rank = rank_ref[pl.ds(e, 1), :]
    comb = comb_ref[pl.ds(e, 1), :]
    n_tiles = (cnt_ref[e] + MOE_TILE - 1) // MOE_TILE

    def tile(j, carry):
        slot = (lax.broadcasted_iota(jnp.int32, (MOE_TILE, tb), 0) + j * MOE_TILE).astype(F32)
        hit = rank == slot
        onehot = jnp.where(hit, 1.0, 0.0).astype(BF16)
        xe = jnp.dot(onehot, hb_ref[...], preferred_element_type=F32).astype(BF16)
        gate = jnp.dot(xe, wg_ref[...], preferred_element_type=F32)
        up = jnp.dot(xe, wu_ref[...], preferred_element_type=F32)
        y = jnp.dot((_silu(gate) * up).astype(BF16), wd_ref[...], preferred_element_type=F32)
        w_slot = jnp.sum(jnp.where(hit, comb, 0.0), axis=1, keepdims=True)
        acc_ref[...] += lax.dot_general(onehot, (y * w_slot).astype(BF16), TN, preferred_element_type=F32)
        return carry

    lax.fori_loop(0, n_tiles, tile, 0)

    @pl.when(e == N_EXPERTS - 1)
    def _finish():
        out_ref[...] = x_ref[...] + g_ref[...] * acc_ref[...]


def _moe(x, sc, sh, g, nw, router_w, router_b, w_gate, w_up, w_down, seq, row_fn, shared_mod):
    t, d = x.shape
    tb = min(t if shared_mod else seq, 1024)
    per_seq = t if shared_mod else seq // tb
    dff = w_gate.shape[-1]
    upper = jnp.asarray(np.triu(np.ones((tb, tb), np.float32), 1), BF16)
    mod_spec = pl.BlockSpec((None, 1, d), lambda i, e: (row_fn(i // per_seq), 0, 0))
    tok = pl.BlockSpec((tb, d), lambda i, e: (i, 0))
    return pl.pallas_call(
        _moe_kernel,
        grid=(t // tb, N_EXPERTS),
        in_specs=[tok, mod_spec, mod_spec, mod_spec, _const_spec((1, d)), _const_spec((N_EXPERTS, d)),
                  _const_spec((N_EXPERTS, 1)), _const_spec(upper.shape),
                  pl.BlockSpec((None, d, dff), lambda i, e: (e, 0, 0)),
                  pl.BlockSpec((None, d, dff), lambda i, e: (e, 0, 0)),
                  pl.BlockSpec((None, dff, d), lambda i, e: (e, 0, 0))],
        out_specs=tok,
        out_shape=jax.ShapeDtypeStruct((t, d), F32),
        scratch_shapes=[pltpu.VMEM((tb, d), BF16), pltpu.VMEM((N_EXPERTS, tb), F32), pltpu.VMEM((N_EXPERTS, tb), F32),
                        pltpu.VMEM((tb, d), F32), pltpu.SMEM((N_EXPERTS,), jnp.int32)],
        compiler_params=_params(("parallel", "arbitrary")),
        name="moe",
    )(x, sc, sh, g, nw, router_w.T, router_b.reshape(N_EXPERTS, 1), upper, w_gate, w_up, w_down)


def _proj1_kernel(x_ref, sc_ref, sh_ref, nw_ref, w_ref, wlr_ref, p_ref, lr_ref):
    h = _norm_mod(x_ref[...], nw_ref[...], sc_ref[...], sh_ref[...]).astype(BF16)
    p_ref[...] = jnp.dot(h, w_ref[...], preferred_element_type=F32).astype(p_ref.dtype)
    lr_ref[...] = jnp.dot(h, wlr_ref[...], preferred_element_type=F32)


def _proj1(x, sc, sh, nw, w_main, w_lr, seq, row_fn):
    t, d = x.shape
    tm = min(seq, 512)
    per_seq = seq // tm
    mod_spec = pl.BlockSpec((None, 1, d), lambda i: (row_fn(i // per_seq), 0, 0))
    tok = lambda n: pl.BlockSpec((tm, n), lambda i: (i, 0))
    return pl.pallas_call(
        _proj1_kernel,
        grid=(t // tm,),
        in_specs=[tok(d), mod_spec, mod_spec, _const_spec((1, d)), _const_spec(w_main.shape), _const_spec(w_lr.shape)],
        out_specs=[tok(w_main.shape[1]), tok(LANES)],
        out_shape=[jax.ShapeDtypeStruct((t, w_main.shape[1]), BF16), jax.ShapeDtypeStruct((t, LANES), F32)],
        compiler_params=_params(("parallel",)),
        name="proj1",
    )(x, sc, sh, nw, w_main, w_lr)


def _log_sigmoid(x):
    return jnp.minimum(x, 0.0) - jnp.log(1.0 + jnp.exp(-jnp.abs(x)))


def _gla_chunk(q, k, v, b, state_t, keep, reverse):
    b_end = b[0:1, :] if reverse else b[GLA_CHUNK - 1:GLA_CHUNK, :]
    q_dec = (q * jnp.exp(b)).astype(BF16)
    k_dec = k * jnp.exp(-b)
    att = lax.dot_general(q_dec, k_dec.astype(BF16), NT, preferred_element_type=F32)
    att = jnp.where(keep, att, 0.0)
    o = jnp.dot(att.astype(BF16), v, preferred_element_type=F32)
    o = o + lax.dot_general(q_dec, state_t.astype(BF16), NT, preferred_element_type=F32)
    e_end = jnp.exp(b_end)
    k_rem = (k_dec * e_end).astype(BF16)
    state_t = state_t * e_end + lax.dot_general(v, k_rem, TN, preferred_element_type=F32)
    return o, state_t


def _gla_kernel(qf_ref, kf_ref, vf_ref, lf_ref, qb_ref, kb_ref, vb_ref, lb_ref, wgf_ref, wgb_ref, bg_ref,
                trif_ref, trib_ref, s0f_ref, s0b_ref, of_ref, ob_ref, sf_ref, sb_ref, stf_ref, stb_ref, *, dk, dv):
    i = pl.program_id(1)
    nb = pl.num_programs(1)
    tc = qf_ref.shape[0]

    @pl.when(i == 0)
    def _init():
        stf_ref[...] = s0f_ref[...].astype(F32)
        stb_ref[...] = s0b_ref[...].astype(F32)

    r = lax.broadcasted_iota(jnp.int32, (GLA_CHUNK, GLA_CHUNK), 0)
    c = lax.broadcasted_iota(jnp.int32, (GLA_CHUNK, GLA_CHUNK), 1)
    scale = dk ** -0.5

    def decay_sums(l_ref, wg_ref, bias, tri_ref):
        gate = _log_sigmoid(jnp.dot(l_ref[...], wg_ref[...], precision=HI, preferred_element_type=F32)
                            + bias) / GLA_GATE_NORM
        total = None
        rest = gate
        for _ in range(3):
            part = rest.astype(BF16)
            rest = rest - part.astype(F32)
            term = jnp.dot(tri_ref[...], part, preferred_element_type=F32)
            total = term if total is None else total + term
        return total

    b_f = decay_sums(lf_ref, wgf_ref, bg_ref[0:1, :], trif_ref)
    b_b = decay_sums(lb_ref, wgb_ref, bg_ref[1:2, :], trib_ref)
    n_chunks = tc // GLA_CHUNK
    states = [[stf_ref[h], stb_ref[h]] for h in range(GLA_HEADS)]
    for step in range(n_chunks):
        for h in range(GLA_HEADS):
            ks, vs = slice(h * dk, (h + 1) * dk), slice(h * dv, (h + 1) * dv)
            rs = slice(step * GLA_CHUNK, (step + 1) * GLA_CHUNK)
            o, states[h][0] = _gla_chunk(qf_ref[rs, ks].astype(F32) * scale, kf_ref[rs, ks].astype(F32),
                                         vf_ref[rs, vs], b_f[rs, ks], states[h][0], c <= r, False)
            of_ref[rs, vs] = o.astype(of_ref.dtype)
            ci = n_chunks - 1 - step
            rs = slice(ci * GLA_CHUNK, (ci + 1) * GLA_CHUNK)
            o, states[h][1] = _gla_chunk(qb_ref[rs, ks].astype(F32) * scale, kb_ref[rs, ks].astype(F32),
                                         vb_ref[rs, vs], b_b[rs, ks], states[h][1], c >= r, True)
            ob_ref[rs, vs] = o.astype(ob_ref.dtype)
    for h in range(GLA_HEADS):
        stf_ref[h] = states[h][0]
        stb_ref[h] = states[h][1]

    @pl.when(i == nb - 1)
    def _emit():
        sf_ref[...] = stf_ref[...]
        sb_ref[...] = stb_ref[...]


def _gla(p, lr, wgf, wgb, bg, s0f, s0b, batch, seq, dk, dv):
    tc = min(seq, 256)
    nb = seq // tc
    hk, hv = GLA_HEADS * dk, GLA_HEADS * dv
    fwd = lambda width, col: pl.BlockSpec((tc, width), lambda b, i: (b * nb + i, col))
    bwd = lambda width, col: pl.BlockSpec((tc, width), lambda b, i: (b * nb + nb - 1 - i, col))
    st_spec = pl.BlockSpec((None, GLA_HEADS, dv, dk), lambda b, i: (b, 0, 0, 0))
    v_col = 2 * hk // hv
    idx = np.arange(tc)
    same = (idx[:, None] // GLA_CHUNK) == (idx[None, :] // GLA_CHUNK)
    tri_f = jnp.asarray(same & (idx[None, :] <= idx[:, None]), BF16)
    tri_b = jnp.asarray(same & (idx[None, :] >= idx[:, None]), BF16)
    return pl.pallas_call(
        functools.partial(_gla_kernel, dk=dk, dv=dv),
        grid=(batch, nb),
        in_specs=[fwd(hk, 0), fwd(hk, 1), fwd(hv, v_col), fwd(LANES, 0),
                  bwd(hk, 0), bwd(hk, 1), bwd(hv, v_col), bwd(LANES, 0),
                  _const_spec(wgf.shape), _const_spec(wgb.shape), _const_spec(bg.shape),
                  _const_spec(tri_f.shape), _const_spec(tri_b.shape), st_spec, st_spec],
        out_specs=[fwd(hv, 0), bwd(hv, 0), st_spec, st_spec],
        out_shape=[jax.ShapeDtypeStruct((batch * seq, hv), BF16)] * 2
        + [jax.ShapeDtypeStruct((batch, GLA_HEADS, dv, dk), F32)] * 2,
        scratch_shapes=[pltpu.VMEM((GLA_HEADS, dv, dk), F32)] * 2,
        compiler_params=_params(("parallel", "arbitrary")),
        name="gla_scan",
    )(p, p, p, lr, p, p, p, lr, wgf, wgb, bg, tri_f, tri_b, s0f, s0b)


def _outproj1_kernel(x_ref, of_ref, ob_ref, go_ref, g_ref, nw_ref, w_ref, out_ref, *, dv):
    o = of_ref[...].astype(F32) + ob_ref[...].astype(F32)
    parts = []
    for h in range(GLA_HEADS):
        oh = o[:, h * dv:(h + 1) * dv]
        ms = jnp.mean(oh * oh, axis=1, keepdims=True)
        parts.append(oh * lax.rsqrt(ms + EPS))
    o = jnp.concatenate(parts, axis=1) * nw_ref[...] * _silu(go_ref[...].astype(F32))
    out_ref[...] = x_ref[...] + g_ref[...] * jnp.dot(o.astype(BF16), w_ref[...], preferred_element_type=F32)


def _outproj1(x, o_f, o_b, p, g, out_norm_w, w_out, seq, row_fn, dv):
    t, d = x.shape
    hv = GLA_HEADS * dv
    tm = min(seq, 512)
    per_seq = seq // tm
    tok = lambda n, col=0: pl.BlockSpec((tm, n), lambda i: (i, col))
    go_col = (p.shape[1] - hv) // hv
    nw = jnp.tile(out_norm_w, GLA_HEADS).reshape(1, hv)
    wb = w_out.astype(BF16)
    return pl.pallas_call(
        functools.partial(_outproj1_kernel, dv=dv),
        grid=(t // tm,),
        in_specs=[tok(d), tok(hv), tok(hv), tok(hv, go_col),
                  pl.BlockSpec((None, 1, d), lambda i: (row_fn(i // per_seq), 0, 0)),
                  _const_spec((1, hv)), _const_spec(wb.shape)],
        out_specs=tok(d),
        out_shape=jax.ShapeDtypeStruct((t, d), F32),
        compiler_params=_params(("parallel",)),
        name="outproj1",
    )(x, o_f, o_b, p, g, nw, wb)


def _mod_rows(mod_l, d):
    return [mod_l[:, j * d:(j + 1) * d].reshape(mod_l.shape[0], 1, d) for j in range(6)]


def _trunk(x, batch, seq, mods, row_fn, shared_mod, norm_w, ab, cp, moe_w, cache, s0, emit_cache):
    d = x.shape[-1]
    t = batch * seq
    x = x.reshape(t, d)
    (w_in0, conv_w, conv_b, filt, hy_skip, qn_w, kn_w, lam_p, subln_w, w_out0) = ab
    (w_in1, gate_up_w, gate_up_b, out_norm_w, w_out1) = cp
    router_w, router_b, wg, wu, wd = moe_w
    width = d // 2
    hy_in = 3 * width

    sh1, sc1, g1, sh2, sc2, g2 = mods[0]
    nseg = DA_HEADS * 2
    qkw = jnp.stack([jnp.tile(qn_w, nseg), jnp.tile(kn_w, nseg)])
    seg = np.arange(DA_HEADS * DA_VD) // DA_DH
    ones = jnp.asarray(seg[:, None] == seg[None, :], BF16)
    v_cols = w_in0[:, hy_in + 2 * DA_HEADS * DA_VD:]
    w_chan = jnp.concatenate([w_in0[:, :hy_in], v_cols], axis=1).T.astype(BF16)
    outs = _proj0(x, sc1, sh1, norm_w[0, 0].reshape(1, d), w_in0[:, hy_in:].astype(BF16), w_chan, qkw, ones,
                  batch, seq, row_fn, rope=cache is not None, emit_cache=emit_cache)
    chan, q, k = outs[:3]
    y_hy = _hyena(chan, conv_w, conv_b, filt, hy_skip, seq, width)
    lam_init = 0.8 - 0.6 * math.exp(-0.3 * 0)
    o = _diff_attention(q, k, chan, hy_in, cache, lam_p, subln_w, batch, seq, lam_init)
    x = _outproj0(x, y_hy, o, g1, w_out0, seq, row_fn)
    x = _moe(x, sc2, sh2, g2, norm_w[0, 1].reshape(1, d), router_w, router_b, wg[0], wu[0], wd[0], seq, row_fn,
             shared_mod)

    sh1, sc1, g1, sh2, sc2, g2 = mods[1]
    dk = d // 2 // GLA_HEADS
    dv = d // GLA_HEADS
    n_main = GLA_HEADS * (2 * dk + 2 * dv)
    w_lr = jnp.pad(w_in1[:, n_main:], ((0, 0), (0, LANES - 2 * GLA_RANK))).astype(BF16)
    p, lr = _proj1(x, sc1, sh1, norm_w[1, 0].reshape(1, d), w_in1[:, :n_main].astype(BF16), w_lr, seq, row_fn)
    wgf = jnp.pad(gate_up_w[0], ((0, LANES - GLA_RANK), (0, 0)))
    wgb = jnp.pad(gate_up_w[1], ((GLA_RANK, LANES - 2 * GLA_RANK), (0, 0)))
    o_f, o_b, s_f, s_b = _gla(p, lr, wgf, wgb, gate_up_b, jnp.swapaxes(s0[0], 2, 3), jnp.swapaxes(s0[1], 2, 3),
                              batch, seq, dk, dv)
    s_f, s_b = jnp.swapaxes(s_f, 2, 3), jnp.swapaxes(s_b, 2, 3)
    x = _outproj1(x, o_f, o_b, p, g1, out_norm_w, w_out1, seq, row_fn, dv)
    x = _moe(x, sc2, sh2, g2, norm_w[1, 1].reshape(1, d), router_w, router_b, wg[1], wu[1], wd[1], seq, row_fn,
             shared_mod)
    return x.reshape(batch, seq, d), outs[3:], (s_f, s_b)


def kernel(x_prompt, x_sample, cache_l0_k, cache_l0_v, state_l1_fwd, state_l1_bwd, c, c_ctx, w_mod, b_mod, norm_w,
           l0_w_in, l0_conv_w, l0_conv_b, l0_filt_w1, l0_filt_b1, l0_filt_freq, l0_filt_w2, l0_filt_b2, l0_filt_w3,
           l0_hy_skip, l0_qn_w, l0_kn_w, l0_lambda, l0_subln_w, l0_w_out, l1_w_in, l1_gate_up_w, l1_gate_up_b,
           l1_out_norm_w, l1_w_out, moe_router_w, moe_router_b, moe_w_gate, moe_w_up, moe_w_down):
    d = x_prompt.shape[-1]
    n_lat = c.shape[0]
    ctx_row = n_lat
    rows = 2 * SUBLANES
    cond = jnp.zeros((rows, d), F32).at[:n_lat].set(c).at[ctx_row].set(c_ctx)
    mod = _modulation(cond, w_mod, b_mod)
    mods = [_mod_rows(mod[l], d) for l in range(mod.shape[0])]
    filt = (l0_filt_w1, l0_filt_b1, l0_filt_freq, l0_filt_w2, l0_filt_b2, l0_filt_w3)
    ab = (l0_w_in, l0_conv_w, l0_conv_b, filt, l0_hy_skip, l0_qn_w, l0_kn_w, l0_lambda, l0_subln_w, l0_w_out)
    cp = (l1_w_in, l1_gate_up_w, l1_gate_up_b, l1_out_norm_w, l1_w_out)
    moe_w = (moe_router_w, moe_router_b, moe_w_gate.astype(BF16), moe_w_up.astype(BF16), moe_w_down.astype(BF16))

    b_ctx, l_ctx = x_prompt.shape[:2]
    zero_state = jnp.zeros((b_ctx,) + state_l1_fwd.shape[1:], F32)
    y_prompt, (new_k, new_v), (new_sf, new_sb) = _trunk(
        x_prompt, b_ctx, l_ctx, mods, lambda b: ctx_row, True, norm_w, ab, cp, moe_w, None,
        (zero_state, zero_state), True)
    b_lat, l_lat = x_sample.shape[:2]
    y_sample, _, _ = _trunk(
        x_sample, b_lat, l_lat, mods, lambda b: b, False, norm_w, ab, cp, moe_w, (cache_l0_k, cache_l0_v),
        (state_l1_fwd, state_l1_bwd), False)
    return (y_prompt, y_sample, new_k, new_v, new_sf, new_sb)
```

```python
import cmath
import functools
import math

import numpy as np
import jax
import jax.numpy as jnp
from jax import lax
from jax.experimental import pallas as pl
from jax.experimental.pallas import tpu as pltpu

F32 = jnp.float32
BF16 = jnp.bfloat16
HI = lax.Precision.HIGHEST
EPS = 1e-6

LANES = 128
SUBLANES = 8
VMEM_LIMIT = 56 << 20

DA_HEADS = 4
DA_DH = 64
DA_VD = 2 * DA_DH
ROPE_NF = DA_DH // 4
ROPE_BASE = 10000.0
GRID_W = 64
HY_BANDS = 16
HY_TARGET = 1e-2
HY_MAX_DECAY = math.log(HY_TARGET) / 0.3
HY_MIN_DECAY = math.log(HY_TARGET) / 1.5
GLA_HEADS = 4
GLA_RANK = 16
GLA_GATE_NORM = 16.0
GLA_CHUNK = 64
N_EXPERTS = 16
N_GROUPS = 4
EPG = N_EXPERTS // N_GROUPS
MOE_TILE = 160
MOE_EXPERTS_PER_STEP = 4
NT = (((1,), (1,)), ((), ()))
TN = (((0,), (0,)), ((), ()))


def _params(sem):
    return pltpu.CompilerParams(dimension_semantics=sem, vmem_limit_bytes=VMEM_LIMIT)


def _const_spec(shape):
    nd = len(shape)
    return pl.BlockSpec(shape, lambda *_: (0,) * nd)


def _silu(x):
    return x * (1.0 / (1.0 + jnp.exp(-x)))


def _norm_mod(x, nw, sc, sh):
    ms = jnp.mean(x * x, axis=-1, keepdims=True)
    return x * lax.rsqrt(ms + EPS) * nw * (1.0 + sc) + sh


def _mod_kernel(c_ref, w_ref, b_ref, o_ref):
    s = _silu(c_ref[...])
    o_ref[...] = jnp.dot(s, w_ref[...], precision=HI, preferred_element_type=F32) + b_ref[...]


def _modulation(cond, w_mod, b_mod):
    depth, d, d6 = w_mod.shape
    r = cond.shape[0]
    tn = 1536
    return pl.pallas_call(
        _mod_kernel,
        grid=(depth, d6 // tn),
        in_specs=[
            pl.BlockSpec((r, d), lambda l, j: (0, 0)),
            pl.BlockSpec((None, d, tn), lambda l, j: (l, 0, j)),
            pl.BlockSpec((None, 1, tn), lambda l, j: (l, 0, j)),
        ],
        out_specs=pl.BlockSpec((None, r, tn), lambda l, j: (l, 0, j)),
        out_shape=jax.ShapeDtypeStruct((depth, r, d6), F32),
        compiler_params=_params(("parallel", "parallel")),
        name="modulation",
    )(cond, w_mod, b_mod.reshape(depth, 1, d6))


def _proj0_kernel(x_ref, sc_ref, sh_ref, nw_ref, wtok_ref, wchan_ref, qkw_ref, ones_ref, cos_ref, sin_ref,
                  chan_ref, q_ref, k_ref, *cache_refs, rope):
    h = _norm_mod(x_ref[...], nw_ref[...], sc_ref[...], sh_ref[...]).astype(BF16)
    chan_ref[...] = lax.dot_general(wchan_ref[...], h, NT, preferred_element_type=F32).astype(chan_ref.dtype)
    p = jnp.dot(h, wtok_ref[...], preferred_element_type=F32)
    w = DA_HEADS * DA_VD
    ones = ones_ref[...]

    def segnorm(z, gain):
        ss = jnp.dot((z * z).astype(BF16), ones, preferred_element_type=F32)
        return z * lax.rsqrt(ss * (1.0 / DA_DH) + EPS) * gain

    q = segnorm(p[:, :w], qkw_ref[0:1, :])
    k = segnorm(p[:, w:2 * w], qkw_ref[1:2, :])
    if cache_refs:
        kn_ref, vn_ref = cache_refs
        v = p[:, 2 * w:]
        for hh in range(DA_HEADS):
            kn_ref[hh] = k[:, hh * DA_VD:(hh + 1) * DA_VD]
            vn_ref[hh] = v[:, hh * DA_VD:(hh + 1) * DA_VD]
    if rope:
        cos = jnp.concatenate([cos_ref[...]] * (w // LANES), axis=1)
        sin = jnp.concatenate([sin_ref[...]] * (w // LANES), axis=1)
        lane = lax.broadcasted_iota(jnp.int32, q.shape, 1)
        first = (lane % (2 * ROPE_NF)) < ROPE_NF

        def rot(z):
            partner = jnp.where(first, pltpu.roll(z, w - ROPE_NF, 1), pltpu.roll(z, ROPE_NF, 1))
            return z * cos + partner * sin

        q = rot(q)
        k = rot(k)
    q_ref[...] = (q * (DA_DH ** -0.5 * math.log2(math.e))).astype(q_ref.dtype)
    k_ref[...] = k.astype(k_ref.dtype)


def _rope_tables(seq):
    t = np.arange(seq)
    lane = np.arange(LANES)
    d = lane % DA_DH
    axis = d // (2 * ROPE_NF)
    part = (d % (2 * ROPE_NF)) // ROPE_NF
    f = d % ROPE_NF
    pos = jnp.where(axis[None, :] == 0, (t // GRID_W)[:, None], (t % GRID_W)[:, None]).astype(F32)
    inv = ROPE_BASE ** (-jnp.arange(ROPE_NF, dtype=F32) / ROPE_NF)
    ang = pos * inv[f][None, :]
    sign = jnp.asarray(np.where(part == 0, -1.0, 1.0), F32)[None, :]
    return jnp.cos(ang), jnp.sin(ang) * sign


def _proj0(x, sc, sh, nw, wqkv, whyt, qkw, ones, batch, seq, row_fn, rope, emit_cache):
    t, d = x.shape
    tm = min(seq, 512)
    per_seq = seq // tm
    w = DA_HEADS * DA_VD
    hy_in = whyt.shape[0]
    if not emit_cache:
        wqkv = wqkv[:, :2 * w]
    if rope:
        cos, sin = _rope_tables(seq)
    else:
        cos = jnp.ones((seq, LANES), F32)
        sin = jnp.zeros((seq, LANES), F32)
    mod_spec = pl.BlockSpec((None, 1, d), lambda i: (row_fn(i // per_seq), 0, 0))
    tab_spec = pl.BlockSpec((tm, LANES), lambda i: (i % per_seq, 0))
    tok = lambda n: pl.BlockSpec((tm, n), lambda i: (i, 0))
    out_specs = [pl.BlockSpec((None, hy_in, tm), lambda i: (i // per_seq, 0, i % per_seq)), tok(w), tok(w)]
    out_shape = [jax.ShapeDtypeStruct((batch, hy_in, seq), BF16)] + [jax.ShapeDtypeStruct((t, w), BF16)] * 2
    if emit_cache:
        cspec = pl.BlockSpec((None, DA_HEADS, tm, DA_VD), lambda i: (i // per_seq, 0, i % per_seq, 0))
        out_specs += [cspec, cspec]
        out_shape += [jax.ShapeDtypeStruct((batch, DA_HEADS, seq, DA_VD), F32)] * 2
    return pl.pallas_call(
        functools.partial(_proj0_kernel, rope=rope),
        grid=(t // tm,),
        in_specs=[tok(d), mod_spec, mod_spec, _const_spec((1, d)), _const_spec(wqkv.shape), _const_spec(whyt.shape),
                  _const_spec(qkw.shape), _const_spec(ones.shape), tab_spec, tab_spec],
        out_specs=out_specs,
        out_shape=out_shape,
        compiler_params=_params(("parallel",)),
        name="proj0",
    )(x, sc, sh, nw, wqkv, whyt, qkw, ones, cos, sin)


def _c_add(a, b, sign=1.0):
    if b is None:
        return a
    if a is None:
        return b if sign > 0 else tuple(None if p is None else -p for p in b)
    out = []
    for pa, pb in zip(a, b):
        if pb is None:
            out.append(pa)
        elif pa is None:
            out.append(pb if sign > 0 else -pb)
        else:
            out.append(pa + pb if sign > 0 else pa - pb)
    return tuple(out)


def _c_mul_const(w, a):
    if a is None:
        return None
    wr = 0.0 if abs(w.real) < 1e-12 else w.real
    wi = 0.0 if abs(w.imag) < 1e-12 else w.imag
    ar, ai = a

    def scaled(c, p):
        if p is None or c == 0.0:
            return None
        if c == 1.0:
            return p
        if c == -1.0:
            return -p
        return c * p

    re = _c_add((scaled(wr, ar),), (scaled(wi, ai),), -1.0)[0]
    im = _c_add((scaled(wr, ai),), (scaled(wi, ar),), 1.0)[0]
    return (re, im)


def _fft_list(vals, sign, first_half_only=False):
    n = len(vals)
    if n == 1:
        return list(vals)
    ev = _fft_list(vals[0::2], sign)
    od = _fft_list(vals[1::2], sign)
    out = [None] * n
    for k in range(n // 2):
        tw = _c_mul_const(cmath.exp(sign * 2j * math.pi * k / n), od[k])
        out[k] = _c_add(ev[k], tw, 1.0)
        if not first_half_only:
            out[k + n // 2] = _c_add(ev[k], tw, -1.0)
    return out


def _dft_mats(n2, total):
    k = np.arange(n2)
    ang = 2.0 * np.pi * np.outer(k, k) / n2
    fr, fi = np.cos(ang), -np.sin(ang)
    fwd = np.block([[fr, fi], [-fi, fr]])
    inv = np.block([[fr, -fi], [fi, fr]]) / total
    return jnp.asarray(fwd, BF16), jnp.asarray(inv, BF16)


def _twiddles(n1, n2):
    ang = 2.0 * np.pi * np.outer(np.arange(n1), np.arange(n2)) / (n1 * n2)
    return jnp.asarray(np.cos(ang), F32), jnp.asarray(-np.sin(ang), F32)


def _fft_split(seq):
    n = 2 * seq
    n2 = min(512, n // 2)
    return n // n2, n2


def _across_fwd(load, n_in, z_ref, twr_ref, twi_ref, n1, n2, rows):
    per_row = n2 // LANES

    def body(it, carry):
        r0 = pl.multiple_of((it // per_row) * SUBLANES, SUBLANES)
        l0 = pl.multiple_of((it % per_row) * LANES, LANES)
        vals = [load(j, r0, l0) for j in range(n_in)] + [None] * (n1 - n_in)
        outs = _fft_list(vals, -1.0)
        for k1 in range(n1):
            twr = twr_ref[pl.ds(k1, 1), pl.ds(l0, LANES)]
            twi = twi_ref[pl.ds(k1, 1), pl.ds(l0, LANES)]
            re, im = outs[k1]
            zero = jnp.zeros((SUBLANES, LANES), F32)
            re = zero if re is None else re
            im = zero if im is None else im
            z_ref[k1, pl.ds(r0, SUBLANES), pl.ds(l0, LANES)] = re * twr - im * twi
            z_ref[k1, pl.ds(r0, SUBLANES), pl.ds(n2 + l0, LANES)] = re * twi + im * twr
        return carry

    lax.fori_loop(0, (rows // SUBLANES) * per_row, body, 0)


def _hy_hidden_kernel(w1_ref, b1_ref, freq_ref, w2_ref, b2_ref, bands_ref, o_ref, *, seq):
    n = 2 * seq
    k = lax.broadcasted_iota(jnp.int32, (n, 1), 0)
    pos = jnp.where(k < seq, k, n - k).astype(F32)
    t = pos / (seq - 1)
    ang = 2.0 * math.pi * pos / seq * bands_ref[...]
    z = (t * w1_ref[0:1, :]
         + jnp.dot(jnp.cos(ang), w1_ref[1:1 + HY_BANDS, :], precision=HI, preferred_element_type=F32)
         + jnp.dot(-jnp.sin(ang), w1_ref[1 + HY_BANDS:, :], precision=HI, preferred_element_type=F32)
         + b1_ref[...])
    hid = jnp.sin(freq_ref[0:1, :] * z)
    hid = jnp.sin(freq_ref[1:2, :] * (jnp.dot(hid, w2_ref[...], precision=HI, preferred_element_type=F32) + b2_ref[...]))
    o_ref[...] = hid


def _hy_hidden(seq, f_w1, f_b1, f_freq, f_w2, f_b2):
    ffn = f_w1.shape[1]
    bands = jnp.linspace(1e-4, HY_BANDS - 1, HY_BANDS, dtype=F32).reshape(1, HY_BANDS)
    args = (f_w1, f_b1.reshape(1, ffn), f_freq, f_w2, f_b2.reshape(1, ffn), bands)
    return pl.pallas_call(
        functools.partial(_hy_hidden_kernel, seq=seq),
        grid=(1,),
        in_specs=[_const_spec(a.shape) for a in args],
        out_specs=_const_spec((2 * seq, ffn)),
        out_shape=jax.ShapeDtypeStruct((2 * seq, ffn), F32),
        compiler_params=_params(("arbitrary",)),
        name="hyena_filter_hidden",
    )(*args)


def _hy_spec_kernel(hid_ref, wb_ref, wa_ref, dec_ref, twr_ref, twi_ref, gf_ref, o_ref, taps_ref, z_ref, *, seq, n1, n2):
    ct = wb_ref.shape[0]
    hid = hid_ref[...]
    back = lax.dot_general(wb_ref[...], hid[:seq], NT, precision=HI, preferred_element_type=F32)
    ahead = lax.dot_general(wa_ref[...], hid[seq:], NT, precision=HI, preferred_element_type=F32)
    lane = lax.broadcasted_iota(jnp.int32, (1, seq), 1)
    dec = dec_ref[...]
    t_back = lane.astype(F32) / (seq - 1)
    t_ahead = (seq - lane).astype(F32) / (seq - 1)
    back = back * jnp.exp(-t_back * dec)
    ahead = jnp.where(lane == 0, 0.0, ahead * jnp.exp(-t_ahead * dec))
    norm = (jnp.sum(jnp.abs(back), axis=1, keepdims=True) + jnp.sum(jnp.abs(ahead), axis=1, keepdims=True)) + EPS
    taps_ref[:, :seq] = back / norm
    taps_ref[:, seq:] = ahead / norm

    def load(j, r0, l0):
        return (taps_ref[pl.ds(r0, SUBLANES), pl.ds(j * n2 + l0, LANES)], None)

    _across_fwd(load, n1, z_ref, twr_ref, twi_ref, n1, n2, ct)
    for k1 in range(n1):
        o_ref[k1] = jnp.dot(z_ref[k1].astype(BF16), gf_ref[...], preferred_element_type=F32)


def _hy_filter_spectra(hid, f_w3, seq, width, ct):
    n1, n2 = _fft_split(seq)
    ffn = f_w3.shape[0]
    w3t = f_w3.T.reshape(4, width, ffn)
    decay = jnp.abs(jnp.linspace(HY_MIN_DECAY, HY_MAX_DECAY, width, dtype=F32)).reshape(width, 1)
    twr, twi = _twiddles(n1, n2)
    gf, _ = _dft_mats(n2, 2 * seq)
    return pl.pallas_call(
        functools.partial(_hy_spec_kernel, seq=seq, n1=n1, n2=n2),
        grid=(2, width // ct),
        in_specs=[
            _const_spec(hid.shape),
            pl.BlockSpec((None, ct, ffn), lambda i, c: (2 * i, c, 0)),
            pl.BlockSpec((None, ct, ffn), lambda i, c: (2 * i + 1, c, 0)),
            pl.BlockSpec((ct, 1), lambda i, c: (c, 0)),
            _const_spec(twr.shape), _const_spec(twi.shape), _const_spec(gf.shape),
        ],
        out_specs=pl.BlockSpec((None, n1, ct, 2 * n2), lambda i, c: (i, 0, c, 0)),
        out_shape=jax.ShapeDtypeStruct((2, n1, width, 2 * n2), F32),
        scratch_shapes=[pltpu.VMEM((ct, 2 * seq), F32), pltpu.VMEM((n1, ct, 2 * n2), F32)],
        compiler_params=_params(("parallel", "parallel")),
        name="hyena_filter_spectra",
    )(hid, w3t, w3t, decay, twr, twi, gf)


def _short_conv_rows(ref, cw, r0, nrows, seq):
    x = ref[pl.ds(r0, nrows), :].astype(F32)
    lane = lax.broadcasted_iota(jnp.int32, x.shape, 1)
    left = jnp.where(lane == 0, 0.0, pltpu.roll(x, 1, 1))
    right = jnp.where(lane == seq - 1, 0.0, pltpu.roll(x, seq - 1, 1))
    return left * cw[:, 0:1] + x * cw[:, 1:2] + right * cw[:, 2:3] + cw[:, 3:4]


def _hy_conv_kernel(uin_ref, gin_ref, cwu_ref, cwg_ref, d_ref, h_ref, twr_ref, twi_ref, gf_ref, gi_ref,
                    o_ref, u_ref, z_ref, *, seq, n1, n2, conv_u):
    ct = uin_ref.shape[1]
    rows16 = 2 * SUBLANES

    def prep(i, carry):
        r0 = pl.multiple_of(i * rows16, rows16)
        cwg = cwg_ref[pl.ds(r0, rows16), :]
        cwu = cwu_ref[pl.ds(r0, rows16), :]
        for e in range(2):
            o_ref[e, pl.ds(r0, rows16), :] = _short_conv_rows(gin_ref.at[e], cwg, r0, rows16, seq)
            if conv_u:
                u_ref[e, pl.ds(r0, rows16), :] = _short_conv_rows(uin_ref.at[e], cwu, r0, rows16, seq)
            else:
                u_ref[e, pl.ds(r0, rows16), :] = uin_ref[e, pl.ds(r0, rows16), :].astype(F32)
        return carry

    lax.fori_loop(0, ct // rows16, prep, 0)

    def load(j, r0, l0):
        return (u_ref[0, pl.ds(r0, SUBLANES), pl.ds(j * n2 + l0, LANES)],
                u_ref[1, pl.ds(r0, SUBLANES), pl.ds(j * n2 + l0, LANES)])

    _across_fwd(load, n1 // 2, z_ref, twr_ref, twi_ref, n1, n2, ct)

    def within(pair, carry):
        ka, kb = 2 * pair, 2 * pair + 1
        zz = jnp.concatenate([z_ref[ka], z_ref[kb]], axis=0)
        y = jnp.dot(zz.astype(BF16), gf_ref[...], preferred_element_type=F32)
        hh = jnp.concatenate([h_ref[ka], h_ref[kb]], axis=0)
        yr, yi = y[:, :n2], y[:, n2:]
        hr, hi = hh[:, :n2], hh[:, n2:]
        p = jnp.concatenate([yr * hr - yi * hi, yr * hi + yi * hr], axis=1)
        q = jnp.dot(p.astype(BF16), gi_ref[...], preferred_element_type=F32)
        z_ref[ka] = q[:ct]
        z_ref[kb] = q[ct:]
        return carry

    lax.fori_loop(0, n1 // 2, within, 0)

    per_row = n2 // LANES

    def finish(it, carry):
        r0 = pl.multiple_of((it // per_row) * SUBLANES, SUBLANES)
        l0 = pl.multiple_of((it % per_row) * LANES, LANES)
        vals = []
        for k1 in range(n1):
            twr = twr_ref[pl.ds(k1, 1), pl.ds(l0, LANES)]
            twi = twi_ref[pl.ds(k1, 1), pl.ds(l0, LANES)]
            re = z_ref[k1, pl.ds(r0, SUBLANES), pl.ds(l0, LANES)]
            im = z_ref[k1, pl.ds(r0, SUBLANES), pl.ds(n2 + l0, LANES)]
            vals.append((re * twr + im * twi, im * twr - re * twi))
        outs = _fft_list(vals, 1.0, first_half_only=True)
        d = d_ref[pl.ds(r0, SUBLANES), :]
        for j in range(n1 // 2):
            sl = (pl.ds(r0, SUBLANES), pl.ds(j * n2 + l0, LANES))
            for e in range(2):
                o_ref[(e,) + sl] = o_ref[(e,) + sl] * (outs[j][e] + u_ref[(e,) + sl] * d)
        return carry

    lax.fori_loop(0, (ct // SUBLANES) * per_row, finish, 0)


def _hy_conv(u, u_off, g, g_off, cw, spectra, conv_idx, d_skip, seq, width, ct, conv_u):
    batch = u.shape[0]
    n1, n2 = _fft_split(seq)
    twr, twi = _twiddles(n1, n2)
    gf, gi = _dft_mats(n2, 2 * seq)
    uo, go = u_off // ct, g_off // ct
    return pl.pallas_call(
        functools.partial(_hy_conv_kernel, seq=seq, n1=n1, n2=n2, conv_u=conv_u),
        grid=(width // ct, batch // 2),
        in_specs=[
            pl.BlockSpec((2, ct, seq), lambda c, p: (p, uo + c, 0)),
            pl.BlockSpec((2, ct, seq), lambda c, p: (p, go + c, 0)),
            pl.BlockSpec((ct, 4), lambda c, p: (uo + c, 0)),
            pl.BlockSpec((ct, 4), lambda c, p: (go + c, 0)),
            pl.BlockSpec((ct, 1), lambda c, p: (c, 0)),
            pl.BlockSpec((None, n1, ct, 2 * n2), lambda c, p: (conv_idx, 0, c, 0), pipeline_mode=pl.Buffered(1)),
            _const_spec(twr.shape), _const_spec(twi.shape), _const_spec(gf.shape), _const_spec(gi.shape),
        ],
        out_specs=pl.BlockSpec((2, ct, seq), lambda c, p: (p, c, 0)),
        out_shape=jax.ShapeDtypeStruct((batch, width, seq), F32),
        scratch_shapes=[pltpu.VMEM((2, ct, seq), F32), pltpu.VMEM((n1, ct, 2 * n2), F32)],
        compiler_params=_params(("arbitrary", "arbitrary")),
        name="hyena_conv",
    )(u, g, cw, cw, d_skip, spectra, twr, twi, gf, gi)


def _hyena(hyt, conv_w, conv_b, filt, hy_skip, seq, width):
    f_w1, f_b1, f_freq, f_w2, f_b2, f_w3 = filt
    ct = LANES
    hid = _hy_hidden(seq, f_w1, f_b1, f_freq, f_w2, f_b2)
    spectra = _hy_filter_spectra(hid, f_w3, seq, width, ct)
    cw = jnp.concatenate([conv_w.T, conv_b[:, None]], axis=1)
    y1 = _hy_conv(hyt, 0, hyt, width, cw, spectra, 0, hy_skip[0].reshape(width, 1), seq, width, ct, True)
    return _hy_conv(y1, 0, hyt, 2 * width, cw, spectra, 1, hy_skip[1].reshape(width, 1), seq, width, ct, False)


def _attn_kernel(q_ref, k_ref, vt_ref, *rest, lam_init, with_cache, kc):
    if with_cache:
        kc_ref, vct_ref, lam_ref, sw_ref, o_ref, kmax_ref = rest
    else:
        lam_ref, sw_ref, o_ref, kmax_ref = rest
    lp = lam_ref[...]
    lam = (jnp.exp(jnp.sum(lp[0:1] * lp[1:2], axis=1, keepdims=True))
           - jnp.exp(jnp.sum(lp[2:3] * lp[3:4], axis=1, keepdims=True)) + lam_init)
    q = q_ref[...]
    lane = lax.broadcasted_iota(jnp.int32, q.shape, 1)
    zero = jnp.zeros_like(q)
    qs = (jnp.where(lane < DA_DH, q, zero), jnp.where(lane >= DA_DH, q, zero))
    seq = k_ref.shape[0]
    chunks = [(k_ref[c * kc:(c + 1) * kc, :], vt_ref[:, c * kc:(c + 1) * kc]) for c in range(seq // kc)]
    if with_cache:
        chunks.append((kc_ref[...].astype(BF16), vct_ref[...].astype(BF16)))
    tq = q.shape[0]
    sub = lax.broadcasted_iota(jnp.int32, (SUBLANES, DA_VD), 0)
    ln = lax.broadcasted_iota(jnp.int32, (SUBLANES, DA_VD), 1)
    pick = jnp.where((ln // DA_DH) == sub, 1.0, 0.0).astype(BF16)

    def sq_norms(x):
        xf = x.astype(F32)
        return lax.dot_general(pick, (xf * xf).astype(BF16), NT, preferred_element_type=F32)

    @pl.when(pl.program_id(2) == 0)
    def _key_norms():
        best = None
        for kk, _ in chunks:
            cur = jnp.max(sq_norms(kk), axis=1, keepdims=True)
            best = cur if best is None else jnp.maximum(best, cur)
        kmax_ref[...] = jnp.broadcast_to(best, kmax_ref.shape)

    def scores(ci):
        return [lax.dot_general(chunks[ci][0], qm, NT, preferred_element_type=F32) for qm in qs]

    def finish(acc1, l1, acc2, l2):
        o = acc1 * (1.0 / l1) - acc2 * (lam / l2)
        ms = jnp.mean(o * o, axis=0, keepdims=True)
        o_ref[...] = (o * lax.rsqrt(ms + EPS) * sw_ref[...] * (1.0 - lam_init)).astype(o_ref.dtype)

    bound = jnp.sqrt(sq_norms(q) * kmax_ref[:, 0:1]) * 1.02
    lsum = [jnp.zeros((SUBLANES, tq), F32), jnp.zeros((SUBLANES, tq), F32)]
    acc = [None, None]
    s_next = scores(0)
    for ci, (kk, vv) in enumerate(chunks):
        s_cur = s_next
        if ci + 1 < len(chunks):
            s_next = scores(ci + 1)
        for mi in range(2):
            p = jnp.exp2(s_cur[mi] - bound[mi:mi + 1, :])
            lsum[mi] = lsum[mi] + jnp.sum(p.reshape(-1, SUBLANES, tq), axis=0)
            part = jnp.dot(vv, p.astype(BF16), preferred_element_type=F32)
            acc[mi] = part if acc[mi] is None else acc[mi] + part
    l1 = jnp.sum(lsum[0], axis=0, keepdims=True)
    l2 = jnp.sum(lsum[1], axis=0, keepdims=True)
    finish(acc[0], l1, acc[1], l2)
    safe = jnp.min(jnp.minimum(l1, l2)) > 2.0 ** -60

    @pl.when(jnp.logical_not(safe))
    def _online():
        state = [None, None]
        for ci, (kk, vv) in enumerate(chunks):
            s_cur = scores(ci)
            for mi in range(2):
                s = s_cur[mi]
                cmax = jnp.max(s, axis=0, keepdims=True)
                if state[mi] is None:
                    m = cmax
                    p = jnp.exp2(s - m)
                    l = jnp.sum(p, axis=0, keepdims=True)
                    a = jnp.dot(vv, p.astype(BF16), preferred_element_type=F32)
                else:
                    m_old, l, a = state[mi]
                    m = jnp.maximum(m_old, cmax)
                    alpha = jnp.exp2(m_old - m)
                    p = jnp.exp2(s - m)
                    l = l * alpha + jnp.sum(p, axis=0, keepdims=True)
                    a = a * alpha + jnp.dot(vv, p.astype(BF16), preferred_element_type=F32)
                state[mi] = (m, l, a)
        finish(state[0][2], state[0][1], state[1][2], state[1][1])


def _diff_attention(q, k, chan, v_row0, cache, lam_p, subln_w, batch, seq, lam_init):
    tq = min(seq, 256)
    kc = min(seq, 512)
    nq = seq // tq
    vb = v_row0 // DA_VD
    grid = (batch, DA_HEADS, nq)
    in_specs = [pl.BlockSpec((tq, DA_VD), lambda b, h, i: (b * nq + i, h)),
                pl.BlockSpec((seq, DA_VD), lambda b, h, i: (b, h)),
                pl.BlockSpec((None, DA_VD, seq), lambda b, h, i: (b, vb + h, 0))]
    args = [q, k, chan]
    if cache is not None:
        past = cache[0].shape[2]
        in_specs += [pl.BlockSpec((None, None, past, DA_VD), lambda b, h, i: (b, h, 0, 0)),
                     pl.BlockSpec((None, None, DA_VD, past), lambda b, h, i: (b, h, 0, 0))]
        args += [cache[0], jnp.swapaxes(cache[1], 2, 3)]
    in_specs += [_const_spec(lam_p.shape), _const_spec((DA_VD, 1))]
    args += [lam_p, subln_w.reshape(DA_VD, 1)]
    return pl.pallas_call(
        functools.partial(_attn_kernel, lam_init=lam_init, with_cache=cache is not None, kc=kc),
        grid=grid,
        in_specs=in_specs,
        out_specs=pl.BlockSpec((None, DA_VD, tq), lambda b, h, i: (b, h, i)),
        out_shape=jax.ShapeDtypeStruct((batch, DA_HEADS * DA_VD, seq), BF16),
        scratch_shapes=[pltpu.VMEM((SUBLANES, LANES), F32)],
        compiler_params=_params(("parallel", "parallel", "arbitrary")),
        name="diff_attention",
    )(*args)


def _outproj0_kernel(x_ref, yt_ref, ot_ref, g_ref, wy_ref, wo_ref, out_ref):
    mix = lax.dot_general(yt_ref[...].astype(BF16), wy_ref[...], TN, preferred_element_type=F32)
    mix = mix + lax.dot_general(ot_ref[...], wo_ref[...], TN, preferred_element_type=F32)
    out_ref[...] = x_ref[...] + g_ref[...] * mix


def _outproj0(x, yt, ot, g, w_out, seq, row_fn):
    t, d = x.shape
    width = yt.shape[1]
    tm = min(seq, 512)
    per_seq = seq // tm
    wy = w_out[:width].astype(BF16)
    wo = w_out[width:].astype(BF16)
    tok = pl.BlockSpec((tm, d), lambda i: (i, 0))
    chan = lambda n: pl.BlockSpec((None, n, tm), lambda i: (i // per_seq, 0, i % per_seq))
    return pl.pallas_call(
        _outproj0_kernel,
        grid=(t // tm,),
        in_specs=[tok, chan(width), chan(ot.shape[1]),
                  pl.BlockSpec((None, 1, d), lambda i: (row_fn(i // per_seq), 0, 0)),
                  _const_spec(wy.shape), _const_spec(wo.shape)],
        out_specs=tok,
        out_shape=jax.ShapeDtypeStruct((t, d), F32),
        compiler_params=_params(("parallel",)),
        name="outproj0",
    )(x, yt, ot, g, wy, wo)


def _top2_of_rows(rows):
    n = len(rows)
    v1 = functools.reduce(jnp.maximum, rows)
    i1 = jnp.full(rows[0].shape, n - 1, jnp.int32)
    for j in range(n - 2, -1, -1):
        i1 = jnp.where(rows[j] == v1, j, i1)
    masked = [jnp.where(i1 == j, -jnp.inf, rows[j]) for j in range(n)]
    v2 = functools.reduce(jnp.maximum, masked)
    i2 = jnp.full(rows[0].shape, n - 1, jnp.int32)
    for j in range(n - 2, -1, -1):
        i2 = jnp.where(masked[j] == v2, j, i2)
    return v1, i1, v2, i2


def _select_rows(idx, rows):
    out = rows[-1]
    for j in range(len(rows) - 2, -1, -1):
        out = jnp.where(idx == j, rows[j], out)
    return out


def _moe_kernel(x_ref, sc_ref, sh_ref, g_ref, nw_ref, rwt_ref, rb_ref, upper_ref, wg_ref, wu_ref, wd_ref,
                out_ref, hb_ref, rank_ref, comb_ref, acc_ref, sel_ref, y_ref, cnt_ref):
    step = pl.program_id(1)
    tb = x_ref.shape[0]

    @pl.when(step == 0)
    def _route():
        h = _norm_mod(x_ref[...], nw_ref[...], sc_ref[...], sh_ref[...])
        hb = h.astype(BF16)
        hb_ref[...] = hb
        h_lo = (h - hb.astype(F32)).astype(BF16)
        logits = (lax.dot_general(rwt_ref[0], hb, NT, preferred_element_type=F32)
                  + lax.dot_general(rwt_ref[0], h_lo, NT, preferred_element_type=F32)
                  + lax.dot_general(rwt_ref[1], hb, NT, preferred_element_type=F32))
        s = 1.0 / (1.0 + jnp.exp(-logits))
        s_sel = s + rb_ref[...]
        groups = tb // LANES

        def pack(row):
            return jnp.concatenate([row[:, c * LANES:(c + 1) * LANES] for c in range(groups)], axis=0)

        def unpack(tile):
            return jnp.concatenate([tile[c:c + 1, :] for c in range(groups)], axis=1)

        s_rows = [pack(s[j:j + 1, :]) for j in range(N_EXPERTS)]
        sel_rows = [pack(s_sel[j:j + 1, :]) for j in range(N_EXPERTS)]
        tops = [_top2_of_rows(sel_rows[g * EPG:(g + 1) * EPG]) for g in range(N_GROUPS)]
        scores = [tp[0] + tp[2] for tp in tops]
        best = functools.reduce(jnp.maximum, scores)
        g_idx = jnp.full(best.shape, N_GROUPS - 1, jnp.int32)
        for g in range(N_GROUPS - 2, -1, -1):
            g_idx = jnp.where(scores[g] == best, g, g_idx)
        e1 = g_idx * EPG + _select_rows(g_idx, [tp[1] for tp in tops])
        e2 = g_idx * EPG + _select_rows(g_idx, [tp[3] for tp in tops])
        w1 = _select_rows(e1, s_rows)
        w2 = _select_rows(e2, s_rows)
        tot = w1 + w2
        w1, w2 = w1 / tot, w2 / tot
        masks = [jnp.where((e1 == j) | (e2 == j), 1.0, 0.0) for j in range(N_EXPERTS)]
        comb_ref[...] = jnp.concatenate(
            [unpack(jnp.where(e1 == j, w1, 0.0) + jnp.where(e2 == j, w2, 0.0)) for j in range(N_EXPERTS)], axis=0)
        mask = jnp.concatenate([unpack(m) for m in masks], axis=0)
        rank = jnp.dot(mask.astype(BF16), upper_ref[...], preferred_element_type=F32)
        rank_ref[...] = jnp.where(mask > 0.0, rank, -1.0)
        for j in range(N_EXPERTS):
            cnt_ref[j] = jnp.sum(masks[j]).astype(jnp.int32)
        acc_ref[...] = jnp.zeros_like(acc_ref)

    def expert_tile(j, tile_idx):
        e = step * MOE_EXPERTS_PER_STEP + j
        rank = rank_ref[pl.ds(e, 1), :]
        comb = comb_ref[pl.ds(e, 1), :]
        slot = (lax.broadcasted_iota(jnp.int32, (MOE_TILE, tb), 0) + tile_idx * MOE_TILE).astype(F32)
        hit = rank == slot
        onehot = jnp.where(hit, 1.0, 0.0).astype(BF16)
        xe = jnp.dot(onehot, hb_ref[...], preferred_element_type=F32).astype(BF16)
        gate = jnp.dot(xe, wg_ref[j], preferred_element_type=F32)
        up = jnp.dot(xe, wu_ref[j], preferred_element_type=F32)
        y = jnp.dot((_silu(gate) * up).astype(BF16), wd_ref[j], preferred_element_type=F32)
        w_slot = jnp.sum(jnp.where(hit, comb, 0.0), axis=1, keepdims=True)
        return onehot, (y * w_slot).astype(BF16)

    for j in range(MOE_EXPERTS_PER_STEP):
        onehot, yw = expert_tile(j, 0)
        sel_ref[j * MOE_TILE:(j + 1) * MOE_TILE, :] = onehot
        y_ref[j * MOE_TILE:(j + 1) * MOE_TILE, :] = yw

        def overflow(tile_idx, carry, j=j):
            onehot, yw = expert_tile(j, tile_idx)
            acc_ref[...] += lax.dot_general(onehot, yw, TN, preferred_element_type=F32)
            return carry

        n_tiles = (cnt_ref[step * MOE_EXPERTS_PER_STEP + j] + MOE_TILE - 1) // MOE_TILE
        lax.fori_loop(1, n_tiles, overflow, 0)
    acc_ref[...] += lax.dot_general(sel_ref[...], y_ref[...], TN, preferred_element_type=F32)

    @pl.when(step == pl.num_programs(1) - 1)
    def _finish():
        out_ref[...] = x_ref[...] + g_ref[...] * acc_ref[...]


def _moe(x, sc, sh, g, nw, router_w, router_b, w_gate, w_up, w_down, seq, row_fn, shared_mod):
    t, d = x.shape
    tb = min(t if shared_mod else seq, 1024)
    per_seq = t if shared_mod else seq // tb
    dff = w_gate.shape[-1]
    upper = jnp.asarray(np.triu(np.ones((tb, tb), np.float32), 1), BF16)
    mod_spec = pl.BlockSpec((None, 1, d), lambda i, e: (row_fn(i // per_seq), 0, 0))
    tok = pl.BlockSpec((tb, d), lambda i, e: (i, 0))
    per = MOE_EXPERTS_PER_STEP
    stacked = per * MOE_TILE
    rw_hi = router_w.T.astype(BF16)
    rwt = jnp.stack([rw_hi, (router_w.T - rw_hi.astype(F32)).astype(BF16)])
    return pl.pallas_call(
        _moe_kernel,
        grid=(t // tb, N_EXPERTS // per),
        in_specs=[tok, mod_spec, mod_spec, mod_spec, _const_spec((1, d)), _const_spec((2, N_EXPERTS, d)),
                  _const_spec((N_EXPERTS, 1)), _const_spec(upper.shape),
                  pl.BlockSpec((per, d, dff), lambda i, e: (e, 0, 0)),
                  pl.BlockSpec((per, d, dff), lambda i, e: (e, 0, 0)),
                  pl.BlockSpec((per, dff, d), lambda i, e: (e, 0, 0))],
        out_specs=tok,
        out_shape=jax.ShapeDtypeStruct((t, d), F32),
        scratch_shapes=[pltpu.VMEM((tb, d), BF16), pltpu.VMEM((N_EXPERTS, tb), F32), pltpu.VMEM((N_EXPERTS, tb), F32),
                        pltpu.VMEM((tb, d), F32), pltpu.VMEM((stacked, tb), BF16), pltpu.VMEM((stacked, d), BF16),
                        pltpu.SMEM((N_EXPERTS,), jnp.int32)],
        compiler_params=_params(("parallel", "arbitrary")),
        name="moe",
    )(x, sc, sh, g, nw, rwt, router_b.reshape(N_EXPERTS, 1), upper, w_gate, w_up, w_down)


def _proj1_kernel(x_ref, sc_ref, sh_ref, nw_ref, w_ref, wlr_ref, p_ref, lr_ref):
    h = _norm_mod(x_ref[...], nw_ref[...], sc_ref[...], sh_ref[...]).astype(BF16)
    p_ref[...] = jnp.dot(h, w_ref[...], preferred_element_type=F32).astype(p_ref.dtype)
    lr_ref[...] = jnp.dot(h, wlr_ref[...], preferred_element_type=F32)


def _proj1(x, sc, sh, nw, w_main, w_lr, seq, row_fn):
    t, d = x.shape
    tm = min(seq, 512)
    per_seq = seq // tm
    mod_spec = pl.BlockSpec((None, 1, d), lambda i: (row_fn(i // per_seq), 0, 0))
    tok = lambda n: pl.BlockSpec((tm, n), lambda i: (i, 0))
    return pl.pallas_call(
        _proj1_kernel,
        grid=(t // tm,),
        in_specs=[tok(d), mod_spec, mod_spec, _const_spec((1, d)), _const_spec(w_main.shape), _const_spec(w_lr.shape)],
        out_specs=[tok(w_main.shape[1]), tok(LANES)],
        out_shape=[jax.ShapeDtypeStruct((t, w_main.shape[1]), BF16), jax.ShapeDtypeStruct((t, LANES), F32)],
        compiler_params=_params(("parallel",)),
        name="proj1",
    )(x, sc, sh, nw, w_main, w_lr)


def _log_sigmoid(x):
    return jnp.minimum(x, 0.0) - jnp.log(1.0 + jnp.exp(-jnp.abs(x)))


def _gla_chunk(q, k, v, b, state_t, keep, reverse):
    b_end = b[0:1, :] if reverse else b[GLA_CHUNK - 1:GLA_CHUNK, :]
    q_dec = (q * jnp.exp(b)).astype(BF16)
    k_dec = k * jnp.exp(-b)
    att = lax.dot_general(q_dec, k_dec.astype(BF16), NT, preferred_element_type=F32)
    att = jnp.where(keep, att, 0.0)
    o = jnp.dot(att.astype(BF16), v, preferred_element_type=F32)
    o = o + lax.dot_general(q_dec, state_t.astype(BF16), NT, preferred_element_type=F32)
    e_end = jnp.exp(b_end)
    k_rem = (k_dec * e_end).astype(BF16)
    state_t = state_t * e_end + lax.dot_general(v, k_rem, TN, preferred_element_type=F32)
    return o, state_t


def _gla_kernel(qf_ref, kf_ref, vf_ref, lf_ref, qb_ref, kb_ref, vb_ref, lb_ref, wgf_ref, wgb_ref, bg_ref,
                trif_ref, trib_ref, s0f_ref, s0b_ref, of_ref, ob_ref, sf_ref, sb_ref, stf_ref, stb_ref, *, dk, dv):
    i = pl.program_id(1)
    nb = pl.num_programs(1)
    tc = qf_ref.shape[0]

    @pl.when(i == 0)
    def _init():
        stf_ref[...] = s0f_ref[...].astype(F32)
        stb_ref[...] = s0b_ref[...].astype(F32)

    r = lax.broadcasted_iota(jnp.int32, (GLA_CHUNK, GLA_CHUNK), 0)
    c = lax.broadcasted_iota(jnp.int32, (GLA_CHUNK, GLA_CHUNK), 1)
    scale = dk ** -0.5

    def decay_sums(l_ref, wg_ref, bias, tri_ref):
        gate = _log_sigmoid(jnp.dot(l_ref[...], wg_ref[...], precision=HI, preferred_element_type=F32)
                            + bias) / GLA_GATE_NORM
        total = None
        rest = gate
        for _ in range(3):
            part = rest.astype(BF16)
            rest = rest - part.astype(F32)
            term = jnp.dot(tri_ref[...], part, preferred_element_type=F32)
            total = term if total is None else total + term
        return total

    b_f = decay_sums(lf_ref, wgf_ref, bg_ref[0:1, :], trif_ref)
    b_b = decay_sums(lb_ref, wgb_ref, bg_ref[1:2, :], trib_ref)
    n_chunks = tc // GLA_CHUNK
    states = [[stf_ref[h], stb_ref[h]] for h in range(GLA_HEADS)]
    for step in range(n_chunks):
        for h in range(GLA_HEADS):
            ks, vs = slice(h * dk, (h + 1) * dk), slice(h * dv, (h + 1) * dv)
            rs = slice(step * GLA_CHUNK, (step + 1) * GLA_CHUNK)
            o, states[h][0] = _gla_chunk(qf_ref[rs, ks].astype(F32) * scale, kf_ref[rs, ks].astype(F32),
                                         vf_ref[rs, vs], b_f[rs, ks], states[h][0], c <= r, False)
            of_ref[rs, vs] = o.astype(of_ref.dtype)
            ci = n_chunks - 1 - step
            rs = slice(ci * GLA_CHUNK, (ci + 1) * GLA_CHUNK)
            o, states[h][1] = _gla_chunk(qb_ref[rs, ks].astype(F32) * scale, kb_ref[rs, ks].astype(F32),
                                         vb_ref[rs, vs], b_b[rs, ks], states[h][1], c >= r, True)
            ob_ref[rs, vs] = o.astype(ob_ref.dtype)
    for h in range(GLA_HEADS):
        stf_ref[h] = states[h][0]
        stb_ref[h] = states[h][1]

    @pl.when(i == nb - 1)
    def _emit():
        sf_ref[...] = stf_ref[...]
        sb_ref[...] = stb_ref[...]


def _gla(p, lr, wgf, wgb, bg, s0f, s0b, batch, seq, dk, dv):
    tc = min(seq, 256)
    nb = seq // tc
    hk, hv = GLA_HEADS * dk, GLA_HEADS * dv
    fwd = lambda width, col: pl.BlockSpec((tc, width), lambda b, i: (b * nb + i, col))
    bwd = lambda width, col: pl.BlockSpec((tc, width), lambda b, i: (b * nb + nb - 1 - i, col))
    st_spec = pl.BlockSpec((None, GLA_HEADS, dv, dk), lambda b, i: (b, 0, 0, 0))
    v_col = 2 * hk // hv
    idx = np.arange(tc)
    same = (idx[:, None] // GLA_CHUNK) == (idx[None, :] // GLA_CHUNK)
    tri_f = jnp.asarray(same & (idx[None, :] <= idx[:, None]), BF16)
    tri_b = jnp.asarray(same & (idx[None, :] >= idx[:, None]), BF16)
    return pl.pallas_call(
        functools.partial(_gla_kernel, dk=dk, dv=dv),
        grid=(batch, nb),
        in_specs=[fwd(hk, 0), fwd(hk, 1), fwd(hv, v_col), fwd(LANES, 0),
                  bwd(hk, 0), bwd(hk, 1), bwd(hv, v_col), bwd(LANES, 0),
                  _const_spec(wgf.shape), _const_spec(wgb.shape), _const_spec(bg.shape),
                  _const_spec(tri_f.shape), _const_spec(tri_b.shape), st_spec, st_spec],
        out_specs=[fwd(hv, 0), bwd(hv, 0), st_spec, st_spec],
        out_shape=[jax.ShapeDtypeStruct((batch * seq, hv), BF16)] * 2
        + [jax.ShapeDtypeStruct((batch, GLA_HEADS, dv, dk), F32)] * 2,
        scratch_shapes=[pltpu.VMEM((GLA_HEADS, dv, dk), F32)] * 2,
        compiler_params=_params(("parallel", "arbitrary")),
        name="gla_scan",
    )(p, p, p, lr, p, p, p, lr, wgf, wgb, bg, tri_f, tri_b, s0f, s0b)


def _outproj1_kernel(x_ref, of_ref, ob_ref, go_ref, g_ref, nw_ref, w_ref, out_ref, *, dv):
    o = of_ref[...].astype(F32) + ob_ref[...].astype(F32)
    parts = []
    for h in range(GLA_HEADS):
        oh = o[:, h * dv:(h + 1) * dv]
        ms = jnp.mean(oh * oh, axis=1, keepdims=True)
        parts.append(oh * lax.rsqrt(ms + EPS))
    o = jnp.concatenate(parts, axis=1) * nw_ref[...] * _silu(go_ref[...].astype(F32))
    out_ref[...] = x_ref[...] + g_ref[...] * jnp.dot(o.astype(BF16), w_ref[...], preferred_element_type=F32)


def _outproj1(x, o_f, o_b, p, g, out_norm_w, w_out, seq, row_fn, dv):
    t, d = x.shape
    hv = GLA_HEADS * dv
    tm = min(seq, 512)
    per_seq = seq // tm
    tok = lambda n, col=0: pl.BlockSpec((tm, n), lambda i: (i, col))
    go_col = (p.shape[1] - hv) // hv
    nw = jnp.tile(out_norm_w, GLA_HEADS).reshape(1, hv)
    wb = w_out.astype(BF16)
    return pl.pallas_call(
        functools.partial(_outproj1_kernel, dv=dv),
        grid=(t // tm,),
        in_specs=[tok(d), tok(hv), tok(hv), tok(hv, go_col),
                  pl.BlockSpec((None, 1, d), lambda i: (row_fn(i // per_seq), 0, 0)),
                  _const_spec((1, hv)), _const_spec(wb.shape)],
        out_specs=tok(d),
        out_shape=jax.ShapeDtypeStruct((t, d), F32),
        compiler_params=_params(("parallel",)),
        name="outproj1",
    )(x, o_f, o_b, p, g, nw, wb)


def _mod_rows(mod_l, d):
    return [mod_l[:, j * d:(j + 1) * d].reshape(mod_l.shape[0], 1, d) for j in range(6)]


def _trunk(x, batch, seq, mods, row_fn, shared_mod, norm_w, ab, cp, moe_w, cache, s0, emit_cache):
    d = x.shape[-1]
    t = batch * seq
    x = x.reshape(t, d)
    (w_in0, conv_w, conv_b, filt, hy_skip, qn_w, kn_w, lam_p, subln_w, w_out0) = ab
    (w_in1, gate_up_w, gate_up_b, out_norm_w, w_out1) = cp
    router_w, router_b, wg, wu, wd = moe_w
    width = d // 2
    hy_in = 3 * width

    sh1, sc1, g1, sh2, sc2, g2 = mods[0]
    nseg = DA_HEADS * 2
    qkw = jnp.stack([jnp.tile(qn_w, nseg), jnp.tile(kn_w, nseg)])
    seg = np.arange(DA_HEADS * DA_VD) // DA_DH
    ones = jnp.asarray(seg[:, None] == seg[None, :], BF16)
    v_cols = w_in0[:, hy_in + 2 * DA_HEADS * DA_VD:]
    w_chan = jnp.concatenate([w_in0[:, :hy_in], v_cols], axis=1).T.astype(BF16)
    outs = _proj0(x, sc1, sh1, norm_w[0, 0].reshape(1, d), w_in0[:, hy_in:].astype(BF16), w_chan, qkw, ones,
                  batch, seq, row_fn, rope=cache is not None, emit_cache=emit_cache)
    chan, q, k = outs[:3]
    y_hy = _hyena(chan, conv_w, conv_b, filt, hy_skip, seq, width)
    lam_init = 0.8 - 0.6 * math.exp(-0.3 * 0)
    o = _diff_attention(q, k, chan, hy_in, cache, lam_p, subln_w, batch, seq, lam_init)
    x = _outproj0(x, y_hy, o, g1, w_out0, seq, row_fn)
    x = _moe(x, sc2, sh2, g2, norm_w[0, 1].reshape(1, d), router_w, router_b, wg[0], wu[0], wd[0], seq, row_fn,
             shared_mod)

    sh1, sc1, g1, sh2, sc2, g2 = mods[1]
    dk = d // 2 // GLA_HEADS
    dv = d // GLA_HEADS
    n_main = GLA_HEADS * (2 * dk + 2 * dv)
    w_lr = jnp.pad(w_in1[:, n_main:], ((0, 0), (0, LANES - 2 * GLA_RANK))).astype(BF16)
    p, lr = _proj1(x, sc1, sh1, norm_w[1, 0].reshape(1, d), w_in1[:, :n_main].astype(BF16), w_lr, seq, row_fn)
    wgf = jnp.pad(gate_up_w[0], ((0, LANES - GLA_RANK), (0, 0)))
    wgb = jnp.pad(gate_up_w[1], ((GLA_RANK, LANES - 2 * GLA_RANK), (0, 0)))
    o_f, o_b, s_f, s_b = _gla(p, lr, wgf, wgb, gate_up_b, jnp.swapaxes(s0[0], 2, 3), jnp.swapaxes(s0[1], 2, 3),
                              batch, seq, dk, dv)
    s_f, s_b = jnp.swapaxes(s_f, 2, 3), jnp.swapaxes(s_b, 2, 3)
    x = _outproj1(x, o_f, o_b, p, g1, out_norm_w, w_out1, seq, row_fn, dv)
    x = _moe(x, sc2, sh2, g2, norm_w[1, 1].reshape(1, d), router_w, router_b, wg[1], wu[1], wd[1], seq, row_fn,
             shared_mod)
    return x.reshape(batch, seq, d), outs[3:], (s_f, s_b)


def kernel(x_prompt, x_sample, cache_l0_k, cache_l0_v, state_l1_fwd, state_l1_bwd, c, c_ctx, w_mod, b_mod, norm_w,
           l0_w_in, l0_conv_w, l0_conv_b, l0_filt_w1, l0_filt_b1, l0_filt_freq, l0_filt_w2, l0_filt_b2, l0_filt_w3,
           l0_hy_skip, l0_qn_w, l0_kn_w, l0_lambda, l0_subln_w, l0_w_out, l1_w_in, l1_gate_up_w, l1_gate_up_b,
           l1_out_norm_w, l1_w_out, moe_router_w, moe_router_b, moe_w_gate, moe_w_up, moe_w_down):
    d = x_prompt.shape[-1]
    n_lat = c.shape[0]
    ctx_row = n_lat
    rows = 2 * SUBLANES
    cond = jnp.zeros((rows, d), F32).at[:n_lat].set(c).at[ctx_row].set(c_ctx)
    mod = _modulation(cond, w_mod, b_mod)
    mods = [_mod_rows(mod[l], d) for l in range(mod.shape[0])]
    filt = (l0_filt_w1, l0_filt_b1, l0_filt_freq, l0_filt_w2, l0_filt_b2, l0_filt_w3)
    ab = (l0_w_in, l0_conv_w, l0_conv_b, filt, l0_hy_skip, l0_qn_w, l0_kn_w, l0_lambda, l0_subln_w, l0_w_out)
    cp = (l1_w_in, l1_gate_up_w, l1_gate_up_b, l1_out_norm_w, l1_w_out)
    moe_w = (moe_router_w, moe_router_b, moe_w_gate.astype(BF16), moe_w_up.astype(BF16), moe_w_down.astype(BF16))

    b_ctx, l_ctx = x_prompt.shape[:2]
    zero_state = jnp.zeros((b_ctx,) + state_l1_fwd.shape[1:], F32)
    y_prompt, (new_k, new_v), (new_sf, new_sb) = _trunk(
        x_prompt, b_ctx, l_ctx, mods, lambda b: ctx_row, True, norm_w, ab, cp, moe_w, None,
        (zero_state, zero_state), True)
    b_lat, l_lat = x_sample.shape[:2]
    y_sample, _, _ = _trunk(
        x_sample, b_lat, l_lat, mods, lambda b: b, False, norm_w, ab, cp, moe_w, (cache_l0_k, cache_l0_v),
        (state_l1_fwd, state_l1_bwd), False)
    return (y_prompt, y_sample, new_k, new_v, new_sf, new_sb)
```

```python
import cmath
import functools
import math

import numpy as np
import jax
import jax.numpy as jnp
from jax import lax
from jax.experimental import pallas as pl
from jax.experimental.pallas import tpu as pltpu

F32 = jnp.float32
BF16 = jnp.bfloat16
HI = lax.Precision.HIGHEST
EPS = 1e-6

LANES = 128
SUBLANES = 8
VMEM_LIMIT = 56 << 20

DA_HEADS = 4
DA_DH = 64
DA_VD = 2 * DA_DH
ROPE_NF = DA_DH // 4
ROPE_BASE = 10000.0
GRID_W = 64
HY_BANDS = 16
HY_TARGET = 1e-2
HY_MAX_DECAY = math.log(HY_TARGET) / 0.3
HY_MIN_DECAY = math.log(HY_TARGET) / 1.5
GLA_HEADS = 4
GLA_RANK = 16
GLA_GATE_NORM = 16.0
GLA_CHUNK = 64
N_EXPERTS = 16
N_GROUPS = 4
EPG = N_EXPERTS // N_GROUPS
MOE_TILE = 160
MOE_EXPERTS_PER_STEP = 4
NT = (((1,), (1,)), ((), ()))
TN = (((0,), (0,)), ((), ()))


def _params(sem):
    return pltpu.CompilerParams(dimension_semantics=sem, vmem_limit_bytes=VMEM_LIMIT)


def _const_spec(shape):
    nd = len(shape)
    return pl.BlockSpec(shape, lambda *_: (0,) * nd)


def _silu(x):
    return x * (1.0 / (1.0 + jnp.exp(-x)))


def _norm_mod(x, nw, sc, sh):
    ms = jnp.mean(x * x, axis=-1, keepdims=True)
    return x * lax.rsqrt(ms + EPS) * nw * (1.0 + sc) + sh


def _mod_kernel(c_ref, w_ref, b_ref, o_ref):
    s = _silu(c_ref[...])
    o_ref[...] = jnp.dot(s, w_ref[...], precision=HI, preferred_element_type=F32) + b_ref[...]


def _modulation(cond, w_mod, b_mod):
    depth, d, d6 = w_mod.shape
    r = cond.shape[0]
    tn = 1536
    return pl.pallas_call(
        _mod_kernel,
        grid=(depth, d6 // tn),
        in_specs=[
            pl.BlockSpec((r, d), lambda l, j: (0, 0)),
            pl.BlockSpec((None, d, tn), lambda l, j: (l, 0, j)),
            pl.BlockSpec((None, 1, tn), lambda l, j: (l, 0, j)),
        ],
        out_specs=pl.BlockSpec((None, r, tn), lambda l, j: (l, 0, j)),
        out_shape=jax.ShapeDtypeStruct((depth, r, d6), F32),
        compiler_params=_params(("parallel", "parallel")),
        name="modulation",
    )(cond, w_mod, b_mod.reshape(depth, 1, d6))


def _proj0_kernel(x_ref, sc_ref, sh_ref, nw_ref, wtok_ref, wchan_ref, qkw_ref, ones_ref, cos_ref, sin_ref,
                  chan_ref, q_ref, k_ref, *cache_refs, rope):
    h = _norm_mod(x_ref[...], nw_ref[...], sc_ref[...], sh_ref[...]).astype(BF16)
    chan_ref[...] = lax.dot_general(wchan_ref[...], h, NT, preferred_element_type=F32).astype(chan_ref.dtype)
    p = jnp.dot(h, wtok_ref[...], preferred_element_type=F32)
    w = DA_HEADS * DA_VD
    ones = ones_ref[...]

    def segnorm(z, gain):
        ss = jnp.dot((z * z).astype(BF16), ones, preferred_element_type=F32)
        return z * lax.rsqrt(ss * (1.0 / DA_DH) + EPS) * gain

    q = segnorm(p[:, :w], qkw_ref[0:1, :])
    k = segnorm(p[:, w:2 * w], qkw_ref[1:2, :])
    if cache_refs:
        kn_ref, vn_ref = cache_refs
        v = p[:, 2 * w:]
        for hh in range(DA_HEADS):
            kn_ref[hh] = k[:, hh * DA_VD:(hh + 1) * DA_VD]
            vn_ref[hh] = v[:, hh * DA_VD:(hh + 1) * DA_VD]
    if rope:
        cos = jnp.concatenate([cos_ref[...]] * (w // LANES), axis=1)
        sin = jnp.concatenate([sin_ref[...]] * (w // LANES), axis=1)
        lane = lax.broadcasted_iota(jnp.int32, q.shape, 1)
        first = (lane % (2 * ROPE_NF)) < ROPE_NF

        def rot(z):
            partner = jnp.where(first, pltpu.roll(z, w - ROPE_NF, 1), pltpu.roll(z, ROPE_NF, 1))
            return z * cos + partner * sin

        q = rot(q)
        k = rot(k)
    q_ref[...] = (q * (DA_DH ** -0.5 * math.log2(math.e))).astype(q_ref.dtype)
    k_ref[...] = k.astype(k_ref.dtype)


def _rope_tables(seq):
    t = np.arange(seq)
    lane = np.arange(LANES)
    d = lane % DA_DH
    axis = d // (2 * ROPE_NF)
    part = (d % (2 * ROPE_NF)) // ROPE_NF
    f = d % ROPE_NF
    pos = jnp.where(axis[None, :] == 0, (t // GRID_W)[:, None], (t % GRID_W)[:, None]).astype(F32)
    inv = ROPE_BASE ** (-jnp.arange(ROPE_NF, dtype=F32) / ROPE_NF)
    ang = pos * inv[f][None, :]
    sign = jnp.asarray(np.where(part == 0, -1.0, 1.0), F32)[None, :]
    return jnp.cos(ang), jnp.sin(ang) * sign


def _proj0(x, sc, sh, nw, wqkv, whyt, qkw, ones, batch, seq, row_fn, rope, emit_cache):
    t, d = x.shape
    tm = min(seq, 512)
    per_seq = seq // tm
    w = DA_HEADS * DA_VD
    hy_in = whyt.shape[0]
    if not emit_cache:
        wqkv = wqkv[:, :2 * w]
    if rope:
        cos, sin = _rope_tables(seq)
    else:
        cos = jnp.ones((seq, LANES), F32)
        sin = jnp.zeros((seq, LANES), F32)
    mod_spec = pl.BlockSpec((None, 1, d), lambda i: (row_fn(i // per_seq), 0, 0))
    tab_spec = pl.BlockSpec((tm, LANES), lambda i: (i % per_seq, 0))
    tok = lambda n: pl.BlockSpec((tm, n), lambda i: (i, 0))
    out_specs = [pl.BlockSpec((None, hy_in, tm), lambda i: (i // per_seq, 0, i % per_seq)), tok(w), tok(w)]
    out_shape = [jax.ShapeDtypeStruct((batch, hy_in, seq), BF16)] + [jax.ShapeDtypeStruct((t, w), BF16)] * 2
    if emit_cache:
        cspec = pl.BlockSpec((None, DA_HEADS, tm, DA_VD), lambda i: (i // per_seq, 0, i % per_seq, 0))
        out_specs += [cspec, cspec]
        out_shape += [jax.ShapeDtypeStruct((batch, DA_HEADS, seq, DA_VD), F32)] * 2
    return pl.pallas_call(
        functools.partial(_proj0_kernel, rope=rope),
        grid=(t // tm,),
        in_specs=[tok(d), mod_spec, mod_spec, _const_spec((1, d)), _const_spec(wqkv.shape), _const_spec(whyt.shape),
                  _const_spec(qkw.shape), _const_spec(ones.shape), tab_spec, tab_spec],
        out_specs=out_specs,
        out_shape=out_shape,
        compiler_params=_params(("parallel",)),
        name="proj0",
    )(x, sc, sh, nw, wqkv, whyt, qkw, ones, cos, sin)


def _c_add(a, b, sign=1.0):
    if b is None:
        return a
    if a is None:
        return b if sign > 0 else tuple(None if p is None else -p for p in b)
    out = []
    for pa, pb in zip(a, b):
        if pb is None:
            out.append(pa)
        elif pa is None:
            out.append(pb if sign > 0 else -pb)
        else:
            out.append(pa + pb if sign > 0 else pa - pb)
    return tuple(out)


def _c_mul_const(w, a):
    if a is None:
        return None
    wr = 0.0 if abs(w.real) < 1e-12 else w.real
    wi = 0.0 if abs(w.imag) < 1e-12 else w.imag
    ar, ai = a

    def scaled(c, p):
        if p is None or c == 0.0:
            return None
        if c == 1.0:
            return p
        if c == -1.0:
            return -p
        return c * p

    re = _c_add((scaled(wr, ar),), (scaled(wi, ai),), -1.0)[0]
    im = _c_add((scaled(wr, ai),), (scaled(wi, ar),), 1.0)[0]
    return (re, im)


def _fft_list(vals, sign, first_half_only=False):
    n = len(vals)
    if n == 1:
        return list(vals)
    ev = _fft_list(vals[0::2], sign)
    od = _fft_list(vals[1::2], sign)
    out = [None] * n
    for k in range(n // 2):
        tw = _c_mul_const(cmath.exp(sign * 2j * math.pi * k / n), od[k])
        out[k] = _c_add(ev[k], tw, 1.0)
        if not first_half_only:
            out[k + n // 2] = _c_add(ev[k], tw, -1.0)
    return out


def _dft_mats(n2, total):
    k = np.arange(n2)
    ang = 2.0 * np.pi * np.outer(k, k) / n2
    fr, fi = np.cos(ang), -np.sin(ang)
    fwd = np.block([[fr, fi], [-fi, fr]])
    inv = np.block([[fr, -fi], [fi, fr]]) / total
    return jnp.asarray(fwd, BF16), jnp.asarray(inv, BF16)


def _twiddles(n1, n2):
    ang = 2.0 * np.pi * np.outer(np.arange(n1), np.arange(n2)) / (n1 * n2)
    return jnp.asarray(np.cos(ang), F32), jnp.asarray(-np.sin(ang), F32)


def _fft_split(seq):
    n = 2 * seq
    n2 = min(512, n // 2)
    return n // n2, n2


def _across_fwd(load, n_in, z_ref, twr_ref, twi_ref, n1, n2, rows):
    per_row = n2 // LANES

    def body(it, carry):
        r0 = pl.multiple_of((it // per_row) * SUBLANES, SUBLANES)
        l0 = pl.multiple_of((it % per_row) * LANES, LANES)
        vals = [load(j, r0, l0) for j in range(n_in)] + [None] * (n1 - n_in)
        outs = _fft_list(vals, -1.0)
        for k1 in range(n1):
            twr = twr_ref[pl.ds(k1, 1), pl.ds(l0, LANES)]
            twi = twi_ref[pl.ds(k1, 1), pl.ds(l0, LANES)]
            re, im = outs[k1]
            zero = jnp.zeros((SUBLANES, LANES), F32)
            re = zero if re is None else re
            im = zero if im is None else im
            z_ref[k1, pl.ds(r0, SUBLANES), pl.ds(l0, LANES)] = re * twr - im * twi
            z_ref[k1, pl.ds(r0, SUBLANES), pl.ds(n2 + l0, LANES)] = re * twi + im * twr
        return carry

    lax.fori_loop(0, (rows // SUBLANES) * per_row, body, 0)


def _hy_hidden_kernel(w1_ref, b1_ref, freq_ref, w2_ref, b2_ref, bands_ref, o_ref, *, seq):
    n = 2 * seq
    k = lax.broadcasted_iota(jnp.int32, (n, 1), 0)
    pos = jnp.where(k < seq, k, n - k).astype(F32)
    t = pos / (seq - 1)
    ang = 2.0 * math.pi * pos / seq * bands_ref[...]
    z = (t * w1_ref[0:1, :]
         + jnp.dot(jnp.cos(ang), w1_ref[1:1 + HY_BANDS, :], precision=HI, preferred_element_type=F32)
         + jnp.dot(-jnp.sin(ang), w1_ref[1 + HY_BANDS:, :], precision=HI, preferred_element_type=F32)
         + b1_ref[...])
    hid = jnp.sin(freq_ref[0:1, :] * z)
    hid = jnp.sin(freq_ref[1:2, :] * (jnp.dot(hid, w2_ref[...], precision=HI, preferred_element_type=F32) + b2_ref[...]))
    o_ref[...] = hid


def _hy_hidden(seq, f_w1, f_b1, f_freq, f_w2, f_b2):
    ffn = f_w1.shape[1]
    bands = jnp.linspace(1e-4, HY_BANDS - 1, HY_BANDS, dtype=F32).reshape(1, HY_BANDS)
    args = (f_w1, f_b1.reshape(1, ffn), f_freq, f_w2, f_b2.reshape(1, ffn), bands)
    return pl.pallas_call(
        functools.partial(_hy_hidden_kernel, seq=seq),
        grid=(1,),
        in_specs=[_const_spec(a.shape) for a in args],
        out_specs=_const_spec((2 * seq, ffn)),
        out_shape=jax.ShapeDtypeStruct((2 * seq, ffn), F32),
        compiler_params=_params(("arbitrary",)),
        name="hyena_filter_hidden",
    )(*args)


def _hy_spec_kernel(hid_ref, wb_ref, wa_ref, dec_ref, twr_ref, twi_ref, gf_ref, o_ref, taps_ref, z_ref, *, seq, n1, n2):
    ct = wb_ref.shape[0]
    hid = hid_ref[...]
    back = lax.dot_general(wb_ref[...], hid[:seq], NT, precision=HI, preferred_element_type=F32)
    ahead = lax.dot_general(wa_ref[...], hid[seq:], NT, precision=HI, preferred_element_type=F32)
    lane = lax.broadcasted_iota(jnp.int32, (1, seq), 1)
    dec = dec_ref[...]
    t_back = lane.astype(F32) / (seq - 1)
    t_ahead = (seq - lane).astype(F32) / (seq - 1)
    back = back * jnp.exp(-t_back * dec)
    ahead = jnp.where(lane == 0, 0.0, ahead * jnp.exp(-t_ahead * dec))
    norm = (jnp.sum(jnp.abs(back), axis=1, keepdims=True) + jnp.sum(jnp.abs(ahead), axis=1, keepdims=True)) + EPS
    taps_ref[:, :seq] = back / norm
    taps_ref[:, seq:] = ahead / norm

    def load(j, r0, l0):
        return (taps_ref[pl.ds(r0, SUBLANES), pl.ds(j * n2 + l0, LANES)], None)

    _across_fwd(load, n1, z_ref, twr_ref, twi_ref, n1, n2, ct)
    for k1 in range(n1):
        o_ref[k1] = jnp.dot(z_ref[k1].astype(BF16), gf_ref[...], preferred_element_type=F32)


def _hy_filter_spectra(hid, f_w3, seq, width, ct):
    n1, n2 = _fft_split(seq)
    ffn = f_w3.shape[0]
    w3t = f_w3.T.reshape(4, width, ffn)
    decay = jnp.abs(jnp.linspace(HY_MIN_DECAY, HY_MAX_DECAY, width, dtype=F32)).reshape(width, 1)
    twr, twi = _twiddles(n1, n2)
    gf, _ = _dft_mats(n2, 2 * seq)
    return pl.pallas_call(
        functools.partial(_hy_spec_kernel, seq=seq, n1=n1, n2=n2),
        grid=(2, width // ct),
        in_specs=[
            _const_spec(hid.shape),
            pl.BlockSpec((None, ct, ffn), lambda i, c: (2 * i, c, 0)),
            pl.BlockSpec((None, ct, ffn), lambda i, c: (2 * i + 1, c, 0)),
            pl.BlockSpec((ct, 1), lambda i, c: (c, 0)),
            _const_spec(twr.shape), _const_spec(twi.shape), _const_spec(gf.shape),
        ],
        out_specs=pl.BlockSpec((None, n1, ct, 2 * n2), lambda i, c: (i, 0, c, 0)),
        out_shape=jax.ShapeDtypeStruct((2, n1, width, 2 * n2), F32),
        scratch_shapes=[pltpu.VMEM((ct, 2 * seq), F32), pltpu.VMEM((n1, ct, 2 * n2), F32)],
        compiler_params=_params(("parallel", "parallel")),
        name="hyena_filter_spectra",
    )(hid, w3t, w3t, decay, twr, twi, gf)


def _short_conv_rows(ref, cw, r0, nrows, seq):
    x = ref[pl.ds(r0, nrows), :].astype(F32)
    lane = lax.broadcasted_iota(jnp.int32, x.shape, 1)
    left = jnp.where(lane == 0, 0.0, pltpu.roll(x, 1, 1))
    right = jnp.where(lane == seq - 1, 0.0, pltpu.roll(x, seq - 1, 1))
    return left * cw[:, 0:1] + x * cw[:, 1:2] + right * cw[:, 2:3] + cw[:, 3:4]


def _hy_conv_kernel(uin_ref, gin_ref, cwu_ref, cwg_ref, d_ref, h_ref, twr_ref, twi_ref, gf_ref, gi_ref,
                    o_ref, u_ref, z_ref, *, seq, n1, n2, conv_u):
    ct = uin_ref.shape[1]
    rows16 = 2 * SUBLANES

    def prep(i, carry):
        r0 = pl.multiple_of(i * rows16, rows16)
        cwg = cwg_ref[pl.ds(r0, rows16), :]
        cwu = cwu_ref[pl.ds(r0, rows16), :]
        for e in range(2):
            o_ref[e, pl.ds(r0, rows16), :] = _short_conv_rows(gin_ref.at[e], cwg, r0, rows16, seq)
            if conv_u:
                u_ref[e, pl.ds(r0, rows16), :] = _short_conv_rows(uin_ref.at[e], cwu, r0, rows16, seq)
            else:
                u_ref[e, pl.ds(r0, rows16), :] = uin_ref[e, pl.ds(r0, rows16), :].astype(F32)
        return carry

    lax.fori_loop(0, ct // rows16, prep, 0)

    def load(j, r0, l0):
        return (u_ref[0, pl.ds(r0, SUBLANES), pl.ds(j * n2 + l0, LANES)],
                u_ref[1, pl.ds(r0, SUBLANES), pl.ds(j * n2 + l0, LANES)])

    _across_fwd(load, n1 // 2, z_ref, twr_ref, twi_ref, n1, n2, ct)

    def within(pair, carry):
        ka, kb = 2 * pair, 2 * pair + 1
        zz = jnp.concatenate([z_ref[ka], z_ref[kb]], axis=0)
        y = jnp.dot(zz.astype(BF16), gf_ref[...], preferred_element_type=F32)
        hh = jnp.concatenate([h_ref[ka], h_ref[kb]], axis=0)
        yr, yi = y[:, :n2], y[:, n2:]
        hr, hi = hh[:, :n2], hh[:, n2:]
        p = jnp.concatenate([yr * hr - yi * hi, yr * hi + yi * hr], axis=1)
        q = jnp.dot(p.astype(BF16), gi_ref[...], preferred_element_type=F32)
        z_ref[ka] = q[:ct]
        z_ref[kb] = q[ct:]
        return carry

    lax.fori_loop(0, n1 // 2, within, 0)

    per_row = n2 // LANES

    def finish(it, carry):
        r0 = pl.multiple_of((it // per_row) * SUBLANES, SUBLANES)
        l0 = pl.multiple_of((it % per_row) * LANES, LANES)
        vals = []
        for k1 in range(n1):
            twr = twr_ref[pl.ds(k1, 1), pl.ds(l0, LANES)]
            twi = twi_ref[pl.ds(k1, 1), pl.ds(l0, LANES)]
            re = z_ref[k1, pl.ds(r0, SUBLANES), pl.ds(l0, LANES)]
            im = z_ref[k1, pl.ds(r0, SUBLANES), pl.ds(n2 + l0, LANES)]
            vals.append((re * twr + im * twi, im * twr - re * twi))
        outs = _fft_list(vals, 1.0, first_half_only=True)
        d = d_ref[pl.ds(r0, SUBLANES), :]
        for j in range(n1 // 2):
            sl = (pl.ds(r0, SUBLANES), pl.ds(j * n2 + l0, LANES))
            for e in range(2):
                o_ref[(e,) + sl] = o_ref[(e,) + sl] * (outs[j][e] + u_ref[(e,) + sl] * d)
        return carry

    lax.fori_loop(0, (ct // SUBLANES) * per_row, finish, 0, unroll=2)


def _hy_conv(u, u_off, g, g_off, cw, spectra, conv_idx, d_skip, seq, width, ct, conv_u):
    batch = u.shape[0]
    n1, n2 = _fft_split(seq)
    twr, twi = _twiddles(n1, n2)
    gf, gi = _dft_mats(n2, 2 * seq)
    uo, go = u_off // ct, g_off // ct
    return pl.pallas_call(
        functools.partial(_hy_conv_kernel, seq=seq, n1=n1, n2=n2, conv_u=conv_u),
        grid=(width // ct, batch // 2),
        in_specs=[
            pl.BlockSpec((2, ct, seq), lambda c, p: (p, uo + c, 0)),
            pl.BlockSpec((2, ct, seq), lambda c, p: (p, go + c, 0)),
            pl.BlockSpec((ct, 4), lambda c, p: (uo + c, 0)),
            pl.BlockSpec((ct, 4), lambda c, p: (go + c, 0)),
            pl.BlockSpec((ct, 1), lambda c, p: (c, 0)),
            pl.BlockSpec((None, n1, ct, 2 * n2), lambda c, p: (conv_idx, 0, c, 0), pipeline_mode=pl.Buffered(1)),
            _const_spec(twr.shape), _const_spec(twi.shape), _const_spec(gf.shape), _const_spec(gi.shape),
        ],
        out_specs=pl.BlockSpec((2, ct, seq), lambda c, p: (p, c, 0)),
        out_shape=jax.ShapeDtypeStruct((batch, width, seq), F32),
        scratch_shapes=[pltpu.VMEM((2, ct, seq), F32), pltpu.VMEM((n1, ct, 2 * n2), F32)],
        compiler_params=_params(("arbitrary", "arbitrary")),
        name="hyena_conv",
    )(u, g, cw, cw, d_skip, spectra, twr, twi, gf, gi)


def _hyena(hyt, conv_w, conv_b, filt, hy_skip, seq, width):
    f_w1, f_b1, f_freq, f_w2, f_b2, f_w3 = filt
    ct = LANES
    hid = _hy_hidden(seq, f_w1, f_b1, f_freq, f_w2, f_b2)
    spectra = _hy_filter_spectra(hid, f_w3, seq, width, ct)
    cw = jnp.concatenate([conv_w.T, conv_b[:, None]], axis=1)
    y1 = _hy_conv(hyt, 0, hyt, width, cw, spectra, 0, hy_skip[0].reshape(width, 1), seq, width, ct, True)
    return _hy_conv(y1, 0, hyt, 2 * width, cw, spectra, 1, hy_skip[1].reshape(width, 1), seq, width, ct, False)


def _attn_kernel(q_ref, k_ref, vt_ref, *rest, lam_init, with_cache, kc):
    if with_cache:
        kc_ref, vct_ref, lam_ref, sw_ref, o_ref, kmax_ref = rest
    else:
        lam_ref, sw_ref, o_ref, kmax_ref = rest
    lp = lam_ref[...]
    lam = (jnp.exp(jnp.sum(lp[0:1] * lp[1:2], axis=1, keepdims=True))
           - jnp.exp(jnp.sum(lp[2:3] * lp[3:4], axis=1, keepdims=True)) + lam_init)
    q = q_ref[...]
    lane = lax.broadcasted_iota(jnp.int32, q.shape, 1)
    zero = jnp.zeros_like(q)
    qs = (jnp.where(lane < DA_DH, q, zero), jnp.where(lane >= DA_DH, q, zero))
    seq = k_ref.shape[0]
    chunks = [(k_ref[c * kc:(c + 1) * kc, :], vt_ref[:, c * kc:(c + 1) * kc]) for c in range(seq // kc)]
    if with_cache:
        chunks.append((kc_ref[...].astype(BF16), vct_ref[...].astype(BF16)))
    tq = q.shape[0]
    sub = lax.broadcasted_iota(jnp.int32, (SUBLANES, DA_VD), 0)
    ln = lax.broadcasted_iota(jnp.int32, (SUBLANES, DA_VD), 1)
    pick = jnp.where((ln // DA_DH) == sub, 1.0, 0.0).astype(BF16)

    def sq_norms(x):
        xf = x.astype(F32)
        return lax.dot_general(pick, (xf * xf).astype(BF16), NT, preferred_element_type=F32)

    @pl.when(pl.program_id(2) == 0)
    def _key_norms():
        best = None
        for kk, _ in chunks:
            cur = jnp.max(sq_norms(kk), axis=1, keepdims=True)
            best = cur if best is None else jnp.maximum(best, cur)
        kmax_ref[...] = jnp.broadcast_to(best, kmax_ref.shape)

    def scores(ci):
        return [lax.dot_general(chunks[ci][0], qm, NT, preferred_element_type=F32) for qm in qs]

    def finish(acc1, l1, acc2, l2):
        o = acc1 * (1.0 / l1) - acc2 * (lam / l2)
        ms = jnp.mean(o * o, axis=0, keepdims=True)
        o_ref[...] = (o * lax.rsqrt(ms + EPS) * sw_ref[...] * (1.0 - lam_init)).astype(o_ref.dtype)

    bound = jnp.sqrt(sq_norms(q) * kmax_ref[:, 0:1]) * 1.02
    lsum = [jnp.zeros((SUBLANES, tq), F32), jnp.zeros((SUBLANES, tq), F32)]
    acc = [None, None]
    s_next = scores(0)
    for ci, (kk, vv) in enumerate(chunks):
        s_cur = s_next
        if ci + 1 < len(chunks):
            s_next = scores(ci + 1)
        for mi in range(2):
            p = jnp.exp2(s_cur[mi] - bound[mi:mi + 1, :])
            lsum[mi] = lsum[mi] + jnp.sum(p.reshape(-1, SUBLANES, tq), axis=0)
            part = jnp.dot(vv, p.astype(BF16), preferred_element_type=F32)
            acc[mi] = part if acc[mi] is None else acc[mi] + part
    l1 = jnp.sum(lsum[0], axis=0, keepdims=True)
    l2 = jnp.sum(lsum[1], axis=0, keepdims=True)
    finish(acc[0], l1, acc[1], l2)
    safe = jnp.min(jnp.minimum(l1, l2)) > 2.0 ** -60

    @pl.when(jnp.logical_not(safe))
    def _online():
        state = [None, None]
        for ci, (kk, vv) in enumerate(chunks):
            s_cur = scores(ci)
            for mi in range(2):
                s = s_cur[mi]
                cmax = jnp.max(s, axis=0, keepdims=True)
                if state[mi] is None:
                    m = cmax
                    p = jnp.exp2(s - m)
                    l = jnp.sum(p, axis=0, keepdims=True)
                    a = jnp.dot(vv, p.astype(BF16), preferred_element_type=F32)
                else:
                    m_old, l, a = state[mi]
                    m = jnp.maximum(m_old, cmax)
                    alpha = jnp.exp2(m_old - m)
                    p = jnp.exp2(s - m)
                    l = l * alpha + jnp.sum(p, axis=0, keepdims=True)
                    a = a * alpha + jnp.dot(vv, p.astype(BF16), preferred_element_type=F32)
                state[mi] = (m, l, a)
        finish(state[0][2], state[0][1], state[1][2], state[1][1])


def _diff_attention(q, k, chan, v_row0, cache, lam_p, subln_w, batch, seq, lam_init):
    tq = min(seq, 256)
    kc = min(seq, 512)
    nq = seq // tq
    vb = v_row0 // DA_VD
    grid = (batch, DA_HEADS, nq)
    in_specs = [pl.BlockSpec((tq, DA_VD), lambda b, h, i: (b * nq + i, h)),
                pl.BlockSpec((seq, DA_VD), lambda b, h, i: (b, h)),
                pl.BlockSpec((None, DA_VD, seq), lambda b, h, i: (b, vb + h, 0))]
    args = [q, k, chan]
    if cache is not None:
        past = cache[0].shape[2]
        in_specs += [pl.BlockSpec((None, None, past, DA_VD), lambda b, h, i: (b, h, 0, 0)),
                     pl.BlockSpec((None, None, DA_VD, past), lambda b, h, i: (b, h, 0, 0))]
        args += [cache[0], jnp.swapaxes(cache[1], 2, 3)]
    in_specs += [_const_spec(lam_p.shape), _const_spec((DA_VD, 1))]
    args += [lam_p, subln_w.reshape(DA_VD, 1)]
    return pl.pallas_call(
        functools.partial(_attn_kernel, lam_init=lam_init, with_cache=cache is not None, kc=kc),
        grid=grid,
        in_specs=in_specs,
        out_specs=pl.BlockSpec((None, DA_VD, tq), lambda b, h, i: (b, h, i)),
        out_shape=jax.ShapeDtypeStruct((batch, DA_HEADS * DA_VD, seq), BF16),
        scratch_shapes=[pltpu.VMEM((SUBLANES, LANES), F32)],
        compiler_params=_params(("parallel", "parallel", "arbitrary")),
        name="diff_attention",
    )(*args)


def _outproj0_kernel(x_ref, yt_ref, ot_ref, g_ref, wy_ref, wo_ref, out_ref):
    mix = lax.dot_general(yt_ref[...].astype(BF16), wy_ref[...], TN, preferred_element_type=F32)
    mix = mix + lax.dot_general(ot_ref[...], wo_ref[...], TN, preferred_element_type=F32)
    out_ref[...] = x_ref[...] + g_ref[...] * mix


def _outproj0(x, yt, ot, g, w_out, seq, row_fn):
    t, d = x.shape
    width = yt.shape[1]
    tm = min(seq, 512)
    per_seq = seq // tm
    wy = w_out[:width].astype(BF16)
    wo = w_out[width:].astype(BF16)
    tok = pl.BlockSpec((tm, d), lambda i: (i, 0))
    chan = lambda n: pl.BlockSpec((None, n, tm), lambda i: (i // per_seq, 0, i % per_seq))
    return pl.pallas_call(
        _outproj0_kernel,
        grid=(t // tm,),
        in_specs=[tok, chan(width), chan(ot.shape[1]),
                  pl.BlockSpec((None, 1, d), lambda i: (row_fn(i // per_seq), 0, 0)),
                  _const_spec(wy.shape), _const_spec(wo.shape)],
        out_specs=tok,
        out_shape=jax.ShapeDtypeStruct((t, d), F32),
        compiler_params=_params(("parallel",)),
        name="outproj0",
    )(x, yt, ot, g, wy, wo)


def _top2_of_rows(rows):
    n = len(rows)
    v1 = functools.reduce(jnp.maximum, rows)
    i1 = jnp.full(rows[0].shape, n - 1, jnp.int32)
    for j in range(n - 2, -1, -1):
        i1 = jnp.where(rows[j] == v1, j, i1)
    masked = [jnp.where(i1 == j, -jnp.inf, rows[j]) for j in range(n)]
    v2 = functools.reduce(jnp.maximum, masked)
    i2 = jnp.full(rows[0].shape, n - 1, jnp.int32)
    for j in range(n - 2, -1, -1):
        i2 = jnp.where(masked[j] == v2, j, i2)
    return v1, i1, v2, i2


def _select_rows(idx, rows):
    out = rows[-1]
    for j in range(len(rows) - 2, -1, -1):
        out = jnp.where(idx == j, rows[j], out)
    return out


def _moe_kernel(x_ref, sc_ref, sh_ref, g_ref, nw_ref, rwt_ref, rb_ref, upper_ref, wg_ref, wu_ref, wd_ref,
                out_ref, hb_ref, rank_ref, comb_ref, acc_ref, sel_ref, y_ref, cnt_ref):
    step = pl.program_id(1)
    tb = x_ref.shape[0]

    @pl.when(step == 0)
    def _route():
        h = _norm_mod(x_ref[...], nw_ref[...], sc_ref[...], sh_ref[...])
        hb = h.astype(BF16)
        hb_ref[...] = hb
        h_lo = (h - hb.astype(F32)).astype(BF16)
        logits = (lax.dot_general(rwt_ref[0], hb, NT, preferred_element_type=F32)
                  + lax.dot_general(rwt_ref[0], h_lo, NT, preferred_element_type=F32)
                  + lax.dot_general(rwt_ref[1], hb, NT, preferred_element_type=F32))
        s = 1.0 / (1.0 + jnp.exp(-logits))
        s_sel = s + rb_ref[...]
        groups = tb // LANES

        def pack(row):
            return jnp.concatenate([row[:, c * LANES:(c + 1) * LANES] for c in range(groups)], axis=0)

        def unpack(tile):
            return jnp.concatenate([tile[c:c + 1, :] for c in range(groups)], axis=1)

        s_rows = [pack(s[j:j + 1, :]) for j in range(N_EXPERTS)]
        sel_rows = [pack(s_sel[j:j + 1, :]) for j in range(N_EXPERTS)]
        tops = [_top2_of_rows(sel_rows[g * EPG:(g + 1) * EPG]) for g in range(N_GROUPS)]
        scores = [tp[0] + tp[2] for tp in tops]
        best = functools.reduce(jnp.maximum, scores)
        g_idx = jnp.full(best.shape, N_GROUPS - 1, jnp.int32)
        for g in range(N_GROUPS - 2, -1, -1):
            g_idx = jnp.where(scores[g] == best, g, g_idx)
        e1 = g_idx * EPG + _select_rows(g_idx, [tp[1] for tp in tops])
        e2 = g_idx * EPG + _select_rows(g_idx, [tp[3] for tp in tops])
        w1 = _select_rows(e1, s_rows)
        w2 = _select_rows(e2, s_rows)
        tot = w1 + w2
        w1, w2 = w1 / tot, w2 / tot
        masks = [jnp.where((e1 == j) | (e2 == j), 1.0, 0.0) for j in range(N_EXPERTS)]
        comb_ref[...] = jnp.concatenate(
            [unpack(jnp.where(e1 == j, w1, 0.0) + jnp.where(e2 == j, w2, 0.0)) for j in range(N_EXPERTS)], axis=0)
        mask = jnp.concatenate([unpack(m) for m in masks], axis=0)
        rank = jnp.dot(mask.astype(BF16), upper_ref[...], preferred_element_type=F32)
        rank_ref[...] = jnp.where(mask > 0.0, rank, -1.0)
        for j in range(N_EXPERTS):
            cnt_ref[j] = jnp.sum(masks[j]).astype(jnp.int32)
        acc_ref[...] = jnp.zeros_like(acc_ref)

    def expert_tile(j, tile_idx):
        e = step * MOE_EXPERTS_PER_STEP + j
        rank = rank_ref[pl.ds(e, 1), :]
        comb = comb_ref[pl.ds(e, 1), :]
        slot = (lax.broadcasted_iota(jnp.int32, (MOE_TILE, tb), 0) + tile_idx * MOE_TILE).astype(F32)
        hit = rank == slot
        onehot = jnp.where(hit, 1.0, 0.0).astype(BF16)
        xe = jnp.dot(onehot, hb_ref[...], preferred_element_type=F32).astype(BF16)
        gate = jnp.dot(xe, wg_ref[j], preferred_element_type=F32)
        up = jnp.dot(xe, wu_ref[j], preferred_element_type=F32)
        y = jnp.dot((_silu(gate) * up).astype(BF16), wd_ref[j], preferred_element_type=F32)
        w_slot = jnp.sum(jnp.where(hit, comb, 0.0), axis=1, keepdims=True)
        return onehot, (y * w_slot).astype(BF16)

    for j in range(MOE_EXPERTS_PER_STEP):
        onehot, yw = expert_tile(j, 0)
        sel_ref[j * MOE_TILE:(j + 1) * MOE_TILE, :] = onehot
        y_ref[j * MOE_TILE:(j + 1) * MOE_TILE, :] = yw

        def overflow(tile_idx, carry, j=j):
            onehot, yw = expert_tile(j, tile_idx)
            acc_ref[...] += lax.dot_general(onehot, yw, TN, preferred_element_type=F32)
            return carry

        n_tiles = (cnt_ref[step * MOE_EXPERTS_PER_STEP + j] + MOE_TILE - 1) // MOE_TILE
        lax.fori_loop(1, n_tiles, overflow, 0)
    acc_ref[...] += lax.dot_general(sel_ref[...], y_ref[...], TN, preferred_element_type=F32)

    @pl.when(step == pl.num_programs(1) - 1)
    def _finish():
        out_ref[...] = x_ref[...] + g_ref[...] * acc_ref[...]


def _moe(x, sc, sh, g, nw, router_w, router_b, w_gate, w_up, w_down, seq, row_fn, shared_mod):
    t, d = x.shape
    tb = min(t if shared_mod else seq, 1024)
    per_seq = t if shared_mod else seq // tb
    dff = w_gate.shape[-1]
    upper = jnp.asarray(np.triu(np.ones((tb, tb), np.float32), 1), BF16)
    mod_spec = pl.BlockSpec((None, 1, d), lambda i, e: (row_fn(i // per_seq), 0, 0))
    tok = pl.BlockSpec((tb, d), lambda i, e: (i, 0))
    per = MOE_EXPERTS_PER_STEP
    stacked = per * MOE_TILE
    rw_hi = router_w.T.astype(BF16)
    rwt = jnp.stack([rw_hi, (router_w.T - rw_hi.astype(F32)).astype(BF16)])
    return pl.pallas_call(
        _moe_kernel,
        grid=(t // tb, N_EXPERTS // per),
        in_specs=[tok, mod_spec, mod_spec, mod_spec, _const_spec((1, d)), _const_spec((2, N_EXPERTS, d)),
                  _const_spec((N_EXPERTS, 1)), _const_spec(upper.shape),
                  pl.BlockSpec((per, d, dff), lambda i, e: (e, 0, 0)),
                  pl.BlockSpec((per, d, dff), lambda i, e: (e, 0, 0)),
                  pl.BlockSpec((per, dff, d), lambda i, e: (e, 0, 0))],
        out_specs=tok,
        out_shape=jax.ShapeDtypeStruct((t, d), F32),
        scratch_shapes=[pltpu.VMEM((tb, d), BF16), pltpu.VMEM((N_EXPERTS, tb), F32), pltpu.VMEM((N_EXPERTS, tb), F32),
                        pltpu.VMEM((tb, d), F32), pltpu.VMEM((stacked, tb), BF16), pltpu.VMEM((stacked, d), BF16),
                        pltpu.SMEM((N_EXPERTS,), jnp.int32)],
        compiler_params=_params(("parallel", "arbitrary")),
        name="moe",
    )(x, sc, sh, g, nw, rwt, router_b.reshape(N_EXPERTS, 1), upper, w_gate, w_up, w_down)


def _proj1_kernel(x_ref, sc_ref, sh_ref, nw_ref, w_ref, wlr_ref, p_ref, lr_ref):
    h = _norm_mod(x_ref[...], nw_ref[...], sc_ref[...], sh_ref[...]).astype(BF16)
    p_ref[...] = jnp.dot(h, w_ref[...], preferred_element_type=F32).astype(p_ref.dtype)
    lr_ref[...] = jnp.dot(h, wlr_ref[...], preferred_element_type=F32)


def _proj1(x, sc, sh, nw, w_main, w_lr, seq, row_fn):
    t, d = x.shape
    tm = min(seq, 512)
    per_seq = seq // tm
    mod_spec = pl.BlockSpec((None, 1, d), lambda i: (row_fn(i // per_seq), 0, 0))
    tok = lambda n: pl.BlockSpec((tm, n), lambda i: (i, 0))
    return pl.pallas_call(
        _proj1_kernel,
        grid=(t // tm,),
        in_specs=[tok(d), mod_spec, mod_spec, _const_spec((1, d)), _const_spec(w_main.shape), _const_spec(w_lr.shape)],
        out_specs=[tok(w_main.shape[1]), tok(LANES)],
        out_shape=[jax.ShapeDtypeStruct((t, w_main.shape[1]), BF16), jax.ShapeDtypeStruct((t, LANES), F32)],
        compiler_params=_params(("parallel",)),
        name="proj1",
    )(x, sc, sh, nw, w_main, w_lr)


def _log_sigmoid(x):
    return jnp.minimum(x, 0.0) - jnp.log(1.0 + jnp.exp(-jnp.abs(x)))


def _gla_kernel(qf_ref, kf_ref, vf_ref, lf_ref, qb_ref, kb_ref, vb_ref, lb_ref, wgf_ref, wgb_ref, bg_ref,
                trif_ref, trib_ref, s0f_ref, s0b_ref, of_ref, ob_ref, sf_ref, sb_ref, stf_ref, stb_ref, *, dk, dv):
    i = pl.program_id(1)
    nb = pl.num_programs(1)
    tc = qf_ref.shape[0]

    @pl.when(i == 0)
    def _init():
        stf_ref[...] = s0f_ref[...].astype(F32)
        stb_ref[...] = s0b_ref[...].astype(F32)

    r = lax.broadcasted_iota(jnp.int32, (GLA_CHUNK, GLA_CHUNK), 0)
    c = lax.broadcasted_iota(jnp.int32, (GLA_CHUNK, GLA_CHUNK), 1)
    scale = dk ** -0.5

    def decay_sums(l_ref, wg_ref, bias, tri_ref):
        gate = _log_sigmoid(jnp.dot(l_ref[...], wg_ref[...], precision=HI, preferred_element_type=F32)
                            + bias) / GLA_GATE_NORM
        total = None
        rest = gate
        for _ in range(3):
            part = rest.astype(BF16)
            rest = rest - part.astype(F32)
            term = jnp.dot(tri_ref[...], part, preferred_element_type=F32)
            total = term if total is None else total + term
        return total

    b_f = decay_sums(lf_ref, wgf_ref, bg_ref[0:1, :], trif_ref)
    b_b = decay_sums(lb_ref, wgb_ref, bg_ref[1:2, :], trib_ref)
    n_chunks = tc // GLA_CHUNK
    chains = []
    for h in range(GLA_HEADS):
        chains.append((qf_ref, kf_ref, vf_ref, of_ref, b_f, stf_ref, h, False))
        chains.append((qb_ref, kb_ref, vb_ref, ob_ref, b_b, stb_ref, h, True))
    states = [st_ref[h] for (_, _, _, _, _, st_ref, h, _) in chains]
    for step in range(n_chunks):
        prepared = []
        for (q_ref, k_ref, v_ref, _, b_all, _, h, reverse) in chains:
            ci = n_chunks - 1 - step if reverse else step
            rs = slice(ci * GLA_CHUNK, (ci + 1) * GLA_CHUNK)
            ks, vs = slice(h * dk, (h + 1) * dk), slice(h * dv, (h + 1) * dv)
            b = b_all[rs, ks]
            b_end = b[0:1, :] if reverse else b[GLA_CHUNK - 1:GLA_CHUNK, :]
            q_dec = (q_ref[rs, ks].astype(F32) * scale * jnp.exp(b)).astype(BF16)
            k_dec = k_ref[rs, ks].astype(F32) * jnp.exp(-b)
            e_end = jnp.exp(b_end)
            prepared.append((rs, vs, q_dec, k_dec.astype(BF16), (k_dec * e_end).astype(BF16), e_end, v_ref[rs, vs]))
        first = []
        for n, (rs, vs, q_dec, k_bf, k_rem, e_end, v) in enumerate(prepared):
            att = lax.dot_general(q_dec, k_bf, NT, preferred_element_type=F32)
            carry = lax.dot_general(q_dec, states[n].astype(BF16), NT, preferred_element_type=F32)
            upd = lax.dot_general(v, k_rem, TN, preferred_element_type=F32)
            first.append((att, carry, upd))
        for n, (rs, vs, q_dec, k_bf, k_rem, e_end, v) in enumerate(prepared):
            att, carry, upd = first[n]
            reverse = chains[n][7]
            att = jnp.where((c >= r) if reverse else (c <= r), att, 0.0).astype(BF16)
            o = jnp.dot(att, v, preferred_element_type=F32) + carry
            chains[n][3][rs, vs] = o.astype(chains[n][3].dtype)
            states[n] = states[n] * e_end + upd
    for n, (_, _, _, _, _, st_ref, h, _) in enumerate(chains):
        st_ref[h] = states[n]

    @pl.when(i == nb - 1)
    def _emit():
        sf_ref[...] = stf_ref[...]
        sb_ref[...] = stb_ref[...]


def _gla(p, lr, wgf, wgb, bg, s0f, s0b, batch, seq, dk, dv):
    tc = min(seq, 256)
    nb = seq // tc
    hk, hv = GLA_HEADS * dk, GLA_HEADS * dv
    fwd = lambda width, col: pl.BlockSpec((tc, width), lambda b, i: (b * nb + i, col))
    bwd = lambda width, col: pl.BlockSpec((tc, width), lambda b, i: (b * nb + nb - 1 - i, col))
    st_spec = pl.BlockSpec((None, GLA_HEADS, dv, dk), lambda b, i: (b, 0, 0, 0))
    v_col = 2 * hk // hv
    idx = np.arange(tc)
    same = (idx[:, None] // GLA_CHUNK) == (idx[None, :] // GLA_CHUNK)
    tri_f = jnp.asarray(same & (idx[None, :] <= idx[:, None]), BF16)
    tri_b = jnp.asarray(same & (idx[None, :] >= idx[:, None]), BF16)
    return pl.pallas_call(
        functools.partial(_gla_kernel, dk=dk, dv=dv),
        grid=(batch, nb),
        in_specs=[fwd(hk, 0), fwd(hk, 1), fwd(hv, v_col), fwd(LANES, 0),
                  bwd(hk, 0), bwd(hk, 1), bwd(hv, v_col), bwd(LANES, 0),
                  _const_spec(wgf.shape), _const_spec(wgb.shape), _const_spec(bg.shape),
                  _const_spec(tri_f.shape), _const_spec(tri_b.shape), st_spec, st_spec],
        out_specs=[fwd(hv, 0), bwd(hv, 0), st_spec, st_spec],
        out_shape=[jax.ShapeDtypeStruct((batch * seq, hv), BF16)] * 2
        + [jax.ShapeDtypeStruct((batch, GLA_HEADS, dv, dk), F32)] * 2,
        scratch_shapes=[pltpu.VMEM((GLA_HEADS, dv, dk), F32)] * 2,
        compiler_params=_params(("parallel", "arbitrary")),
        name="gla_scan",
    )(p, p, p, lr, p, p, p, lr, wgf, wgb, bg, tri_f, tri_b, s0f, s0b)


def _outproj1_kernel(x_ref, of_ref, ob_ref, go_ref, g_ref, nw_ref, w_ref, out_ref, *, dv):
    o = of_ref[...].astype(F32) + ob_ref[...].astype(F32)
    parts = []
    for h in range(GLA_HEADS):
        oh = o[:, h * dv:(h + 1) * dv]
        ms = jnp.mean(oh * oh, axis=1, keepdims=True)
        parts.append(oh * lax.rsqrt(ms + EPS))
    o = jnp.concatenate(parts, axis=1) * nw_ref[...] * _silu(go_ref[...].astype(F32))
    out_ref[...] = x_ref[...] + g_ref[...] * jnp.dot(o.astype(BF16), w_ref[...], preferred_element_type=F32)


def _outproj1(x, o_f, o_b, p, g, out_norm_w, w_out, seq, row_fn, dv):
    t, d = x.shape
    hv = GLA_HEADS * dv
    tm = min(seq, 512)
    per_seq = seq // tm
    tok = lambda n, col=0: pl.BlockSpec((tm, n), lambda i: (i, col))
    go_col = (p.shape[1] - hv) // hv
    nw = jnp.tile(out_norm_w, GLA_HEADS).reshape(1, hv)
    wb = w_out.astype(BF16)
    return pl.pallas_call(
        functools.partial(_outproj1_kernel, dv=dv),
        grid=(t // tm,),
        in_specs=[tok(d), tok(hv), tok(hv), tok(hv, go_col),
                  pl.BlockSpec((None, 1, d), lambda i: (row_fn(i // per_seq), 0, 0)),
                  _const_spec((1, hv)), _const_spec(wb.shape)],
        out_specs=tok(d),
        out_shape=jax.ShapeDtypeStruct((t, d), F32),
        compiler_params=_params(("parallel",)),
        name="outproj1",
    )(x, o_f, o_b, p, g, nw, wb)


def _mod_rows(mod_l, d):
    return [mod_l[:, j * d:(j + 1) * d].reshape(mod_l.shape[0], 1, d) for j in range(6)]


def _trunk(x, batch, seq, mods, row_fn, shared_mod, norm_w, ab, cp, moe_w, cache, s0, emit_cache):
    d = x.shape[-1]
    t = batch * seq
    x = x.reshape(t, d)
    (w_in0, conv_w, conv_b, filt, hy_skip, qn_w, kn_w, lam_p, subln_w, w_out0) = ab
    (w_in1, gate_up_w, gate_up_b, out_norm_w, w_out1) = cp
    router_w, router_b, wg, wu, wd = moe_w
    width = d // 2
    hy_in = 3 * width

    sh1, sc1, g1, sh2, sc2, g2 = mods[0]
    nseg = DA_HEADS * 2
    qkw = jnp.stack([jnp.tile(qn_w, nseg), jnp.tile(kn_w, nseg)])
    seg = np.arange(DA_HEADS * DA_VD) // DA_DH
    ones = jnp.asarray(seg[:, None] == seg[None, :], BF16)
    v_cols = w_in0[:, hy_in + 2 * DA_HEADS * DA_VD:]
    w_chan = jnp.concatenate([w_in0[:, :hy_in], v_cols], axis=1).T.astype(BF16)
    outs = _proj0(x, sc1, sh1, norm_w[0, 0].reshape(1, d), w_in0[:, hy_in:].astype(BF16), w_chan, qkw, ones,
                  batch, seq, row_fn, rope=cache is not None, emit_cache=emit_cache)
    chan, q, k = outs[:3]
    y_hy = _hyena(chan, conv_w, conv_b, filt, hy_skip, seq, width)
    lam_init = 0.8 - 0.6 * math.exp(-0.3 * 0)
    o = _diff_attention(q, k, chan, hy_in, cache, lam_p, subln_w, batch, seq, lam_init)
    x = _outproj0(x, y_hy, o, g1, w_out0, seq, row_fn)
    x = _moe(x, sc2, sh2, g2, norm_w[0, 1].reshape(1, d), router_w, router_b, wg[0], wu[0], wd[0], seq, row_fn,
             shared_mod)

    sh1, sc1, g1, sh2, sc2, g2 = mods[1]
    dk = d // 2 // GLA_HEADS
    dv = d // GLA_HEADS
    n_main = GLA_HEADS * (2 * dk + 2 * dv)
    w_lr = jnp.pad(w_in1[:, n_main:], ((0, 0), (0, LANES - 2 * GLA_RANK))).astype(BF16)
    p, lr = _proj1(x, sc1, sh1, norm_w[1, 0].reshape(1, d), w_in1[:, :n_main].astype(BF16), w_lr, seq, row_fn)
    wgf = jnp.pad(gate_up_w[0], ((0, LANES - GLA_RANK), (0, 0)))
    wgb = jnp.pad(gate_up_w[1], ((GLA_RANK, LANES - 2 * GLA_RANK), (0, 0)))
    o_f, o_b, s_f, s_b = _gla(p, lr, wgf, wgb, gate_up_b, jnp.swapaxes(s0[0], 2, 3), jnp.swapaxes(s0[1], 2, 3),
                              batch, seq, dk, dv)
    s_f, s_b = jnp.swapaxes(s_f, 2, 3), jnp.swapaxes(s_b, 2, 3)
    x = _outproj1(x, o_f, o_b, p, g1, out_norm_w, w_out1, seq, row_fn, dv)
    x = _moe(x, sc2, sh2, g2, norm_w[1, 1].reshape(1, d), router_w, router_b, wg[1], wu[1], wd[1], seq, row_fn,
             shared_mod)
    return x.reshape(batch, seq, d), outs[3:], (s_f, s_b)


def kernel(x_prompt, x_sample, cache_l0_k, cache_l0_v, state_l1_fwd, state_l1_bwd, c, c_ctx, w_mod, b_mod, norm_w,
           l0_w_in, l0_conv_w, l0_conv_b, l0_filt_w1, l0_filt_b1, l0_filt_freq, l0_filt_w2, l0_filt_b2, l0_filt_w3,
           l0_hy_skip, l0_qn_w, l0_kn_w, l0_lambda, l0_subln_w, l0_w_out, l1_w_in, l1_gate_up_w, l1_gate_up_b,
           l1_out_norm_w, l1_w_out, moe_router_w, moe_router_b, moe_w_gate, moe_w_up, moe_w_down):
    d = x_prompt.shape[-1]
    n_lat = c.shape[0]
    ctx_row = n_lat
    rows = 2 * SUBLANES
    cond = jnp.zeros((rows, d), F32).at[:n_lat].set(c).at[ctx_row].set(c_ctx)
    mod = _modulation(cond, w_mod, b_mod)
    mods = [_mod_rows(mod[l], d) for l in range(mod.shape[0])]
    filt = (l0_filt_w1, l0_filt_b1, l0_filt_freq, l0_filt_w2, l0_filt_b2, l0_filt_w3)
    ab = (l0_w_in, l0_conv_w, l0_conv_b, filt, l0_hy_skip, l0_qn_w, l0_kn_w, l0_lambda, l0_subln_w, l0_w_out)
    cp = (l1_w_in, l1_gate_up_w, l1_gate_up_b, l1_out_norm_w, l1_w_out)
    moe_w = (moe_router_w, moe_router_b, moe_w_gate.astype(BF16), moe_w_up.astype(BF16), moe_w_down.astype(BF16))

    b_ctx, l_ctx = x_prompt.shape[:2]
    zero_state = jnp.zeros((b_ctx,) + state_l1_fwd.shape[1:], F32)
    y_prompt, (new_k, new_v), (new_sf, new_sb) = _trunk(
        x_prompt, b_ctx, l_ctx, mods, lambda b: ctx_row, True, norm_w, ab, cp, moe_w, None,
        (zero_state, zero_state), True)
    b_lat, l_lat = x_sample.shape[:2]
    y_sample, _, _ = _trunk(
        x_sample, b_lat, l_lat, mods, lambda b: b, False, norm_w, ab, cp, moe_w, (cache_l0_k, cache_l0_v),
        (state_l1_fwd, state_l1_bwd), False)
    return (y_prompt, y_sample, new_k, new_v, new_sf, new_sb)
```

```python
import cmath
import functools
import math

import numpy as np
import jax
import jax.numpy as jnp
from jax import lax
from jax.experimental import pallas as pl
from jax.experimental.pallas import tpu as pltpu

F32 = jnp.float32
BF16 = jnp.bfloat16
HI = lax.Precision.HIGHEST
EPS = 1e-6

LANES = 128
SUBLANES = 8
VMEM_LIMIT = 56 << 20

DA_HEADS = 4
DA_DH = 64
DA_VD = 2 * DA_DH
ROPE_NF = DA_DH // 4
ROPE_BASE = 10000.0
GRID_W = 64
HY_BANDS = 16
HY_TARGET = 1e-2
HY_MAX_DECAY = math.log(HY_TARGET) / 0.3
HY_MIN_DECAY = math.log(HY_TARGET) / 1.5
GLA_HEADS = 4
GLA_RANK = 16
GLA_GATE_NORM = 16.0
GLA_CHUNK = 64
N_EXPERTS = 16
N_GROUPS = 4
EPG = N_EXPERTS // N_GROUPS
MOE_TILE = 160
MOE_EXPERTS_PER_STEP = 4
NT = (((1,), (1,)), ((), ()))
TN = (((0,), (0,)), ((), ()))


def _params(sem):
    return pltpu.CompilerParams(dimension_semantics=sem, vmem_limit_bytes=VMEM_LIMIT)


def _const_spec(shape):
    nd = len(shape)
    return pl.BlockSpec(shape, lambda *_: (0,) * nd)


def _silu(x):
    return x * (1.0 / (1.0 + jnp.exp(-x)))


def _norm_mod(x, nw, sc, sh):
    ms = jnp.mean(x * x, axis=-1, keepdims=True)
    return x * lax.rsqrt(ms + EPS) * nw * (1.0 + sc) + sh


def _mod_kernel(c_ref, w_ref, b_ref, o_ref):
    s = _silu(c_ref[...])
    o_ref[...] = jnp.dot(s, w_ref[...], precision=HI, preferred_element_type=F32) + b_ref[...]


def _modulation(cond, w_mod, b_mod):
    depth, d, d6 = w_mod.shape
    r = cond.shape[0]
    tn = 1536
    return pl.pallas_call(
        _mod_kernel,
        grid=(depth, d6 // tn),
        in_specs=[
            pl.BlockSpec((r, d), lambda l, j: (0, 0)),
            pl.BlockSpec((None, d, tn), lambda l, j: (l, 0, j)),
            pl.BlockSpec((None, 1, tn), lambda l, j: (l, 0, j)),
        ],
        out_specs=pl.BlockSpec((None, r, tn), lambda l, j: (l, 0, j)),
        out_shape=jax.ShapeDtypeStruct((depth, r, d6), F32),
        compiler_params=_params(("parallel", "parallel")),
        name="modulation",
    )(cond, w_mod, b_mod.reshape(depth, 1, d6))


def _proj0_kernel(x_ref, sc_ref, sh_ref, nw_ref, wtok_ref, wchan_ref, qkw_ref, ones_ref, cos_ref, sin_ref,
                  chan_ref, q_ref, k_ref, *cache_refs, rope):
    h = _norm_mod(x_ref[...], nw_ref[...], sc_ref[...], sh_ref[...]).astype(BF16)
    chan_ref[...] = lax.dot_general(wchan_ref[...], h, NT, preferred_element_type=F32).astype(chan_ref.dtype)
    p = jnp.dot(h, wtok_ref[...], preferred_element_type=F32)
    w = DA_HEADS * DA_VD
    ones = ones_ref[...]

    def segnorm(z, gain):
        ss = jnp.dot((z * z).astype(BF16), ones, preferred_element_type=F32)
        return z * lax.rsqrt(ss * (1.0 / DA_DH) + EPS) * gain

    q = segnorm(p[:, :w], qkw_ref[0:1, :])
    k = segnorm(p[:, w:2 * w], qkw_ref[1:2, :])
    if cache_refs:
        kn_ref, vn_ref = cache_refs
        v = p[:, 2 * w:]
        for hh in range(DA_HEADS):
            kn_ref[hh] = k[:, hh * DA_VD:(hh + 1) * DA_VD]
            vn_ref[hh] = v[:, hh * DA_VD:(hh + 1) * DA_VD]
    if rope:
        cos = jnp.concatenate([cos_ref[...]] * (w // LANES), axis=1)
        sin = jnp.concatenate([sin_ref[...]] * (w // LANES), axis=1)
        lane = lax.broadcasted_iota(jnp.int32, q.shape, 1)
        first = (lane % (2 * ROPE_NF)) < ROPE_NF

        def rot(z):
            partner = jnp.where(first, pltpu.roll(z, w - ROPE_NF, 1), pltpu.roll(z, ROPE_NF, 1))
            return z * cos + partner * sin

        q = rot(q)
        k = rot(k)
    q_ref[...] = (q * (DA_DH ** -0.5 * math.log2(math.e))).astype(q_ref.dtype)
    k_ref[...] = k.astype(k_ref.dtype)


def _rope_tables(seq):
    t = np.arange(seq)
    lane = np.arange(LANES)
    d = lane % DA_DH
    axis = d // (2 * ROPE_NF)
    part = (d % (2 * ROPE_NF)) // ROPE_NF
    f = d % ROPE_NF
    pos = jnp.where(axis[None, :] == 0, (t // GRID_W)[:, None], (t % GRID_W)[:, None]).astype(F32)
    inv = ROPE_BASE ** (-jnp.arange(ROPE_NF, dtype=F32) / ROPE_NF)
    ang = pos * inv[f][None, :]
    sign = jnp.asarray(np.where(part == 0, -1.0, 1.0), F32)[None, :]
    return jnp.cos(ang), jnp.sin(ang) * sign


def _proj0(x, sc, sh, nw, wqkv, whyt, qkw, ones, batch, seq, row_fn, rope, emit_cache):
    t, d = x.shape
    tm = min(seq, 512)
    per_seq = seq // tm
    w = DA_HEADS * DA_VD
    hy_in = whyt.shape[0]
    if not emit_cache:
        wqkv = wqkv[:, :2 * w]
    if rope:
        cos, sin = _rope_tables(seq)
    else:
        cos = jnp.ones((seq, LANES), F32)
        sin = jnp.zeros((seq, LANES), F32)
    mod_spec = pl.BlockSpec((None, 1, d), lambda i: (row_fn(i // per_seq), 0, 0))
    tab_spec = pl.BlockSpec((tm, LANES), lambda i: (i % per_seq, 0))
    tok = lambda n: pl.BlockSpec((tm, n), lambda i: (i, 0))
    out_specs = [pl.BlockSpec((None, hy_in, tm), lambda i: (i // per_seq, 0, i % per_seq)), tok(w), tok(w)]
    out_shape = [jax.ShapeDtypeStruct((batch, hy_in, seq), BF16)] + [jax.ShapeDtypeStruct((t, w), BF16)] * 2
    if emit_cache:
        cspec = pl.BlockSpec((None, DA_HEADS, tm, DA_VD), lambda i: (i // per_seq, 0, i % per_seq, 0))
        out_specs += [cspec, cspec]
        out_shape += [jax.ShapeDtypeStruct((batch, DA_HEADS, seq, DA_VD), F32)] * 2
    return pl.pallas_call(
        functools.partial(_proj0_kernel, rope=rope),
        grid=(t // tm,),
        in_specs=[tok(d), mod_spec, mod_spec, _const_spec((1, d)), _const_spec(wqkv.shape), _const_spec(whyt.shape),
                  _const_spec(qkw.shape), _const_spec(ones.shape), tab_spec, tab_spec],
        out_specs=out_specs,
        out_shape=out_shape,
        compiler_params=_params(("parallel",)),
        name="proj0",
    )(x, sc, sh, nw, wqkv, whyt, qkw, ones, cos, sin)


def _c_add(a, b, sign=1.0):
    if b is None:
        return a
    if a is None:
        return b if sign > 0 else tuple(None if p is None else -p for p in b)
    out = []
    for pa, pb in zip(a, b):
        if pb is None:
            out.append(pa)
        elif pa is None:
            out.append(pb if sign > 0 else -pb)
        else:
            out.append(pa + pb if sign > 0 else pa - pb)
    return tuple(out)


def _c_mul_const(w, a):
    if a is None:
        return None
    wr = 0.0 if abs(w.real) < 1e-12 else w.real
    wi = 0.0 if abs(w.imag) < 1e-12 else w.imag
    ar, ai = a

    def scaled(c, p):
        if p is None or c == 0.0:
            return None
        if c == 1.0:
            return p
        if c == -1.0:
            return -p
        return c * p

    re = _c_add((scaled(wr, ar),), (scaled(wi, ai),), -1.0)[0]
    im = _c_add((scaled(wr, ai),), (scaled(wi, ar),), 1.0)[0]
    return (re, im)


def _fft_list(vals, sign, first_half_only=False):
    n = len(vals)
    if n == 1:
        return list(vals)
    ev = _fft_list(vals[0::2], sign)
    od = _fft_list(vals[1::2], sign)
    out = [None] * n
    for k in range(n // 2):
        tw = _c_mul_const(cmath.exp(sign * 2j * math.pi * k / n), od[k])
        out[k] = _c_add(ev[k], tw, 1.0)
        if not first_half_only:
            out[k + n // 2] = _c_add(ev[k], tw, -1.0)
    return out


def _dft_mats(n2, total):
    k = np.arange(n2)
    ang = 2.0 * np.pi * np.outer(k, k) / n2
    fr, fi = np.cos(ang), -np.sin(ang)
    fwd = np.block([[fr, fi], [-fi, fr]])
    inv = np.block([[fr, -fi], [fi, fr]]) / total
    return jnp.asarray(fwd, BF16), jnp.asarray(inv, BF16)


def _twiddles(n1, n2):
    ang = 2.0 * np.pi * np.outer(np.arange(n1), np.arange(n2)) / (n1 * n2)
    return jnp.asarray(np.cos(ang), F32), jnp.asarray(-np.sin(ang), F32)


def _fft_split(seq):
    n = 2 * seq
    n2 = min(512, n // 2)
    return n // n2, n2


def _across_fwd(load, n_in, z_ref, twr_ref, twi_ref, n1, n2, rows):
    per_row = n2 // LANES

    def body(it, carry):
        r0 = pl.multiple_of((it // per_row) * SUBLANES, SUBLANES)
        l0 = pl.multiple_of((it % per_row) * LANES, LANES)
        vals = [load(j, r0, l0) for j in range(n_in)] + [None] * (n1 - n_in)
        outs = _fft_list(vals, -1.0)
        for k1 in range(n1):
            twr = twr_ref[pl.ds(k1, 1), pl.ds(l0, LANES)]
            twi = twi_ref[pl.ds(k1, 1), pl.ds(l0, LANES)]
            re, im = outs[k1]
            zero = jnp.zeros((SUBLANES, LANES), F32)
            re = zero if re is None else re
            im = zero if im is None else im
            z_ref[k1, pl.ds(r0, SUBLANES), pl.ds(l0, LANES)] = re * twr - im * twi
            z_ref[k1, pl.ds(r0, SUBLANES), pl.ds(n2 + l0, LANES)] = re * twi + im * twr
        return carry

    lax.fori_loop(0, (rows // SUBLANES) * per_row, body, 0)


def _hy_hidden_kernel(w1_ref, b1_ref, freq_ref, w2_ref, b2_ref, bands_ref, o_ref, *, seq):
    n = 2 * seq
    k = lax.broadcasted_iota(jnp.int32, (n, 1), 0)
    pos = jnp.where(k < seq, k, n - k).astype(F32)
    t = pos / (seq - 1)
    ang = 2.0 * math.pi * pos / seq * bands_ref[...]
    z = (t * w1_ref[0:1, :]
         + jnp.dot(jnp.cos(ang), w1_ref[1:1 + HY_BANDS, :], precision=HI, preferred_element_type=F32)
         + jnp.dot(-jnp.sin(ang), w1_ref[1 + HY_BANDS:, :], precision=HI, preferred_element_type=F32)
         + b1_ref[...])
    hid = jnp.sin(freq_ref[0:1, :] * z)
    hid = jnp.sin(freq_ref[1:2, :] * (jnp.dot(hid, w2_ref[...], precision=HI, preferred_element_type=F32) + b2_ref[...]))
    o_ref[...] = hid


def _hy_hidden(seq, f_w1, f_b1, f_freq, f_w2, f_b2):
    ffn = f_w1.shape[1]
    bands = jnp.linspace(1e-4, HY_BANDS - 1, HY_BANDS, dtype=F32).reshape(1, HY_BANDS)
    args = (f_w1, f_b1.reshape(1, ffn), f_freq, f_w2, f_b2.reshape(1, ffn), bands)
    return pl.pallas_call(
        functools.partial(_hy_hidden_kernel, seq=seq),
        grid=(1,),
        in_specs=[_const_spec(a.shape) for a in args],
        out_specs=_const_spec((2 * seq, ffn)),
        out_shape=jax.ShapeDtypeStruct((2 * seq, ffn), F32),
        compiler_params=_params(("arbitrary",)),
        name="hyena_filter_hidden",
    )(*args)


def _hy_spec_kernel(hid_ref, wb_ref, wa_ref, dec_ref, twr_ref, twi_ref, gf_ref, o_ref, taps_ref, z_ref, *, seq, n1, n2):
    ct = wb_ref.shape[0]
    hid = hid_ref[...]
    back = lax.dot_general(wb_ref[...], hid[:seq], NT, precision=HI, preferred_element_type=F32)
    ahead = lax.dot_general(wa_ref[...], hid[seq:], NT, precision=HI, preferred_element_type=F32)
    lane = lax.broadcasted_iota(jnp.int32, (1, seq), 1)
    dec = dec_ref[...]
    t_back = lane.astype(F32) / (seq - 1)
    t_ahead = (seq - lane).astype(F32) / (seq - 1)
    back = back * jnp.exp(-t_back * dec)
    ahead = jnp.where(lane == 0, 0.0, ahead * jnp.exp(-t_ahead * dec))
    norm = (jnp.sum(jnp.abs(back), axis=1, keepdims=True) + jnp.sum(jnp.abs(ahead), axis=1, keepdims=True)) + EPS
    taps_ref[:, :seq] = back / norm
    taps_ref[:, seq:] = ahead / norm

    def load(j, r0, l0):
        return (taps_ref[pl.ds(r0, SUBLANES), pl.ds(j * n2 + l0, LANES)], None)

    _across_fwd(load, n1, z_ref, twr_ref, twi_ref, n1, n2, ct)
    for k1 in range(n1):
        o_ref[k1] = jnp.dot(z_ref[k1].astype(BF16), gf_ref[...], preferred_element_type=F32)


def _hy_filter_spectra(hid, f_w3, seq, width, ct):
    n1, n2 = _fft_split(seq)
    ffn = f_w3.shape[0]
    w3t = f_w3.T.reshape(4, width, ffn)
    decay = jnp.abs(jnp.linspace(HY_MIN_DECAY, HY_MAX_DECAY, width, dtype=F32)).reshape(width, 1)
    twr, twi = _twiddles(n1, n2)
    gf, _ = _dft_mats(n2, 2 * seq)
    return pl.pallas_call(
        functools.partial(_hy_spec_kernel, seq=seq, n1=n1, n2=n2),
        grid=(2, width // ct),
        in_specs=[
            _const_spec(hid.shape),
            pl.BlockSpec((None, ct, ffn), lambda i, c: (2 * i, c, 0)),
            pl.BlockSpec((None, ct, ffn), lambda i, c: (2 * i + 1, c, 0)),
            pl.BlockSpec((ct, 1), lambda i, c: (c, 0)),
            _const_spec(twr.shape), _const_spec(twi.shape), _const_spec(gf.shape),
        ],
        out_specs=pl.BlockSpec((None, n1, ct, 2 * n2), lambda i, c: (i, 0, c, 0)),
        out_shape=jax.ShapeDtypeStruct((2, n1, width, 2 * n2), F32),
        scratch_shapes=[pltpu.VMEM((ct, 2 * seq), F32), pltpu.VMEM((n1, ct, 2 * n2), F32)],
        compiler_params=_params(("parallel", "parallel")),
        name="hyena_filter_spectra",
    )(hid, w3t, w3t, decay, twr, twi, gf)


def _short_conv_rows(ref, cw, r0, nrows, seq):
    x = ref[pl.ds(r0, nrows), :].astype(F32)
    lane = lax.broadcasted_iota(jnp.int32, x.shape, 1)
    left = jnp.where(lane == 0, 0.0, pltpu.roll(x, 1, 1))
    right = jnp.where(lane == seq - 1, 0.0, pltpu.roll(x, seq - 1, 1))
    return left * cw[:, 0:1] + x * cw[:, 1:2] + right * cw[:, 2:3] + cw[:, 3:4]


def _hy_conv_kernel(uin_ref, gin_ref, cwu_ref, cwg_ref, d_ref, h_ref, twr_ref, twi_ref, gf_ref, gi_ref,
                    o_ref, u_ref, z_ref, *, seq, n1, n2, conv_u):
    ct = uin_ref.shape[1]
    rows16 = 2 * SUBLANES

    def prep(i, carry):
        r0 = pl.multiple_of(i * rows16, rows16)
        cwg = cwg_ref[pl.ds(r0, rows16), :]
        cwu = cwu_ref[pl.ds(r0, rows16), :]
        for e in range(2):
            o_ref[e, pl.ds(r0, rows16), :] = _short_conv_rows(gin_ref.at[e], cwg, r0, rows16, seq)
            if conv_u:
                u_ref[e, pl.ds(r0, rows16), :] = _short_conv_rows(uin_ref.at[e], cwu, r0, rows16, seq)
            else:
                u_ref[e, pl.ds(r0, rows16), :] = uin_ref[e, pl.ds(r0, rows16), :].astype(F32)
        return carry

    lax.fori_loop(0, ct // rows16, prep, 0)

    def load(j, r0, l0):
        return (u_ref[0, pl.ds(r0, SUBLANES), pl.ds(j * n2 + l0, LANES)],
                u_ref[1, pl.ds(r0, SUBLANES), pl.ds(j * n2 + l0, LANES)])

    _across_fwd(load, n1 // 2, z_ref, twr_ref, twi_ref, n1, n2, ct)

    def within(pair, carry):
        ka, kb = 2 * pair, 2 * pair + 1
        zz = jnp.concatenate([z_ref[ka], z_ref[kb]], axis=0)
        y = jnp.dot(zz.astype(BF16), gf_ref[...], preferred_element_type=F32)
        hh = jnp.concatenate([h_ref[ka], h_ref[kb]], axis=0)
        yr, yi = y[:, :n2], y[:, n2:]
        hr, hi = hh[:, :n2], hh[:, n2:]
        p = jnp.concatenate([yr * hr - yi * hi, yr * hi + yi * hr], axis=1)
        q = jnp.dot(p.astype(BF16), gi_ref[...], preferred_element_type=F32)
        z_ref[ka] = q[:ct]
        z_ref[kb] = q[ct:]
        return carry

    lax.fori_loop(0, n1 // 2, within, 0)

    per_row = n2 // LANES

    def finish(it, carry):
        r0 = pl.multiple_of((it // per_row) * SUBLANES, SUBLANES)
        l0 = pl.multiple_of((it % per_row) * LANES, LANES)
        vals = []
        for k1 in range(n1):
            twr = twr_ref[pl.ds(k1, 1), pl.ds(l0, LANES)]
            twi = twi_ref[pl.ds(k1, 1), pl.ds(l0, LANES)]
            re = z_ref[k1, pl.ds(r0, SUBLANES), pl.ds(l0, LANES)]
            im = z_ref[k1, pl.ds(r0, SUBLANES), pl.ds(n2 + l0, LANES)]
            vals.append((re * twr + im * twi, im * twr - re * twi))
        outs = _fft_list(vals, 1.0, first_half_only=True)
        d = d_ref[pl.ds(r0, SUBLANES), :]
        for j in range(n1 // 2):
            sl = (pl.ds(r0, SUBLANES), pl.ds(j * n2 + l0, LANES))
            for e in range(2):
                o_ref[(e,) + sl] = o_ref[(e,) + sl] * (outs[j][e] + u_ref[(e,) + sl] * d)
        return carry

    lax.fori_loop(0, (ct // SUBLANES) * per_row, finish, 0, unroll=2)


def _hy_conv(u, u_off, g, g_off, cw, spectra, conv_idx, d_skip, seq, width, ct, conv_u):
    batch = u.shape[0]
    n1, n2 = _fft_split(seq)
    twr, twi = _twiddles(n1, n2)
    gf, gi = _dft_mats(n2, 2 * seq)
    uo, go = u_off // ct, g_off // ct
    return pl.pallas_call(
        functools.partial(_hy_conv_kernel, seq=seq, n1=n1, n2=n2, conv_u=conv_u),
        grid=(width // ct, batch // 2),
        in_specs=[
            pl.BlockSpec((2, ct, seq), lambda c, p: (p, uo + c, 0)),
            pl.BlockSpec((2, ct, seq), lambda c, p: (p, go + c, 0)),
            pl.BlockSpec((ct, 4), lambda c, p: (uo + c, 0)),
            pl.BlockSpec((ct, 4), lambda c, p: (go + c, 0)),
            pl.BlockSpec((ct, 1), lambda c, p: (c, 0)),
            pl.BlockSpec((None, n1, ct, 2 * n2), lambda c, p: (conv_idx, 0, c, 0), pipeline_mode=pl.Buffered(1)),
            _const_spec(twr.shape), _const_spec(twi.shape), _const_spec(gf.shape), _const_spec(gi.shape),
        ],
        out_specs=pl.BlockSpec((2, ct, seq), lambda c, p: (p, c, 0)),
        out_shape=jax.ShapeDtypeStruct((batch, width, seq), F32),
        scratch_shapes=[pltpu.VMEM((2, ct, seq), F32), pltpu.VMEM((n1, ct, 2 * n2), F32)],
        compiler_params=_params(("arbitrary", "arbitrary")),
        name="hyena_conv",
    )(u, g, cw, cw, d_skip, spectra, twr, twi, gf, gi)


def _hyena(hyt, conv_w, conv_b, filt, hy_skip, seq, width):
    f_w1, f_b1, f_freq, f_w2, f_b2, f_w3 = filt
    ct = LANES
    hid = _hy_hidden(seq, f_w1, f_b1, f_freq, f_w2, f_b2)
    spectra = _hy_filter_spectra(hid, f_w3, seq, width, ct)
    cw = jnp.concatenate([conv_w.T, conv_b[:, None]], axis=1)
    y1 = _hy_conv(hyt, 0, hyt, width, cw, spectra, 0, hy_skip[0].reshape(width, 1), seq, width, ct, True)
    return _hy_conv(y1, 0, hyt, 2 * width, cw, spectra, 1, hy_skip[1].reshape(width, 1), seq, width, ct, False)


def _attn_kernel(q_ref, k_ref, vt_ref, *rest, lam_init, with_cache, kc):
    if with_cache:
        kc_ref, vct_ref, lam_ref, sw_ref, o_ref, kmax_ref = rest
    else:
        lam_ref, sw_ref, o_ref, kmax_ref = rest
    lp = lam_ref[...]
    lam = (jnp.exp(jnp.sum(lp[0:1] * lp[1:2], axis=1, keepdims=True))
           - jnp.exp(jnp.sum(lp[2:3] * lp[3:4], axis=1, keepdims=True)) + lam_init)
    q = q_ref[...]
    lane = lax.broadcasted_iota(jnp.int32, q.shape, 1)
    zero = jnp.zeros_like(q)
    qs = (jnp.where(lane < DA_DH, q, zero), jnp.where(lane >= DA_DH, q, zero))
    seq = k_ref.shape[0]
    chunks = [(k_ref[c * kc:(c + 1) * kc, :], vt_ref[:, c * kc:(c + 1) * kc]) for c in range(seq // kc)]
    if with_cache:
        chunks.append((kc_ref[...].astype(BF16), vct_ref[...].astype(BF16)))
    tq = q.shape[0]
    sub = lax.broadcasted_iota(jnp.int32, (SUBLANES, DA_VD), 0)
    ln = lax.broadcasted_iota(jnp.int32, (SUBLANES, DA_VD), 1)
    pick = jnp.where((ln // DA_DH) == sub, 1.0, 0.0).astype(BF16)

    def sq_norms(x):
        xf = x.astype(F32)
        return lax.dot_general(pick, (xf * xf).astype(BF16), NT, preferred_element_type=F32)

    @pl.when(pl.program_id(2) == 0)
    def _key_norms():
        best = None
        for kk, _ in chunks:
            cur = jnp.max(sq_norms(kk), axis=1, keepdims=True)
            best = cur if best is None else jnp.maximum(best, cur)
        kmax_ref[...] = jnp.broadcast_to(best, kmax_ref.shape)

    def scores(ci):
        return [lax.dot_general(chunks[ci][0], qm, NT, preferred_element_type=F32) for qm in qs]

    def finish(acc1, l1, acc2, l2):
        o = acc1 * (1.0 / l1) - acc2 * (lam / l2)
        ms = jnp.mean(o * o, axis=0, keepdims=True)
        o_ref[...] = (o * lax.rsqrt(ms + EPS) * sw_ref[...] * (1.0 - lam_init)).astype(o_ref.dtype)

    bound = jnp.sqrt(sq_norms(q) * kmax_ref[:, 0:1]) * 1.02
    lsum = [jnp.zeros((SUBLANES, tq), F32), jnp.zeros((SUBLANES, tq), F32)]
    acc = [None, None]
    s_next = scores(0)
    for ci, (kk, vv) in enumerate(chunks):
        s_cur = s_next
        if ci + 1 < len(chunks):
            s_next = scores(ci + 1)
        for mi in range(2):
            p = jnp.exp2(s_cur[mi] - bound[mi:mi + 1, :])
            lsum[mi] = lsum[mi] + jnp.sum(p.reshape(-1, SUBLANES, tq), axis=0)
            part = jnp.dot(vv, p.astype(BF16), preferred_element_type=F32)
            acc[mi] = part if acc[mi] is None else acc[mi] + part
    l1 = jnp.sum(lsum[0], axis=0, keepdims=True)
    l2 = jnp.sum(lsum[1], axis=0, keepdims=True)
    finish(acc[0], l1, acc[1], l2)
    safe = jnp.min(jnp.minimum(l1, l2)) > 2.0 ** -60

    @pl.when(jnp.logical_not(safe))
    def _online():
        state = [None, None]
        for ci, (kk, vv) in enumerate(chunks):
            s_cur = scores(ci)
            for mi in range(2):
                s = s_cur[mi]
                cmax = jnp.max(s, axis=0, keepdims=True)
                if state[mi] is None:
                    m = cmax
                    p = jnp.exp2(s - m)
                    l = jnp.sum(p, axis=0, keepdims=True)
                    a = jnp.dot(vv, p.astype(BF16), preferred_element_type=F32)
                else:
                    m_old, l, a = state[mi]
                    m = jnp.maximum(m_old, cmax)
                    alpha = jnp.exp2(m_old - m)
                    p = jnp.exp2(s - m)
                    l = l * alpha + jnp.sum(p, axis=0, keepdims=True)
                    a = a * alpha + jnp.dot(vv, p.astype(BF16), preferred_element_type=F32)
                state[mi] = (m, l, a)
        finish(state[0][2], state[0][1], state[1][2], state[1][1])


def _diff_attention(q, k, chan, v_row0, cache, lam_p, subln_w, batch, seq, lam_init):
    tq = min(seq, 512)
    kc = min(seq, 512)
    nq = seq // tq
    vb = v_row0 // DA_VD
    grid = (batch, DA_HEADS, nq)
    in_specs = [pl.BlockSpec((tq, DA_VD), lambda b, h, i: (b * nq + i, h)),
                pl.BlockSpec((seq, DA_VD), lambda b, h, i: (b, h)),
                pl.BlockSpec((None, DA_VD, seq), lambda b, h, i: (b, vb + h, 0))]
    args = [q, k, chan]
    if cache is not None:
        past = cache[0].shape[2]
        in_specs += [pl.BlockSpec((None, None, past, DA_VD), lambda b, h, i: (b, h, 0, 0)),
                     pl.BlockSpec((None, None, DA_VD, past), lambda b, h, i: (b, h, 0, 0))]
        args += [cache[0], jnp.swapaxes(cache[1], 2, 3)]
    in_specs += [_const_spec(lam_p.shape), _const_spec((DA_VD, 1))]
    args += [lam_p, subln_w.reshape(DA_VD, 1)]
    return pl.pallas_call(
        functools.partial(_attn_kernel, lam_init=lam_init, with_cache=cache is not None, kc=kc),
        grid=grid,
        in_specs=in_specs,
        out_specs=pl.BlockSpec((None, DA_VD, tq), lambda b, h, i: (b, h, i)),
        out_shape=jax.ShapeDtypeStruct((batch, DA_HEADS * DA_VD, seq), BF16),
        scratch_shapes=[pltpu.VMEM((SUBLANES, LANES), F32)],
        compiler_params=_params(("parallel", "parallel", "arbitrary")),
        name="diff_attention",
    )(*args)


def _outproj0_kernel(x_ref, yt_ref, ot_ref, g_ref, wy_ref, wo_ref, out_ref):
    mix = lax.dot_general(yt_ref[...].astype(BF16), wy_ref[...], TN, preferred_element_type=F32)
    mix = mix + lax.dot_general(ot_ref[...], wo_ref[...], TN, preferred_element_type=F32)
    out_ref[...] = x_ref[...] + g_ref[...] * mix


def _outproj0(x, yt, ot, g, w_out, seq, row_fn):
    t, d = x.shape
    width = yt.shape[1]
    tm = min(seq, 512)
    per_seq = seq // tm
    wy = w_out[:width].astype(BF16)
    wo = w_out[width:].astype(BF16)
    tok = pl.BlockSpec((tm, d), lambda i: (i, 0))
    chan = lambda n: pl.BlockSpec((None, n, tm), lambda i: (i // per_seq, 0, i % per_seq))
    return pl.pallas_call(
        _outproj0_kernel,
        grid=(t // tm,),
        in_specs=[tok, chan(width), chan(ot.shape[1]),
                  pl.BlockSpec((None, 1, d), lambda i: (row_fn(i // per_seq), 0, 0)),
                  _const_spec(wy.shape), _const_spec(wo.shape)],
        out_specs=tok,
        out_shape=jax.ShapeDtypeStruct((t, d), F32),
        compiler_params=_params(("parallel",)),
        name="outproj0",
    )(x, yt, ot, g, wy, wo)


def _top2_of_rows(rows):
    n = len(rows)
    v1 = functools.reduce(jnp.maximum, rows)
    i1 = jnp.full(rows[0].shape, n - 1, jnp.int32)
    for j in range(n - 2, -1, -1):
        i1 = jnp.where(rows[j] == v1, j, i1)
    masked = [jnp.where(i1 == j, -jnp.inf, rows[j]) for j in range(n)]
    v2 = functools.reduce(jnp.maximum, masked)
    i2 = jnp.full(rows[0].shape, n - 1, jnp.int32)
    for j in range(n - 2, -1, -1):
        i2 = jnp.where(masked[j] == v2, j, i2)
    return v1, i1, v2, i2


def _select_rows(idx, rows):
    out = rows[-1]
    for j in range(len(rows) - 2, -1, -1):
        out = jnp.where(idx == j, rows[j], out)
    return out


def _moe_kernel(x_ref, sc_ref, sh_ref, g_ref, nw_ref, rwt_ref, rb_ref, upper_ref, wg_ref, wu_ref, wd_ref,
                out_ref, hb_ref, rank_ref, comb_ref, acc_ref, sel_ref, y_ref, cnt_ref):
    step = pl.program_id(1)
    tb = x_ref.shape[0]

    @pl.when(step == 0)
    def _route():
        h = _norm_mod(x_ref[...], nw_ref[...], sc_ref[...], sh_ref[...])
        hb = h.astype(BF16)
        hb_ref[...] = hb
        h_lo = (h - hb.astype(F32)).astype(BF16)
        logits = (lax.dot_general(rwt_ref[0], hb, NT, preferred_element_type=F32)
                  + lax.dot_general(rwt_ref[0], h_lo, NT, preferred_element_type=F32)
                  + lax.dot_general(rwt_ref[1], hb, NT, preferred_element_type=F32))
        s = 1.0 / (1.0 + jnp.exp(-logits))
        s_sel = s + rb_ref[...]
        groups = tb // LANES

        def pack(row):
            return jnp.concatenate([row[:, c * LANES:(c + 1) * LANES] for c in range(groups)], axis=0)

        def unpack(tile):
            return jnp.concatenate([tile[c:c + 1, :] for c in range(groups)], axis=1)

        s_rows = [pack(s[j:j + 1, :]) for j in range(N_EXPERTS)]
        sel_rows = [pack(s_sel[j:j + 1, :]) for j in range(N_EXPERTS)]
        tops = [_top2_of_rows(sel_rows[g * EPG:(g + 1) * EPG]) for g in range(N_GROUPS)]
        scores = [tp[0] + tp[2] for tp in tops]
        best = functools.reduce(jnp.maximum, scores)
        g_idx = jnp.full(best.shape, N_GROUPS - 1, jnp.int32)
        for g in range(N_GROUPS - 2, -1, -1):
            g_idx = jnp.where(scores[g] == best, g, g_idx)
        e1 = g_idx * EPG + _select_rows(g_idx, [tp[1] for tp in tops])
        e2 = g_idx * EPG + _select_rows(g_idx, [tp[3] for tp in tops])
        w1 = _select_rows(e1, s_rows)
        w2 = _select_rows(e2, s_rows)
        tot = w1 + w2
        w1, w2 = w1 / tot, w2 / tot
        masks = [jnp.where((e1 == j) | (e2 == j), 1.0, 0.0) for j in range(N_EXPERTS)]
        comb_ref[...] = jnp.concatenate(
            [unpack(jnp.where(e1 == j, w1, 0.0) + jnp.where(e2 == j, w2, 0.0)) for j in range(N_EXPERTS)], axis=0)
        mask = jnp.concatenate([unpack(m) for m in masks], axis=0)
        rank = jnp.dot(mask.astype(BF16), upper_ref[...], preferred_element_type=F32)
        rank_ref[...] = jnp.where(mask > 0.0, rank, -1.0)
        for j in range(N_EXPERTS):
            cnt_ref[j] = jnp.sum(masks[j]).astype(jnp.int32)
        acc_ref[...] = jnp.zeros_like(acc_ref)

    def expert_tile(j, tile_idx):
        e = step * MOE_EXPERTS_PER_STEP + j
        rank = rank_ref[pl.ds(e, 1), :]
        comb = comb_ref[pl.ds(e, 1), :]
        slot = (lax.broadcasted_iota(jnp.int32, (MOE_TILE, tb), 0) + tile_idx * MOE_TILE).astype(F32)
        hit = rank == slot
        onehot = jnp.where(hit, 1.0, 0.0).astype(BF16)
        xe = jnp.dot(onehot, hb_ref[...], preferred_element_type=F32).astype(BF16)
        gate = jnp.dot(xe, wg_ref[j], preferred_element_type=F32)
        up = jnp.dot(xe, wu_ref[j], preferred_element_type=F32)
        y = jnp.dot((_silu(gate) * up).astype(BF16), wd_ref[j], preferred_element_type=F32)
        w_slot = jnp.sum(jnp.where(hit, comb, 0.0), axis=1, keepdims=True)
        return onehot, (y * w_slot).astype(BF16)

    experts = range(MOE_EXPERTS_PER_STEP)
    slot0 = lax.broadcasted_iota(jnp.int32, (MOE_TILE, tb), 0).astype(F32)
    hits = [rank_ref[pl.ds(step * MOE_EXPERTS_PER_STEP + j, 1), :] == slot0 for j in experts]
    for j in experts:
        sel_ref[j * MOE_TILE:(j + 1) * MOE_TILE, :] = jnp.where(hits[j], 1.0, 0.0).astype(BF16)
    xes = [jnp.dot(sel_ref[j * MOE_TILE:(j + 1) * MOE_TILE, :], hb_ref[...],
                   preferred_element_type=F32).astype(BF16) for j in experts]
    hidden = [(jnp.dot(xes[j], wg_ref[j], preferred_element_type=F32),
               jnp.dot(xes[j], wu_ref[j], preferred_element_type=F32)) for j in experts]
    acts = [(_silu(gate) * up).astype(BF16) for gate, up in hidden]
    ys = [jnp.dot(acts[j], wd_ref[j], preferred_element_type=F32) for j in experts]
    for j in experts:
        comb = comb_ref[pl.ds(step * MOE_EXPERTS_PER_STEP + j, 1), :]
        w_slot = jnp.sum(jnp.where(hits[j], comb, 0.0), axis=1, keepdims=True)
        y_ref[j * MOE_TILE:(j + 1) * MOE_TILE, :] = (ys[j] * w_slot).astype(BF16)

    acc_ref[...] += lax.dot_general(sel_ref[...], y_ref[...], TN, preferred_element_type=F32)

    for j in experts:
        def overflow(tile_idx, carry, j=j):
            onehot, yw = expert_tile(j, tile_idx)
            acc_ref[...] += lax.dot_general(onehot, yw, TN, preferred_element_type=F32)
            return carry

        n_tiles = (cnt_ref[step * MOE_EXPERTS_PER_STEP + j] + MOE_TILE - 1) // MOE_TILE
        lax.fori_loop(1, n_tiles, overflow, 0)

    @pl.when(step == pl.num_programs(1) - 1)
    def _finish():
        out_ref[...] = x_ref[...] + g_ref[...] * acc_ref[...]


def _moe(x, sc, sh, g, nw, router_w, router_b, w_gate, w_up, w_down, seq, row_fn, shared_mod):
    t, d = x.shape
    tb = min(t if shared_mod else seq, 1024)
    per_seq = t if shared_mod else seq // tb
    dff = w_gate.shape[-1]
    upper = jnp.asarray(np.triu(np.ones((tb, tb), np.float32), 1), BF16)
    mod_spec = pl.BlockSpec((None, 1, d), lambda i, e: (row_fn(i // per_seq), 0, 0))
    tok = pl.BlockSpec((tb, d), lambda i, e: (i, 0))
    per = MOE_EXPERTS_PER_STEP
    stacked = per * MOE_TILE
    rw_hi = router_w.T.astype(BF16)
    rwt = jnp.stack([rw_hi, (router_w.T - rw_hi.astype(F32)).astype(BF16)])
    return pl.pallas_call(
        _moe_kernel,
        grid=(t // tb, N_EXPERTS // per),
        in_specs=[tok, mod_spec, mod_spec, mod_spec, _const_spec((1, d)), _const_spec((2, N_EXPERTS, d)),
                  _const_spec((N_EXPERTS, 1)), _const_spec(upper.shape),
                  pl.BlockSpec((per, d, dff), lambda i, e: (e, 0, 0)),
                  pl.BlockSpec((per, d, dff), lambda i, e: (e, 0, 0)),
                  pl.BlockSpec((per, dff, d), lambda i, e: (e, 0, 0))],
        out_specs=tok,
        out_shape=jax.ShapeDtypeStruct((t, d), F32),
        scratch_shapes=[pltpu.VMEM((tb, d), BF16), pltpu.VMEM((N_EXPERTS, tb), F32), pltpu.VMEM((N_EXPERTS, tb), F32),
                        pltpu.VMEM((tb, d), F32), pltpu.VMEM((stacked, tb), BF16), pltpu.VMEM((stacked, d), BF16),
                        pltpu.SMEM((N_EXPERTS,), jnp.int32)],
        compiler_params=_params(("parallel", "arbitrary")),
        name="moe",
    )(x, sc, sh, g, nw, rwt, router_b.reshape(N_EXPERTS, 1), upper, w_gate, w_up, w_down)


def _proj1_kernel(x_ref, sc_ref, sh_ref, nw_ref, w_ref, wlr_ref, p_ref, lr_ref):
    h = _norm_mod(x_ref[...], nw_ref[...], sc_ref[...], sh_ref[...]).astype(BF16)
    p_ref[...] = jnp.dot(h, w_ref[...], preferred_element_type=F32).astype(p_ref.dtype)
    lr_ref[...] = jnp.dot(h, wlr_ref[...], preferred_element_type=F32)


def _proj1(x, sc, sh, nw, w_main, w_lr, seq, row_fn):
    t, d = x.shape
    tm = min(seq, 512)
    per_seq = seq // tm
    mod_spec = pl.BlockSpec((None, 1, d), lambda i: (row_fn(i // per_seq), 0, 0))
    tok = lambda n: pl.BlockSpec((tm, n), lambda i: (i, 0))
    return pl.pallas_call(
        _proj1_kernel,
        grid=(t // tm,),
        in_specs=[tok(d), mod_spec, mod_spec, _const_spec((1, d)), _const_spec(w_main.shape), _const_spec(w_lr.shape)],
        out_specs=[tok(w_main.shape[1]), tok(LANES)],
        out_shape=[jax.ShapeDtypeStruct((t, w_main.shape[1]), BF16), jax.ShapeDtypeStruct((t, LANES), F32)],
        compiler_params=_params(("parallel",)),
        name="proj1",
    )(x, sc, sh, nw, w_main, w_lr)


def _log_sigmoid(x):
    return jnp.minimum(x, 0.0) - jnp.log(1.0 + jnp.exp(-jnp.abs(x)))


def _gla_kernel(qf_ref, kf_ref, vf_ref, lf_ref, qb_ref, kb_ref, vb_ref, lb_ref, wgf_ref, wgb_ref, bg_ref,
                trif_ref, trib_ref, s0f_ref, s0b_ref, of_ref, ob_ref, sf_ref, sb_ref, stf_ref, stb_ref, *, dk, dv):
    i = pl.program_id(1)
    nb = pl.num_programs(1)
    tc = qf_ref.shape[0]

    @pl.when(i == 0)
    def _init():
        stf_ref[...] = s0f_ref[...].astype(F32)
        stb_ref[...] = s0b_ref[...].astype(F32)

    r = lax.broadcasted_iota(jnp.int32, (GLA_CHUNK, GLA_CHUNK), 0)
    c = lax.broadcasted_iota(jnp.int32, (GLA_CHUNK, GLA_CHUNK), 1)
    scale = dk ** -0.5

    def decay_sums(l_ref, wg_ref, bias, tri_ref):
        gate = _log_sigmoid(jnp.dot(l_ref[...], wg_ref[...], precision=HI, preferred_element_type=F32)
                            + bias) / GLA_GATE_NORM
        total = None
        rest = gate
        for _ in range(3):
            part = rest.astype(BF16)
            rest = rest - part.astype(F32)
            term = jnp.dot(tri_ref[...], part, preferred_element_type=F32)
            total = term if total is None else total + term
        return total

    b_f = decay_sums(lf_ref, wgf_ref, bg_ref[0:1, :], trif_ref)
    b_b = decay_sums(lb_ref, wgb_ref, bg_ref[1:2, :], trib_ref)
    n_chunks = tc // GLA_CHUNK
    chains = []
    for h in range(GLA_HEADS):
        chains.append((qf_ref, kf_ref, vf_ref, of_ref, b_f, stf_ref, h, False))
        chains.append((qb_ref, kb_ref, vb_ref, ob_ref, b_b, stb_ref, h, True))
    states = [st_ref[h] for (_, _, _, _, _, st_ref, h, _) in chains]
    for step in range(n_chunks):
        prepared = []
        for (q_ref, k_ref, v_ref, _, b_all, _, h, reverse) in chains:
            ci = n_chunks - 1 - step if reverse else step
            rs = slice(ci * GLA_CHUNK, (ci + 1) * GLA_CHUNK)
            ks, vs = slice(h * dk, (h + 1) * dk), slice(h * dv, (h + 1) * dv)
            b = b_all[rs, ks]
            b_end = b[0:1, :] if reverse else b[GLA_CHUNK - 1:GLA_CHUNK, :]
            q_dec = (q_ref[rs, ks].astype(F32) * scale * jnp.exp(b)).astype(BF16)
            k_dec = k_ref[rs, ks].astype(F32) * jnp.exp(-b)
            e_end = jnp.exp(b_end)
            prepared.append((rs, vs, q_dec, k_dec.astype(BF16), (k_dec * e_end).astype(BF16), e_end, v_ref[rs, vs]))
        first = []
        for n, (rs, vs, q_dec, k_bf, k_rem, e_end, v) in enumerate(prepared):
            att = lax.dot_general(q_dec, k_bf, NT, preferred_element_type=F32)
            carry = lax.dot_general(q_dec, states[n].astype(BF16), NT, preferred_element_type=F32)
            upd = lax.dot_general(v, k_rem, TN, preferred_element_type=F32)
            first.append((att, carry, upd))
        for n, (rs, vs, q_dec, k_bf, k_rem, e_end, v) in enumerate(prepared):
            att, carry, upd = first[n]
            reverse = chains[n][7]
            att = jnp.where((c >= r) if reverse else (c <= r), att, 0.0).astype(BF16)
            o = jnp.dot(att, v, preferred_element_type=F32) + carry
            chains[n][3][rs, vs] = o.astype(chains[n][3].dtype)
            states[n] = states[n] * e_end + upd
    for n, (_, _, _, _, _, st_ref, h, _) in enumerate(chains):
        st_ref[h] = states[n]

    @pl.when(i == nb - 1)
    def _emit():
        sf_ref[...] = stf_ref[...]
        sb_ref[...] = stb_ref[...]


def _gla(p, lr, wgf, wgb, bg, s0f, s0b, batch, seq, dk, dv):
    tc = min(seq, 256)
    nb = seq // tc
    hk, hv = GLA_HEADS * dk, GLA_HEADS * dv
    fwd = lambda width, col: pl.BlockSpec((tc, width), lambda b, i: (b * nb + i, col))
    bwd = lambda width, col: pl.BlockSpec((tc, width), lambda b, i: (b * nb + nb - 1 - i, col))
    st_spec = pl.BlockSpec((None, GLA_HEADS, dv, dk), lambda b, i: (b, 0, 0, 0))
    v_col = 2 * hk // hv
    idx = np.arange(tc)
    same = (idx[:, None] // GLA_CHUNK) == (idx[None, :] // GLA_CHUNK)
    tri_f = jnp.asarray(same & (idx[None, :] <= idx[:, None]), BF16)
    tri_b = jnp.asarray(same & (idx[None, :] >= idx[:, None]), BF16)
    return pl.pallas_call(
        functools.partial(_gla_kernel, dk=dk, dv=dv),
        grid=(batch, nb),
        in_specs=[fwd(hk, 0), fwd(hk, 1), fwd(hv, v_col), fwd(LANES, 0),
                  bwd(hk, 0), bwd(hk, 1), bwd(hv, v_col), bwd(LANES, 0),
                  _const_spec(wgf.shape), _const_spec(wgb.shape), _const_spec(bg.shape),
                  _const_spec(tri_f.shape), _const_spec(tri_b.shape), st_spec, st_spec],
        out_specs=[fwd(hv, 0), bwd(hv, 0), st_spec, st_spec],
        out_shape=[jax.ShapeDtypeStruct((batch * seq, hv), BF16)] * 2
        + [jax.ShapeDtypeStruct((batch, GLA_HEADS, dv, dk), F32)] * 2,
        scratch_shapes=[pltpu.VMEM((GLA_HEADS, dv, dk), F32)] * 2,
        compiler_params=_params(("parallel", "arbitrary")),
        name="gla_scan",
    )(p, p, p, lr, p, p, p, lr, wgf, wgb, bg, tri_f, tri_b, s0f, s0b)


def _outproj1_kernel(x_ref, of_ref, ob_ref, go_ref, g_ref, nw_ref, w_ref, out_ref, *, dv):
    o = of_ref[...].astype(F32) + ob_ref[...].astype(F32)
    parts = []
    for h in range(GLA_HEADS):
        oh = o[:, h * dv:(h + 1) * dv]
        ms = jnp.mean(oh * oh, axis=1, keepdims=True)
        parts.append(oh * lax.rsqrt(ms + EPS))
    o = jnp.concatenate(parts, axis=1) * nw_ref[...] * _silu(go_ref[...].astype(F32))
    out_ref[...] = x_ref[...] + g_ref[...] * jnp.dot(o.astype(BF16), w_ref[...], preferred_element_type=F32)


def _outproj1(x, o_f, o_b, p, g, out_norm_w, w_out, seq, row_fn, dv):
    t, d = x.shape
    hv = GLA_HEADS * dv
    tm = min(seq, 512)
    per_seq = seq // tm
    tok = lambda n, col=0: pl.BlockSpec((tm, n), lambda i: (i, col))
    go_col = (p.shape[1] - hv) // hv
    nw = jnp.tile(out_norm_w, GLA_HEADS).reshape(1, hv)
    wb = w_out.astype(BF16)
    return pl.pallas_call(
        functools.partial(_outproj1_kernel, dv=dv),
        grid=(t // tm,),
        in_specs=[tok(d), tok(hv), tok(hv), tok(hv, go_col),
                  pl.BlockSpec((None, 1, d), lambda i: (row_fn(i // per_seq), 0, 0)),
                  _const_spec((1, hv)), _const_spec(wb.shape)],
        out_specs=tok(d),
        out_shape=jax.ShapeDtypeStruct((t, d), F32),
        compiler_params=_params(("parallel",)),
        name="outproj1",
    )(x, o_f, o_b, p, g, nw, wb)


def _mod_rows(mod_l, d):
    return [mod_l[:, j * d:(j + 1) * d].reshape(mod_l.shape[0], 1, d) for j in range(6)]


def _trunk(x, batch, seq, mods, row_fn, shared_mod, norm_w, ab, cp, moe_w, cache, s0, emit_cache):
    d = x.shape[-1]
    t = batch * seq
    x = x.reshape(t, d)
    (w_in0, conv_w, conv_b, filt, hy_skip, qn_w, kn_w, lam_p, subln_w, w_out0) = ab
    (w_in1, gate_up_w, gate_up_b, out_norm_w, w_out1) = cp
    router_w, router_b, wg, wu, wd = moe_w
    width = d // 2
    hy_in = 3 * width

    sh1, sc1, g1, sh2, sc2, g2 = mods[0]
    nseg = DA_HEADS * 2
    qkw = jnp.stack([jnp.tile(qn_w, nseg), jnp.tile(kn_w, nseg)])
    seg = np.arange(DA_HEADS * DA_VD) // DA_DH
    ones = jnp.asarray(seg[:, None] == seg[None, :], BF16)
    v_cols = w_in0[:, hy_in + 2 * DA_HEADS * DA_VD:]
    w_chan = jnp.concatenate([w_in0[:, :hy_in], v_cols], axis=1).T.astype(BF16)
    outs = _proj0(x, sc1, sh1, norm_w[0, 0].reshape(1, d), w_in0[:, hy_in:].astype(BF16), w_chan, qkw, ones,
                  batch, seq, row_fn, rope=cache is not None, emit_cache=emit_cache)
    chan, q, k = outs[:3]
    y_hy = _hyena(chan, conv_w, conv_b, filt, hy_skip, seq, width)
    lam_init = 0.8 - 0.6 * math.exp(-0.3 * 0)
    o = _diff_attention(q, k, chan, hy_in, cache, lam_p, subln_w, batch, seq, lam_init)
    x = _outproj0(x, y_hy, o, g1, w_out0, seq, row_fn)
    x = _moe(x, sc2, sh2, g2, norm_w[0, 1].reshape(1, d), router_w, router_b, wg[0], wu[0], wd[0], seq, row_fn,
             shared_mod)

    sh1, sc1, g1, sh2, sc2, g2 = mods[1]
    dk = d // 2 // GLA_HEADS
    dv = d // GLA_HEADS
    n_main = GLA_HEADS * (2 * dk + 2 * dv)
    w_lr = jnp.pad(w_in1[:, n_main:], ((0, 0), (0, LANES - 2 * GLA_RANK))).astype(BF16)
    p, lr = _proj1(x, sc1, sh1, norm_w[1, 0].reshape(1, d), w_in1[:, :n_main].astype(BF16), w_lr, seq, row_fn)
    wgf = jnp.pad(gate_up_w[0], ((0, LANES - GLA_RANK), (0, 0)))
    wgb = jnp.pad(gate_up_w[1], ((GLA_RANK, LANES - 2 * GLA_RANK), (0, 0)))
    o_f, o_b, s_f, s_b = _gla(p, lr, wgf, wgb, gate_up_b, jnp.swapaxes(s0[0], 2, 3), jnp.swapaxes(s0[1], 2, 3),
                              batch, seq, dk, dv)
    s_f, s_b = jnp.swapaxes(s_f, 2, 3), jnp.swapaxes(s_b, 2, 3)
    x = _outproj1(x, o_f, o_b, p, g1, out_norm_w, w_out1, seq, row_fn, dv)
    x = _moe(x, sc2, sh2, g2, norm_w[1, 1].reshape(1, d), router_w, router_b, wg[1], wu[1], wd[1], seq, row_fn,
             shared_mod)
    return x.reshape(batch, seq, d), outs[3:], (s_f, s_b)


def kernel(x_prompt, x_sample, cache_l0_k, cache_l0_v, state_l1_fwd, state_l1_bwd, c, c_ctx, w_mod, b_mod, norm_w,
           l0_w_in, l0_conv_w, l0_conv_b, l0_filt_w1, l0_filt_b1, l0_filt_freq, l0_filt_w2, l0_filt_b2, l0_filt_w3,
           l0_hy_skip, l0_qn_w, l0_kn_w, l0_lambda, l0_subln_w, l0_w_out, l1_w_in, l1_gate_up_w, l1_gate_up_b,
           l1_out_norm_w, l1_w_out, moe_router_w, moe_router_b, moe_w_gate, moe_w_up, moe_w_down):
    d = x_prompt.shape[-1]
    n_lat = c.shape[0]
    ctx_row = n_lat
    rows = 2 * SUBLANES
    cond = jnp.zeros((rows, d), F32).at[:n_lat].set(c).at[ctx_row].set(c_ctx)
    mod = _modulation(cond, w_mod, b_mod)
    mods = [_mod_rows(mod[l], d) for l in range(mod.shape[0])]
    filt = (l0_filt_w1, l0_filt_b1, l0_filt_freq, l0_filt_w2, l0_filt_b2, l0_filt_w3)
    ab = (l0_w_in, l0_conv_w, l0_conv_b, filt, l0_hy_skip, l0_qn_w, l0_kn_w, l0_lambda, l0_subln_w, l0_w_out)
    cp = (l1_w_in, l1_gate_up_w, l1_gate_up_b, l1_out_norm_w, l1_w_out)
    moe_w = (moe_router_w, moe_router_b, moe_w_gate.astype(BF16), moe_w_up.astype(BF16), moe_w_down.astype(BF16))

    b_ctx, l_ctx = x_prompt.shape[:2]
    zero_state = jnp.zeros((b_ctx,) + state_l1_fwd.shape[1:], F32)
    y_prompt, (new_k, new_v), (new_sf, new_sb) = _trunk(
        x_prompt, b_ctx, l_ctx, mods, lambda b: ctx_row, True, norm_w, ab, cp, moe_w, None,
        (zero_state, zero_state), True)
    b_lat, l_lat = x_sample.shape[:2]
    y_sample, _, _ = _trunk(
        x_sample, b_lat, l_lat, mods, lambda b: b, False, norm_w, ab, cp, moe_w, (cache_l0_k, cache_l0_v),
        (state_l1_fwd, state_l1_bwd), False)
    return (y_prompt, y_sample, new_k, new_v, new_sf, new_sb)
```

```python
import cmath
import functools
import math

import numpy as np
import jax
import jax.numpy as jnp
from jax import lax
from jax.experimental import pallas as pl
from jax.experimental.pallas import tpu as pltpu

F32 = jnp.float32
BF16 = jnp.bfloat16
HI = lax.Precision.HIGHEST
EPS = 1e-6

LANES = 128
SUBLANES = 8
VMEM_LIMIT = 56 << 20

DA_HEADS = 4
DA_DH = 64
DA_VD = 2 * DA_DH
ROPE_NF = DA_DH // 4
ROPE_BASE = 10000.0
GRID_W = 64
HY_BANDS = 16
HY_TARGET = 1e-2
HY_MAX_DECAY = math.log(HY_TARGET) / 0.3
HY_MIN_DECAY = math.log(HY_TARGET) / 1.5
GLA_HEADS = 4
GLA_RANK = 16
GLA_GATE_NORM = 16.0
GLA_CHUNK = 64
N_EXPERTS = 16
N_GROUPS = 4
EPG = N_EXPERTS // N_GROUPS
MOE_TILE = 160
MOE_EXPERTS_PER_STEP = 4
NT = (((1,), (1,)), ((), ()))
TN = (((0,), (0,)), ((), ()))


def _params(sem):
    return pltpu.CompilerParams(dimension_semantics=sem, vmem_limit_bytes=VMEM_LIMIT)


def _const_spec(shape):
    nd = len(shape)
    return pl.BlockSpec(shape, lambda *_: (0,) * nd)


def _silu(x):
    return x * (1.0 / (1.0 + jnp.exp(-x)))


def _norm_mod(x, nw, sc, sh):
    ms = jnp.mean(x * x, axis=-1, keepdims=True)
    return x * lax.rsqrt(ms + EPS) * nw * (1.0 + sc) + sh


def _mod_kernel(c_ref, w_ref, b_ref, o_ref):
    s = _silu(c_ref[...])
    o_ref[...] = jnp.dot(s, w_ref[...], precision=HI, preferred_element_type=F32) + b_ref[...]


def _modulation(cond, w_mod, b_mod):
    depth, d, d6 = w_mod.shape
    r = cond.shape[0]
    tn = 1536
    return pl.pallas_call(
        _mod_kernel,
        grid=(depth, d6 // tn),
        in_specs=[
            pl.BlockSpec((r, d), lambda l, j: (0, 0)),
            pl.BlockSpec((None, d, tn), lambda l, j: (l, 0, j)),
            pl.BlockSpec((None, 1, tn), lambda l, j: (l, 0, j)),
        ],
        out_specs=pl.BlockSpec((None, r, tn), lambda l, j: (l, 0, j)),
        out_shape=jax.ShapeDtypeStruct((depth, r, d6), F32),
        compiler_params=_params(("parallel", "parallel")),
        name="modulation",
    )(cond, w_mod, b_mod.reshape(depth, 1, d6))


def _proj0_kernel(x_ref, sc_ref, sh_ref, nw_ref, wtok_ref, wchan_ref, qkw_ref, ones_ref, cos_ref, sin_ref,
                  chan_ref, q_ref, k_ref, *cache_refs, rope):
    h = _norm_mod(x_ref[...], nw_ref[...], sc_ref[...], sh_ref[...]).astype(BF16)
    chan_ref[...] = lax.dot_general(wchan_ref[...], h, NT, preferred_element_type=F32).astype(chan_ref.dtype)
    p = jnp.dot(h, wtok_ref[...], preferred_element_type=F32)
    w = DA_HEADS * DA_VD
    ones = ones_ref[...]

    def segnorm(z, gain):
        ss = jnp.dot((z * z).astype(BF16), ones, preferred_element_type=F32)
        return z * lax.rsqrt(ss * (1.0 / DA_DH) + EPS) * gain

    q = segnorm(p[:, :w], qkw_ref[0:1, :])
    k = segnorm(p[:, w:2 * w], qkw_ref[1:2, :])
    if cache_refs:
        kn_ref, vn_ref = cache_refs
        v = p[:, 2 * w:]
        for hh in range(DA_HEADS):
            kn_ref[hh] = k[:, hh * DA_VD:(hh + 1) * DA_VD]
            vn_ref[hh] = v[:, hh * DA_VD:(hh + 1) * DA_VD]
    if rope:
        cos = jnp.concatenate([cos_ref[...]] * (w // LANES), axis=1)
        sin = jnp.concatenate([sin_ref[...]] * (w // LANES), axis=1)
        lane = lax.broadcasted_iota(jnp.int32, q.shape, 1)
        first = (lane % (2 * ROPE_NF)) < ROPE_NF

        def rot(z):
            partner = jnp.where(first, pltpu.roll(z, w - ROPE_NF, 1), pltpu.roll(z, ROPE_NF, 1))
            return z * cos + partner * sin

        q = rot(q)
        k = rot(k)
    q_ref[...] = (q * (DA_DH ** -0.5 * math.log2(math.e))).astype(q_ref.dtype)
    k_ref[...] = k.astype(k_ref.dtype)


def _rope_tables(seq):
    t = np.arange(seq)
    lane = np.arange(LANES)
    d = lane % DA_DH
    axis = d // (2 * ROPE_NF)
    part = (d % (2 * ROPE_NF)) // ROPE_NF
    f = d % ROPE_NF
    pos = jnp.where(axis[None, :] == 0, (t // GRID_W)[:, None], (t % GRID_W)[:, None]).astype(F32)
    inv = ROPE_BASE ** (-jnp.arange(ROPE_NF, dtype=F32) / ROPE_NF)
    ang = pos * inv[f][None, :]
    sign = jnp.asarray(np.where(part == 0, -1.0, 1.0), F32)[None, :]
    return jnp.cos(ang), jnp.sin(ang) * sign


def _proj0(x, sc, sh, nw, wqkv, whyt, qkw, ones, batch, seq, row_fn, rope, emit_cache):
    t, d = x.shape
    tm = min(seq, 512)
    per_seq = seq // tm
    w = DA_HEADS * DA_VD
    hy_in = whyt.shape[0]
    if not emit_cache:
        wqkv = wqkv[:, :2 * w]
    if rope:
        cos, sin = _rope_tables(seq)
    else:
        cos = jnp.ones((seq, LANES), F32)
        sin = jnp.zeros((seq, LANES), F32)
    mod_spec = pl.BlockSpec((None, 1, d), lambda i: (row_fn(i // per_seq), 0, 0))
    tab_spec = pl.BlockSpec((tm, LANES), lambda i: (i % per_seq, 0))
    tok = lambda n: pl.BlockSpec((tm, n), lambda i: (i, 0))
    out_specs = [pl.BlockSpec((None, hy_in, tm), lambda i: (i // per_seq, 0, i % per_seq)), tok(w), tok(w)]
    out_shape = [jax.ShapeDtypeStruct((batch, hy_in, seq), BF16)] + [jax.ShapeDtypeStruct((t, w), BF16)] * 2
    if emit_cache:
        cspec = pl.BlockSpec((None, DA_HEADS, tm, DA_VD), lambda i: (i // per_seq, 0, i % per_seq, 0))
        out_specs += [cspec, cspec]
        out_shape += [jax.ShapeDtypeStruct((batch, DA_HEADS, seq, DA_VD), F32)] * 2
    return pl.pallas_call(
        functools.partial(_proj0_kernel, rope=rope),
        grid=(t // tm,),
        in_specs=[tok(d), mod_spec, mod_spec, _const_spec((1, d)), _const_spec(wqkv.shape), _const_spec(whyt.shape),
                  _const_spec(qkw.shape), _const_spec(ones.shape), tab_spec, tab_spec],
        out_specs=out_specs,
        out_shape=out_shape,
        compiler_params=_params(("parallel",)),
        name="proj0",
    )(x, sc, sh, nw, wqkv, whyt, qkw, ones, cos, sin)


def _c_add(a, b, sign=1.0):
    if b is None:
        return a
    if a is None:
        return b if sign > 0 else tuple(None if p is None else -p for p in b)
    out = []
    for pa, pb in zip(a, b):
        if pb is None:
            out.append(pa)
        elif pa is None:
            out.append(pb if sign > 0 else -pb)
        else:
            out.append(pa + pb if sign > 0 else pa - pb)
    return tuple(out)


def _c_mul_const(w, a):
    if a is None:
        return None
    wr = 0.0 if abs(w.real) < 1e-12 else w.real
    wi = 0.0 if abs(w.imag) < 1e-12 else w.imag
    ar, ai = a

    def scaled(c, p):
        if p is None or c == 0.0:
            return None
        if c == 1.0:
            return p
        if c == -1.0:
            return -p
        return c * p

    re = _c_add((scaled(wr, ar),), (scaled(wi, ai),), -1.0)[0]
    im = _c_add((scaled(wr, ai),), (scaled(wi, ar),), 1.0)[0]
    return (re, im)


def _fft_list(vals, sign, first_half_only=False):
    n = len(vals)
    if n == 1:
        return list(vals)
    ev = _fft_list(vals[0::2], sign)
    od = _fft_list(vals[1::2], sign)
    out = [None] * n
    for k in range(n // 2):
        tw = _c_mul_const(cmath.exp(sign * 2j * math.pi * k / n), od[k])
        out[k] = _c_add(ev[k], tw, 1.0)
        if not first_half_only:
            out[k + n // 2] = _c_add(ev[k], tw, -1.0)
    return out


def _dft_mats(n2, total):
    k = np.arange(n2)
    ang = 2.0 * np.pi * np.outer(k, k) / n2
    fr, fi = np.cos(ang), -np.sin(ang)
    fwd = np.block([[fr, fi], [-fi, fr]])
    inv = np.block([[fr, -fi], [fi, fr]]) / total
    return jnp.asarray(fwd, BF16), jnp.asarray(inv, BF16)


def _twiddles(n1, n2):
    ang = 2.0 * np.pi * np.outer(np.arange(n1), np.arange(n2)) / (n1 * n2)
    return jnp.asarray(np.cos(ang), F32), jnp.asarray(-np.sin(ang), F32)


def _fft_split(seq):
    n = 2 * seq
    n2 = min(256, n // 2)
    return n // n2, n2


def _across_fwd(load, n_in, z_ref, twr_ref, twi_ref, n1, n2, rows):
    per_row = n2 // LANES

    def body(it, carry):
        r0 = pl.multiple_of((it // per_row) * SUBLANES, SUBLANES)
        l0 = pl.multiple_of((it % per_row) * LANES, LANES)
        vals = [load(j, r0, l0) for j in range(n_in)] + [None] * (n1 - n_in)
        outs = _fft_list(vals, -1.0)
        for k1 in range(n1):
            twr = twr_ref[pl.ds(k1, 1), pl.ds(l0, LANES)]
            twi = twi_ref[pl.ds(k1, 1), pl.ds(l0, LANES)]
            re, im = outs[k1]
            zero = jnp.zeros((SUBLANES, LANES), F32)
            re = zero if re is None else re
            im = zero if im is None else im
            z_ref[k1, pl.ds(r0, SUBLANES), pl.ds(l0, LANES)] = re * twr - im * twi
            z_ref[k1, pl.ds(r0, SUBLANES), pl.ds(n2 + l0, LANES)] = re * twi + im * twr
        return carry

    lax.fori_loop(0, (rows // SUBLANES) * per_row, body, 0)


def _hy_hidden_kernel(w1_ref, b1_ref, freq_ref, w2_ref, b2_ref, bands_ref, o_ref, *, seq):
    n = 2 * seq
    k = lax.broadcasted_iota(jnp.int32, (n, 1), 0)
    pos = jnp.where(k < seq, k, n - k).astype(F32)
    t = pos / (seq - 1)
    ang = 2.0 * math.pi * pos / seq * bands_ref[...]
    z = (t * w1_ref[0:1, :]
         + jnp.dot(jnp.cos(ang), w1_ref[1:1 + HY_BANDS, :], precision=HI, preferred_element_type=F32)
         + jnp.dot(-jnp.sin(ang), w1_ref[1 + HY_BANDS:, :], precision=HI, preferred_element_type=F32)
         + b1_ref[...])
    hid = jnp.sin(freq_ref[0:1, :] * z)
    hid = jnp.sin(freq_ref[1:2, :] * (jnp.dot(hid, w2_ref[...], precision=HI, preferred_element_type=F32) + b2_ref[...]))
    o_ref[...] = hid


def _hy_hidden(seq, f_w1, f_b1, f_freq, f_w2, f_b2):
    ffn = f_w1.shape[1]
    bands = jnp.linspace(1e-4, HY_BANDS - 1, HY_BANDS, dtype=F32).reshape(1, HY_BANDS)
    args = (f_w1, f_b1.reshape(1, ffn), f_freq, f_w2, f_b2.reshape(1, ffn), bands)
    return pl.pallas_call(
        functools.partial(_hy_hidden_kernel, seq=seq),
        grid=(1,),
        in_specs=[_const_spec(a.shape) for a in args],
        out_specs=_const_spec((2 * seq, ffn)),
        out_shape=jax.ShapeDtypeStruct((2 * seq, ffn), F32),
        compiler_params=_params(("arbitrary",)),
        name="hyena_filter_hidden",
    )(*args)


def _hy_spec_kernel(hid_ref, wb_ref, wa_ref, dec_ref, twr_ref, twi_ref, gf_ref, o_ref, taps_ref, z_ref, *, seq, n1, n2):
    ct = wb_ref.shape[0]
    hid = hid_ref[...]
    back = lax.dot_general(wb_ref[...], hid[:seq], NT, precision=HI, preferred_element_type=F32)
    ahead = lax.dot_general(wa_ref[...], hid[seq:], NT, precision=HI, preferred_element_type=F32)
    lane = lax.broadcasted_iota(jnp.int32, (1, seq), 1)
    dec = dec_ref[...]
    t_back = lane.astype(F32) / (seq - 1)
    t_ahead = (seq - lane).astype(F32) / (seq - 1)
    back = back * jnp.exp(-t_back * dec)
    ahead = jnp.where(lane == 0, 0.0, ahead * jnp.exp(-t_ahead * dec))
    norm = (jnp.sum(jnp.abs(back), axis=1, keepdims=True) + jnp.sum(jnp.abs(ahead), axis=1, keepdims=True)) + EPS
    taps_ref[:, :seq] = back / norm
    taps_ref[:, seq:] = ahead / norm

    def load(j, r0, l0):
        return (taps_ref[pl.ds(r0, SUBLANES), pl.ds(j * n2 + l0, LANES)], None)

    _across_fwd(load, n1, z_ref, twr_ref, twi_ref, n1, n2, ct)
    for k1 in range(n1):
        o_ref[k1] = jnp.dot(z_ref[k1].astype(BF16), gf_ref[...], preferred_element_type=F32)


def _hy_filter_spectra(hid, f_w3, seq, width, ct):
    n1, n2 = _fft_split(seq)
    ffn = f_w3.shape[0]
    w3t = f_w3.T.reshape(4, width, ffn)
    decay = jnp.abs(jnp.linspace(HY_MIN_DECAY, HY_MAX_DECAY, width, dtype=F32)).reshape(width, 1)
    twr, twi = _twiddles(n1, n2)
    gf, _ = _dft_mats(n2, 2 * seq)
    return pl.pallas_call(
        functools.partial(_hy_spec_kernel, seq=seq, n1=n1, n2=n2),
        grid=(2, width // ct),
        in_specs=[
            _const_spec(hid.shape),
            pl.BlockSpec((None, ct, ffn), lambda i, c: (2 * i, c, 0)),
            pl.BlockSpec((None, ct, ffn), lambda i, c: (2 * i + 1, c, 0)),
            pl.BlockSpec((ct, 1), lambda i, c: (c, 0)),
            _const_spec(twr.shape), _const_spec(twi.shape), _const_spec(gf.shape),
        ],
        out_specs=pl.BlockSpec((None, n1, ct, 2 * n2), lambda i, c: (i, 0, c, 0)),
        out_shape=jax.ShapeDtypeStruct((2, n1, width, 2 * n2), F32),
        scratch_shapes=[pltpu.VMEM((ct, 2 * seq), F32), pltpu.VMEM((n1, ct, 2 * n2), F32)],
        compiler_params=_params(("parallel", "parallel")),
        name="hyena_filter_spectra",
    )(hid, w3t, w3t, decay, twr, twi, gf)


def _short_conv_rows(ref, cw, r0, nrows, seq):
    x = ref[pl.ds(r0, nrows), :].astype(F32)
    lane = lax.broadcasted_iota(jnp.int32, (nrows, LANES), 1)
    left = pltpu.roll(x, 1, 1)
    right = pltpu.roll(x, seq - 1, 1)
    left = jnp.concatenate([jnp.where(lane == 0, 0.0, left[:, :LANES]), left[:, LANES:]], axis=1)
    right = jnp.concatenate([right[:, :seq - LANES], jnp.where(lane == LANES - 1, 0.0, right[:, seq - LANES:])], axis=1)
    return left * cw[:, 0:1] + x * cw[:, 1:2] + right * cw[:, 2:3] + cw[:, 3:4]


def _hy_conv_kernel(uin_ref, gin_ref, cwu_ref, cwg_ref, d_ref, h_ref, twr_ref, twi_ref, gf_ref, gi_ref,
                    o_ref, u_ref, z_ref, *, seq, n1, n2, conv_u):
    ct = uin_ref.shape[1]
    rows16 = 2 * SUBLANES

    def prep(i, carry):
        r0 = pl.multiple_of(i * rows16, rows16)
        cwg = cwg_ref[pl.ds(r0, rows16), :]
        cwu = cwu_ref[pl.ds(r0, rows16), :]
        for e in range(2):
            o_ref[e, pl.ds(r0, rows16), :] = _short_conv_rows(gin_ref.at[e], cwg, r0, rows16, seq)
            if conv_u:
                u_ref[e, pl.ds(r0, rows16), :] = _short_conv_rows(uin_ref.at[e], cwu, r0, rows16, seq)
            else:
                u_ref[e, pl.ds(r0, rows16), :] = uin_ref[e, pl.ds(r0, rows16), :].astype(F32)
        return carry

    lax.fori_loop(0, ct // rows16, prep, 0)

    def load(j, r0, l0):
        return (u_ref[0, pl.ds(r0, SUBLANES), pl.ds(j * n2 + l0, LANES)],
                u_ref[1, pl.ds(r0, SUBLANES), pl.ds(j * n2 + l0, LANES)])

    _across_fwd(load, n1 // 2, z_ref, twr_ref, twi_ref, n1, n2, ct)

    def within(pair, carry):
        ka, kb = 2 * pair, 2 * pair + 1
        zz = jnp.concatenate([z_ref[ka], z_ref[kb]], axis=0)
        y = jnp.dot(zz.astype(BF16), gf_ref[...], preferred_element_type=F32)
        hh = jnp.concatenate([h_ref[ka], h_ref[kb]], axis=0)
        yr, yi = y[:, :n2], y[:, n2:]
        hr, hi = hh[:, :n2], hh[:, n2:]
        p = jnp.concatenate([yr * hr - yi * hi, yr * hi + yi * hr], axis=1)
        q = jnp.dot(p.astype(BF16), gi_ref[...], preferred_element_type=F32)
        z_ref[ka] = q[:ct]
        z_ref[kb] = q[ct:]
        return carry

    lax.fori_loop(0, n1 // 2, within, 0)

    per_row = n2 // LANES

    def finish(it, carry):
        r0 = pl.multiple_of((it // per_row) * SUBLANES, SUBLANES)
        l0 = pl.multiple_of((it % per_row) * LANES, LANES)
        vals = []
        for k1 in range(n1):
            twr = twr_ref[pl.ds(k1, 1), pl.ds(l0, LANES)]
            twi = twi_ref[pl.ds(k1, 1), pl.ds(l0, LANES)]
            re = z_ref[k1, pl.ds(r0, SUBLANES), pl.ds(l0, LANES)]
            im = z_ref[k1, pl.ds(r0, SUBLANES), pl.ds(n2 + l0, LANES)]
            vals.append((re * twr + im * twi, im * twr - re * twi))
        outs = _fft_list(vals, 1.0, first_half_only=True)
        d = d_ref[pl.ds(r0, SUBLANES), :]
        for j in range(n1 // 2):
            sl = (pl.ds(r0, SUBLANES), pl.ds(j * n2 + l0, LANES))
            for e in range(2):
                o_ref[(e,) + sl] = o_ref[(e,) + sl] * (outs[j][e] + u_ref[(e,) + sl] * d)
        return carry

    lax.fori_loop(0, (ct // SUBLANES) * per_row, finish, 0, unroll=2 if n1 <= 16 else 1)


def _hy_conv(u, u_off, g, g_off, cw, spectra, conv_idx, d_skip, seq, width, ct, conv_u):
    batch = u.shape[0]
    n1, n2 = _fft_split(seq)
    twr, twi = _twiddles(n1, n2)
    gf, gi = _dft_mats(n2, 2 * seq)
    uo, go = u_off // ct, g_off // ct
    return pl.pallas_call(
        functools.partial(_hy_conv_kernel, seq=seq, n1=n1, n2=n2, conv_u=conv_u),
        grid=(width // ct, batch // 2),
        in_specs=[
            pl.BlockSpec((2, ct, seq), lambda c, p: (p, uo + c, 0)),
            pl.BlockSpec((2, ct, seq), lambda c, p: (p, go + c, 0)),
            pl.BlockSpec((ct, 4), lambda c, p: (uo + c, 0)),
            pl.BlockSpec((ct, 4), lambda c, p: (go + c, 0)),
            pl.BlockSpec((ct, 1), lambda c, p: (c, 0)),
            pl.BlockSpec((None, n1, ct, 2 * n2), lambda c, p: (conv_idx, 0, c, 0), pipeline_mode=pl.Buffered(1)),
            _const_spec(twr.shape), _const_spec(twi.shape), _const_spec(gf.shape), _const_spec(gi.shape),
        ],
        out_specs=pl.BlockSpec((2, ct, seq), lambda c, p: (p, c, 0)),
        out_shape=jax.ShapeDtypeStruct((batch, width, seq), F32),
        scratch_shapes=[pltpu.VMEM((2, ct, seq), F32), pltpu.VMEM((n1, ct, 2 * n2), F32)],
        compiler_params=_params(("arbitrary", "arbitrary")),
        name="hyena_conv",
    )(u, g, cw, cw, d_skip, spectra, twr, twi, gf, gi)


def _hyena(hyt, conv_w, conv_b, filt, hy_skip, seq, width):
    f_w1, f_b1, f_freq, f_w2, f_b2, f_w3 = filt
    ct = LANES
    hid = _hy_hidden(seq, f_w1, f_b1, f_freq, f_w2, f_b2)
    spectra = _hy_filter_spectra(hid, f_w3, seq, width, ct)
    cw = jnp.concatenate([conv_w.T, conv_b[:, None]], axis=1)
    y1 = _hy_conv(hyt, 0, hyt, width, cw, spectra, 0, hy_skip[0].reshape(width, 1), seq, width, ct, True)
    return _hy_conv(y1, 0, hyt, 2 * width, cw, spectra, 1, hy_skip[1].reshape(width, 1), seq, width, ct, False)


def _attn_kernel(q_ref, k_ref, vt_ref, *rest, lam_init, with_cache, kc):
    if with_cache:
        kc_ref, vct_ref, lam_ref, sw_ref, o_ref, kmax_ref = rest
    else:
        lam_ref, sw_ref, o_ref, kmax_ref = rest
    lp = lam_ref[...]
    lam = (jnp.exp(jnp.sum(lp[0:1] * lp[1:2], axis=1, keepdims=True))
           - jnp.exp(jnp.sum(lp[2:3] * lp[3:4], axis=1, keepdims=True)) + lam_init)
    q = q_ref[...]
    lane = lax.broadcasted_iota(jnp.int32, q.shape, 1)
    zero = jnp.zeros_like(q)
    qs = (jnp.where(lane < DA_DH, q, zero), jnp.where(lane >= DA_DH, q, zero))
    seq = k_ref.shape[0]
    chunks = [(k_ref[c * kc:(c + 1) * kc, :], vt_ref[:, c * kc:(c + 1) * kc]) for c in range(seq // kc)]
    if with_cache:
        chunks.append((kc_ref[...].astype(BF16), vct_ref[...].astype(BF16)))
    tq = q.shape[0]
    sub = lax.broadcasted_iota(jnp.int32, (SUBLANES, DA_VD), 0)
    ln = lax.broadcasted_iota(jnp.int32, (SUBLANES, DA_VD), 1)
    pick = jnp.where((ln // DA_DH) == sub, 1.0, 0.0).astype(BF16)

    def sq_norms(x):
        xf = x.astype(F32)
        return lax.dot_general(pick, (xf * xf).astype(BF16), NT, preferred_element_type=F32)

    @pl.when(pl.program_id(2) == 0)
    def _key_norms():
        best = None
        for kk, _ in chunks:
            cur = jnp.max(sq_norms(kk), axis=1, keepdims=True)
            best = cur if best is None else jnp.maximum(best, cur)
        kmax_ref[...] = jnp.broadcast_to(best, kmax_ref.shape)

    def scores(ci):
        return [lax.dot_general(chunks[ci][0], qm, NT, preferred_element_type=F32) for qm in qs]

    def finish(acc1, l1, acc2, l2):
        o = acc1 * (1.0 / l1) - acc2 * (lam / l2)
        ms = jnp.mean(o * o, axis=0, keepdims=True)
        o_ref[...] = (o * lax.rsqrt(ms + EPS) * sw_ref[...] * (1.0 - lam_init)).astype(o_ref.dtype)

    bound = jnp.sqrt(sq_norms(q) * kmax_ref[:, 0:1]) * 1.02
    lsum = [jnp.zeros((SUBLANES, tq), F32), jnp.zeros((SUBLANES, tq), F32)]
    acc = [None, None]
    s_next = scores(0)
    for ci, (kk, vv) in enumerate(chunks):
        s_cur = s_next
        if ci + 1 < len(chunks):
            s_next = scores(ci + 1)
        for mi in range(2):
            p = jnp.exp2(s_cur[mi] - bound[mi:mi + 1, :])
            lsum[mi] = lsum[mi] + jnp.sum(p.reshape(-1, SUBLANES, tq), axis=0)
            part = jnp.dot(vv, p.astype(BF16), preferred_element_type=F32)
            acc[mi] = part if acc[mi] is None else acc[mi] + part
    l1 = jnp.sum(lsum[0], axis=0, keepdims=True)
    l2 = jnp.sum(lsum[1], axis=0, keepdims=True)
    finish(acc[0], l1, acc[1], l2)
    safe = jnp.min(jnp.minimum(l1, l2)) > 2.0 ** -60

    @pl.when(jnp.logical_not(safe))
    def _online():
        state = [None, None]
        for ci, (kk, vv) in enumerate(chunks):
            s_cur = scores(ci)
            for mi in range(2):
                s = s_cur[mi]
                cmax = jnp.max(s, axis=0, keepdims=True)
                if state[mi] is None:
                    m = cmax
                    p = jnp.exp2(s - m)
                    l = jnp.sum(p, axis=0, keepdims=True)
                    a = jnp.dot(vv, p.astype(BF16), preferred_element_type=F32)
                else:
                    m_old, l, a = state[mi]
                    m = jnp.maximum(m_old, cmax)
                    alpha = jnp.exp2(m_old - m)
                    p = jnp.exp2(s - m)
                    l = l * alpha + jnp.sum(p, axis=0, keepdims=True)
                    a = a * alpha + jnp.dot(vv, p.astype(BF16), preferred_element_type=F32)
                state[mi] = (m, l, a)
        finish(state[0][2], state[0][1], state[1][2], state[1][1])


def _diff_attention(q, k, chan, v_row0, cache, lam_p, subln_w, batch, seq, lam_init):
    tq = min(seq, 512)
    kc = min(seq, 512)
    nq = seq // tq
    vb = v_row0 // DA_VD
    grid = (batch, DA_HEADS, nq)
    in_specs = [pl.BlockSpec((tq, DA_VD), lambda b, h, i: (b * nq + i, h)),
                pl.BlockSpec((seq, DA_VD), lambda b, h, i: (b, h)),
                pl.BlockSpec((None, DA_VD, seq), lambda b, h, i: (b, vb + h, 0))]
    args = [q, k, chan]
    if cache is not None:
        past = cache[0].shape[2]
        in_specs += [pl.BlockSpec((None, None, past, DA_VD), lambda b, h, i: (b, h, 0, 0)),
                     pl.BlockSpec((None, None, DA_VD, past), lambda b, h, i: (b, h, 0, 0))]
        args += [cache[0], jnp.swapaxes(cache[1], 2, 3)]
    in_specs += [_const_spec(lam_p.shape), _const_spec((DA_VD, 1))]
    args += [lam_p, subln_w.reshape(DA_VD, 1)]
    return pl.pallas_call(
        functools.partial(_attn_kernel, lam_init=lam_init, with_cache=cache is not None, kc=kc),
        grid=grid,
        in_specs=in_specs,
        out_specs=pl.BlockSpec((None, DA_VD, tq), lambda b, h, i: (b, h, i)),
        out_shape=jax.ShapeDtypeStruct((batch, DA_HEADS * DA_VD, seq), BF16),
        scratch_shapes=[pltpu.VMEM((SUBLANES, LANES), F32)],
        compiler_params=_params(("parallel", "parallel", "arbitrary")),
        name="diff_attention",
    )(*args)


def _outproj0_kernel(x_ref, yt_ref, ot_ref, g_ref, wy_ref, wo_ref, out_ref):
    mix = lax.dot_general(yt_ref[...].astype(BF16), wy_ref[...], TN, preferred_element_type=F32)
    mix = mix + lax.dot_general(ot_ref[...], wo_ref[...], TN, preferred_element_type=F32)
    out_ref[...] = x_ref[...] + g_ref[...] * mix


def _outproj0(x, yt, ot, g, w_out, seq, row_fn):
    t, d = x.shape
    width = yt.shape[1]
    tm = min(seq, 512)
    per_seq = seq // tm
    wy = w_out[:width].astype(BF16)
    wo = w_out[width:].astype(BF16)
    tok = pl.BlockSpec((tm, d), lambda i: (i, 0))
    chan = lambda n: pl.BlockSpec((None, n, tm), lambda i: (i // per_seq, 0, i % per_seq))
    return pl.pallas_call(
        _outproj0_kernel,
        grid=(t // tm,),
        in_specs=[tok, chan(width), chan(ot.shape[1]),
                  pl.BlockSpec((None, 1, d), lambda i: (row_fn(i // per_seq), 0, 0)),
                  _const_spec(wy.shape), _const_spec(wo.shape)],
        out_specs=tok,
        out_shape=jax.ShapeDtypeStruct((t, d), F32),
        compiler_params=_params(("parallel",)),
        name="outproj0",
    )(x, yt, ot, g, wy, wo)


def _top2_of_rows(rows):
    n = len(rows)
    v1 = functools.reduce(jnp.maximum, rows)
    i1 = jnp.full(rows[0].shape, n - 1, jnp.int32)
    for j in range(n - 2, -1, -1):
        i1 = jnp.where(rows[j] == v1, j, i1)
    masked = [jnp.where(i1 == j, -jnp.inf, rows[j]) for j in range(n)]
    v2 = functools.reduce(jnp.maximum, masked)
    i2 = jnp.full(rows[0].shape, n - 1, jnp.int32)
    for j in range(n - 2, -1, -1):
        i2 = jnp.where(masked[j] == v2, j, i2)
    return v1, i1, v2, i2


def _select_rows(idx, rows):
    out = rows[-1]
    for j in range(len(rows) - 2, -1, -1):
        out = jnp.where(idx == j, rows[j], out)
    return out


def _moe_kernel(x_ref, sc_ref, sh_ref, g_ref, nw_ref, rwt_ref, rb_ref, upper_ref, wg_ref, wu_ref, wd_ref,
                out_ref, hb_ref, rank_ref, comb_ref, acc_ref, sel_ref, y_ref, cnt_ref):
    step = pl.program_id(1)
    tb = x_ref.shape[0]

    @pl.when(step == 0)
    def _route():
        h = _norm_mod(x_ref[...], nw_ref[...], sc_ref[...], sh_ref[...])
        hb = h.astype(BF16)
        hb_ref[...] = hb
        h_lo = (h - hb.astype(F32)).astype(BF16)
        logits = (lax.dot_general(rwt_ref[0], hb, NT, preferred_element_type=F32)
                  + lax.dot_general(rwt_ref[0], h_lo, NT, preferred_element_type=F32)
                  + lax.dot_general(rwt_ref[1], hb, NT, preferred_element_type=F32))
        s = 1.0 / (1.0 + jnp.exp(-logits))
        s_sel = s + rb_ref[...]
        groups = tb // LANES

        def pack(row):
            return jnp.concatenate([row[:, c * LANES:(c + 1) * LANES] for c in range(groups)], axis=0)

        def unpack(tile):
            return jnp.concatenate([tile[c:c + 1, :] for c in range(groups)], axis=1)

        s_rows = [pack(s[j:j + 1, :]) for j in range(N_EXPERTS)]
        sel_rows = [pack(s_sel[j:j + 1, :]) for j in range(N_EXPERTS)]
        tops = [_top2_of_rows(sel_rows[g * EPG:(g + 1) * EPG]) for g in range(N_GROUPS)]
        scores = [tp[0] + tp[2] for tp in tops]
        best = functools.reduce(jnp.maximum, scores)
        g_idx = jnp.full(best.shape, N_GROUPS - 1, jnp.int32)
        for g in range(N_GROUPS - 2, -1, -1):
            g_idx = jnp.where(scores[g] == best, g, g_idx)
        e1 = g_idx * EPG + _select_rows(g_idx, [tp[1] for tp in tops])
        e2 = g_idx * EPG + _select_rows(g_idx, [tp[3] for tp in tops])
        w1 = _select_rows(e1, s_rows)
        w2 = _select_rows(e2, s_rows)
        tot = w1 + w2
        w1, w2 = w1 / tot, w2 / tot
        masks = [jnp.where((e1 == j) | (e2 == j), 1.0, 0.0) for j in range(N_EXPERTS)]
        comb_ref[...] = jnp.concatenate(
            [unpack(jnp.where(e1 == j, w1, 0.0) + jnp.where(e2 == j, w2, 0.0)) for j in range(N_EXPERTS)], axis=0)
        mask = jnp.concatenate([unpack(m) for m in masks], axis=0)
        rank = jnp.dot(mask.astype(BF16), upper_ref[...], preferred_element_type=F32)
        rank_ref[...] = jnp.where(mask > 0.0, rank, -1.0)
        for j in range(N_EXPERTS):
            cnt_ref[j] = jnp.sum(masks[j]).astype(jnp.int32)
        acc_ref[...] = jnp.zeros_like(acc_ref)

    def expert_tile(j, tile_idx):
        e = step * MOE_EXPERTS_PER_STEP + j
        rank = rank_ref[pl.ds(e, 1), :]
        comb = comb_ref[pl.ds(e, 1), :]
        slot = (lax.broadcasted_iota(jnp.int32, (MOE_TILE, tb), 0) + tile_idx * MOE_TILE).astype(F32)
        hit = rank == slot
        onehot = jnp.where(hit, 1.0, 0.0).astype(BF16)
        xe = jnp.dot(onehot, hb_ref[...], preferred_element_type=F32).astype(BF16)
        gate = jnp.dot(xe, wg_ref[j], preferred_element_type=F32)
        up = jnp.dot(xe, wu_ref[j], preferred_element_type=F32)
        y = jnp.dot((_silu(gate) * up).astype(BF16), wd_ref[j], preferred_element_type=F32)
        w_slot = jnp.sum(jnp.where(hit, comb, 0.0), axis=1, keepdims=True)
        return onehot, (y * w_slot).astype(BF16)

    experts = range(MOE_EXPERTS_PER_STEP)
    slot0 = lax.broadcasted_iota(jnp.int32, (MOE_TILE, tb), 0).astype(F32)
    hits = [rank_ref[pl.ds(step * MOE_EXPERTS_PER_STEP + j, 1), :] == slot0 for j in experts]
    for j in experts:
        sel_ref[j * MOE_TILE:(j + 1) * MOE_TILE, :] = jnp.where(hits[j], 1.0, 0.0).astype(BF16)
    xes = [jnp.dot(sel_ref[j * MOE_TILE:(j + 1) * MOE_TILE, :], hb_ref[...],
                   preferred_element_type=F32).astype(BF16) for j in experts]
    hidden = [(jnp.dot(xes[j], wg_ref[j], preferred_element_type=F32),
               jnp.dot(xes[j], wu_ref[j], preferred_element_type=F32)) for j in experts]
    acts = [(_silu(gate) * up).astype(BF16) for gate, up in hidden]
    ys = [jnp.dot(acts[j], wd_ref[j], preferred_element_type=F32) for j in experts]
    for j in experts:
        comb = comb_ref[pl.ds(step * MOE_EXPERTS_PER_STEP + j, 1), :]
        w_slot = jnp.sum(jnp.where(hits[j], comb, 0.0), axis=1, keepdims=True)
        y_ref[j * MOE_TILE:(j + 1) * MOE_TILE, :] = (ys[j] * w_slot).astype(BF16)

    acc_ref[...] += lax.dot_general(sel_ref[...], y_ref[...], TN, preferred_element_type=F32)

    for j in experts:
        def overflow(tile_idx, carry, j=j):
            onehot, yw = expert_tile(j, tile_idx)
            acc_ref[...] += lax.dot_general(onehot, yw, TN, preferred_element_type=F32)
            return carry

        n_tiles = (cnt_ref[step * MOE_EXPERTS_PER_STEP + j] + MOE_TILE - 1) // MOE_TILE
        lax.fori_loop(1, n_tiles, overflow, 0)

    @pl.when(step == pl.num_programs(1) - 1)
    def _finish():
        out_ref[...] = x_ref[...] + g_ref[...] * acc_ref[...]


def _moe(x, sc, sh, g, nw, router_w, router_b, w_gate, w_up, w_down, seq, row_fn, shared_mod):
    t, d = x.shape
    tb = min(t if shared_mod else seq, 1024)
    per_seq = t if shared_mod else seq // tb
    dff = w_gate.shape[-1]
    upper = jnp.asarray(np.triu(np.ones((tb, tb), np.float32), 1), BF16)
    mod_spec = pl.BlockSpec((None, 1, d), lambda i, e: (row_fn(i // per_seq), 0, 0))
    tok = pl.BlockSpec((tb, d), lambda i, e: (i, 0))
    per = MOE_EXPERTS_PER_STEP
    stacked = per * MOE_TILE
    rw_hi = router_w.T.astype(BF16)
    rwt = jnp.stack([rw_hi, (router_w.T - rw_hi.astype(F32)).astype(BF16)])
    return pl.pallas_call(
        _moe_kernel,
        grid=(t // tb, N_EXPERTS // per),
        in_specs=[tok, mod_spec, mod_spec, mod_spec, _const_spec((1, d)), _const_spec((2, N_EXPERTS, d)),
                  _const_spec((N_EXPERTS, 1)), _const_spec(upper.shape),
                  pl.BlockSpec((per, d, dff), lambda i, e: (e, 0, 0)),
                  pl.BlockSpec((per, d, dff), lambda i, e: (e, 0, 0)),
                  pl.BlockSpec((per, dff, d), lambda i, e: (e, 0, 0))],
        out_specs=tok,
        out_shape=jax.ShapeDtypeStruct((t, d), F32),
        scratch_shapes=[pltpu.VMEM((tb, d), BF16), pltpu.VMEM((N_EXPERTS, tb), F32), pltpu.VMEM((N_EXPERTS, tb), F32),
                        pltpu.VMEM((tb, d), F32), pltpu.VMEM((stacked, tb), BF16), pltpu.VMEM((stacked, d), BF16),
                        pltpu.SMEM((N_EXPERTS,), jnp.int32)],
        compiler_params=_params(("parallel", "arbitrary")),
        name="moe",
    )(x, sc, sh, g, nw, rwt, router_b.reshape(N_EXPERTS, 1), upper, w_gate, w_up, w_down)


def _proj1_kernel(x_ref, sc_ref, sh_ref, nw_ref, w_ref, wlr_ref, p_ref, lr_ref):
    h = _norm_mod(x_ref[...], nw_ref[...], sc_ref[...], sh_ref[...]).astype(BF16)
    p_ref[...] = jnp.dot(h, w_ref[...], preferred_element_type=F32).astype(p_ref.dtype)
    lr_ref[...] = jnp.dot(h, wlr_ref[...], preferred_element_type=F32)


def _proj1(x, sc, sh, nw, w_main, w_lr, seq, row_fn):
    t, d = x.shape
    tm = min(seq, 512)
    per_seq = seq // tm
    mod_spec = pl.BlockSpec((None, 1, d), lambda i: (row_fn(i // per_seq), 0, 0))
    tok = lambda n: pl.BlockSpec((tm, n), lambda i: (i, 0))
    return pl.pallas_call(
        _proj1_kernel,
        grid=(t // tm,),
        in_specs=[tok(d), mod_spec, mod_spec, _const_spec((1, d)), _const_spec(w_main.shape), _const_spec(w_lr.shape)],
        out_specs=[tok(w_main.shape[1]), tok(LANES)],
        out_shape=[jax.ShapeDtypeStruct((t, w_main.shape[1]), BF16), jax.ShapeDtypeStruct((t, LANES), F32)],
        compiler_params=_params(("parallel",)),
        name="proj1",
    )(x, sc, sh, nw, w_main, w_lr)


def _log_sigmoid(x):
    return jnp.minimum(x, 0.0) - jnp.log(1.0 + jnp.exp(-jnp.abs(x)))


def _gla_kernel(qf_ref, kf_ref, vf_ref, lf_ref, qb_ref, kb_ref, vb_ref, lb_ref, wgf_ref, wgb_ref, bg_ref,
                trif_ref, trib_ref, s0f_ref, s0b_ref, of_ref, ob_ref, sf_ref, sb_ref, stf_ref, stb_ref, *, dk, dv):
    i = pl.program_id(1)
    nb = pl.num_programs(1)
    tc = qf_ref.shape[0]

    @pl.when(i == 0)
    def _init():
        stf_ref[...] = s0f_ref[...].astype(F32)
        stb_ref[...] = s0b_ref[...].astype(F32)

    r = lax.broadcasted_iota(jnp.int32, (GLA_CHUNK, GLA_CHUNK), 0)
    c = lax.broadcasted_iota(jnp.int32, (GLA_CHUNK, GLA_CHUNK), 1)
    scale = dk ** -0.5

    def decay_sums(l_ref, wg_ref, bias, tri_ref):
        gate = _log_sigmoid(jnp.dot(l_ref[...], wg_ref[...], precision=HI, preferred_element_type=F32)
                            + bias) / GLA_GATE_NORM
        total = None
        rest = gate
        for _ in range(3):
            part = rest.astype(BF16)
            rest = rest - part.astype(F32)
            term = jnp.dot(tri_ref[...], part, preferred_element_type=F32)
            total = term if total is None else total + term
        return total

    b_f = decay_sums(lf_ref, wgf_ref, bg_ref[0:1, :], trif_ref)
    b_b = decay_sums(lb_ref, wgb_ref, bg_ref[1:2, :], trib_ref)
    n_chunks = tc // GLA_CHUNK
    chains = []
    for h in range(GLA_HEADS):
        chains.append((qf_ref, kf_ref, vf_ref, of_ref, b_f, stf_ref, h, False))
        chains.append((qb_ref, kb_ref, vb_ref, ob_ref, b_b, stb_ref, h, True))
    states = [st_ref[h] for (_, _, _, _, _, st_ref, h, _) in chains]
    for step in range(n_chunks):
        prepared = []
        for (q_ref, k_ref, v_ref, _, b_all, _, h, reverse) in chains:
            ci = n_chunks - 1 - step if reverse else step
            rs = slice(ci * GLA_CHUNK, (ci + 1) * GLA_CHUNK)
            ks, vs = slice(h * dk, (h + 1) * dk), slice(h * dv, (h + 1) * dv)
            b = b_all[rs, ks]
            b_end = b[0:1, :] if reverse else b[GLA_CHUNK - 1:GLA_CHUNK, :]
            q_dec = (q_ref[rs, ks].astype(F32) * scale * jnp.exp(b)).astype(BF16)
            k_dec = k_ref[rs, ks].astype(F32) * jnp.exp(-b)
            e_end = jnp.exp(b_end)
            prepared.append((rs, vs, q_dec, k_dec.astype(BF16), (k_dec * e_end).astype(BF16), e_end, v_ref[rs, vs]))
        first = []
        for n, (rs, vs, q_dec, k_bf, k_rem, e_end, v) in enumerate(prepared):
            att = lax.dot_general(q_dec, k_bf, NT, preferred_element_type=F32)
            carry = lax.dot_general(q_dec, states[n].astype(BF16), NT, preferred_element_type=F32)
            upd = lax.dot_general(v, k_rem, TN, preferred_element_type=F32)
            first.append((att, carry, upd))
        for n, (rs, vs, q_dec, k_bf, k_rem, e_end, v) in enumerate(prepared):
            att, carry, upd = first[n]
            reverse = chains[n][7]
            att = jnp.where((c >= r) if reverse else (c <= r), att, 0.0).astype(BF16)
            o = jnp.dot(att, v, preferred_element_type=F32) + carry
            chains[n][3][rs, vs] = o.astype(chains[n][3].dtype)
            states[n] = states[n] * e_end + upd
    for n, (_, _, _, _, _, st_ref, h, _) in enumerate(chains):
        st_ref[h] = states[n]

    @pl.when(i == nb - 1)
    def _emit():
        sf_ref[...] = stf_ref[...]
        sb_ref[...] = stb_ref[...]


def _gla(p, lr, wgf, wgb, bg, s0f, s0b, batch, seq, dk, dv):
    tc = min(seq, 256)
    nb = seq // tc
    hk, hv = GLA_HEADS * dk, GLA_HEADS * dv
    fwd = lambda width, col: pl.BlockSpec((tc, width), lambda b, i: (b * nb + i, col))
    bwd = lambda width, col: pl.BlockSpec((tc, width), lambda b, i: (b * nb + nb - 1 - i, col))
    st_spec = pl.BlockSpec((None, GLA_HEADS, dv, dk), lambda b, i: (b, 0, 0, 0))
    v_col = 2 * hk // hv
    idx = np.arange(tc)
    same = (idx[:, None] // GLA_CHUNK) == (idx[None, :] // GLA_CHUNK)
    tri_f = jnp.asarray(same & (idx[None, :] <= idx[:, None]), BF16)
    tri_b = jnp.asarray(same & (idx[None, :] >= idx[:, None]), BF16)
    return pl.pallas_call(
        functools.partial(_gla_kernel, dk=dk, dv=dv),
        grid=(batch, nb),
        in_specs=[fwd(hk, 0), fwd(hk, 1), fwd(hv, v_col), fwd(LANES, 0),
                  bwd(hk, 0), bwd(hk, 1), bwd(hv, v_col), bwd(LANES, 0),
                  _const_spec(wgf.shape), _const_spec(wgb.shape), _const_spec(bg.shape),
                  _const_spec(tri_f.shape), _const_spec(tri_b.shape), st_spec, st_spec],
        out_specs=[fwd(hv, 0), bwd(hv, 0), st_spec, st_spec],
        out_shape=[jax.ShapeDtypeStruct((batch * seq, hv), BF16)] * 2
        + [jax.ShapeDtypeStruct((batch, GLA_HEADS, dv, dk), F32)] * 2,
        scratch_shapes=[pltpu.VMEM((GLA_HEADS, dv, dk), F32)] * 2,
        compiler_params=_params(("parallel", "arbitrary")),
        name="gla_scan",
    )(p, p, p, lr, p, p, p, lr, wgf, wgb, bg, tri_f, tri_b, s0f, s0b)


def _outproj1_kernel(x_ref, of_ref, ob_ref, go_ref, g_ref, nw_ref, w_ref, out_ref, *, dv):
    o = of_ref[...].astype(F32) + ob_ref[...].astype(F32)
    parts = []
    for h in range(GLA_HEADS):
        oh = o[:, h * dv:(h + 1) * dv]
        ms = jnp.mean(oh * oh, axis=1, keepdims=True)
        parts.append(oh * lax.rsqrt(ms + EPS))
    o = jnp.concatenate(parts, axis=1) * nw_ref[...] * _silu(go_ref[...].astype(F32))
    out_ref[...] = x_ref[...] + g_ref[...] * jnp.dot(o.astype(BF16), w_ref[...], preferred_element_type=F32)


def _outproj1(x, o_f, o_b, p, g, out_norm_w, w_out, seq, row_fn, dv):
    t, d = x.shape
    hv = GLA_HEADS * dv
    tm = min(seq, 512)
    per_seq = seq // tm
    tok = lambda n, col=0: pl.BlockSpec((tm, n), lambda i: (i, col))
    go_col = (p.shape[1] - hv) // hv
    nw = jnp.tile(out_norm_w, GLA_HEADS).reshape(1, hv)
    wb = w_out.astype(BF16)
    return pl.pallas_call(
        functools.partial(_outproj1_kernel, dv=dv),
        grid=(t // tm,),
        in_specs=[tok(d), tok(hv), tok(hv), tok(hv, go_col),
                  pl.BlockSpec((None, 1, d), lambda i: (row_fn(i // per_seq), 0, 0)),
                  _const_spec((1, hv)), _const_spec(wb.shape)],
        out_specs=tok(d),
        out_shape=jax.ShapeDtypeStruct((t, d), F32),
        compiler_params=_params(("parallel",)),
        name="outproj1",
    )(x, o_f, o_b, p, g, nw, wb)


def _mod_rows(mod_l, d):
    return [mod_l[:, j * d:(j + 1) * d].reshape(mod_l.shape[0], 1, d) for j in range(6)]


def _trunk(x, batch, seq, mods, row_fn, shared_mod, norm_w, ab, cp, moe_w, cache, s0, emit_cache):
    d = x.shape[-1]
    t = batch * seq
    x = x.reshape(t, d)
    (w_in0, conv_w, conv_b, filt, hy_skip, qn_w, kn_w, lam_p, subln_w, w_out0) = ab
    (w_in1, gate_up_w, gate_up_b, out_norm_w, w_out1) = cp
    router_w, router_b, wg, wu, wd = moe_w
    width = d // 2
    hy_in = 3 * width

    sh1, sc1, g1, sh2, sc2, g2 = mods[0]
    nseg = DA_HEADS * 2
    qkw = jnp.stack([jnp.tile(qn_w, nseg), jnp.tile(kn_w, nseg)])
    seg = np.arange(DA_HEADS * DA_VD) // DA_DH
    ones = jnp.asarray(seg[:, None] == seg[None, :], BF16)
    v_cols = w_in0[:, hy_in + 2 * DA_HEADS * DA_VD:]
    w_chan = jnp.concatenate([w_in0[:, :hy_in], v_cols], axis=1).T.astype(BF16)
    outs = _proj0(x, sc1, sh1, norm_w[0, 0].reshape(1, d), w_in0[:, hy_in:].astype(BF16), w_chan, qkw, ones,
                  batch, seq, row_fn, rope=cache is not None, emit_cache=emit_cache)
    chan, q, k = outs[:3]
    y_hy = _hyena(chan, conv_w, conv_b, filt, hy_skip, seq, width)
    lam_init = 0.8 - 0.6 * math.exp(-0.3 * 0)
    o = _diff_attention(q, k, chan, hy_in, cache, lam_p, subln_w, batch, seq, lam_init)
    x = _outproj0(x, y_hy, o, g1, w_out0, seq, row_fn)
    x = _moe(x, sc2, sh2, g2, norm_w[0, 1].reshape(1, d), router_w, router_b, wg[0], wu[0], wd[0], seq, row_fn,
             shared_mod)

    sh1, sc1, g1, sh2, sc2, g2 = mods[1]
    dk = d // 2 // GLA_HEADS
    dv = d // GLA_HEADS
    n_main = GLA_HEADS * (2 * dk + 2 * dv)
    w_lr = jnp.pad(w_in1[:, n_main:], ((0, 0), (0, LANES - 2 * GLA_RANK))).astype(BF16)
    p, lr = _proj1(x, sc1, sh1, norm_w[1, 0].reshape(1, d), w_in1[:, :n_main].astype(BF16), w_lr, seq, row_fn)
    wgf = jnp.pad(gate_up_w[0], ((0, LANES - GLA_RANK), (0, 0)))
    wgb = jnp.pad(gate_up_w[1], ((GLA_RANK, LANES - 2 * GLA_RANK), (0, 0)))
    o_f, o_b, s_f, s_b = _gla(p, lr, wgf, wgb, gate_up_b, jnp.swapaxes(s0[0], 2, 3), jnp.swapaxes(s0[1], 2, 3),
                              batch, seq, dk, dv)
    s_f, s_b = jnp.swapaxes(s_f, 2, 3), jnp.swapaxes(s_b, 2, 3)
    x = _outproj1(x, o_f, o_b, p, g1, out_norm_w, w_out1, seq, row_fn, dv)
    x = _moe(x, sc2, sh2, g2, norm_w[1, 1].reshape(1, d), router_w, router_b, wg[1], wu[1], wd[1], seq, row_fn,
             shared_mod)
    return x.reshape(batch, seq, d), outs[3:], (s_f, s_b)


def kernel(x_prompt, x_sample, cache_l0_k, cache_l0_v, state_l1_fwd, state_l1_bwd, c, c_ctx, w_mod, b_mod, norm_w,
           l0_w_in, l0_conv_w, l0_conv_b, l0_filt_w1, l0_filt_b1, l0_filt_freq, l0_filt_w2, l0_filt_b2, l0_filt_w3,
           l0_hy_skip, l0_qn_w, l0_kn_w, l0_lambda, l0_subln_w, l0_w_out, l1_w_in, l1_gate_up_w, l1_gate_up_b,
           l1_out_norm_w, l1_w_out, moe_router_w, moe_router_b, moe_w_gate, moe_w_up, moe_w_down):
    d = x_prompt.shape[-1]
    n_lat = c.shape[0]
    ctx_row = n_lat
    rows = 2 * SUBLANES
    cond = jnp.zeros((rows, d), F32).at[:n_lat].set(c).at[ctx_row].set(c_ctx)
    mod = _modulation(cond, w_mod, b_mod)
    mods = [_mod_rows(mod[l], d) for l in range(mod.shape[0])]
    filt = (l0_filt_w1, l0_filt_b1, l0_filt_freq, l0_filt_w2, l0_filt_b2, l0_filt_w3)
    ab = (l0_w_in, l0_conv_w, l0_conv_b, filt, l0_hy_skip, l0_qn_w, l0_kn_w, l0_lambda, l0_subln_w, l0_w_out)
    cp = (l1_w_in, l1_gate_up_w, l1_gate_up_b, l1_out_norm_w, l1_w_out)
    moe_w = (moe_router_w, moe_router_b, moe_w_gate.astype(BF16), moe_w_up.astype(BF16), moe_w_down.astype(BF16))

    b_ctx, l_ctx = x_prompt.shape[:2]
    zero_state = jnp.zeros((b_ctx,) + state_l1_fwd.shape[1:], F32)
    y_prompt, (new_k, new_v), (new_sf, new_sb) = _trunk(
        x_prompt, b_ctx, l_ctx, mods, lambda b: ctx_row, True, norm_w, ab, cp, moe_w, None,
        (zero_state, zero_state), True)
    b_lat, l_lat = x_sample.shape[:2]
    y_sample, _, _ = _trunk(
        x_sample, b_lat, l_lat, mods, lambda b: b, False, norm_w, ab, cp, moe_w, (cache_l0_k, cache_l0_v),
        (state_l1_fwd, state_l1_bwd), False)
    return (y_prompt, y_sample, new_k, new_v, new_sf, new_sb)
```

```python
import cmath
import functools
import math

import numpy as np
import jax
import jax.numpy as jnp
from jax import lax
from jax.experimental import pallas as pl
from jax.experimental.pallas import tpu as pltpu

F32 = jnp.float32
BF16 = jnp.bfloat16
HI = lax.Precision.HIGHEST
EPS = 1e-6

LANES = 128
SUBLANES = 8
VMEM_LIMIT = 56 << 20

DA_HEADS = 4
DA_DH = 64
DA_VD = 2 * DA_DH
ROPE_NF = DA_DH // 4
ROPE_BASE = 10000.0
GRID_W = 64
HY_BANDS = 16
HY_TARGET = 1e-2
HY_MAX_DECAY = math.log(HY_TARGET) / 0.3
HY_MIN_DECAY = math.log(HY_TARGET) / 1.5
GLA_HEADS = 4
GLA_RANK = 16
GLA_GATE_NORM = 16.0
GLA_CHUNK = 64
N_EXPERTS = 16
N_GROUPS = 4
EPG = N_EXPERTS // N_GROUPS
MOE_TILE = 160
MOE_EXPERTS_PER_STEP = 4
NT = (((1,), (1,)), ((), ()))
TN = (((0,), (0,)), ((), ()))


def _params(sem):
    return pltpu.CompilerParams(dimension_semantics=sem, vmem_limit_bytes=VMEM_LIMIT)


def _const_spec(shape):
    nd = len(shape)
    return pl.BlockSpec(shape, lambda *_: (0,) * nd)


def _silu(x):
    return x * (1.0 / (1.0 + jnp.exp(-x)))


def _norm_mod(x, nw, sc, sh):
    ms = jnp.mean(x * x, axis=-1, keepdims=True)
    return x * lax.rsqrt(ms + EPS) * nw * (1.0 + sc) + sh


def _mod_kernel(c_ref, w_ref, b_ref, o_ref):
    s = _silu(c_ref[...])
    o_ref[...] = jnp.dot(s, w_ref[...], precision=HI, preferred_element_type=F32) + b_ref[...]


def _modulation(cond, w_mod, b_mod):
    depth, d, d6 = w_mod.shape
    r = cond.shape[0]
    tn = 1536
    return pl.pallas_call(
        _mod_kernel,
        grid=(depth, d6 // tn),
        in_specs=[
            pl.BlockSpec((r, d), lambda l, j: (0, 0)),
            pl.BlockSpec((None, d, tn), lambda l, j: (l, 0, j)),
            pl.BlockSpec((None, 1, tn), lambda l, j: (l, 0, j)),
        ],
        out_specs=pl.BlockSpec((None, r, tn), lambda l, j: (l, 0, j)),
        out_shape=jax.ShapeDtypeStruct((depth, r, d6), F32),
        compiler_params=_params(("parallel", "parallel")),
        name="modulation",
    )(cond, w_mod, b_mod.reshape(depth, 1, d6))


def _proj0_kernel(x_ref, sc_ref, sh_ref, nw_ref, wtok_ref, wchan_ref, qkw_ref, ones_ref, cos_ref, sin_ref,
                  chan_ref, q_ref, k_ref, *cache_refs, rope):
    h = _norm_mod(x_ref[...], nw_ref[...], sc_ref[...], sh_ref[...]).astype(BF16)
    chan_ref[...] = lax.dot_general(wchan_ref[...], h, NT, preferred_element_type=F32).astype(chan_ref.dtype)
    p = jnp.dot(h, wtok_ref[...], preferred_element_type=F32)
    w = DA_HEADS * DA_VD
    ones = ones_ref[...]

    def segnorm(z, gain):
        ss = jnp.dot((z * z).astype(BF16), ones, preferred_element_type=F32)
        return z * lax.rsqrt(ss * (1.0 / DA_DH) + EPS) * gain

    q = segnorm(p[:, :w], qkw_ref[0:1, :])
    k = segnorm(p[:, w:2 * w], qkw_ref[1:2, :])
    if cache_refs:
        kn_ref, vn_ref = cache_refs
        v = p[:, 2 * w:]
        for hh in range(DA_HEADS):
            kn_ref[hh] = k[:, hh * DA_VD:(hh + 1) * DA_VD]
            vn_ref[hh] = v[:, hh * DA_VD:(hh + 1) * DA_VD]
    if rope:
        cos = jnp.concatenate([cos_ref[...]] * (w // LANES), axis=1)
        sin = jnp.concatenate([sin_ref[...]] * (w // LANES), axis=1)
        lane = lax.broadcasted_iota(jnp.int32, q.shape, 1)
        first = (lane % (2 * ROPE_NF)) < ROPE_NF

        def rot(z):
            partner = jnp.where(first, pltpu.roll(z, w - ROPE_NF, 1), pltpu.roll(z, ROPE_NF, 1))
            return z * cos + partner * sin

        q = rot(q)
        k = rot(k)
    q_ref[...] = (q * (DA_DH ** -0.5 * math.log2(math.e))).astype(q_ref.dtype)
    k_ref[...] = k.astype(k_ref.dtype)


def _rope_tables(seq):
    t = np.arange(seq)
    lane = np.arange(LANES)
    d = lane % DA_DH
    axis = d // (2 * ROPE_NF)
    part = (d % (2 * ROPE_NF)) // ROPE_NF
    f = d % ROPE_NF
    pos = jnp.where(axis[None, :] == 0, (t // GRID_W)[:, None], (t % GRID_W)[:, None]).astype(F32)
    inv = ROPE_BASE ** (-jnp.arange(ROPE_NF, dtype=F32) / ROPE_NF)
    ang = pos * inv[f][None, :]
    sign = jnp.asarray(np.where(part == 0, -1.0, 1.0), F32)[None, :]
    return jnp.cos(ang), jnp.sin(ang) * sign


def _proj0(x, sc, sh, nw, wqkv, whyt, qkw, ones, batch, seq, row_fn, rope, emit_cache):
    t, d = x.shape
    tm = min(seq, 512)
    per_seq = seq // tm
    w = DA_HEADS * DA_VD
    hy_in = whyt.shape[0]
    if not emit_cache:
        wqkv = wqkv[:, :2 * w]
    if rope:
        cos, sin = _rope_tables(seq)
    else:
        cos = jnp.ones((seq, LANES), F32)
        sin = jnp.zeros((seq, LANES), F32)
    mod_spec = pl.BlockSpec((None, 1, d), lambda i: (row_fn(i // per_seq), 0, 0))
    tab_spec = pl.BlockSpec((tm, LANES), lambda i: (i % per_seq, 0))
    tok = lambda n: pl.BlockSpec((tm, n), lambda i: (i, 0))
    out_specs = [pl.BlockSpec((None, hy_in, tm), lambda i: (i // per_seq, 0, i % per_seq)), tok(w), tok(w)]
    out_shape = [jax.ShapeDtypeStruct((batch, hy_in, seq), BF16)] + [jax.ShapeDtypeStruct((t, w), BF16)] * 2
    if emit_cache:
        cspec = pl.BlockSpec((None, DA_HEADS, tm, DA_VD), lambda i: (i // per_seq, 0, i % per_seq, 0))
        out_specs += [cspec, cspec]
        out_shape += [jax.ShapeDtypeStruct((batch, DA_HEADS, seq, DA_VD), F32)] * 2
    return pl.pallas_call(
        functools.partial(_proj0_kernel, rope=rope),
        grid=(t // tm,),
        in_specs=[tok(d), mod_spec, mod_spec, _const_spec((1, d)), _const_spec(wqkv.shape), _const_spec(whyt.shape),
                  _const_spec(qkw.shape), _const_spec(ones.shape), tab_spec, tab_spec],
        out_specs=out_specs,
        out_shape=out_shape,
        compiler_params=_params(("parallel",)),
        name="proj0",
    )(x, sc, sh, nw, wqkv, whyt, qkw, ones, cos, sin)


def _c_add(a, b, sign=1.0):
    if b is None:
        return a
    if a is None:
        return b if sign > 0 else tuple(None if p is None else -p for p in b)
    out = []
    for pa, pb in zip(a, b):
        if pb is None:
            out.append(pa)
        elif pa is None:
            out.append(pb if sign > 0 else -pb)
        else:
            out.append(pa + pb if sign > 0 else pa - pb)
    return tuple(out)


def _c_mul_const(w, a):
    if a is None:
        return None
    wr = 0.0 if abs(w.real) < 1e-12 else w.real
    wi = 0.0 if abs(w.imag) < 1e-12 else w.imag
    ar, ai = a

    def scaled(c, p):
        if p is None or c == 0.0:
            return None
        if c == 1.0:
            return p
        if c == -1.0:
            return -p
        return c * p

    re = _c_add((scaled(wr, ar),), (scaled(wi, ai),), -1.0)[0]
    im = _c_add((scaled(wr, ai),), (scaled(wi, ar),), 1.0)[0]
    return (re, im)


def _fft_list(vals, sign, first_half_only=False):
    n = len(vals)
    if n == 1:
        return list(vals)
    ev = _fft_list(vals[0::2], sign)
    od = _fft_list(vals[1::2], sign)
    out = [None] * n
    for k in range(n // 2):
        tw = _c_mul_const(cmath.exp(sign * 2j * math.pi * k / n), od[k])
        out[k] = _c_add(ev[k], tw, 1.0)
        if not first_half_only:
            out[k + n // 2] = _c_add(ev[k], tw, -1.0)
    return out


def _dft_mats(n2, total):
    k = np.arange(n2)
    ang = 2.0 * np.pi * np.outer(k, k) / n2
    fr, fi = np.cos(ang), -np.sin(ang)
    fwd = np.block([[fr, fi], [-fi, fr]])
    inv = np.block([[fr, -fi], [fi, fr]]) / total
    return jnp.asarray(fwd, BF16), jnp.asarray(inv, BF16)


def _twiddles(n1, n2):
    ang = 2.0 * np.pi * np.outer(np.arange(n1), np.arange(n2)) / (n1 * n2)
    return jnp.asarray(np.cos(ang), F32), jnp.asarray(-np.sin(ang), F32)


def _fft_split(seq):
    n = 2 * seq
    n2 = min(256, n // 2)
    return n // n2, n2


def _across_fwd(load, n_in, z_ref, twr_ref, twi_ref, n1, n2, rows):
    per_row = n2 // LANES

    def body(it, carry):
        r0 = pl.multiple_of((it // per_row) * SUBLANES, SUBLANES)
        l0 = pl.multiple_of((it % per_row) * LANES, LANES)
        vals = [load(j, r0, l0) for j in range(n_in)] + [None] * (n1 - n_in)
        outs = _fft_list(vals, -1.0)
        for k1 in range(n1):
            twr = twr_ref[pl.ds(k1, 1), pl.ds(l0, LANES)]
            twi = twi_ref[pl.ds(k1, 1), pl.ds(l0, LANES)]
            re, im = outs[k1]
            zero = jnp.zeros((SUBLANES, LANES), F32)
            re = zero if re is None else re
            im = zero if im is None else im
            z_ref[k1, pl.ds(r0, SUBLANES), pl.ds(l0, LANES)] = re * twr - im * twi
            z_ref[k1, pl.ds(r0, SUBLANES), pl.ds(n2 + l0, LANES)] = re * twi + im * twr
        return carry

    lax.fori_loop(0, (rows // SUBLANES) * per_row, body, 0)


def _hy_hidden_kernel(w1_ref, b1_ref, freq_ref, w2_ref, b2_ref, bands_ref, o_ref, *, seq):
    n = 2 * seq
    k = lax.broadcasted_iota(jnp.int32, (1, n), 1)
    pos = jnp.where(k < seq, k, n - k).astype(F32)
    t = pos / (seq - 1)
    ang = 2.0 * math.pi * pos / seq * bands_ref[...]
    z = (w1_ref[:, 0:1] * t
         + jnp.dot(w1_ref[:, 1:1 + HY_BANDS], jnp.cos(ang), precision=HI, preferred_element_type=F32)
         + jnp.dot(w1_ref[:, 1 + HY_BANDS:], -jnp.sin(ang), precision=HI, preferred_element_type=F32)
         + b1_ref[...])
    hid = jnp.sin(freq_ref[:, 0:1] * z)
    hid = jnp.sin(freq_ref[:, 1:2] * (jnp.dot(w2_ref[...], hid, precision=HI, preferred_element_type=F32) + b2_ref[...]))
    o_ref[...] = hid


def _hy_hidden(seq, f_w1, f_b1, f_freq, f_w2, f_b2):
    ffn = f_w1.shape[1]
    bands = jnp.linspace(1e-4, HY_BANDS - 1, HY_BANDS, dtype=F32).reshape(HY_BANDS, 1)
    args = (f_w1.T, f_b1.reshape(ffn, 1), f_freq.T, f_w2.T, f_b2.reshape(ffn, 1), bands)
    return pl.pallas_call(
        functools.partial(_hy_hidden_kernel, seq=seq),
        grid=(1,),
        in_specs=[_const_spec(a.shape) for a in args],
        out_specs=_const_spec((ffn, 2 * seq)),
        out_shape=jax.ShapeDtypeStruct((ffn, 2 * seq), F32),
        compiler_params=_params(("arbitrary",)),
        name="hyena_filter_hidden",
    )(*args)


def _hy_spec_kernel(hid_ref, wb_ref, wa_ref, dec_ref, twr_ref, twi_ref, gf_ref, o_ref, taps_ref, z_ref, *, seq, n1, n2):
    ct = wb_ref.shape[0]
    hid = hid_ref[...]
    back = jnp.dot(wb_ref[...], hid[:, :seq], precision=HI, preferred_element_type=F32)
    ahead = jnp.dot(wa_ref[...], hid[:, seq:], precision=HI, preferred_element_type=F32)
    lane = lax.broadcasted_iota(jnp.int32, (1, seq), 1)
    dec = dec_ref[...]
    t_back = lane.astype(F32) / (seq - 1)
    t_ahead = (seq - lane).astype(F32) / (seq - 1)
    back = back * jnp.exp(-t_back * dec)
    ahead = jnp.where(lane == 0, 0.0, ahead * jnp.exp(-t_ahead * dec))
    norm = (jnp.sum(jnp.abs(back), axis=1, keepdims=True) + jnp.sum(jnp.abs(ahead), axis=1, keepdims=True)) + EPS
    taps_ref[:, :seq] = back / norm
    taps_ref[:, seq:] = ahead / norm

    def load(j, r0, l0):
        return (taps_ref[pl.ds(r0, SUBLANES), pl.ds(j * n2 + l0, LANES)], None)

    _across_fwd(load, n1, z_ref, twr_ref, twi_ref, n1, n2, ct)
    for k1 in range(n1):
        o_ref[k1] = jnp.dot(z_ref[k1].astype(BF16), gf_ref[...], preferred_element_type=F32)


def _hy_filter_spectra(hid, f_w3, seq, width, ct):
    n1, n2 = _fft_split(seq)
    ffn = f_w3.shape[0]
    w3t = f_w3.T.reshape(4, width, ffn)
    decay = jnp.abs(jnp.linspace(HY_MIN_DECAY, HY_MAX_DECAY, width, dtype=F32)).reshape(width, 1)
    twr, twi = _twiddles(n1, n2)
    gf, _ = _dft_mats(n2, 2 * seq)
    return pl.pallas_call(
        functools.partial(_hy_spec_kernel, seq=seq, n1=n1, n2=n2),
        grid=(2, width // ct),
        in_specs=[
            _const_spec(hid.shape),
            pl.BlockSpec((None, ct, ffn), lambda i, c: (2 * i, c, 0)),
            pl.BlockSpec((None, ct, ffn), lambda i, c: (2 * i + 1, c, 0)),
            pl.BlockSpec((ct, 1), lambda i, c: (c, 0)),
            _const_spec(twr.shape), _const_spec(twi.shape), _const_spec(gf.shape),
        ],
        out_specs=pl.BlockSpec((None, n1, ct, 2 * n2), lambda i, c: (i, 0, c, 0)),
        out_shape=jax.ShapeDtypeStruct((2, n1, width, 2 * n2), F32),
        scratch_shapes=[pltpu.VMEM((ct, 2 * seq), F32), pltpu.VMEM((n1, ct, 2 * n2), F32)],
        compiler_params=_params(("parallel", "parallel")),
        name="hyena_filter_spectra",
    )(hid, w3t, w3t, decay, twr, twi, gf)


def _short_conv_rows(ref, cw, r0, nrows, seq):
    x = ref[pl.ds(r0, nrows), :].astype(F32)
    lane = lax.broadcasted_iota(jnp.int32, (nrows, LANES), 1)
    left = pltpu.roll(x, 1, 1)
    right = pltpu.roll(x, seq - 1, 1)
    left = jnp.concatenate([jnp.where(lane == 0, 0.0, left[:, :LANES]), left[:, LANES:]], axis=1)
    right = jnp.concatenate([right[:, :seq - LANES], jnp.where(lane == LANES - 1, 0.0, right[:, seq - LANES:])], axis=1)
    return left * cw[:, 0:1] + x * cw[:, 1:2] + right * cw[:, 2:3] + cw[:, 3:4]


def _hy_conv_kernel(uin_ref, gin_ref, cwu_ref, cwg_ref, d_ref, h_ref, twr_ref, twi_ref, gf_ref, gi_ref,
                    o_ref, u_ref, z_ref, *, seq, n1, n2, conv_u):
    ct = uin_ref.shape[1]
    rows16 = 2 * SUBLANES

    def prep(i, carry):
        r0 = pl.multiple_of(i * rows16, rows16)
        cwg = cwg_ref[pl.ds(r0, rows16), :]
        cwu = cwu_ref[pl.ds(r0, rows16), :]
        for e in range(2):
            o_ref[e, pl.ds(r0, rows16), :] = _short_conv_rows(gin_ref.at[e], cwg, r0, rows16, seq)
            if conv_u:
                u_ref[e, pl.ds(r0, rows16), :] = _short_conv_rows(uin_ref.at[e], cwu, r0, rows16, seq)
            else:
                u_ref[e, pl.ds(r0, rows16), :] = uin_ref[e, pl.ds(r0, rows16), :].astype(F32)
        return carry

    lax.fori_loop(0, ct // rows16, prep, 0)

    def load(j, r0, l0):
        return (u_ref[0, pl.ds(r0, SUBLANES), pl.ds(j * n2 + l0, LANES)],
                u_ref[1, pl.ds(r0, SUBLANES), pl.ds(j * n2 + l0, LANES)])

    _across_fwd(load, n1 // 2, z_ref, twr_ref, twi_ref, n1, n2, ct)

    group = 4 if n1 % 4 == 0 else 2

    def within(g, carry):
        ks = [group * g + j for j in range(group)]
        zz = jnp.concatenate([z_ref[k1] for k1 in ks], axis=0)
        y = jnp.dot(zz.astype(BF16), gf_ref[...], preferred_element_type=F32)
        hh = jnp.concatenate([h_ref[k1] for k1 in ks], axis=0)
        yr, yi = y[:, :n2], y[:, n2:]
        hr, hi = hh[:, :n2], hh[:, n2:]
        p = jnp.concatenate([yr * hr - yi * hi, yr * hi + yi * hr], axis=1)
        q = jnp.dot(p.astype(BF16), gi_ref[...], preferred_element_type=F32)
        for j, k1 in enumerate(ks):
            z_ref[k1] = q[j * ct:(j + 1) * ct]
        return carry

    lax.fori_loop(0, n1 // group, within, 0)

    per_row = n2 // LANES

    def finish(it, carry):
        r0 = pl.multiple_of((it // per_row) * SUBLANES, SUBLANES)
        l0 = pl.multiple_of((it % per_row) * LANES, LANES)
        vals = []
        for k1 in range(n1):
            twr = twr_ref[pl.ds(k1, 1), pl.ds(l0, LANES)]
            twi = twi_ref[pl.ds(k1, 1), pl.ds(l0, LANES)]
            re = z_ref[k1, pl.ds(r0, SUBLANES), pl.ds(l0, LANES)]
            im = z_ref[k1, pl.ds(r0, SUBLANES), pl.ds(n2 + l0, LANES)]
            vals.append((re * twr + im * twi, im * twr - re * twi))
        outs = _fft_list(vals, 1.0, first_half_only=True)
        d = d_ref[pl.ds(r0, SUBLANES), :]
        for j in range(n1 // 2):
            sl = (pl.ds(r0, SUBLANES), pl.ds(j * n2 + l0, LANES))
            for e in range(2):
                o_ref[(e,) + sl] = o_ref[(e,) + sl] * (outs[j][e] + u_ref[(e,) + sl] * d)
        return carry

    lax.fori_loop(0, (ct // SUBLANES) * per_row, finish, 0, unroll=2 if n1 <= 16 else 1)


def _hy_conv(u, u_off, g, g_off, cw, spectra, conv_idx, d_skip, seq, width, ct, conv_u):
    batch = u.shape[0]
    n1, n2 = _fft_split(seq)
    twr, twi = _twiddles(n1, n2)
    gf, gi = _dft_mats(n2, 2 * seq)
    uo, go = u_off // ct, g_off // ct
    return pl.pallas_call(
        functools.partial(_hy_conv_kernel, seq=seq, n1=n1, n2=n2, conv_u=conv_u),
        grid=(width // ct, batch // 2),
        in_specs=[
            pl.BlockSpec((2, ct, seq), lambda c, p: (p, uo + c, 0)),
            pl.BlockSpec((2, ct, seq), lambda c, p: (p, go + c, 0)),
            pl.BlockSpec((ct, 4), lambda c, p: (uo + c, 0)),
            pl.BlockSpec((ct, 4), lambda c, p: (go + c, 0)),
            pl.BlockSpec((ct, 1), lambda c, p: (c, 0)),
            pl.BlockSpec((None, n1, ct, 2 * n2), lambda c, p: (conv_idx, 0, c, 0), pipeline_mode=pl.Buffered(1)),
            _const_spec(twr.shape), _const_spec(twi.shape), _const_spec(gf.shape), _const_spec(gi.shape),
        ],
        out_specs=pl.BlockSpec((2, ct, seq), lambda c, p: (p, c, 0)),
        out_shape=jax.ShapeDtypeStruct((batch, width, seq), F32),
        scratch_shapes=[pltpu.VMEM((2, ct, seq), F32), pltpu.VMEM((n1, ct, 2 * n2), F32)],
        compiler_params=_params(("arbitrary", "arbitrary")),
        name="hyena_conv",
    )(u, g, cw, cw, d_skip, spectra, twr, twi, gf, gi)


def _hyena(hyt, conv_w, conv_b, filt, hy_skip, seq, width):
    f_w1, f_b1, f_freq, f_w2, f_b2, f_w3 = filt
    ct = LANES
    hid = _hy_hidden(seq, f_w1, f_b1, f_freq, f_w2, f_b2)
    spectra = _hy_filter_spectra(hid, f_w3, seq, width, ct)
    cw = jnp.concatenate([conv_w.T, conv_b[:, None]], axis=1)
    y1 = _hy_conv(hyt, 0, hyt, width, cw, spectra, 0, hy_skip[0].reshape(width, 1), seq, width, ct, True)
    return _hy_conv(y1, 0, hyt, 2 * width, cw, spectra, 1, hy_skip[1].reshape(width, 1), seq, width, ct, False)


def _attn_kernel(q_ref, k_ref, vt_ref, *rest, lam_init, with_cache, kc):
    if with_cache:
        kc_ref, vct_ref, lam_ref, sw_ref, o_ref, kmax_ref = rest
    else:
        lam_ref, sw_ref, o_ref, kmax_ref = rest
    lp = lam_ref[...]
    lam = (jnp.exp(jnp.sum(lp[0:1] * lp[1:2], axis=1, keepdims=True))
           - jnp.exp(jnp.sum(lp[2:3] * lp[3:4], axis=1, keepdims=True)) + lam_init)
    q = q_ref[...]
    lane = lax.broadcasted_iota(jnp.int32, q.shape, 1)
    zero = jnp.zeros_like(q)
    qs = (jnp.where(lane < DA_DH, q, zero), jnp.where(lane >= DA_DH, q, zero))
    seq = k_ref.shape[0]
    chunks = [(k_ref[c * kc:(c + 1) * kc, :], vt_ref[:, c * kc:(c + 1) * kc]) for c in range(seq // kc)]
    if with_cache:
        chunks.append((kc_ref[...].astype(BF16), vct_ref[...].astype(BF16)))
    tq = q.shape[0]
    sub = lax.broadcasted_iota(jnp.int32, (SUBLANES, DA_VD), 0)
    ln = lax.broadcasted_iota(jnp.int32, (SUBLANES, DA_VD), 1)
    pick = jnp.where((ln // DA_DH) == sub, 1.0, 0.0).astype(BF16)

    def sq_norms(x):
        xf = x.astype(F32)
        return lax.dot_general(pick, (xf * xf).astype(BF16), NT, preferred_element_type=F32)

    @pl.when(pl.program_id(2) == 0)
    def _key_norms():
        best = None
        for kk, _ in chunks:
            cur = jnp.max(sq_norms(kk), axis=1, keepdims=True)
            best = cur if best is None else jnp.maximum(best, cur)
        kmax_ref[...] = jnp.broadcast_to(best, kmax_ref.shape)

    def scores(ci):
        return [lax.dot_general(chunks[ci][0], qm, NT, preferred_element_type=F32) for qm in qs]

    def finish(acc1, l1, acc2, l2):
        o = acc1 * (1.0 / l1) - acc2 * (lam / l2)
        ms = jnp.mean(o * o, axis=0, keepdims=True)
        o_ref[...] = (o * lax.rsqrt(ms + EPS) * sw_ref[...] * (1.0 - lam_init)).astype(o_ref.dtype)

    bound = jnp.sqrt(sq_norms(q) * kmax_ref[:, 0:1]) * 1.02
    lsum = [jnp.zeros((SUBLANES, tq), F32), jnp.zeros((SUBLANES, tq), F32)]
    acc = [None, None]
    s_next = scores(0)
    for ci, (kk, vv) in enumerate(chunks):
        s_cur = s_next
        if ci + 1 < len(chunks):
            s_next = scores(ci + 1)
        for mi in range(2):
            p = jnp.exp2(s_cur[mi] - bound[mi:mi + 1, :])
            lsum[mi] = lsum[mi] + jnp.sum(p.reshape(-1, SUBLANES, tq), axis=0)
            part = jnp.dot(vv, p.astype(BF16), preferred_element_type=F32)
            acc[mi] = part if acc[mi] is None else acc[mi] + part
    l1 = jnp.sum(lsum[0], axis=0, keepdims=True)
    l2 = jnp.sum(lsum[1], axis=0, keepdims=True)
    finish(acc[0], l1, acc[1], l2)
    safe = jnp.min(jnp.minimum(l1, l2)) > 2.0 ** -60

    @pl.when(jnp.logical_not(safe))
    def _online():
        state = [None, None]
        for ci, (kk, vv) in enumerate(chunks):
            s_cur = scores(ci)
            for mi in range(2):
                s = s_cur[mi]
                cmax = jnp.max(s, axis=0, keepdims=True)
                if state[mi] is None:
                    m = cmax
                    p = jnp.exp2(s - m)
                    l = jnp.sum(p, axis=0, keepdims=True)
                    a = jnp.dot(vv, p.astype(BF16), preferred_element_type=F32)
                else:
                    m_old, l, a = state[mi]
                    m = jnp.maximum(m_old, cmax)
                    alpha = jnp.exp2(m_old - m)
                    p = jnp.exp2(s - m)
                    l = l * alpha + jnp.sum(p, axis=0, keepdims=True)
                    a = a * alpha + jnp.dot(vv, p.astype(BF16), preferred_element_type=F32)
                state[mi] = (m, l, a)
        finish(state[0][2], state[0][1], state[1][2], state[1][1])


def _diff_attention(q, k, chan, v_row0, cache, lam_p, subln_w, batch, seq, lam_init):
    tq = min(seq, 512)
    kc = min(seq, 512)
    nq = seq // tq
    vb = v_row0 // DA_VD
    grid = (batch, DA_HEADS, nq)
    in_specs = [pl.BlockSpec((tq, DA_VD), lambda b, h, i: (b * nq + i, h)),
                pl.BlockSpec((seq, DA_VD), lambda b, h, i: (b, h)),
                pl.BlockSpec((None, DA_VD, seq), lambda b, h, i: (b, vb + h, 0))]
    args = [q, k, chan]
    if cache is not None:
        past = cache[0].shape[2]
        in_specs += [pl.BlockSpec((None, None, past, DA_VD), lambda b, h, i: (b, h, 0, 0)),
                     pl.BlockSpec((None, None, DA_VD, past), lambda b, h, i: (b, h, 0, 0))]
        args += [cache[0], jnp.swapaxes(cache[1], 2, 3)]
    in_specs += [_const_spec(lam_p.shape), _const_spec((DA_VD, 1))]
    args += [lam_p, subln_w.reshape(DA_VD, 1)]
    return pl.pallas_call(
        functools.partial(_attn_kernel, lam_init=lam_init, with_cache=cache is not None, kc=kc),
        grid=grid,
        in_specs=in_specs,
        out_specs=pl.BlockSpec((None, DA_VD, tq), lambda b, h, i: (b, h, i)),
        out_shape=jax.ShapeDtypeStruct((batch, DA_HEADS * DA_VD, seq), BF16),
        scratch_shapes=[pltpu.VMEM((SUBLANES, LANES), F32)],
        compiler_params=_params(("parallel", "parallel", "arbitrary")),
        name="diff_attention",
    )(*args)


def _outproj0_kernel(x_ref, yt_ref, ot_ref, g_ref, wy_ref, wo_ref, out_ref):
    mix = lax.dot_general(yt_ref[...].astype(BF16), wy_ref[...], TN, preferred_element_type=F32)
    mix = mix + lax.dot_general(ot_ref[...], wo_ref[...], TN, preferred_element_type=F32)
    out_ref[...] = x_ref[...] + g_ref[...] * mix


def _outproj0(x, yt, ot, g, w_out, seq, row_fn):
    t, d = x.shape
    width = yt.shape[1]
    tm = min(seq, 512)
    per_seq = seq // tm
    wy = w_out[:width].astype(BF16)
    wo = w_out[width:].astype(BF16)
    tok = pl.BlockSpec((tm, d), lambda i: (i, 0))
    chan = lambda n: pl.BlockSpec((None, n, tm), lambda i: (i // per_seq, 0, i % per_seq))
    return pl.pallas_call(
        _outproj0_kernel,
        grid=(t // tm,),
        in_specs=[tok, chan(width), chan(ot.shape[1]),
                  pl.BlockSpec((None, 1, d), lambda i: (row_fn(i // per_seq), 0, 0)),
                  _const_spec(wy.shape), _const_spec(wo.shape)],
        out_specs=tok,
        out_shape=jax.ShapeDtypeStruct((t, d), F32),
        compiler_params=_params(("parallel",)),
        name="outproj0",
    )(x, yt, ot, g, wy, wo)


def _top2_of_rows(rows):
    n = len(rows)
    v1 = functools.reduce(jnp.maximum, rows)
    i1 = jnp.full(rows[0].shape, n - 1, jnp.int32)
    for j in range(n - 2, -1, -1):
        i1 = jnp.where(rows[j] == v1, j, i1)
    masked = [jnp.where(i1 == j, -jnp.inf, rows[j]) for j in range(n)]
    v2 = functools.reduce(jnp.maximum, masked)
    i2 = jnp.full(rows[0].shape, n - 1, jnp.int32)
    for j in range(n - 2, -1, -1):
        i2 = jnp.where(masked[j] == v2, j, i2)
    return v1, i1, v2, i2


def _select_rows(idx, rows):
    out = rows[-1]
    for j in range(len(rows) - 2, -1, -1):
        out = jnp.where(idx == j, rows[j], out)
    return out


def _moe_kernel(x_ref, sc_ref, sh_ref, g_ref, nw_ref, rwt_ref, rb_ref, upper_ref, wg_ref, wu_ref, wd_ref,
                out_ref, hb_ref, rank_ref, comb_ref, acc_ref, sel_ref, y_ref, cnt_ref):
    step = pl.program_id(1)
    tb = x_ref.shape[0]

    @pl.when(step == 0)
    def _route():
        h = _norm_mod(x_ref[...], nw_ref[...], sc_ref[...], sh_ref[...])
        hb = h.astype(BF16)
        hb_ref[...] = hb
        h_lo = (h - hb.astype(F32)).astype(BF16)
        logits = (lax.dot_general(rwt_ref[0], hb, NT, preferred_element_type=F32)
                  + lax.dot_general(rwt_ref[0], h_lo, NT, preferred_element_type=F32)
                  + lax.dot_general(rwt_ref[1], hb, NT, preferred_element_type=F32))
        s = 1.0 / (1.0 + jnp.exp(-logits))
        s_sel = s + rb_ref[...]
        groups = tb // LANES

        def pack(row):
            return jnp.concatenate([row[:, c * LANES:(c + 1) * LANES] for c in range(groups)], axis=0)

        def unpack(tile):
            return jnp.concatenate([tile[c:c + 1, :] for c in range(groups)], axis=1)

        s_rows = [pack(s[j:j + 1, :]) for j in range(N_EXPERTS)]
        sel_rows = [pack(s_sel[j:j + 1, :]) for j in range(N_EXPERTS)]
        tops = [_top2_of_rows(sel_rows[g * EPG:(g + 1) * EPG]) for g in range(N_GROUPS)]
        scores = [tp[0] + tp[2] for tp in tops]
        best = functools.reduce(jnp.maximum, scores)
        g_idx = jnp.full(best.shape, N_GROUPS - 1, jnp.int32)
        for g in range(N_GROUPS - 2, -1, -1):
            g_idx = jnp.where(scores[g] == best, g, g_idx)
        e1 = g_idx * EPG + _select_rows(g_idx, [tp[1] for tp in tops])
        e2 = g_idx * EPG + _select_rows(g_idx, [tp[3] for tp in tops])
        w1 = _select_rows(e1, s_rows)
        w2 = _select_rows(e2, s_rows)
        tot = w1 + w2
        w1, w2 = w1 / tot, w2 / tot
        masks = [jnp.where((e1 == j) | (e2 == j), 1.0, 0.0) for j in range(N_EXPERTS)]
        comb_ref[...] = jnp.concatenate(
            [unpack(jnp.where(e1 == j, w1, 0.0) + jnp.where(e2 == j, w2, 0.0)) for j in range(N_EXPERTS)], axis=0)
        mask = jnp.concatenate([unpack(m) for m in masks], axis=0)
        rank = jnp.dot(mask.astype(BF16), upper_ref[...], preferred_element_type=F32)
        rank_ref[...] = jnp.where(mask > 0.0, rank, -1.0)
        for j in range(N_EXPERTS):
            cnt_ref[j] = jnp.sum(masks[j]).astype(jnp.int32)
        acc_ref[...] = jnp.zeros_like(acc_ref)

    def expert_tile(j, tile_idx):
        e = step * MOE_EXPERTS_PER_STEP + j
        rank = rank_ref[pl.ds(e, 1), :]
        comb = comb_ref[pl.ds(e, 1), :]
        slot = (lax.broadcasted_iota(jnp.int32, (MOE_TILE, tb), 0) + tile_idx * MOE_TILE).astype(F32)
        hit = rank == slot
        onehot = jnp.where(hit, 1.0, 0.0).astype(BF16)
        xe = jnp.dot(onehot, hb_ref[...], preferred_element_type=F32).astype(BF16)
        gate = jnp.dot(xe, wg_ref[j], preferred_element_type=F32)
        up = jnp.dot(xe, wu_ref[j], preferred_element_type=F32)
        y = jnp.dot((_silu(gate) * up).astype(BF16), wd_ref[j], preferred_element_type=F32)
        w_slot = jnp.sum(jnp.where(hit, comb, 0.0), axis=1, keepdims=True)
        return onehot, (y * w_slot).astype(BF16)

    experts = range(MOE_EXPERTS_PER_STEP)
    slot0 = lax.broadcasted_iota(jnp.int32, (MOE_TILE, tb), 0).astype(F32)
    hits = [rank_ref[pl.ds(step * MOE_EXPERTS_PER_STEP + j, 1), :] == slot0 for j in experts]
    for j in experts:
        sel_ref[j * MOE_TILE:(j + 1) * MOE_TILE, :] = jnp.where(hits[j], 1.0, 0.0).astype(BF16)
    xes = [jnp.dot(sel_ref[j * MOE_TILE:(j + 1) * MOE_TILE, :], hb_ref[...],
                   preferred_element_type=F32).astype(BF16) for j in experts]
    hidden = [(jnp.dot(xes[j], wg_ref[j], preferred_element_type=F32),
               jnp.dot(xes[j], wu_ref[j], preferred_element_type=F32)) for j in experts]
    acts = [(_silu(gate) * up).astype(BF16) for gate, up in hidden]
    ys = [jnp.dot(acts[j], wd_ref[j], preferred_element_type=F32) for j in experts]
    for j in experts:
        comb = comb_ref[pl.ds(step * MOE_EXPERTS_PER_STEP + j, 1), :]
        w_slot = jnp.sum(jnp.where(hits[j], comb, 0.0), axis=1, keepdims=True)
        y_ref[j * MOE_TILE:(j + 1) * MOE_TILE, :] = (ys[j] * w_slot).astype(BF16)

    acc_ref[...] += lax.dot_general(sel_ref[...], y_ref[...], TN, preferred_element_type=F32)

    for j in experts:
        def overflow(tile_idx, carry, j=j):
            onehot, yw = expert_tile(j, tile_idx)
            acc_ref[...] += lax.dot_general(onehot, yw, TN, preferred_element_type=F32)
            return carry

        n_tiles = (cnt_ref[step * MOE_EXPERTS_PER_STEP + j] + MOE_TILE - 1) // MOE_TILE
        lax.fori_loop(1, n_tiles, overflow, 0)

    @pl.when(step == pl.num_programs(1) - 1)
    def _finish():
        out_ref[...] = x_ref[...] + g_ref[...] * acc_ref[...]


def _moe(x, sc, sh, g, nw, router_w, router_b, w_gate, w_up, w_down, seq, row_fn, shared_mod):
    t, d = x.shape
    tb = min(t if shared_mod else seq, 1024)
    per_seq = t if shared_mod else seq // tb
    dff = w_gate.shape[-1]
    upper = jnp.asarray(np.triu(np.ones((tb, tb), np.float32), 1), BF16)
    mod_spec = pl.BlockSpec((None, 1, d), lambda i, e: (row_fn(i // per_seq), 0, 0))
    tok = pl.BlockSpec((tb, d), lambda i, e: (i, 0))
    per = MOE_EXPERTS_PER_STEP
    stacked = per * MOE_TILE
    rw_hi = router_w.T.astype(BF16)
    rwt = jnp.stack([rw_hi, (router_w.T - rw_hi.astype(F32)).astype(BF16)])
    return pl.pallas_call(
        _moe_kernel,
        grid=(t // tb, N_EXPERTS // per),
        in_specs=[tok, mod_spec, mod_spec, mod_spec, _const_spec((1, d)), _const_spec((2, N_EXPERTS, d)),
                  _const_spec((N_EXPERTS, 1)), _const_spec(upper.shape),
                  pl.BlockSpec((per, d, dff), lambda i, e: (e, 0, 0)),
                  pl.BlockSpec((per, d, dff), lambda i, e: (e, 0, 0)),
                  pl.BlockSpec((per, dff, d), lambda i, e: (e, 0, 0))],
        out_specs=tok,
        out_shape=jax.ShapeDtypeStruct((t, d), F32),
        scratch_shapes=[pltpu.VMEM((tb, d), BF16), pltpu.VMEM((N_EXPERTS, tb), F32), pltpu.VMEM((N_EXPERTS, tb), F32),
                        pltpu.VMEM((tb, d), F32), pltpu.VMEM((stacked, tb), BF16), pltpu.VMEM((stacked, d), BF16),
                        pltpu.SMEM((N_EXPERTS,), jnp.int32)],
        compiler_params=_params(("parallel", "arbitrary")),
        name="moe",
    )(x, sc, sh, g, nw, rwt, router_b.reshape(N_EXPERTS, 1), upper, w_gate, w_up, w_down)


def _proj1_kernel(x_ref, sc_ref, sh_ref, nw_ref, w_ref, wlr_ref, p_ref, lr_ref):
    h = _norm_mod(x_ref[...], nw_ref[...], sc_ref[...], sh_ref[...]).astype(BF16)
    p_ref[...] = jnp.dot(h, w_ref[...], preferred_element_type=F32).astype(p_ref.dtype)
    lr_ref[...] = jnp.dot(h, wlr_ref[...], preferred_element_type=F32)


def _proj1(x, sc, sh, nw, w_main, w_lr, seq, row_fn):
    t, d = x.shape
    tm = min(seq, 512)
    per_seq = seq // tm
    mod_spec = pl.BlockSpec((None, 1, d), lambda i: (row_fn(i // per_seq), 0, 0))
    tok = lambda n: pl.BlockSpec((tm, n), lambda i: (i, 0))
    return pl.pallas_call(
        _proj1_kernel,
        grid=(t // tm,),
        in_specs=[tok(d), mod_spec, mod_spec, _const_spec((1, d)), _const_spec(w_main.shape), _const_spec(w_lr.shape)],
        out_specs=[tok(w_main.shape[1]), tok(LANES)],
        out_shape=[jax.ShapeDtypeStruct((t, w_main.shape[1]), BF16), jax.ShapeDtypeStruct((t, LANES), F32)],
        compiler_params=_params(("parallel",)),
        name="proj1",
    )(x, sc, sh, nw, w_main, w_lr)


def _log_sigmoid(x):
    return jnp.minimum(x, 0.0) - jnp.log(1.0 + jnp.exp(-jnp.abs(x)))


def _gla_kernel(qf_ref, kf_ref, vf_ref, lf_ref, qb_ref, kb_ref, vb_ref, lb_ref, wgf_ref, wgb_ref, bg_ref,
                trif_ref, trib_ref, s0f_ref, s0b_ref, of_ref, ob_ref, sf_ref, sb_ref, stf_ref, stb_ref, *, dk, dv):
    i = pl.program_id(1)
    nb = pl.num_programs(1)
    tc = qf_ref.shape[0]

    @pl.when(i == 0)
    def _init():
        stf_ref[...] = s0f_ref[...].astype(F32)
        stb_ref[...] = s0b_ref[...].astype(F32)

    r = lax.broadcasted_iota(jnp.int32, (GLA_CHUNK, GLA_CHUNK), 0)
    c = lax.broadcasted_iota(jnp.int32, (GLA_CHUNK, GLA_CHUNK), 1)
    scale = dk ** -0.5

    def decay_sums(l_ref, wg_ref, bias, tri_ref):
        gate = _log_sigmoid(jnp.dot(l_ref[...], wg_ref[...], precision=HI, preferred_element_type=F32)
                            + bias) / GLA_GATE_NORM
        total = None
        rest = gate
        for _ in range(3):
            part = rest.astype(BF16)
            rest = rest - part.astype(F32)
            term = jnp.dot(tri_ref[...], part, preferred_element_type=F32)
            total = term if total is None else total + term
        return total

    b_f = decay_sums(lf_ref, wgf_ref, bg_ref[0:1, :], trif_ref)
    b_b = decay_sums(lb_ref, wgb_ref, bg_ref[1:2, :], trib_ref)
    n_chunks = tc // GLA_CHUNK
    chains = []
    for h in range(GLA_HEADS):
        chains.append((qf_ref, kf_ref, vf_ref, of_ref, b_f, stf_ref, h, False))
        chains.append((qb_ref, kb_ref, vb_ref, ob_ref, b_b, stb_ref, h, True))
    states = [st_ref[h] for (_, _, _, _, _, st_ref, h, _) in chains]
    for step in range(n_chunks):
        prepared = []
        for (q_ref, k_ref, v_ref, _, b_all, _, h, reverse) in chains:
            ci = n_chunks - 1 - step if reverse else step
            rs = slice(ci * GLA_CHUNK, (ci + 1) * GLA_CHUNK)
            ks, vs = slice(h * dk, (h + 1) * dk), slice(h * dv, (h + 1) * dv)
            b = b_all[rs, ks]
            b_end = b[0:1, :] if reverse else b[GLA_CHUNK - 1:GLA_CHUNK, :]
            q_dec = (q_ref[rs, ks].astype(F32) * scale * jnp.exp(b)).astype(BF16)
            k_dec = k_ref[rs, ks].astype(F32) * jnp.exp(-b)
            e_end = jnp.exp(b_end)
            prepared.append((rs, vs, q_dec, k_dec.astype(BF16), (k_dec * e_end).astype(BF16), e_end, v_ref[rs, vs]))
        first = []
        for n, (rs, vs, q_dec, k_bf, k_rem, e_end, v) in enumerate(prepared):
            att = lax.dot_general(q_dec, k_bf, NT, preferred_element_type=F32)
            carry = lax.dot_general(q_dec, states[n].astype(BF16), NT, preferred_element_type=F32)
            upd = lax.dot_general(v, k_rem, TN, preferred_element_type=F32)
            first.append((att, carry, upd))
        for n, (rs, vs, q_dec, k_bf, k_rem, e_end, v) in enumerate(prepared):
            att, carry, upd = first[n]
            reverse = chains[n][7]
            att = jnp.where((c >= r) if reverse else (c <= r), att, 0.0).astype(BF16)
            o = jnp.dot(att, v, preferred_element_type=F32) + carry
            chains[n][3][rs, vs] = o.astype(chains[n][3].dtype)
            states[n] = states[n] * e_end + upd
    for n, (_, _, _, _, _, st_ref, h, _) in enumerate(chains):
        st_ref[h] = states[n]

    @pl.when(i == nb - 1)
    def _emit():
        sf_ref[...] = stf_ref[...]
        sb_ref[...] = stb_ref[...]


def _gla(p, lr, wgf, wgb, bg, s0f, s0b, batch, seq, dk, dv):
    tc = min(seq, 256)
    nb = seq // tc
    hk, hv = GLA_HEADS * dk, GLA_HEADS * dv
    fwd = lambda width, col: pl.BlockSpec((tc, width), lambda b, i: (b * nb + i, col))
    bwd = lambda width, col: pl.BlockSpec((tc, width), lambda b, i: (b * nb + nb - 1 - i, col))
    st_spec = pl.BlockSpec((None, GLA_HEADS, dv, dk), lambda b, i: (b, 0, 0, 0))
    v_col = 2 * hk // hv
    idx = np.arange(tc)
    same = (idx[:, None] // GLA_CHUNK) == (idx[None, :] // GLA_CHUNK)
    tri_f = jnp.asarray(same & (idx[None, :] <= idx[:, None]), BF16)
    tri_b = jnp.asarray(same & (idx[None, :] >= idx[:, None]), BF16)
    return pl.pallas_call(
        functools.partial(_gla_kernel, dk=dk, dv=dv),
        grid=(batch, nb),
        in_specs=[fwd(hk, 0), fwd(hk, 1), fwd(hv, v_col), fwd(LANES, 0),
                  bwd(hk, 0), bwd(hk, 1), bwd(hv, v_col), bwd(LANES, 0),
                  _const_spec(wgf.shape), _const_spec(wgb.shape), _const_spec(bg.shape),
                  _const_spec(tri_f.shape), _const_spec(tri_b.shape), st_spec, st_spec],
        out_specs=[fwd(hv, 0), bwd(hv, 0), st_spec, st_spec],
        out_shape=[jax.ShapeDtypeStruct((batch * seq, hv), BF16)] * 2
        + [jax.ShapeDtypeStruct((batch, GLA_HEADS, dv, dk), F32)] * 2,
        scratch_shapes=[pltpu.VMEM((GLA_HEADS, dv, dk), F32)] * 2,
        compiler_params=_params(("parallel", "arbitrary")),
        name="gla_scan",
    )(p, p, p, lr, p, p, p, lr, wgf, wgb, bg, tri_f, tri_b, s0f, s0b)


def _outproj1_kernel(x_ref, of_ref, ob_ref, go_ref, g_ref, nw_ref, w_ref, out_ref, *, dv):
    o = of_ref[...].astype(F32) + ob_ref[...].astype(F32)
    parts = []
    for h in range(GLA_HEADS):
        oh = o[:, h * dv:(h + 1) * dv]
        ms = jnp.mean(oh * oh, axis=1, keepdims=True)
        parts.append(oh * lax.rsqrt(ms + EPS))
    o = jnp.concatenate(parts, axis=1) * nw_ref[...] * _silu(go_ref[...].astype(F32))
    out_ref[...] = x_ref[...] + g_ref[...] * jnp.dot(o.astype(BF16), w_ref[...], preferred_element_type=F32)


def _outproj1(x, o_f, o_b, p, g, out_norm_w, w_out, seq, row_fn, dv):
    t, d = x.shape
    hv = GLA_HEADS * dv
    tm = min(seq, 512)
    per_seq = seq // tm
    tok = lambda n, col=0: pl.BlockSpec((tm, n), lambda i: (i, col))
    go_col = (p.shape[1] - hv) // hv
    nw = jnp.tile(out_norm_w, GLA_HEADS).reshape(1, hv)
    wb = w_out.astype(BF16)
    return pl.pallas_call(
        functools.partial(_outproj1_kernel, dv=dv),
        grid=(t // tm,),
        in_specs=[tok(d), tok(hv), tok(hv), tok(hv, go_col),
                  pl.BlockSpec((None, 1, d), lambda i: (row_fn(i // per_seq), 0, 0)),
                  _const_spec((1, hv)), _const_spec(wb.shape)],
        out_specs=tok(d),
        out_shape=jax.ShapeDtypeStruct((t, d), F32),
        compiler_params=_params(("parallel",)),
        name="outproj1",
    )(x, o_f, o_b, p, g, nw, wb)


def _mod_rows(mod_l, d):
    return [mod_l[:, j * d:(j + 1) * d].reshape(mod_l.shape[0], 1, d) for j in range(6)]


def _trunk(x, batch, seq, mods, row_fn, shared_mod, norm_w, ab, cp, moe_w, cache, s0, emit_cache):
    d = x.shape[-1]
    t = batch * seq
    x = x.reshape(t, d)
    (w_in0, conv_w, conv_b, filt, hy_skip, qn_w, kn_w, lam_p, subln_w, w_out0) = ab
    (w_in1, gate_up_w, gate_up_b, out_norm_w, w_out1) = cp
    router_w, router_b, wg, wu, wd = moe_w
    width = d // 2
    hy_in = 3 * width

    sh1, sc1, g1, sh2, sc2, g2 = mods[0]
    nseg = DA_HEADS * 2
    qkw = jnp.stack([jnp.tile(qn_w, nseg), jnp.tile(kn_w, nseg)])
    seg = np.arange(DA_HEADS * DA_VD) // DA_DH
    ones = jnp.asarray(seg[:, None] == seg[None, :], BF16)
    v_cols = w_in0[:, hy_in + 2 * DA_HEADS * DA_VD:]
    w_chan = jnp.concatenate([w_in0[:, :hy_in], v_cols], axis=1).T.astype(BF16)
    outs = _proj0(x, sc1, sh1, norm_w[0, 0].reshape(1, d), w_in0[:, hy_in:].astype(BF16), w_chan, qkw, ones,
                  batch, seq, row_fn, rope=cache is not None, emit_cache=emit_cache)
    chan, q, k = outs[:3]
    y_hy = _hyena(chan, conv_w, conv_b, filt, hy_skip, seq, width)
    lam_init = 0.8 - 0.6 * math.exp(-0.3 * 0)
    o = _diff_attention(q, k, chan, hy_in, cache, lam_p, subln_w, batch, seq, lam_init)
    x = _outproj0(x, y_hy, o, g1, w_out0, seq, row_fn)
    x = _moe(x, sc2, sh2, g2, norm_w[0, 1].reshape(1, d), router_w, router_b, wg[0], wu[0], wd[0], seq, row_fn,
             shared_mod)

    sh1, sc1, g1, sh2, sc2, g2 = mods[1]
    dk = d // 2 // GLA_HEADS
    dv = d // GLA_HEADS
    n_main = GLA_HEADS * (2 * dk + 2 * dv)
    w_lr = jnp.pad(w_in1[:, n_main:], ((0, 0), (0, LANES - 2 * GLA_RANK))).astype(BF16)
    p, lr = _proj1(x, sc1, sh1, norm_w[1, 0].reshape(1, d), w_in1[:, :n_main].astype(BF16), w_lr, seq, row_fn)
    wgf = jnp.pad(gate_up_w[0], ((0, LANES - GLA_RANK), (0, 0)))
    wgb = jnp.pad(gate_up_w[1], ((GLA_RANK, LANES - 2 * GLA_RANK), (0, 0)))
    o_f, o_b, s_f, s_b = _gla(p, lr, wgf, wgb, gate_up_b, jnp.swapaxes(s0[0], 2, 3), jnp.swapaxes(s0[1], 2, 3),
                              batch, seq, dk, dv)
    s_f, s_b = jnp.swapaxes(s_f, 2, 3), jnp.swapaxes(s_b, 2, 3)
    x = _outproj1(x, o_f, o_b, p, g1, out_norm_w, w_out1, seq, row_fn, dv)
    x = _moe(x, sc2, sh2, g2, norm_w[1, 1].reshape(1, d), router_w, router_b, wg[1], wu[1], wd[1], seq, row_fn,
             shared_mod)
    return x.reshape(batch, seq, d), outs[3:], (s_f, s_b)


def kernel(x_prompt, x_sample, cache_l0_k, cache_l0_v, state_l1_fwd, state_l1_bwd, c, c_ctx, w_mod, b_mod, norm_w,
           l0_w_in, l0_conv_w, l0_conv_b, l0_filt_w1, l0_filt_b1, l0_filt_freq, l0_filt_w2, l0_filt_b2, l0_filt_w3,
           l0_hy_skip, l0_qn_w, l0_kn_w, l0_lambda, l0_subln_w, l0_w_out, l1_w_in, l1_gate_up_w, l1_gate_up_b,
           l1_out_norm_w, l1_w_out, moe_router_w, moe_router_b, moe_w_gate, moe_w_up, moe_w_down):
    d = x_prompt.shape[-1]
    n_lat = c.shape[0]
    ctx_row = n_lat
    rows = 2 * SUBLANES
    cond = jnp.zeros((rows, d), F32).at[:n_lat].set(c).at[ctx_row].set(c_ctx)
    mod = _modulation(cond, w_mod, b_mod)
    mods = [_mod_rows(mod[l], d) for l in range(mod.shape[0])]
    filt = (l0_filt_w1, l0_filt_b1, l0_filt_freq, l0_filt_w2, l0_filt_b2, l0_filt_w3)
    ab = (l0_w_in, l0_conv_w, l0_conv_b, filt, l0_hy_skip, l0_qn_w, l0_kn_w, l0_lambda, l0_subln_w, l0_w_out)
    cp = (l1_w_in, l1_gate_up_w, l1_gate_up_b, l1_out_norm_w, l1_w_out)
    moe_w = (moe_router_w, moe_router_b, moe_w_gate.astype(BF16), moe_w_up.astype(BF16), moe_w_down.astype(BF16))

    b_ctx, l_ctx = x_prompt.shape[:2]
    zero_state = jnp.zeros((b_ctx,) + state_l1_fwd.shape[1:], F32)
    y_prompt, (new_k, new_v), (new_sf, new_sb) = _trunk(
        x_prompt, b_ctx, l_ctx, mods, lambda b: ctx_row, True, norm_w, ab, cp, moe_w, None,
        (zero_state, zero_state), True)
    b_lat, l_lat = x_sample.shape[:2]
    y_sample, _, _ = _trunk(
        x_sample, b_lat, l_lat, mods, lambda b: b, False, norm_w, ab, cp, moe_w, (cache_l0_k, cache_l0_v),
        (state_l1_fwd, state_l1_bwd), False)
    return (y_prompt, y_sample, new_k, new_v, new_sf, new_sb)
```

```python
import cmath
import functools
import math

import numpy as np
import jax
import jax.numpy as jnp
from jax import lax
from jax.experimental import pallas as pl
from jax.experimental.pallas import tpu as pltpu

F32 = jnp.float32
BF16 = jnp.bfloat16
HI = lax.Precision.HIGHEST
EPS = 1e-6

LANES = 128
SUBLANES = 8
VMEM_LIMIT = 56 << 20

DA_HEADS = 4
DA_DH = 64
DA_VD = 2 * DA_DH
ROPE_NF = DA_DH // 4
ROPE_BASE = 10000.0
GRID_W = 64
HY_BANDS = 16
HY_TARGET = 1e-2
HY_MAX_DECAY = math.log(HY_TARGET) / 0.3
HY_MIN_DECAY = math.log(HY_TARGET) / 1.5
GLA_HEADS = 4
GLA_RANK = 16
GLA_GATE_NORM = 16.0
GLA_CHUNK = 64
N_EXPERTS = 16
N_GROUPS = 4
EPG = N_EXPERTS // N_GROUPS
TOKEN_TILE = 512
MOD_COL_TILE = 1536
FFT_BLOCK = 256
HY_TILE_ELEMS = 128 * 4096
ATTN_Q_TILE = 512
ATTN_KEY_CHUNK = 512
ATTN_BOUND_PAD = 1.02
ATTN_SUM_FLOOR = 2.0 ** -60
GLA_ROWS_PER_STEP = 256
MOE_BLOCK = 1024
MOE_TILE = 160
MOE_EXPERTS_PER_STEP = 4
NT = (((1,), (1,)), ((), ()))
TN = (((0,), (0,)), ((), ()))


def _params(sem):
    return pltpu.CompilerParams(dimension_semantics=sem, vmem_limit_bytes=VMEM_LIMIT)


def _const_spec(shape):
    nd = len(shape)
    return pl.BlockSpec(shape, lambda *_: (0,) * nd)


def _silu(x):
    return x * (1.0 / (1.0 + jnp.exp(-x)))


def _norm_mod(x, nw, sc, sh):
    ms = jnp.mean(x * x, axis=-1, keepdims=True)
    return x * lax.rsqrt(ms + EPS) * nw * (1.0 + sc) + sh


def _mod_kernel(c_ref, w_ref, b_ref, o_ref):
    s = _silu(c_ref[...])
    o_ref[...] = jnp.dot(s, w_ref[...], precision=HI, preferred_element_type=F32) + b_ref[...]


def _modulation(cond, w_mod, b_mod):
    depth, d, d6 = w_mod.shape
    r = cond.shape[0]
    tn = MOD_COL_TILE
    return pl.pallas_call(
        _mod_kernel,
        grid=(depth, d6 // tn),
        in_specs=[
            pl.BlockSpec((r, d), lambda l, j: (0, 0)),
            pl.BlockSpec((None, d, tn), lambda l, j: (l, 0, j)),
            pl.BlockSpec((None, 1, tn), lambda l, j: (l, 0, j)),
        ],
        out_specs=pl.BlockSpec((None, r, tn), lambda l, j: (l, 0, j)),
        out_shape=jax.ShapeDtypeStruct((depth, r, d6), F32),
        compiler_params=_params(("parallel", "parallel")),
        name="modulation",
    )(cond, w_mod, b_mod.reshape(depth, 1, d6))


def _proj0_kernel(x_ref, sc_ref, sh_ref, nw_ref, wtok_ref, wchan_ref, qkw_ref, ones_ref, cos_ref, sin_ref,
                  chan_ref, q_ref, k_ref, *cache_refs, rope):
    h = _norm_mod(x_ref[...], nw_ref[...], sc_ref[...], sh_ref[...]).astype(BF16)
    chan_ref[...] = lax.dot_general(wchan_ref[...], h, NT, preferred_element_type=F32).astype(chan_ref.dtype)
    p = jnp.dot(h, wtok_ref[...], preferred_element_type=F32)
    w = DA_HEADS * DA_VD
    ones = ones_ref[...]

    def segnorm(z, gain):
        ss = jnp.dot((z * z).astype(BF16), ones, preferred_element_type=F32)
        return z * lax.rsqrt(ss * (1.0 / DA_DH) + EPS) * gain

    q = segnorm(p[:, :w], qkw_ref[0:1, :])
    k = segnorm(p[:, w:2 * w], qkw_ref[1:2, :])
    if cache_refs:
        kn_ref, vn_ref = cache_refs
        v = p[:, 2 * w:]
        for hh in range(DA_HEADS):
            kn_ref[hh] = k[:, hh * DA_VD:(hh + 1) * DA_VD]
            vn_ref[hh] = v[:, hh * DA_VD:(hh + 1) * DA_VD]
    if rope:
        cos = jnp.concatenate([cos_ref[...]] * (w // LANES), axis=1)
        sin = jnp.concatenate([sin_ref[...]] * (w // LANES), axis=1)
        lane = lax.broadcasted_iota(jnp.int32, q.shape, 1)
        first = (lane % (2 * ROPE_NF)) < ROPE_NF

        def rot(z):
            partner = jnp.where(first, pltpu.roll(z, w - ROPE_NF, 1), pltpu.roll(z, ROPE_NF, 1))
            return z * cos + partner * sin

        q = rot(q)
        k = rot(k)
    q_ref[...] = (q * (DA_DH ** -0.5 * math.log2(math.e))).astype(q_ref.dtype)
    k_ref[...] = k.astype(k_ref.dtype)


def _rope_tables(seq):
    t = np.arange(seq)
    lane = np.arange(LANES)
    d = lane % DA_DH
    axis = d // (2 * ROPE_NF)
    part = (d % (2 * ROPE_NF)) // ROPE_NF
    f = d % ROPE_NF
    pos = jnp.where(axis[None, :] == 0, (t // GRID_W)[:, None], (t % GRID_W)[:, None]).astype(F32)
    inv = ROPE_BASE ** (-jnp.arange(ROPE_NF, dtype=F32) / ROPE_NF)
    ang = pos * inv[f][None, :]
    sign = jnp.asarray(np.where(part == 0, -1.0, 1.0), F32)[None, :]
    return jnp.cos(ang), jnp.sin(ang) * sign


def _proj0(x, sc, sh, nw, wqkv, whyt, qkw, ones, batch, seq, row_fn, rope, emit_cache):
    t, d = x.shape
    tm = min(seq, TOKEN_TILE)
    per_seq = seq // tm
    w = DA_HEADS * DA_VD
    hy_in = whyt.shape[0]
    if not emit_cache:
        wqkv = wqkv[:, :2 * w]
    if rope:
        cos, sin = _rope_tables(seq)
    else:
        cos = jnp.ones((seq, LANES), F32)
        sin = jnp.zeros((seq, LANES), F32)
    mod_spec = pl.BlockSpec((None, 1, d), lambda i: (row_fn(i // per_seq), 0, 0))
    tab_spec = pl.BlockSpec((tm, LANES), lambda i: (i % per_seq, 0))
    tok = lambda n: pl.BlockSpec((tm, n), lambda i: (i, 0))
    out_specs = [pl.BlockSpec((None, hy_in, tm), lambda i: (i // per_seq, 0, i % per_seq)), tok(w), tok(w)]
    out_shape = [jax.ShapeDtypeStruct((batch, hy_in, seq), BF16)] + [jax.ShapeDtypeStruct((t, w), BF16)] * 2
    if emit_cache:
        cspec = pl.BlockSpec((None, DA_HEADS, tm, DA_VD), lambda i: (i // per_seq, 0, i % per_seq, 0))
        out_specs += [cspec, cspec]
        out_shape += [jax.ShapeDtypeStruct((batch, DA_HEADS, seq, DA_VD), F32)] * 2
    return pl.pallas_call(
        functools.partial(_proj0_kernel, rope=rope),
        grid=(t // tm,),
        in_specs=[tok(d), mod_spec, mod_spec, _const_spec((1, d)), _const_spec(wqkv.shape), _const_spec(whyt.shape),
                  _const_spec(qkw.shape), _const_spec(ones.shape), tab_spec, tab_spec],
        out_specs=out_specs,
        out_shape=out_shape,
        compiler_params=_params(("parallel",)),
        name="proj0",
    )(x, sc, sh, nw, wqkv, whyt, qkw, ones, cos, sin)


def _c_add(a, b, sign=1.0):
    if b is None:
        return a
    if a is None:
        return b if sign > 0 else tuple(None if p is None else -p for p in b)
    out = []
    for pa, pb in zip(a, b):
        if pb is None:
            out.append(pa)
        elif pa is None:
            out.append(pb if sign > 0 else -pb)
        else:
            out.append(pa + pb if sign > 0 else pa - pb)
    return tuple(out)


def _c_mul_const(w, a):
    if a is None:
        return None
    wr = 0.0 if abs(w.real) < 1e-12 else w.real
    wi = 0.0 if abs(w.imag) < 1e-12 else w.imag
    ar, ai = a

    def scaled(c, p):
        if p is None or c == 0.0:
            return None
        if c == 1.0:
            return p
        if c == -1.0:
            return -p
        return c * p

    re = _c_add((scaled(wr, ar),), (scaled(wi, ai),), -1.0)[0]
    im = _c_add((scaled(wr, ai),), (scaled(wi, ar),), 1.0)[0]
    return (re, im)


def _fft_list(vals, sign, first_half_only=False):
    n = len(vals)
    if n == 1:
        return list(vals)
    ev = _fft_list(vals[0::2], sign)
    od = _fft_list(vals[1::2], sign)
    out = [None] * n
    for k in range(n // 2):
        tw = _c_mul_const(cmath.exp(sign * 2j * math.pi * k / n), od[k])
        out[k] = _c_add(ev[k], tw, 1.0)
        if not first_half_only:
            out[k + n // 2] = _c_add(ev[k], tw, -1.0)
    return out


def _dft_mats(n2, total):
    k = np.arange(n2)
    ang = 2.0 * np.pi * np.outer(k, k) / n2
    fr, fi = np.cos(ang), -np.sin(ang)
    fwd = np.block([[fr, fi], [-fi, fr]])
    inv = np.block([[fr, -fi], [fi, fr]]) / total
    return jnp.asarray(fwd, BF16), jnp.asarray(inv, BF16)


def _twiddles(n1, n2):
    ang = 2.0 * np.pi * np.outer(np.arange(n1), np.arange(n2)) / (n1 * n2)
    return jnp.asarray(np.cos(ang), F32), jnp.asarray(-np.sin(ang), F32)


def _fft_split(seq):
    n = 2 * seq
    n2 = min(FFT_BLOCK, n // 2)
    return n // n2, n2


def _across_fwd(load, n_in, z_ref, twr_ref, twi_ref, n1, n2, rows):
    per_row = n2 // LANES

    def body(it, carry):
        r0 = pl.multiple_of((it // per_row) * SUBLANES, SUBLANES)
        l0 = pl.multiple_of((it % per_row) * LANES, LANES)
        vals = [load(j, r0, l0) for j in range(n_in)] + [None] * (n1 - n_in)
        outs = _fft_list(vals, -1.0)
        for k1 in range(n1):
            twr = twr_ref[pl.ds(k1, 1), pl.ds(l0, LANES)]
            twi = twi_ref[pl.ds(k1, 1), pl.ds(l0, LANES)]
            re, im = outs[k1]
            zero = jnp.zeros((SUBLANES, LANES), F32)
            re = zero if re is None else re
            im = zero if im is None else im
            z_ref[k1, pl.ds(r0, SUBLANES), pl.ds(l0, LANES)] = re * twr - im * twi
            z_ref[k1, pl.ds(r0, SUBLANES), pl.ds(n2 + l0, LANES)] = re * twi + im * twr
        return carry

    lax.fori_loop(0, (rows // SUBLANES) * per_row, body, 0)


def _hy_hidden_kernel(w1_ref, b1_ref, freq_ref, w2_ref, b2_ref, bands_ref, o_ref, *, seq):
    n = 2 * seq
    k = lax.broadcasted_iota(jnp.int32, (1, n), 1)
    pos = jnp.where(k < seq, k, n - k).astype(F32)
    t = pos / (seq - 1)
    ang = 2.0 * math.pi * pos / seq * bands_ref[...]
    z = (w1_ref[:, 0:1] * t
         + jnp.dot(w1_ref[:, 1:1 + HY_BANDS], jnp.cos(ang), precision=HI, preferred_element_type=F32)
         + jnp.dot(w1_ref[:, 1 + HY_BANDS:], -jnp.sin(ang), precision=HI, preferred_element_type=F32)
         + b1_ref[...])
    hid = jnp.sin(freq_ref[:, 0:1] * z)
    hid = jnp.sin(freq_ref[:, 1:2] * (jnp.dot(w2_ref[...], hid, precision=HI, preferred_element_type=F32) + b2_ref[...]))
    o_ref[...] = hid


def _hy_hidden(seq, f_w1, f_b1, f_freq, f_w2, f_b2):
    ffn = f_w1.shape[1]
    bands = jnp.linspace(1e-4, HY_BANDS - 1, HY_BANDS, dtype=F32).reshape(HY_BANDS, 1)
    args = (f_w1.T, f_b1.reshape(ffn, 1), f_freq.T, f_w2.T, f_b2.reshape(ffn, 1), bands)
    return pl.pallas_call(
        functools.partial(_hy_hidden_kernel, seq=seq),
        grid=(1,),
        in_specs=[_const_spec(a.shape) for a in args],
        out_specs=_const_spec((ffn, 2 * seq)),
        out_shape=jax.ShapeDtypeStruct((ffn, 2 * seq), F32),
        compiler_params=_params(("arbitrary",)),
        name="hyena_filter_hidden",
    )(*args)


def _hy_spec_kernel(hid_ref, wb_ref, wa_ref, dec_ref, twr_ref, twi_ref, gf_ref, o_ref, taps_ref, z_ref, *, seq, n1, n2):
    ct = wb_ref.shape[0]
    hid = hid_ref[...]
    back = jnp.dot(wb_ref[...], hid[:, :seq], precision=HI, preferred_element_type=F32)
    ahead = jnp.dot(wa_ref[...], hid[:, seq:], precision=HI, preferred_element_type=F32)
    lane = lax.broadcasted_iota(jnp.int32, (1, seq), 1)
    dec = dec_ref[...]
    t_back = lane.astype(F32) / (seq - 1)
    t_ahead = (seq - lane).astype(F32) / (seq - 1)
    back = back * jnp.exp(-t_back * dec)
    ahead = jnp.where(lane == 0, 0.0, ahead * jnp.exp(-t_ahead * dec))
    norm = (jnp.sum(jnp.abs(back), axis=1, keepdims=True) + jnp.sum(jnp.abs(ahead), axis=1, keepdims=True)) + EPS
    taps_ref[:, :seq] = back / norm
    taps_ref[:, seq:] = ahead / norm

    def load(j, r0, l0):
        return (taps_ref[pl.ds(r0, SUBLANES), pl.ds(j * n2 + l0, LANES)], None)

    _across_fwd(load, n1, z_ref, twr_ref, twi_ref, n1, n2, ct)
    for k1 in range(n1):
        o_ref[k1] = jnp.dot(z_ref[k1].astype(BF16), gf_ref[...], preferred_element_type=F32)


def _hy_filter_spectra(hid, f_w3, seq, width, ct):
    n1, n2 = _fft_split(seq)
    ffn = f_w3.shape[0]
    w3t = f_w3.T.reshape(4, width, ffn)
    decay = jnp.abs(jnp.linspace(HY_MIN_DECAY, HY_MAX_DECAY, width, dtype=F32)).reshape(width, 1)
    twr, twi = _twiddles(n1, n2)
    gf, _ = _dft_mats(n2, 2 * seq)
    return pl.pallas_call(
        functools.partial(_hy_spec_kernel, seq=seq, n1=n1, n2=n2),
        grid=(2, width // ct),
        in_specs=[
            _const_spec(hid.shape),
            pl.BlockSpec((None, ct, ffn), lambda i, c: (2 * i, c, 0)),
            pl.BlockSpec((None, ct, ffn), lambda i, c: (2 * i + 1, c, 0)),
            pl.BlockSpec((ct, 1), lambda i, c: (c, 0)),
            _const_spec(twr.shape), _const_spec(twi.shape), _const_spec(gf.shape),
        ],
        out_specs=pl.BlockSpec((None, n1, ct, 2 * n2), lambda i, c: (i, 0, c, 0)),
        out_shape=jax.ShapeDtypeStruct((2, n1, width, 2 * n2), F32),
        scratch_shapes=[pltpu.VMEM((ct, 2 * seq), F32), pltpu.VMEM((n1, ct, 2 * n2), F32)],
        compiler_params=_params(("parallel", "parallel")),
        name="hyena_filter_spectra",
    )(hid, w3t, w3t, decay, twr, twi, gf)


def _short_conv_rows(ref, cw, r0, nrows, seq):
    x = ref[pl.ds(r0, nrows), :].astype(F32)
    lane = lax.broadcasted_iota(jnp.int32, (nrows, LANES), 1)
    left = pltpu.roll(x, 1, 1)
    right = pltpu.roll(x, seq - 1, 1)
    left = jnp.concatenate([jnp.where(lane == 0, 0.0, left[:, :LANES]), left[:, LANES:]], axis=1)
    right = jnp.concatenate([right[:, :seq - LANES], jnp.where(lane == LANES - 1, 0.0, right[:, seq - LANES:])], axis=1)
    return left * cw[:, 0:1] + x * cw[:, 1:2] + right * cw[:, 2:3] + cw[:, 3:4]


def _hy_conv_kernel(uin_ref, gin_ref, cwu_ref, cwg_ref, d_ref, h_ref, twr_ref, twi_ref, gf_ref, gi_ref,
                    o_ref, u_ref, z_ref, *, seq, n1, n2, conv_u):
    ct = uin_ref.shape[1]
    rows16 = 2 * SUBLANES

    def prep(i, carry):
        r0 = pl.multiple_of(i * rows16, rows16)
        cwg = cwg_ref[pl.ds(r0, rows16), :]
        cwu = cwu_ref[pl.ds(r0, rows16), :]
        for e in range(2):
            o_ref[e, pl.ds(r0, rows16), :] = _short_conv_rows(gin_ref.at[e], cwg, r0, rows16, seq)
            if conv_u:
                u_ref[e, pl.ds(r0, rows16), :] = _short_conv_rows(uin_ref.at[e], cwu, r0, rows16, seq)
            else:
                u_ref[e, pl.ds(r0, rows16), :] = uin_ref[e, pl.ds(r0, rows16), :].astype(F32)
        return carry

    lax.fori_loop(0, ct // rows16, prep, 0)

    def load(j, r0, l0):
        return (u_ref[0, pl.ds(r0, SUBLANES), pl.ds(j * n2 + l0, LANES)],
                u_ref[1, pl.ds(r0, SUBLANES), pl.ds(j * n2 + l0, LANES)])

    _across_fwd(load, n1 // 2, z_ref, twr_ref, twi_ref, n1, n2, ct)

    group = 4 if n1 % 4 == 0 else 2

    def within(g, carry):
        ks = [group * g + j for j in range(group)]
        zz = jnp.concatenate([z_ref[k1] for k1 in ks], axis=0)
        y = jnp.dot(zz.astype(BF16), gf_ref[...], preferred_element_type=F32)
        hh = jnp.concatenate([h_ref[k1] for k1 in ks], axis=0)
        yr, yi = y[:, :n2], y[:, n2:]
        hr, hi = hh[:, :n2], hh[:, n2:]
        p = jnp.concatenate([yr * hr - yi * hi, yr * hi + yi * hr], axis=1)
        q = jnp.dot(p.astype(BF16), gi_ref[...], preferred_element_type=F32)
        for j, k1 in enumerate(ks):
            z_ref[k1] = q[j * ct:(j + 1) * ct]
        return carry

    lax.fori_loop(0, n1 // group, within, 0)

    per_row = n2 // LANES

    def finish(it, carry):
        r0 = pl.multiple_of((it // per_row) * SUBLANES, SUBLANES)
        l0 = pl.multiple_of((it % per_row) * LANES, LANES)
        vals = []
        for k1 in range(n1):
            twr = twr_ref[pl.ds(k1, 1), pl.ds(l0, LANES)]
            twi = twi_ref[pl.ds(k1, 1), pl.ds(l0, LANES)]
            re = z_ref[k1, pl.ds(r0, SUBLANES), pl.ds(l0, LANES)]
            im = z_ref[k1, pl.ds(r0, SUBLANES), pl.ds(n2 + l0, LANES)]
            vals.append((re * twr + im * twi, im * twr - re * twi))
        outs = _fft_list(vals, 1.0, first_half_only=True)
        d = d_ref[pl.ds(r0, SUBLANES), :]
        for j in range(n1 // 2):
            sl = (pl.ds(r0, SUBLANES), pl.ds(j * n2 + l0, LANES))
            for e in range(2):
                o_ref[(e,) + sl] = o_ref[(e,) + sl] * (outs[j][e] + u_ref[(e,) + sl] * d)
        return carry

    lax.fori_loop(0, (ct // SUBLANES) * per_row, finish, 0, unroll=2 if n1 <= 16 else 1)


def _hy_conv(u, u_off, g, g_off, cw, spectra, conv_idx, d_skip, seq, width, ct, conv_u):
    batch = u.shape[0]
    n1, n2 = _fft_split(seq)
    twr, twi = _twiddles(n1, n2)
    gf, gi = _dft_mats(n2, 2 * seq)
    uo, go = u_off // ct, g_off // ct
    return pl.pallas_call(
        functools.partial(_hy_conv_kernel, seq=seq, n1=n1, n2=n2, conv_u=conv_u),
        grid=(width // ct, batch // 2),
        in_specs=[
            pl.BlockSpec((2, ct, seq), lambda c, p: (p, uo + c, 0)),
            pl.BlockSpec((2, ct, seq), lambda c, p: (p, go + c, 0)),
            pl.BlockSpec((ct, 4), lambda c, p: (uo + c, 0)),
            pl.BlockSpec((ct, 4), lambda c, p: (go + c, 0)),
            pl.BlockSpec((ct, 1), lambda c, p: (c, 0)),
            pl.BlockSpec((None, n1, ct, 2 * n2), lambda c, p: (conv_idx, 0, c, 0), pipeline_mode=pl.Buffered(1)),
            _const_spec(twr.shape), _const_spec(twi.shape), _const_spec(gf.shape), _const_spec(gi.shape),
        ],
        out_specs=pl.BlockSpec((2, ct, seq), lambda c, p: (p, c, 0)),
        out_shape=jax.ShapeDtypeStruct((batch, width, seq), F32),
        scratch_shapes=[pltpu.VMEM((2, ct, seq), F32), pltpu.VMEM((n1, ct, 2 * n2), F32)],
        compiler_params=_params(("arbitrary", "arbitrary")),
        name="hyena_conv",
    )(u, g, cw, cw, d_skip, spectra, twr, twi, gf, gi)


def _hyena(hyt, conv_w, conv_b, filt, hy_skip, seq, width):
    f_w1, f_b1, f_freq, f_w2, f_b2, f_w3 = filt
    ct = min(width, LANES * max(1, HY_TILE_ELEMS // (LANES * seq)))
    hid = _hy_hidden(seq, f_w1, f_b1, f_freq, f_w2, f_b2)
    spectra = _hy_filter_spectra(hid, f_w3, seq, width, ct)
    cw = jnp.concatenate([conv_w.T, conv_b[:, None]], axis=1)
    y1 = _hy_conv(hyt, 0, hyt, width, cw, spectra, 0, hy_skip[0].reshape(width, 1), seq, width, ct, True)
    return _hy_conv(y1, 0, hyt, 2 * width, cw, spectra, 1, hy_skip[1].reshape(width, 1), seq, width, ct, False)


def _attn_kernel(q_ref, k_ref, vt_ref, *rest, lam_init, with_cache, kc):
    if with_cache:
        kc_ref, vct_ref, lam_ref, sw_ref, o_ref, kmax_ref = rest
    else:
        lam_ref, sw_ref, o_ref, kmax_ref = rest
    lp = lam_ref[...]
    lam = (jnp.exp(jnp.sum(lp[0:1] * lp[1:2], axis=1, keepdims=True))
           - jnp.exp(jnp.sum(lp[2:3] * lp[3:4], axis=1, keepdims=True)) + lam_init)
    q = q_ref[...]
    lane = lax.broadcasted_iota(jnp.int32, q.shape, 1)
    zero = jnp.zeros_like(q)
    qs = (jnp.where(lane < DA_DH, q, zero), jnp.where(lane >= DA_DH, q, zero))
    seq = k_ref.shape[0]
    chunks = [(k_ref[c * kc:(c + 1) * kc, :], vt_ref[:, c * kc:(c + 1) * kc]) for c in range(seq // kc)]
    if with_cache:
        chunks.append((kc_ref[...].astype(BF16), vct_ref[...].astype(BF16)))
    tq = q.shape[0]
    sub = lax.broadcasted_iota(jnp.int32, (SUBLANES, DA_VD), 0)
    ln = lax.broadcasted_iota(jnp.int32, (SUBLANES, DA_VD), 1)
    pick = jnp.where((ln // DA_DH) == sub, 1.0, 0.0).astype(BF16)

    def sq_norms(x):
        xf = x.astype(F32)
        return lax.dot_general(pick, (xf * xf).astype(BF16), NT, preferred_element_type=F32)

    @pl.when(pl.program_id(2) == 0)
    def _key_norms():
        best = None
        for kk, _ in chunks:
            cur = jnp.max(sq_norms(kk), axis=1, keepdims=True)
            best = cur if best is None else jnp.maximum(best, cur)
        kmax_ref[...] = jnp.broadcast_to(best, kmax_ref.shape)

    def scores(ci):
        return [lax.dot_general(chunks[ci][0], qm, NT, preferred_element_type=F32) for qm in qs]

    def finish(acc1, l1, acc2, l2):
        o = acc1 * (1.0 / l1) - acc2 * (lam / l2)
        ms = jnp.mean(o * o, axis=0, keepdims=True)
        o_ref[...] = (o * lax.rsqrt(ms + EPS) * sw_ref[...] * (1.0 - lam_init)).astype(o_ref.dtype)

    bound = jnp.sqrt(sq_norms(q) * kmax_ref[:, 0:1]) * ATTN_BOUND_PAD
    lsum = [jnp.zeros((SUBLANES, tq), F32), jnp.zeros((SUBLANES, tq), F32)]
    acc = [None, None]
    s_next = scores(0)
    for ci, (kk, vv) in enumerate(chunks):
        s_cur = s_next
        if ci + 1 < len(chunks):
            s_next = scores(ci + 1)
        for mi in range(2):
            p = jnp.exp2(s_cur[mi] - bound[mi:mi + 1, :])
            lsum[mi] = lsum[mi] + jnp.sum(p.reshape(-1, SUBLANES, tq), axis=0)
            part = jnp.dot(vv, p.astype(BF16), preferred_element_type=F32)
            acc[mi] = part if acc[mi] is None else acc[mi] + part
    l1 = jnp.sum(lsum[0], axis=0, keepdims=True)
    l2 = jnp.sum(lsum[1], axis=0, keepdims=True)
    finish(acc[0], l1, acc[1], l2)
    safe = jnp.min(jnp.minimum(l1, l2)) > ATTN_SUM_FLOOR

    @pl.when(jnp.logical_not(safe))
    def _online():
        state = [None, None]
        for ci, (kk, vv) in enumerate(chunks):
            s_cur = scores(ci)
            for mi in range(2):
                s = s_cur[mi]
                cmax = jnp.max(s, axis=0, keepdims=True)
                if state[mi] is None:
                    m = cmax
                    p = jnp.exp2(s - m)
                    l = jnp.sum(p, axis=0, keepdims=True)
                    a = jnp.dot(vv, p.astype(BF16), preferred_element_type=F32)
                else:
                    m_old, l, a = state[mi]
                    m = jnp.maximum(m_old, cmax)
                    alpha = jnp.exp2(m_old - m)
                    p = jnp.exp2(s - m)
                    l = l * alpha + jnp.sum(p, axis=0, keepdims=True)
                    a = a * alpha + jnp.dot(vv, p.astype(BF16), preferred_element_type=F32)
                state[mi] = (m, l, a)
        finish(state[0][2], state[0][1], state[1][2], state[1][1])


def _diff_attention(q, k, chan, v_row0, cache, lam_p, subln_w, batch, seq, lam_init):
    tq = min(seq, ATTN_Q_TILE)
    kc = min(seq, ATTN_KEY_CHUNK)
    nq = seq // tq
    vb = v_row0 // DA_VD
    grid = (batch, DA_HEADS, nq)
    in_specs = [pl.BlockSpec((tq, DA_VD), lambda b, h, i: (b * nq + i, h)),
                pl.BlockSpec((seq, DA_VD), lambda b, h, i: (b, h)),
                pl.BlockSpec((None, DA_VD, seq), lambda b, h, i: (b, vb + h, 0))]
    args = [q, k, chan]
    if cache is not None:
        past = cache[0].shape[2]
        in_specs += [pl.BlockSpec((None, None, past, DA_VD), lambda b, h, i: (b, h, 0, 0)),
                     pl.BlockSpec((None, None, DA_VD, past), lambda b, h, i: (b, h, 0, 0))]
        args += [cache[0], jnp.swapaxes(cache[1], 2, 3)]
    in_specs += [_const_spec(lam_p.shape), _const_spec((DA_VD, 1))]
    args += [lam_p, subln_w.reshape(DA_VD, 1)]
    return pl.pallas_call(
        functools.partial(_attn_kernel, lam_init=lam_init, with_cache=cache is not None, kc=kc),
        grid=grid,
        in_specs=in_specs,
        out_specs=pl.BlockSpec((None, DA_VD, tq), lambda b, h, i: (b, h, i)),
        out_shape=jax.ShapeDtypeStruct((batch, DA_HEADS * DA_VD, seq), BF16),
        scratch_shapes=[pltpu.VMEM((SUBLANES, LANES), F32)],
        compiler_params=_params(("parallel", "parallel", "arbitrary")),
        name="diff_attention",
    )(*args)


def _outproj0_kernel(x_ref, yt_ref, ot_ref, g_ref, wy_ref, wo_ref, out_ref):
    mix = lax.dot_general(yt_ref[...].astype(BF16), wy_ref[...], TN, preferred_element_type=F32)
    mix = mix + lax.dot_general(ot_ref[...], wo_ref[...], TN, preferred_element_type=F32)
    out_ref[...] = x_ref[...] + g_ref[...] * mix


def _outproj0(x, yt, ot, g, w_out, seq, row_fn):
    t, d = x.shape
    width = yt.shape[1]
    tm = min(seq, TOKEN_TILE)
    per_seq = seq // tm
    wy = w_out[:width].astype(BF16)
    wo = w_out[width:].astype(BF16)
    tok = pl.BlockSpec((tm, d), lambda i: (i, 0))
    chan = lambda n: pl.BlockSpec((None, n, tm), lambda i: (i // per_seq, 0, i % per_seq))
    return pl.pallas_call(
        _outproj0_kernel,
        grid=(t // tm,),
        in_specs=[tok, chan(width), chan(ot.shape[1]),
                  pl.BlockSpec((None, 1, d), lambda i: (row_fn(i // per_seq), 0, 0)),
                  _const_spec(wy.shape), _const_spec(wo.shape)],
        out_specs=tok,
        out_shape=jax.ShapeDtypeStruct((t, d), F32),
        compiler_params=_params(("parallel",)),
        name="outproj0",
    )(x, yt, ot, g, wy, wo)


def _top2_of_rows(rows):
    n = len(rows)
    v1 = functools.reduce(jnp.maximum, rows)
    i1 = jnp.full(rows[0].shape, n - 1, jnp.int32)
    for j in range(n - 2, -1, -1):
        i1 = jnp.where(rows[j] == v1, j, i1)
    masked = [jnp.where(i1 == j, -jnp.inf, rows[j]) for j in range(n)]
    v2 = functools.reduce(jnp.maximum, masked)
    i2 = jnp.full(rows[0].shape, n - 1, jnp.int32)
    for j in range(n - 2, -1, -1):
        i2 = jnp.where(masked[j] == v2, j, i2)
    return v1, i1, v2, i2


def _select_rows(idx, rows):
    out = rows[-1]
    for j in range(len(rows) - 2, -1, -1):
        out = jnp.where(idx == j, rows[j], out)
    return out


def _moe_kernel(x_ref, sc_ref, sh_ref, g_ref, nw_ref, rwt_ref, rb_ref, upper_ref, wg_ref, wu_ref, wd_ref,
                out_ref, hb_ref, rank_ref, comb_ref, acc_ref, sel_ref, y_ref, cnt_ref):
    step = pl.program_id(1)
    tb = x_ref.shape[0]

    @pl.when(step == 0)
    def _route():
        h = _norm_mod(x_ref[...], nw_ref[...], sc_ref[...], sh_ref[...])
        hb = h.astype(BF16)
        hb_ref[...] = hb
        h_lo = (h - hb.astype(F32)).astype(BF16)
        logits = (lax.dot_general(rwt_ref[0], hb, NT, preferred_element_type=F32)
                  + lax.dot_general(rwt_ref[0], h_lo, NT, preferred_element_type=F32)
                  + lax.dot_general(rwt_ref[1], hb, NT, preferred_element_type=F32))
        s = 1.0 / (1.0 + jnp.exp(-logits))
        s_sel = s + rb_ref[...]
        groups = tb // LANES

        def pack(row):
            return jnp.concatenate([row[:, c * LANES:(c + 1) * LANES] for c in range(groups)], axis=0)

        def unpack(tile):
            return jnp.concatenate([tile[c:c + 1, :] for c in range(groups)], axis=1)

        s_rows = [pack(s[j:j + 1, :]) for j in range(N_EXPERTS)]
        sel_rows = [pack(s_sel[j:j + 1, :]) for j in range(N_EXPERTS)]
        tops = [_top2_of_rows(sel_rows[g * EPG:(g + 1) * EPG]) for g in range(N_GROUPS)]
        scores = [tp[0] + tp[2] for tp in tops]
        best = functools.reduce(jnp.maximum, scores)
        g_idx = jnp.full(best.shape, N_GROUPS - 1, jnp.int32)
        for g in range(N_GROUPS - 2, -1, -1):
            g_idx = jnp.where(scores[g] == best, g, g_idx)
        e1 = g_idx * EPG + _select_rows(g_idx, [tp[1] for tp in tops])
        e2 = g_idx * EPG + _select_rows(g_idx, [tp[3] for tp in tops])
        w1 = _select_rows(e1, s_rows)
        w2 = _select_rows(e2, s_rows)
        tot = w1 + w2
        w1, w2 = w1 / tot, w2 / tot
        masks = [jnp.where((e1 == j) | (e2 == j), 1.0, 0.0) for j in range(N_EXPERTS)]
        comb_ref[...] = jnp.concatenate(
            [unpack(jnp.where(e1 == j, w1, 0.0) + jnp.where(e2 == j, w2, 0.0)) for j in range(N_EXPERTS)], axis=0)
        mask = jnp.concatenate([unpack(m) for m in masks], axis=0)
        rank = jnp.dot(mask.astype(BF16), upper_ref[...], preferred_element_type=F32)
        rank_ref[...] = jnp.where(mask > 0.0, rank, -1.0)
        for j in range(N_EXPERTS):
            cnt_ref[j] = jnp.sum(masks[j]).astype(jnp.int32)
        acc_ref[...] = jnp.zeros_like(acc_ref)

    def expert_tile(j, tile_idx):
        e = step * MOE_EXPERTS_PER_STEP + j
        rank = rank_ref[pl.ds(e, 1), :]
        comb = comb_ref[pl.ds(e, 1), :]
        slot = (lax.broadcasted_iota(jnp.int32, (MOE_TILE, tb), 0) + tile_idx * MOE_TILE).astype(F32)
        hit = rank == slot
        onehot = jnp.where(hit, 1.0, 0.0).astype(BF16)
        xe = jnp.dot(onehot, hb_ref[...], preferred_element_type=F32).astype(BF16)
        gate = jnp.dot(xe, wg_ref[j], preferred_element_type=F32)
        up = jnp.dot(xe, wu_ref[j], preferred_element_type=F32)
        y = jnp.dot((_silu(gate) * up).astype(BF16), wd_ref[j], preferred_element_type=F32)
        w_slot = jnp.sum(jnp.where(hit, comb, 0.0), axis=1, keepdims=True)
        return onehot, (y * w_slot).astype(BF16)

    experts = range(MOE_EXPERTS_PER_STEP)
    slot0 = lax.broadcasted_iota(jnp.int32, (MOE_TILE, tb), 0).astype(F32)
    hits = [rank_ref[pl.ds(step * MOE_EXPERTS_PER_STEP + j, 1), :] == slot0 for j in experts]
    for j in experts:
        sel_ref[j * MOE_TILE:(j + 1) * MOE_TILE, :] = jnp.where(hits[j], 1.0, 0.0).astype(BF16)
    xes = [jnp.dot(sel_ref[j * MOE_TILE:(j + 1) * MOE_TILE, :], hb_ref[...],
                   preferred_element_type=F32).astype(BF16) for j in experts]
    hidden = [(jnp.dot(xes[j], wg_ref[j], preferred_element_type=F32),
               jnp.dot(xes[j], wu_ref[j], preferred_element_type=F32)) for j in experts]
    acts = [(_silu(gate) * up).astype(BF16) for gate, up in hidden]
    ys = [jnp.dot(acts[j], wd_ref[j], preferred_element_type=F32) for j in experts]
    for j in experts:
        comb = comb_ref[pl.ds(step * MOE_EXPERTS_PER_STEP + j, 1), :]
        w_slot = jnp.sum(jnp.where(hits[j], comb, 0.0), axis=1, keepdims=True)
        y_ref[j * MOE_TILE:(j + 1) * MOE_TILE, :] = (ys[j] * w_slot).astype(BF16)

    acc_ref[...] += lax.dot_general(sel_ref[...], y_ref[...], TN, preferred_element_type=F32)

    for j in experts:
        def overflow(tile_idx, carry, j=j):
            onehot, yw = expert_tile(j, tile_idx)
            acc_ref[...] += lax.dot_general(onehot, yw, TN, preferred_element_type=F32)
            return carry

        n_tiles = (cnt_ref[step * MOE_EXPERTS_PER_STEP + j] + MOE_TILE - 1) // MOE_TILE
        lax.fori_loop(1, n_tiles, overflow, 0)

    @pl.when(step == pl.num_programs(1) - 1)
    def _finish():
        out_ref[...] = x_ref[...] + g_ref[...] * acc_ref[...]


def _moe(x, sc, sh, g, nw, router_w, router_b, w_gate, w_up, w_down, layer, seq, row_fn, shared_mod):
    t, d = x.shape
    tb = min(t if shared_mod else seq, MOE_BLOCK)
    per_seq = t if shared_mod else seq // tb
    dff = w_gate.shape[-1]
    upper = jnp.asarray(np.triu(np.ones((tb, tb), np.float32), 1), BF16)
    mod_spec = pl.BlockSpec((None, 1, d), lambda i, e: (row_fn(i // per_seq), 0, 0))
    tok = pl.BlockSpec((tb, d), lambda i, e: (i, 0))
    per = MOE_EXPERTS_PER_STEP
    stacked = per * MOE_TILE
    rw_hi = router_w.T.astype(BF16)
    rwt = jnp.stack([rw_hi, (router_w.T - rw_hi.astype(F32)).astype(BF16)])
    return pl.pallas_call(
        _moe_kernel,
        grid=(t // tb, N_EXPERTS // per),
        in_specs=[tok, mod_spec, mod_spec, mod_spec, _const_spec((1, d)), _const_spec((2, N_EXPERTS, d)),
                  _const_spec((N_EXPERTS, 1)), _const_spec(upper.shape),
                  pl.BlockSpec((None, per, d, dff), lambda i, e: (layer, e, 0, 0)),
                  pl.BlockSpec((None, per, d, dff), lambda i, e: (layer, e, 0, 0)),
                  pl.BlockSpec((None, per, dff, d), lambda i, e: (layer, e, 0, 0))],
        out_specs=tok,
        out_shape=jax.ShapeDtypeStruct((t, d), F32),
        scratch_shapes=[pltpu.VMEM((tb, d), BF16), pltpu.VMEM((N_EXPERTS, tb), F32), pltpu.VMEM((N_EXPERTS, tb), F32),
                        pltpu.VMEM((tb, d), F32), pltpu.VMEM((stacked, tb), BF16), pltpu.VMEM((stacked, d), BF16),
                        pltpu.SMEM((N_EXPERTS,), jnp.int32)],
        compiler_params=_params(("parallel", "arbitrary")),
        name="moe",
    )(x, sc, sh, g, nw, rwt, router_b.reshape(N_EXPERTS, 1), upper, w_gate, w_up, w_down)


def _proj1_kernel(x_ref, sc_ref, sh_ref, nw_ref, w_ref, wlr_ref, p_ref, lr_ref):
    h = _norm_mod(x_ref[...], nw_ref[...], sc_ref[...], sh_ref[...]).astype(BF16)
    p_ref[...] = jnp.dot(h, w_ref[...], preferred_element_type=F32).astype(p_ref.dtype)
    lr_ref[...] = jnp.dot(h, wlr_ref[...], preferred_element_type=F32)


def _proj1(x, sc, sh, nw, w_main, w_lr, seq, row_fn):
    t, d = x.shape
    tm = min(seq, TOKEN_TILE)
    per_seq = seq // tm
    mod_spec = pl.BlockSpec((None, 1, d), lambda i: (row_fn(i // per_seq), 0, 0))
    tok = lambda n: pl.BlockSpec((tm, n), lambda i: (i, 0))
    return pl.pallas_call(
        _proj1_kernel,
        grid=(t // tm,),
        in_specs=[tok(d), mod_spec, mod_spec, _const_spec((1, d)), _const_spec(w_main.shape), _const_spec(w_lr.shape)],
        out_specs=[tok(w_main.shape[1]), tok(LANES)],
        out_shape=[jax.ShapeDtypeStruct((t, w_main.shape[1]), BF16), jax.ShapeDtypeStruct((t, LANES), F32)],
        compiler_params=_params(("parallel",)),
        name="proj1",
    )(x, sc, sh, nw, w_main, w_lr)


def _log_sigmoid(x):
    return jnp.minimum(x, 0.0) - jnp.log(1.0 + jnp.exp(-jnp.abs(x)))


def _gla_kernel(qf_ref, kf_ref, vf_ref, lf_ref, qb_ref, kb_ref, vb_ref, lb_ref, wgf_ref, wgb_ref, bg_ref,
                trif_ref, trib_ref, s0f_ref, s0b_ref, of_ref, ob_ref, sf_ref, sb_ref, stf_ref, stb_ref, *, dk, dv):
    i = pl.program_id(1)
    nb = pl.num_programs(1)
    tc = qf_ref.shape[0]

    @pl.when(i == 0)
    def _init():
        stf_ref[...] = s0f_ref[...].astype(F32)
        stb_ref[...] = s0b_ref[...].astype(F32)

    r = lax.broadcasted_iota(jnp.int32, (GLA_CHUNK, GLA_CHUNK), 0)
    c = lax.broadcasted_iota(jnp.int32, (GLA_CHUNK, GLA_CHUNK), 1)
    scale = dk ** -0.5

    def decay_sums(l_ref, wg_ref, bias, tri_ref):
        gate = _log_sigmoid(jnp.dot(l_ref[...], wg_ref[...], precision=HI, preferred_element_type=F32)
                            + bias) / GLA_GATE_NORM
        total = None
        rest = gate
        for _ in range(3):
            part = rest.astype(BF16)
            rest = rest - part.astype(F32)
            term = jnp.dot(tri_ref[...], part, preferred_element_type=F32)
            total = term if total is None else total + term
        return total

    b_f = decay_sums(lf_ref, wgf_ref, bg_ref[0:1, :], trif_ref)
    b_b = decay_sums(lb_ref, wgb_ref, bg_ref[1:2, :], trib_ref)
    n_chunks = tc // GLA_CHUNK
    chains = []
    for h in range(GLA_HEADS):
        chains.append((qf_ref, kf_ref, vf_ref, of_ref, b_f, stf_ref, h, False))
        chains.append((qb_ref, kb_ref, vb_ref, ob_ref, b_b, stb_ref, h, True))
    states = [st_ref[h] for (_, _, _, _, _, st_ref, h, _) in chains]
    for step in range(n_chunks):
        prepared = []
        for (q_ref, k_ref, v_ref, _, b_all, _, h, reverse) in chains:
            ci = n_chunks - 1 - step if reverse else step
            rs = slice(ci * GLA_CHUNK, (ci + 1) * GLA_CHUNK)
            ks, vs = slice(h * dk, (h + 1) * dk), slice(h * dv, (h + 1) * dv)
            b = b_all[rs, ks]
            b_end = b[0:1, :] if reverse else b[GLA_CHUNK - 1:GLA_CHUNK, :]
            q_dec = (q_ref[rs, ks].astype(F32) * scale * jnp.exp(b)).astype(BF16)
            k_dec = k_ref[rs, ks].astype(F32) * jnp.exp(-b)
            e_end = jnp.exp(b_end)
            prepared.append((rs, vs, q_dec, k_dec.astype(BF16), (k_dec * e_end).astype(BF16), e_end, v_ref[rs, vs]))
        first = []
        for n, (rs, vs, q_dec, k_bf, k_rem, e_end, v) in enumerate(prepared):
            att = lax.dot_general(q_dec, k_bf, NT, preferred_element_type=F32)
            carry = lax.dot_general(q_dec, states[n].astype(BF16), NT, preferred_element_type=F32)
            upd = lax.dot_general(v, k_rem, TN, preferred_element_type=F32)
            first.append((att, carry, upd))
        for n, (rs, vs, q_dec, k_bf, k_rem, e_end, v) in enumerate(prepared):
            att, carry, upd = first[n]
            reverse = chains[n][7]
            att = jnp.where((c >= r) if reverse else (c <= r), att, 0.0).astype(BF16)
            o = jnp.dot(att, v, preferred_element_type=F32) + carry
            chains[n][3][rs, vs] = o.astype(chains[n][3].dtype)
            states[n] = states[n] * e_end + upd
    for n, (_, _, _, _, _, st_ref, h, _) in enumerate(chains):
        st_ref[h] = states[n]

    @pl.when(i == nb - 1)
    def _emit():
        sf_ref[...] = stf_ref[...]
        sb_ref[...] = stb_ref[...]


def _gla(p, lr, wgf, wgb, bg, s0f, s0b, batch, seq, dk, dv):
    tc = min(seq, GLA_ROWS_PER_STEP)
    nb = seq // tc
    hk, hv = GLA_HEADS * dk, GLA_HEADS * dv
    fwd = lambda width, col: pl.BlockSpec((tc, width), lambda b, i: (b * nb + i, col))
    bwd = lambda width, col: pl.BlockSpec((tc, width), lambda b, i: (b * nb + nb - 1 - i, col))
    st_spec = pl.BlockSpec((None, GLA_HEADS, dv, dk), lambda b, i: (b, 0, 0, 0))
    v_col = 2 * hk // hv
    idx = np.arange(tc)
    same = (idx[:, None] // GLA_CHUNK) == (idx[None, :] // GLA_CHUNK)
    tri_f = jnp.asarray(same & (idx[None, :] <= idx[:, None]), BF16)
    tri_b = jnp.asarray(same & (idx[None, :] >= idx[:, None]), BF16)
    return pl.pallas_call(
        functools.partial(_gla_kernel, dk=dk, dv=dv),
        grid=(batch, nb),
        in_specs=[fwd(hk, 0), fwd(hk, 1), fwd(hv, v_col), fwd(LANES, 0),
                  bwd(hk, 0), bwd(hk, 1), bwd(hv, v_col), bwd(LANES, 0),
                  _const_spec(wgf.shape), _const_spec(wgb.shape), _const_spec(bg.shape),
                  _const_spec(tri_f.shape), _const_spec(tri_b.shape), st_spec, st_spec],
        out_specs=[fwd(hv, 0), bwd(hv, 0), st_spec, st_spec],
        out_shape=[jax.ShapeDtypeStruct((batch * seq, hv), BF16)] * 2
        + [jax.ShapeDtypeStruct((batch, GLA_HEADS, dv, dk), F32)] * 2,
        scratch_shapes=[pltpu.VMEM((GLA_HEADS, dv, dk), F32)] * 2,
        compiler_params=_params(("parallel", "arbitrary")),
        name="gla_scan",
    )(p, p, p, lr, p, p, p, lr, wgf, wgb, bg, tri_f, tri_b, s0f, s0b)


def _outproj1_kernel(x_ref, of_ref, ob_ref, go_ref, g_ref, nw_ref, w_ref, out_ref, *, dv):
    o = of_ref[...].astype(F32) + ob_ref[...].astype(F32)
    parts = []
    for h in range(GLA_HEADS):
        oh = o[:, h * dv:(h + 1) * dv]
        ms = jnp.mean(oh * oh, axis=1, keepdims=True)
        parts.append(oh * lax.rsqrt(ms + EPS))
    o = jnp.concatenate(parts, axis=1) * nw_ref[...] * _silu(go_ref[...].astype(F32))
    out_ref[...] = x_ref[...] + g_ref[...] * jnp.dot(o.astype(BF16), w_ref[...], preferred_element_type=F32)


def _outproj1(x, o_f, o_b, p, g, out_norm_w, w_out, seq, row_fn, dv):
    t, d = x.shape
    hv = GLA_HEADS * dv
    tm = min(seq, TOKEN_TILE)
    per_seq = seq // tm
    tok = lambda n, col=0: pl.BlockSpec((tm, n), lambda i: (i, col))
    go_col = (p.shape[1] - hv) // hv
    nw = jnp.tile(out_norm_w, GLA_HEADS).reshape(1, hv)
    wb = w_out.astype(BF16)
    return pl.pallas_call(
        functools.partial(_outproj1_kernel, dv=dv),
        grid=(t // tm,),
        in_specs=[tok(d), tok(hv), tok(hv), tok(hv, go_col),
                  pl.BlockSpec((None, 1, d), lambda i: (row_fn(i // per_seq), 0, 0)),
                  _const_spec((1, hv)), _const_spec(wb.shape)],
        out_specs=tok(d),
        out_shape=jax.ShapeDtypeStruct((t, d), F32),
        compiler_params=_params(("parallel",)),
        name="outproj1",
    )(x, o_f, o_b, p, g, nw, wb)


def _mod_rows(mod_l, d):
    return [mod_l[:, j * d:(j + 1) * d].reshape(mod_l.shape[0], 1, d) for j in range(6)]


def _trunk(x, batch, seq, mods, row_fn, shared_mod, norm_w, ab, cp, moe_w, cache, s0, emit_cache):
    d = x.shape[-1]
    t = batch * seq
    x = x.reshape(t, d)
    (w_in0, conv_w, conv_b, filt, hy_skip, qn_w, kn_w, lam_p, subln_w, w_out0) = ab
    (w_in1, gate_up_w, gate_up_b, out_norm_w, w_out1) = cp
    router_w, router_b, wg, wu, wd = moe_w
    width = d // 2
    hy_in = 3 * width

    sh1, sc1, g1, sh2, sc2, g2 = mods[0]
    nseg = DA_HEADS * 2
    qkw = jnp.stack([jnp.tile(qn_w, nseg), jnp.tile(kn_w, nseg)])
    seg = np.arange(DA_HEADS * DA_VD) // DA_DH
    ones = jnp.asarray(seg[:, None] == seg[None, :], BF16)
    v_cols = w_in0[:, hy_in + 2 * DA_HEADS * DA_VD:]
    w_chan = jnp.concatenate([w_in0[:, :hy_in], v_cols], axis=1).T.astype(BF16)
    outs = _proj0(x, sc1, sh1, norm_w[0, 0].reshape(1, d), w_in0[:, hy_in:].astype(BF16), w_chan, qkw, ones,
                  batch, seq, row_fn, rope=cache is not None, emit_cache=emit_cache)
    chan, q, k = outs[:3]
    y_hy = _hyena(chan, conv_w, conv_b, filt, hy_skip, seq, width)
    lam_init = 0.8 - 0.6 * math.exp(-0.3 * 0)
    o = _diff_attention(q, k, chan, hy_in, cache, lam_p, subln_w, batch, seq, lam_init)
    x = _outproj0(x, y_hy, o, g1, w_out0, seq, row_fn)
    x = _moe(x, sc2, sh2, g2, norm_w[0, 1].reshape(1, d), router_w, router_b, wg, wu, wd, 0, seq, row_fn,
             shared_mod)

    sh1, sc1, g1, sh2, sc2, g2 = mods[1]
    dk = d // 2 // GLA_HEADS
    dv = d // GLA_HEADS
    n_main = GLA_HEADS * (2 * dk + 2 * dv)
    w_lr = jnp.pad(w_in1[:, n_main:], ((0, 0), (0, LANES - 2 * GLA_RANK))).astype(BF16)
    p, lr = _proj1(x, sc1, sh1, norm_w[1, 0].reshape(1, d), w_in1[:, :n_main].astype(BF16), w_lr, seq, row_fn)
    wgf = jnp.pad(gate_up_w[0], ((0, LANES - GLA_RANK), (0, 0)))
    wgb = jnp.pad(gate_up_w[1], ((GLA_RANK, LANES - 2 * GLA_RANK), (0, 0)))
    o_f, o_b, s_f, s_b = _gla(p, lr, wgf, wgb, gate_up_b, jnp.swapaxes(s0[0], 2, 3), jnp.swapaxes(s0[1], 2, 3),
                              batch, seq, dk, dv)
    s_f, s_b = jnp.swapaxes(s_f, 2, 3), jnp.swapaxes(s_b, 2, 3)
    x = _outproj1(x, o_f, o_b, p, g1, out_norm_w, w_out1, seq, row_fn, dv)
    x = _moe(x, sc2, sh2, g2, norm_w[1, 1].reshape(1, d), router_w, router_b, wg, wu, wd, 1, seq, row_fn,
             shared_mod)
    return x.reshape(batch, seq, d), outs[3:], (s_f, s_b)


def kernel(x_prompt, x_sample, cache_l0_k, cache_l0_v, state_l1_fwd, state_l1_bwd, c, c_ctx, w_mod, b_mod, norm_w,
           l0_w_in, l0_conv_w, l0_conv_b, l0_filt_w1, l0_filt_b1, l0_filt_freq, l0_filt_w2, l0_filt_b2, l0_filt_w3,
           l0_hy_skip, l0_qn_w, l0_kn_w, l0_lambda, l0_subln_w, l0_w_out, l1_w_in, l1_gate_up_w, l1_gate_up_b,
           l1_out_norm_w, l1_w_out, moe_router_w, moe_router_b, moe_w_gate, moe_w_up, moe_w_down):
    d = x_prompt.shape[-1]
    n_lat = c.shape[0]
    ctx_row = n_lat
    rows = 2 * SUBLANES
    cond = jnp.zeros((rows, d), F32).at[:n_lat].set(c).at[ctx_row].set(c_ctx)
    mod = _modulation(cond, w_mod, b_mod)
    mods = [_mod_rows(mod[l], d) for l in range(mod.shape[0])]
    filt = (l0_filt_w1, l0_filt_b1, l0_filt_freq, l0_filt_w2, l0_filt_b2, l0_filt_w3)
    ab = (l0_w_in, l0_conv_w, l0_conv_b, filt, l0_hy_skip, l0_qn_w, l0_kn_w, l0_lambda, l0_subln_w, l0_w_out)
    cp = (l1_w_in, l1_gate_up_w, l1_gate_up_b, l1_out_norm_w, l1_w_out)
    moe_w = (moe_router_w, moe_router_b, moe_w_gate.astype(BF16), moe_w_up.astype(BF16), moe_w_down.astype(BF16))

    b_ctx, l_ctx = x_prompt.shape[:2]
    zero_state = jnp.zeros((b_ctx,) + state_l1_fwd.shape[1:], F32)
    y_prompt, (new_k, new_v), (new_sf, new_sb) = _trunk(
        x_prompt, b_ctx, l_ctx, mods, lambda b: ctx_row, True, norm_w, ab, cp, moe_w, None,
        (zero_state, zero_state), True)
    b_lat, l_lat = x_sample.shape[:2]
    y_sample, _, _ = _trunk(
        x_sample, b_lat, l_lat, mods, lambda b: b, False, norm_w, ab, cp, moe_w, (cache_l0_k, cache_l0_v),
        (state_l1_fwd, state_l1_bwd), False)
    return (y_prompt, y_sample, new_k, new_v, new_sf, new_sb)
```

```python
import cmath
import functools
import math

import numpy as np
import jax
import jax.numpy as jnp
from jax import lax
from jax.experimental import pallas as pl
from jax.experimental.pallas import tpu as pltpu

F32 = jnp.float32
BF16 = jnp.bfloat16
HI = lax.Precision.HIGHEST
EPS = 1e-6

LANES = 128
SUBLANES = 8
VMEM_LIMIT = 56 << 20

DA_HEADS = 4
DA_DH = 64
DA_VD = 2 * DA_DH
ROPE_NF = DA_DH // 4
ROPE_BASE = 10000.0
GRID_W = 64
HY_BANDS = 16
HY_TARGET = 1e-2
HY_MAX_DECAY = math.log(HY_TARGET) / 0.3
HY_MIN_DECAY = math.log(HY_TARGET) / 1.5
GLA_HEADS = 4
GLA_RANK = 16
GLA_GATE_NORM = 16.0
GLA_CHUNK = 64
N_EXPERTS = 16
N_GROUPS = 4
EPG = N_EXPERTS // N_GROUPS
TOKEN_TILE = 1024
MOD_COL_TILE = 1536
FFT_BLOCK = 256
HY_TILE_ELEMS = 128 * 4096
ATTN_Q_TILE = 512
ATTN_KEY_CHUNK = 1024
ATTN_BOUND_PAD = 1.02
ATTN_SUM_FLOOR = 2.0 ** -60
GLA_ROWS_PER_STEP = 256
MOE_BLOCK = 1024
MOE_TILE = 160
MOE_EXPERTS_PER_STEP = 4
NT = (((1,), (1,)), ((), ()))
TN = (((0,), (0,)), ((), ()))


def _params(sem):
    return pltpu.CompilerParams(dimension_semantics=sem, vmem_limit_bytes=VMEM_LIMIT)


def _const_spec(shape):
    nd = len(shape)
    return pl.BlockSpec(shape, lambda *_: (0,) * nd)


def _silu(x):
    return x * (1.0 / (1.0 + jnp.exp(-x)))


def _norm_mod(x, nw, sc, sh):
    ms = jnp.mean(x * x, axis=-1, keepdims=True)
    return x * lax.rsqrt(ms + EPS) * nw * (1.0 + sc) + sh


def _mod_kernel(c_ref, w_ref, b_ref, o_ref):
    s = _silu(c_ref[...])
    o_ref[...] = jnp.dot(s, w_ref[...], precision=HI, preferred_element_type=F32) + b_ref[...]


def _modulation(cond, w_mod, b_mod):
    depth, d, d6 = w_mod.shape
    r = cond.shape[0]
    tn = MOD_COL_TILE
    return pl.pallas_call(
        _mod_kernel,
        grid=(depth, d6 // tn),
        in_specs=[
            pl.BlockSpec((r, d), lambda l, j: (0, 0)),
            pl.BlockSpec((None, d, tn), lambda l, j: (l, 0, j)),
            pl.BlockSpec((None, 1, tn), lambda l, j: (l, 0, j)),
        ],
        out_specs=pl.BlockSpec((None, r, tn), lambda l, j: (l, 0, j)),
        out_shape=jax.ShapeDtypeStruct((depth, r, d6), F32),
        compiler_params=_params(("parallel", "parallel")),
        name="modulation",
    )(cond, w_mod, b_mod.reshape(depth, 1, d6))


def _proj0_kernel(x_ref, sc_ref, sh_ref, nw_ref, wtok_ref, wchan_ref, qkw_ref, ones_ref, cos_ref, sin_ref,
                  chan_ref, q_ref, k_ref, *cache_refs, rope):
    h = _norm_mod(x_ref[...], nw_ref[...], sc_ref[...], sh_ref[...]).astype(BF16)
    chan_ref[...] = lax.dot_general(wchan_ref[...], h, NT, preferred_element_type=F32).astype(chan_ref.dtype)
    p = jnp.dot(h, wtok_ref[...], preferred_element_type=F32)
    w = DA_HEADS * DA_VD
    ones = ones_ref[...]

    def segnorm(z, gain):
        ss = jnp.dot((z * z).astype(BF16), ones, preferred_element_type=F32)
        return z * lax.rsqrt(ss * (1.0 / DA_DH) + EPS) * gain

    q = segnorm(p[:, :w], qkw_ref[0:1, :])
    k = segnorm(p[:, w:2 * w], qkw_ref[1:2, :])
    if cache_refs:
        kn_ref, vn_ref = cache_refs
        v = p[:, 2 * w:]
        for hh in range(DA_HEADS):
            kn_ref[hh] = k[:, hh * DA_VD:(hh + 1) * DA_VD]
            vn_ref[hh] = v[:, hh * DA_VD:(hh + 1) * DA_VD]
    if rope:
        cos = jnp.concatenate([cos_ref[...]] * (w // LANES), axis=1)
        sin = jnp.concatenate([sin_ref[...]] * (w // LANES), axis=1)
        lane = lax.broadcasted_iota(jnp.int32, q.shape, 1)
        first = (lane % (2 * ROPE_NF)) < ROPE_NF

        def rot(z):
            partner = jnp.where(first, pltpu.roll(z, w - ROPE_NF, 1), pltpu.roll(z, ROPE_NF, 1))
            return z * cos + partner * sin

        q = rot(q)
        k = rot(k)
    q_ref[...] = (q * (DA_DH ** -0.5 * math.log2(math.e))).astype(q_ref.dtype)
    k_ref[...] = k.astype(k_ref.dtype)


def _rope_tables(seq):
    t = np.arange(seq)
    lane = np.arange(LANES)
    d = lane % DA_DH
    axis = d // (2 * ROPE_NF)
    part = (d % (2 * ROPE_NF)) // ROPE_NF
    f = d % ROPE_NF
    pos = jnp.where(axis[None, :] == 0, (t // GRID_W)[:, None], (t % GRID_W)[:, None]).astype(F32)
    inv = ROPE_BASE ** (-jnp.arange(ROPE_NF, dtype=F32) / ROPE_NF)
    ang = pos * inv[f][None, :]
    sign = jnp.asarray(np.where(part == 0, -1.0, 1.0), F32)[None, :]
    return jnp.cos(ang), jnp.sin(ang) * sign


def _proj0(x, sc, sh, nw, wqkv, whyt, qkw, ones, batch, seq, row_fn, rope, emit_cache):
    t, d = x.shape
    tm = min(seq, TOKEN_TILE)
    per_seq = seq // tm
    w = DA_HEADS * DA_VD
    hy_in = whyt.shape[0]
    if not emit_cache:
        wqkv = wqkv[:, :2 * w]
    if rope:
        cos, sin = _rope_tables(seq)
    else:
        cos = jnp.ones((seq, LANES), F32)
        sin = jnp.zeros((seq, LANES), F32)
    mod_spec = pl.BlockSpec((None, 1, d), lambda i: (row_fn(i // per_seq), 0, 0))
    tab_spec = pl.BlockSpec((tm, LANES), lambda i: (i % per_seq, 0))
    tok = lambda n: pl.BlockSpec((tm, n), lambda i: (i, 0))
    out_specs = [pl.BlockSpec((None, hy_in, tm), lambda i: (i // per_seq, 0, i % per_seq)), tok(w), tok(w)]
    out_shape = [jax.ShapeDtypeStruct((batch, hy_in, seq), BF16)] + [jax.ShapeDtypeStruct((t, w), BF16)] * 2
    if emit_cache:
        cspec = pl.BlockSpec((None, DA_HEADS, tm, DA_VD), lambda i: (i // per_seq, 0, i % per_seq, 0))
        out_specs += [cspec, cspec]
        out_shape += [jax.ShapeDtypeStruct((batch, DA_HEADS, seq, DA_VD), F32)] * 2
    return pl.pallas_call(
        functools.partial(_proj0_kernel, rope=rope),
        grid=(t // tm,),
        in_specs=[tok(d), mod_spec, mod_spec, _const_spec((1, d)), _const_spec(wqkv.shape), _const_spec(whyt.shape),
                  _const_spec(qkw.shape), _const_spec(ones.shape), tab_spec, tab_spec],
        out_specs=out_specs,
        out_shape=out_shape,
        compiler_params=_params(("parallel",)),
        name="proj0",
    )(x, sc, sh, nw, wqkv, whyt, qkw, ones, cos, sin)


def _c_add(a, b, sign=1.0):
    if b is None:
        return a
    if a is None:
        return b if sign > 0 else tuple(None if p is None else -p for p in b)
    out = []
    for pa, pb in zip(a, b):
        if pb is None:
            out.append(pa)
        elif pa is None:
            out.append(pb if sign > 0 else -pb)
        else:
            out.append(pa + pb if sign > 0 else pa - pb)
    return tuple(out)


def _c_mul_const(w, a):
    if a is None:
        return None
    wr = 0.0 if abs(w.real) < 1e-12 else w.real
    wi = 0.0 if abs(w.imag) < 1e-12 else w.imag
    ar, ai = a

    def scaled(c, p):
        if p is None or c == 0.0:
            return None
        if c == 1.0:
            return p
        if c == -1.0:
            return -p
        return c * p

    re = _c_add((scaled(wr, ar),), (scaled(wi, ai),), -1.0)[0]
    im = _c_add((scaled(wr, ai),), (scaled(wi, ar),), 1.0)[0]
    return (re, im)


def _fft_list(vals, sign, first_half_only=False):
    n = len(vals)
    if n == 1:
        return list(vals)
    ev = _fft_list(vals[0::2], sign)
    od = _fft_list(vals[1::2], sign)
    out = [None] * n
    for k in range(n // 2):
        tw = _c_mul_const(cmath.exp(sign * 2j * math.pi * k / n), od[k])
        out[k] = _c_add(ev[k], tw, 1.0)
        if not first_half_only:
            out[k + n // 2] = _c_add(ev[k], tw, -1.0)
    return out


def _dft_mats(n2, total):
    k = np.arange(n2)
    ang = 2.0 * np.pi * np.outer(k, k) / n2
    fr, fi = np.cos(ang), -np.sin(ang)
    fwd = np.block([[fr, fi], [-fi, fr]])
    inv = np.block([[fr, -fi], [fi, fr]]) / total
    return jnp.asarray(fwd, BF16), jnp.asarray(inv, BF16)


def _twiddles(n1, n2):
    ang = 2.0 * np.pi * np.outer(np.arange(n1), np.arange(n2)) / (n1 * n2)
    return jnp.asarray(np.cos(ang), F32), jnp.asarray(-np.sin(ang), F32)


def _fft_split(seq):
    n = 2 * seq
    n2 = min(FFT_BLOCK, n // 2)
    return n // n2, n2


def _across_fwd(load, n_in, z_ref, twr_ref, twi_ref, n1, n2, rows):
    per_row = n2 // LANES

    def body(it, carry):
        r0 = pl.multiple_of((it // per_row) * SUBLANES, SUBLANES)
        l0 = pl.multiple_of((it % per_row) * LANES, LANES)
        vals = [load(j, r0, l0) for j in range(n_in)] + [None] * (n1 - n_in)
        outs = _fft_list(vals, -1.0)
        for k1 in range(n1):
            twr = twr_ref[pl.ds(k1, 1), pl.ds(l0, LANES)]
            twi = twi_ref[pl.ds(k1, 1), pl.ds(l0, LANES)]
            re, im = outs[k1]
            zero = jnp.zeros((SUBLANES, LANES), F32)
            re = zero if re is None else re
            im = zero if im is None else im
            z_ref[k1, pl.ds(r0, SUBLANES), pl.ds(l0, LANES)] = re * twr - im * twi
            z_ref[k1, pl.ds(r0, SUBLANES), pl.ds(n2 + l0, LANES)] = re * twi + im * twr
        return carry

    lax.fori_loop(0, (rows // SUBLANES) * per_row, body, 0)


def _hy_hidden_kernel(w1_ref, b1_ref, freq_ref, w2_ref, b2_ref, bands_ref, o_ref, *, seq):
    n = 2 * seq
    k = lax.broadcasted_iota(jnp.int32, (1, n), 1)
    pos = jnp.where(k < seq, k, n - k).astype(F32)
    t = pos / (seq - 1)
    ang = 2.0 * math.pi * pos / seq * bands_ref[...]
    z = (w1_ref[:, 0:1] * t
         + jnp.dot(w1_ref[:, 1:1 + HY_BANDS], jnp.cos(ang), precision=HI, preferred_element_type=F32)
         + jnp.dot(w1_ref[:, 1 + HY_BANDS:], -jnp.sin(ang), precision=HI, preferred_element_type=F32)
         + b1_ref[...])
    hid = jnp.sin(freq_ref[:, 0:1] * z)
    hid = jnp.sin(freq_ref[:, 1:2] * (jnp.dot(w2_ref[...], hid, precision=HI, preferred_element_type=F32) + b2_ref[...]))
    o_ref[...] = hid


def _hy_hidden(seq, f_w1, f_b1, f_freq, f_w2, f_b2):
    ffn = f_w1.shape[1]
    bands = jnp.linspace(1e-4, HY_BANDS - 1, HY_BANDS, dtype=F32).reshape(HY_BANDS, 1)
    args = (f_w1.T, f_b1.reshape(ffn, 1), f_freq.T, f_w2.T, f_b2.reshape(ffn, 1), bands)
    return pl.pallas_call(
        functools.partial(_hy_hidden_kernel, seq=seq),
        grid=(1,),
        in_specs=[_const_spec(a.shape) for a in args],
        out_specs=_const_spec((ffn, 2 * seq)),
        out_shape=jax.ShapeDtypeStruct((ffn, 2 * seq), F32),
        compiler_params=_params(("arbitrary",)),
        name="hyena_filter_hidden",
    )(*args)


def _hy_spec_kernel(hid_ref, wb_ref, wa_ref, dec_ref, twr_ref, twi_ref, gf_ref, o_ref, taps_ref, z_ref, *, seq, n1, n2):
    ct = wb_ref.shape[0]
    hid = hid_ref[...]
    back = jnp.dot(wb_ref[...], hid[:, :seq], precision=HI, preferred_element_type=F32)
    ahead = jnp.dot(wa_ref[...], hid[:, seq:], precision=HI, preferred_element_type=F32)
    lane = lax.broadcasted_iota(jnp.int32, (1, seq), 1)
    dec = dec_ref[...]
    t_back = lane.astype(F32) / (seq - 1)
    t_ahead = (seq - lane).astype(F32) / (seq - 1)
    back = back * jnp.exp(-t_back * dec)
    ahead = jnp.where(lane == 0, 0.0, ahead * jnp.exp(-t_ahead * dec))
    norm = (jnp.sum(jnp.abs(back), axis=1, keepdims=True) + jnp.sum(jnp.abs(ahead), axis=1, keepdims=True)) + EPS
    taps_ref[:, :seq] = back / norm
    taps_ref[:, seq:] = ahead / norm

    def load(j, r0, l0):
        return (taps_ref[pl.ds(r0, SUBLANES), pl.ds(j * n2 + l0, LANES)], None)

    _across_fwd(load, n1, z_ref, twr_ref, twi_ref, n1, n2, ct)
    for k1 in range(n1):
        o_ref[k1] = jnp.dot(z_ref[k1].astype(BF16), gf_ref[...], preferred_element_type=F32)


def _hy_filter_spectra(hid, f_w3, seq, width, ct):
    n1, n2 = _fft_split(seq)
    ffn = f_w3.shape[0]
    w3t = f_w3.T.reshape(4, width, ffn)
    decay = jnp.abs(jnp.linspace(HY_MIN_DECAY, HY_MAX_DECAY, width, dtype=F32)).reshape(width, 1)
    twr, twi = _twiddles(n1, n2)
    gf, _ = _dft_mats(n2, 2 * seq)
    return pl.pallas_call(
        functools.partial(_hy_spec_kernel, seq=seq, n1=n1, n2=n2),
        grid=(2, width // ct),
        in_specs=[
            _const_spec(hid.shape),
            pl.BlockSpec((None, ct, ffn), lambda i, c: (2 * i, c, 0)),
            pl.BlockSpec((None, ct, ffn), lambda i, c: (2 * i + 1, c, 0)),
            pl.BlockSpec((ct, 1), lambda i, c: (c, 0)),
            _const_spec(twr.shape), _const_spec(twi.shape), _const_spec(gf.shape),
        ],
        out_specs=pl.BlockSpec((None, n1, ct, 2 * n2), lambda i, c: (i, 0, c, 0)),
        out_shape=jax.ShapeDtypeStruct((2, n1, width, 2 * n2), F32),
        scratch_shapes=[pltpu.VMEM((ct, 2 * seq), F32), pltpu.VMEM((n1, ct, 2 * n2), F32)],
        compiler_params=_params(("parallel", "parallel")),
        name="hyena_filter_spectra",
    )(hid, w3t, w3t, decay, twr, twi, gf)


def _short_conv_rows(ref, cw, r0, nrows, seq):
    x = ref[pl.ds(r0, nrows), :].astype(F32)
    lane = lax.broadcasted_iota(jnp.int32, (nrows, LANES), 1)
    left = pltpu.roll(x, 1, 1)
    right = pltpu.roll(x, seq - 1, 1)
    left = jnp.concatenate([jnp.where(lane == 0, 0.0, left[:, :LANES]), left[:, LANES:]], axis=1)
    right = jnp.concatenate([right[:, :seq - LANES], jnp.where(lane == LANES - 1, 0.0, right[:, seq - LANES:])], axis=1)
    return left * cw[:, 0:1] + x * cw[:, 1:2] + right * cw[:, 2:3] + cw[:, 3:4]


def _hy_conv_kernel(uin_ref, gin_ref, cwu_ref, cwg_ref, d_ref, h_ref, twr_ref, twi_ref, gf_ref, gi_ref,
                    o_ref, u_ref, z_ref, *, seq, n1, n2, conv_u):
    ct = uin_ref.shape[1]
    rows16 = 2 * SUBLANES

    def prep(i, carry):
        r0 = pl.multiple_of(i * rows16, rows16)
        cwg = cwg_ref[pl.ds(r0, rows16), :]
        cwu = cwu_ref[pl.ds(r0, rows16), :]
        for e in range(2):
            o_ref[e, pl.ds(r0, rows16), :] = _short_conv_rows(gin_ref.at[e], cwg, r0, rows16, seq)
            if conv_u:
                u_ref[e, pl.ds(r0, rows16), :] = _short_conv_rows(uin_ref.at[e], cwu, r0, rows16, seq)
            else:
                u_ref[e, pl.ds(r0, rows16), :] = uin_ref[e, pl.ds(r0, rows16), :].astype(F32)
        return carry

    lax.fori_loop(0, ct // rows16, prep, 0)

    def load(j, r0, l0):
        return (u_ref[0, pl.ds(r0, SUBLANES), pl.ds(j * n2 + l0, LANES)],
                u_ref[1, pl.ds(r0, SUBLANES), pl.ds(j * n2 + l0, LANES)])

    _across_fwd(load, n1 // 2, z_ref, twr_ref, twi_ref, n1, n2, ct)

    group = 4 if n1 % 4 == 0 else 2

    def within(g, carry):
        ks = [group * g + j for j in range(group)]
        zz = jnp.concatenate([z_ref[k1] for k1 in ks], axis=0)
        y = jnp.dot(zz.astype(BF16), gf_ref[...], preferred_element_type=F32)
        hh = jnp.concatenate([h_ref[k1] for k1 in ks], axis=0)
        yr, yi = y[:, :n2], y[:, n2:]
        hr, hi = hh[:, :n2], hh[:, n2:]
        p = jnp.concatenate([yr * hr - yi * hi, yr * hi + yi * hr], axis=1)
        q = jnp.dot(p.astype(BF16), gi_ref[...], preferred_element_type=F32)
        for j, k1 in enumerate(ks):
            z_ref[k1] = q[j * ct:(j + 1) * ct]
        return carry

    lax.fori_loop(0, n1 // group, within, 0)

    per_row = n2 // LANES

    def finish(it, carry):
        r0 = pl.multiple_of((it // per_row) * SUBLANES, SUBLANES)
        l0 = pl.multiple_of((it % per_row) * LANES, LANES)
        vals = []
        for k1 in range(n1):
            twr = twr_ref[pl.ds(k1, 1), pl.ds(l0, LANES)]
            twi = twi_ref[pl.ds(k1, 1), pl.ds(l0, LANES)]
            re = z_ref[k1, pl.ds(r0, SUBLANES), pl.ds(l0, LANES)]
            im = z_ref[k1, pl.ds(r0, SUBLANES), pl.ds(n2 + l0, LANES)]
            vals.append((re * twr + im * twi, im * twr - re * twi))
        outs = _fft_list(vals, 1.0, first_half_only=True)
        d = d_ref[pl.ds(r0, SUBLANES), :]
        for j in range(n1 // 2):
            sl = (pl.ds(r0, SUBLANES), pl.ds(j * n2 + l0, LANES))
            for e in range(2):
                o_ref[(e,) + sl] = o_ref[(e,) + sl] * (outs[j][e] + u_ref[(e,) + sl] * d)
        return carry

    lax.fori_loop(0, (ct // SUBLANES) * per_row, finish, 0, unroll=2 if n1 <= 16 else 1)


def _hy_conv(u, u_off, g, g_off, cw, spectra, conv_idx, d_skip, seq, width, ct, conv_u):
    batch = u.shape[0]
    n1, n2 = _fft_split(seq)
    twr, twi = _twiddles(n1, n2)
    gf, gi = _dft_mats(n2, 2 * seq)
    uo, go = u_off // ct, g_off // ct
    return pl.pallas_call(
        functools.partial(_hy_conv_kernel, seq=seq, n1=n1, n2=n2, conv_u=conv_u),
        grid=(width // ct, batch // 2),
        in_specs=[
            pl.BlockSpec((2, ct, seq), lambda c, p: (p, uo + c, 0)),
            pl.BlockSpec((2, ct, seq), lambda c, p: (p, go + c, 0)),
            pl.BlockSpec((ct, 4), lambda c, p: (uo + c, 0)),
            pl.BlockSpec((ct, 4), lambda c, p: (go + c, 0)),
            pl.BlockSpec((ct, 1), lambda c, p: (c, 0)),
            pl.BlockSpec((None, n1, ct, 2 * n2), lambda c, p: (conv_idx, 0, c, 0), pipeline_mode=pl.Buffered(1)),
            _const_spec(twr.shape), _const_spec(twi.shape), _const_spec(gf.shape), _const_spec(gi.shape),
        ],
        out_specs=pl.BlockSpec((2, ct, seq), lambda c, p: (p, c, 0)),
        out_shape=jax.ShapeDtypeStruct((batch, width, seq), F32),
        scratch_shapes=[pltpu.VMEM((2, ct, seq), F32), pltpu.VMEM((n1, ct, 2 * n2), F32)],
        compiler_params=_params(("arbitrary", "arbitrary")),
        name="hyena_conv",
    )(u, g, cw, cw, d_skip, spectra, twr, twi, gf, gi)


def _hyena(hyt, conv_w, conv_b, filt, hy_skip, seq, width):
    f_w1, f_b1, f_freq, f_w2, f_b2, f_w3 = filt
    ct = min(width, LANES * max(1, HY_TILE_ELEMS // (LANES * seq)))
    hid = _hy_hidden(seq, f_w1, f_b1, f_freq, f_w2, f_b2)
    spectra = _hy_filter_spectra(hid, f_w3, seq, width, ct)
    cw = jnp.concatenate([conv_w.T, conv_b[:, None]], axis=1)
    y1 = _hy_conv(hyt, 0, hyt, width, cw, spectra, 0, hy_skip[0].reshape(width, 1), seq, width, ct, True)
    return _hy_conv(y1, 0, hyt, 2 * width, cw, spectra, 1, hy_skip[1].reshape(width, 1), seq, width, ct, False)


def _attn_kernel(q_ref, k_ref, vt_ref, *rest, lam_init, with_cache, kc):
    if with_cache:
        kc_ref, vct_ref, lam_ref, sw_ref, o_ref, kmax_ref = rest
    else:
        lam_ref, sw_ref, o_ref, kmax_ref = rest
    lp = lam_ref[...]
    lam = (jnp.exp(jnp.sum(lp[0:1] * lp[1:2], axis=1, keepdims=True))
           - jnp.exp(jnp.sum(lp[2:3] * lp[3:4], axis=1, keepdims=True)) + lam_init)
    q = q_ref[...]
    lane = lax.broadcasted_iota(jnp.int32, q.shape, 1)
    zero = jnp.zeros_like(q)
    qs = (jnp.where(lane < DA_DH, q, zero), jnp.where(lane >= DA_DH, q, zero))
    seq = k_ref.shape[0]
    chunks = [(k_ref[c * kc:(c + 1) * kc, :], vt_ref[:, c * kc:(c + 1) * kc]) for c in range(seq // kc)]
    if with_cache:
        chunks.append((kc_ref[...].astype(BF16), vct_ref[...].astype(BF16)))
    tq = q.shape[0]
    sub = lax.broadcasted_iota(jnp.int32, (SUBLANES, DA_VD), 0)
    ln = lax.broadcasted_iota(jnp.int32, (SUBLANES, DA_VD), 1)
    pick = jnp.where((ln // DA_DH) == sub, 1.0, 0.0).astype(BF16)

    def sq_norms(x):
        xf = x.astype(F32)
        return lax.dot_general(pick, (xf * xf).astype(BF16), NT, preferred_element_type=F32)

    @pl.when(pl.program_id(2) == 0)
    def _key_norms():
        best = None
        for kk, _ in chunks:
            cur = jnp.max(sq_norms(kk), axis=1, keepdims=True)
            best = cur if best is None else jnp.maximum(best, cur)
        kmax_ref[...] = jnp.broadcast_to(best, kmax_ref.shape)

    def scores(ci):
        return [lax.dot_general(chunks[ci][0], qm, NT, preferred_element_type=F32) for qm in qs]

    def finish(acc1, l1, acc2, l2):
        o = acc1 * (1.0 / l1) - acc2 * (lam / l2)
        ms = jnp.mean(o * o, axis=0, keepdims=True)
        o_ref[...] = (o * lax.rsqrt(ms + EPS) * sw_ref[...] * (1.0 - lam_init)).astype(o_ref.dtype)

    bound = jnp.sqrt(sq_norms(q) * kmax_ref[:, 0:1]) * ATTN_BOUND_PAD
    lsum = [jnp.zeros((SUBLANES, tq), F32), jnp.zeros((SUBLANES, tq), F32)]
    acc = [None, None]
    s_next = scores(0)
    for ci, (kk, vv) in enumerate(chunks):
        s_cur = s_next
        if ci + 1 < len(chunks):
            s_next = scores(ci + 1)
        for mi in range(2):
            p = jnp.exp2(s_cur[mi] - bound[mi:mi + 1, :])
            lsum[mi] = lsum[mi] + jnp.sum(p.reshape(-1, SUBLANES, tq), axis=0)
            part = jnp.dot(vv, p.astype(BF16), preferred_element_type=F32)
            acc[mi] = part if acc[mi] is None else acc[mi] + part
    l1 = jnp.sum(lsum[0], axis=0, keepdims=True)
    l2 = jnp.sum(lsum[1], axis=0, keepdims=True)
    finish(acc[0], l1, acc[1], l2)
    safe = jnp.min(jnp.minimum(l1, l2)) > ATTN_SUM_FLOOR

    @pl.when(jnp.logical_not(safe))
    def _online():
        state = [None, None]
        for ci, (kk, vv) in enumerate(chunks):
            s_cur = scores(ci)
            for mi in range(2):
                s = s_cur[mi]
                cmax = jnp.max(s, axis=0, keepdims=True)
                if state[mi] is None:
                    m = cmax
                    p = jnp.exp2(s - m)
                    l = jnp.sum(p, axis=0, keepdims=True)
                    a = jnp.dot(vv, p.astype(BF16), preferred_element_type=F32)
                else:
                    m_old, l, a = state[mi]
                    m = jnp.maximum(m_old, cmax)
                    alpha = jnp.exp2(m_old - m)
                    p = jnp.exp2(s - m)
                    l = l * alpha + jnp.sum(p, axis=0, keepdims=True)
                    a = a * alpha + jnp.dot(vv, p.astype(BF16), preferred_element_type=F32)
                state[mi] = (m, l, a)
        finish(state[0][2], state[0][1], state[1][2], state[1][1])


def _diff_attention(q, k, chan, v_row0, cache, lam_p, subln_w, batch, seq, lam_init):
    tq = min(seq, ATTN_Q_TILE)
    kc = min(seq, ATTN_KEY_CHUNK)
    nq = seq // tq
    vb = v_row0 // DA_VD
    grid = (batch, DA_HEADS, nq)
    in_specs = [pl.BlockSpec((tq, DA_VD), lambda b, h, i: (b * nq + i, h)),
                pl.BlockSpec((seq, DA_VD), lambda b, h, i: (b, h)),
                pl.BlockSpec((None, DA_VD, seq), lambda b, h, i: (b, vb + h, 0))]
    args = [q, k, chan]
    if cache is not None:
        past = cache[0].shape[2]
        in_specs += [pl.BlockSpec((None, None, past, DA_VD), lambda b, h, i: (b, h, 0, 0)),
                     pl.BlockSpec((None, None, DA_VD, past), lambda b, h, i: (b, h, 0, 0))]
        args += [cache[0], jnp.swapaxes(cache[1], 2, 3)]
    in_specs += [_const_spec(lam_p.shape), _const_spec((DA_VD, 1))]
    args += [lam_p, subln_w.reshape(DA_VD, 1)]
    return pl.pallas_call(
        functools.partial(_attn_kernel, lam_init=lam_init, with_cache=cache is not None, kc=kc),
        grid=grid,
        in_specs=in_specs,
        out_specs=pl.BlockSpec((None, DA_VD, tq), lambda b, h, i: (b, h, i)),
        out_shape=jax.ShapeDtypeStruct((batch, DA_HEADS * DA_VD, seq), BF16),
        scratch_shapes=[pltpu.VMEM((SUBLANES, LANES), F32)],
        compiler_params=_params(("parallel", "parallel", "arbitrary")),
        name="diff_attention",
    )(*args)


def _outproj0_kernel(x_ref, yt_ref, ot_ref, g_ref, wy_ref, wo_ref, out_ref):
    mix = lax.dot_general(yt_ref[...].astype(BF16), wy_ref[...], TN, preferred_element_type=F32)
    mix = mix + lax.dot_general(ot_ref[...], wo_ref[...], TN, preferred_element_type=F32)
    out_ref[...] = x_ref[...] + g_ref[...] * mix


def _outproj0(x, yt, ot, g, w_out, seq, row_fn):
    t, d = x.shape
    width = yt.shape[1]
    tm = min(seq, TOKEN_TILE)
    per_seq = seq // tm
    wy = w_out[:width].astype(BF16)
    wo = w_out[width:].astype(BF16)
    tok = pl.BlockSpec((tm, d), lambda i: (i, 0))
    chan = lambda n: pl.BlockSpec((None, n, tm), lambda i: (i // per_seq, 0, i % per_seq))
    return pl.pallas_call(
        _outproj0_kernel,
        grid=(t // tm,),
        in_specs=[tok, chan(width), chan(ot.shape[1]),
                  pl.BlockSpec((None, 1, d), lambda i: (row_fn(i // per_seq), 0, 0)),
                  _const_spec(wy.shape), _const_spec(wo.shape)],
        out_specs=tok,
        out_shape=jax.ShapeDtypeStruct((t, d), F32),
        compiler_params=_params(("parallel",)),
        name="outproj0",
    )(x, yt, ot, g, wy, wo)


def _top2_of_rows(rows):
    n = len(rows)
    v1 = functools.reduce(jnp.maximum, rows)
    i1 = jnp.full(rows[0].shape, n - 1, jnp.int32)
    for j in range(n - 2, -1, -1):
        i1 = jnp.where(rows[j] == v1, j, i1)
    masked = [jnp.where(i1 == j, -jnp.inf, rows[j]) for j in range(n)]
    v2 = functools.reduce(jnp.maximum, masked)
    i2 = jnp.full(rows[0].shape, n - 1, jnp.int32)
    for j in range(n - 2, -1, -1):
        i2 = jnp.where(masked[j] == v2, j, i2)
    return v1, i1, v2, i2


def _select_rows(idx, rows):
    out = rows[-1]
    for j in range(len(rows) - 2, -1, -1):
        out = jnp.where(idx == j, rows[j], out)
    return out


def _moe_kernel(x_ref, sc_ref, sh_ref, g_ref, nw_ref, rwt_ref, rb_ref, upper_ref, wg_ref, wu_ref, wd_ref,
                out_ref, hb_ref, rank_ref, comb_ref, acc_ref, sel_ref, y_ref, cnt_ref):
    step = pl.program_id(1)
    tb = x_ref.shape[0]

    @pl.when(step == 0)
    def _route():
        h = _norm_mod(x_ref[...], nw_ref[...], sc_ref[...], sh_ref[...])
        hb = h.astype(BF16)
        hb_ref[...] = hb
        h_lo = (h - hb.astype(F32)).astype(BF16)
        logits = (lax.dot_general(rwt_ref[0], hb, NT, preferred_element_type=F32)
                  + lax.dot_general(rwt_ref[0], h_lo, NT, preferred_element_type=F32)
                  + lax.dot_general(rwt_ref[1], hb, NT, preferred_element_type=F32))
        s = 1.0 / (1.0 + jnp.exp(-logits))
        s_sel = s + rb_ref[...]
        groups = tb // LANES

        def pack(row):
            return jnp.concatenate([row[:, c * LANES:(c + 1) * LANES] for c in range(groups)], axis=0)

        def unpack(tile):
            return jnp.concatenate([tile[c:c + 1, :] for c in range(groups)], axis=1)

        s_rows = [pack(s[j:j + 1, :]) for j in range(N_EXPERTS)]
        sel_rows = [pack(s_sel[j:j + 1, :]) for j in range(N_EXPERTS)]
        tops = [_top2_of_rows(sel_rows[g * EPG:(g + 1) * EPG]) for g in range(N_GROUPS)]
        scores = [tp[0] + tp[2] for tp in tops]
        best = functools.reduce(jnp.maximum, scores)
        g_idx = jnp.full(best.shape, N_GROUPS - 1, jnp.int32)
        for g in range(N_GROUPS - 2, -1, -1):
            g_idx = jnp.where(scores[g] == best, g, g_idx)
        e1 = g_idx * EPG + _select_rows(g_idx, [tp[1] for tp in tops])
        e2 = g_idx * EPG + _select_rows(g_idx, [tp[3] for tp in tops])
        w1 = _select_rows(e1, s_rows)
        w2 = _select_rows(e2, s_rows)
        tot = w1 + w2
        w1, w2 = w1 / tot, w2 / tot
        masks = [jnp.where((e1 == j) | (e2 == j), 1.0, 0.0) for j in range(N_EXPERTS)]
        comb_ref[...] = jnp.concatenate(
            [unpack(jnp.where(e1 == j, w1, 0.0) + jnp.where(e2 == j, w2, 0.0)) for j in range(N_EXPERTS)], axis=0)
        mask = jnp.concatenate([unpack(m) for m in masks], axis=0)
        rank = jnp.dot(mask.astype(BF16), upper_ref[...], preferred_element_type=F32)
        rank_ref[...] = jnp.where(mask > 0.0, rank, -1.0)
        for j in range(N_EXPERTS):
            cnt_ref[j] = jnp.sum(masks[j]).astype(jnp.int32)
        acc_ref[...] = jnp.zeros_like(acc_ref)

    def expert_tile(j, tile_idx):
        e = step * MOE_EXPERTS_PER_STEP + j
        rank = rank_ref[pl.ds(e, 1), :]
        comb = comb_ref[pl.ds(e, 1), :]
        slot = (lax.broadcasted_iota(jnp.int32, (MOE_TILE, tb), 0) + tile_idx * MOE_TILE).astype(F32)
        hit = rank == slot
        onehot = jnp.where(hit, 1.0, 0.0).astype(BF16)
        xe = jnp.dot(onehot, hb_ref[...], preferred_element_type=F32).astype(BF16)
        gate = jnp.dot(xe, wg_ref[j], preferred_element_type=F32)
        up = jnp.dot(xe, wu_ref[j], preferred_element_type=F32)
        y = jnp.dot((_silu(gate) * up).astype(BF16), wd_ref[j], preferred_element_type=F32)
        w_slot = jnp.sum(jnp.where(hit, comb, 0.0), axis=1, keepdims=True)
        return onehot, (y * w_slot).astype(BF16)

    experts = range(MOE_EXPERTS_PER_STEP)
    slot0 = lax.broadcasted_iota(jnp.int32, (MOE_TILE, tb), 0).astype(F32)
    hits = [rank_ref[pl.ds(step * MOE_EXPERTS_PER_STEP + j, 1), :] == slot0 for j in experts]
    for j in experts:
        sel_ref[j * MOE_TILE:(j + 1) * MOE_TILE, :] = jnp.where(hits[j], 1.0, 0.0).astype(BF16)
    xes = [jnp.dot(sel_ref[j * MOE_TILE:(j + 1) * MOE_TILE, :], hb_ref[...],
                   preferred_element_type=F32).astype(BF16) for j in experts]
    hidden = [(jnp.dot(xes[j], wg_ref[j], preferred_element_type=F32),
               jnp.dot(xes[j], wu_ref[j], preferred_element_type=F32)) for j in experts]
    acts = [(_silu(gate) * up).astype(BF16) for gate, up in hidden]
    ys = [jnp.dot(acts[j], wd_ref[j], preferred_element_type=F32) for j in experts]
    for j in experts:
        comb = comb_ref[pl.ds(step * MOE_EXPERTS_PER_STEP + j, 1), :]
        w_slot = jnp.sum(jnp.where(hits[j], comb, 0.0), axis=1, keepdims=True)
        y_ref[j * MOE_TILE:(j + 1) * MOE_TILE, :] = (ys[j] * w_slot).astype(BF16)

    acc_ref[...] += lax.dot_general(sel_ref[...], y_ref[...], TN, preferred_element_type=F32)

    for j in experts:
        def overflow(tile_idx, carry, j=j):
            onehot, yw = expert_tile(j, tile_idx)
            acc_ref[...] += lax.dot_general(onehot, yw, TN, preferred_element_type=F32)
            return carry

        n_tiles = (cnt_ref[step * MOE_EXPERTS_PER_STEP + j] + MOE_TILE - 1) // MOE_TILE
        lax.fori_loop(1, n_tiles, overflow, 0)

    @pl.when(step == pl.num_programs(1) - 1)
    def _finish():
        out_ref[...] = x_ref[...] + g_ref[...] * acc_ref[...]


def _moe(x, sc, sh, g, nw, router_w, router_b, w_gate, w_up, w_down, layer, seq, row_fn, shared_mod):
    t, d = x.shape
    tb = min(t if shared_mod else seq, MOE_BLOCK)
    per_seq = t if shared_mod else seq // tb
    dff = w_gate.shape[-1]
    upper = jnp.asarray(np.triu(np.ones((tb, tb), np.float32), 1), BF16)
    mod_spec = pl.BlockSpec((None, 1, d), lambda i, e: (row_fn(i // per_seq), 0, 0))
    tok = pl.BlockSpec((tb, d), lambda i, e: (i, 0))
    per = MOE_EXPERTS_PER_STEP
    stacked = per * MOE_TILE
    rw_hi = router_w.T.astype(BF16)
    rwt = jnp.stack([rw_hi, (router_w.T - rw_hi.astype(F32)).astype(BF16)])
    return pl.pallas_call(
        _moe_kernel,
        grid=(t // tb, N_EXPERTS // per),
        in_specs=[tok, mod_spec, mod_spec, mod_spec, _const_spec((1, d)), _const_spec((2, N_EXPERTS, d)),
                  _const_spec((N_EXPERTS, 1)), _const_spec(upper.shape),
                  pl.BlockSpec((None, per, d, dff), lambda i, e: (layer, e, 0, 0)),
                  pl.BlockSpec((None, per, d, dff), lambda i, e: (layer, e, 0, 0)),
                  pl.BlockSpec((None, per, dff, d), lambda i, e: (layer, e, 0, 0))],
        out_specs=tok,
        out_shape=jax.ShapeDtypeStruct((t, d), F32),
        scratch_shapes=[pltpu.VMEM((tb, d), BF16), pltpu.VMEM((N_EXPERTS, tb), F32), pltpu.VMEM((N_EXPERTS, tb), F32),
                        pltpu.VMEM((tb, d), F32), pltpu.VMEM((stacked, tb), BF16), pltpu.VMEM((stacked, d), BF16),
                        pltpu.SMEM((N_EXPERTS,), jnp.int32)],
        compiler_params=_params(("parallel", "arbitrary")),
        name="moe",
    )(x, sc, sh, g, nw, rwt, router_b.reshape(N_EXPERTS, 1), upper, w_gate, w_up, w_down)


def _proj1_kernel(x_ref, sc_ref, sh_ref, nw_ref, w_ref, wlr_ref, p_ref, lr_ref):
    h = _norm_mod(x_ref[...], nw_ref[...], sc_ref[...], sh_ref[...]).astype(BF16)
    p_ref[...] = jnp.dot(h, w_ref[...], preferred_element_type=F32).astype(p_ref.dtype)
    lr_ref[...] = jnp.dot(h, wlr_ref[...], preferred_element_type=F32)


def _proj1(x, sc, sh, nw, w_main, w_lr, seq, row_fn):
    t, d = x.shape
    tm = min(seq, TOKEN_TILE)
    per_seq = seq // tm
    mod_spec = pl.BlockSpec((None, 1, d), lambda i: (row_fn(i // per_seq), 0, 0))
    tok = lambda n: pl.BlockSpec((tm, n), lambda i: (i, 0))
    return pl.pallas_call(
        _proj1_kernel,
        grid=(t // tm,),
        in_specs=[tok(d), mod_spec, mod_spec, _const_spec((1, d)), _const_spec(w_main.shape), _const_spec(w_lr.shape)],
        out_specs=[tok(w_main.shape[1]), tok(LANES)],
        out_shape=[jax.ShapeDtypeStruct((t, w_main.shape[1]), BF16), jax.ShapeDtypeStruct((t, LANES), F32)],
        compiler_params=_params(("parallel",)),
        name="proj1",
    )(x, sc, sh, nw, w_main, w_lr)


def _log_sigmoid(x):
    return jnp.minimum(x, 0.0) - jnp.log(1.0 + jnp.exp(-jnp.abs(x)))


def _gla_kernel(qf_ref, kf_ref, vf_ref, lf_ref, qb_ref, kb_ref, vb_ref, lb_ref, wgf_ref, wgb_ref, bg_ref,
                trif_ref, trib_ref, s0f_ref, s0b_ref, of_ref, ob_ref, sf_ref, sb_ref, stf_ref, stb_ref, *, dk, dv):
    i = pl.program_id(1)
    nb = pl.num_programs(1)
    tc = qf_ref.shape[0]

    @pl.when(i == 0)
    def _init():
        stf_ref[...] = s0f_ref[...].astype(F32)
        stb_ref[...] = s0b_ref[...].astype(F32)

    r = lax.broadcasted_iota(jnp.int32, (GLA_CHUNK, GLA_CHUNK), 0)
    c = lax.broadcasted_iota(jnp.int32, (GLA_CHUNK, GLA_CHUNK), 1)
    scale = dk ** -0.5

    def decay_sums(l_ref, wg_ref, bias, tri_ref):
        gate = _log_sigmoid(jnp.dot(l_ref[...], wg_ref[...], precision=HI, preferred_element_type=F32)
                            + bias) / GLA_GATE_NORM
        total = None
        rest = gate
        for _ in range(3):
            part = rest.astype(BF16)
            rest = rest - part.astype(F32)
            term = jnp.dot(tri_ref[...], part, preferred_element_type=F32)
            total = term if total is None else total + term
        return total

    b_f = decay_sums(lf_ref, wgf_ref, bg_ref[0:1, :], trif_ref)
    b_b = decay_sums(lb_ref, wgb_ref, bg_ref[1:2, :], trib_ref)
    n_chunks = tc // GLA_CHUNK
    chains = []
    for h in range(GLA_HEADS):
        chains.append((qf_ref, kf_ref, vf_ref, of_ref, b_f, stf_ref, h, False))
        chains.append((qb_ref, kb_ref, vb_ref, ob_ref, b_b, stb_ref, h, True))
    states = [st_ref[h] for (_, _, _, _, _, st_ref, h, _) in chains]
    for step in range(n_chunks):
        prepared = []
        for (q_ref, k_ref, v_ref, _, b_all, _, h, reverse) in chains:
            ci = n_chunks - 1 - step if reverse else step
            rs = slice(ci * GLA_CHUNK, (ci + 1) * GLA_CHUNK)
            ks, vs = slice(h * dk, (h + 1) * dk), slice(h * dv, (h + 1) * dv)
            b = b_all[rs, ks]
            b_end = b[0:1, :] if reverse else b[GLA_CHUNK - 1:GLA_CHUNK, :]
            q_dec = (q_ref[rs, ks].astype(F32) * scale * jnp.exp(b)).astype(BF16)
            k_dec = k_ref[rs, ks].astype(F32) * jnp.exp(-b)
            e_end = jnp.exp(b_end)
            prepared.append((rs, vs, q_dec, k_dec.astype(BF16), (k_dec * e_end).astype(BF16), e_end, v_ref[rs, vs]))
        first = []
        for n, (rs, vs, q_dec, k_bf, k_rem, e_end, v) in enumerate(prepared):
            att = lax.dot_general(q_dec, k_bf, NT, preferred_element_type=F32)
            carry = lax.dot_general(q_dec, states[n].astype(BF16), NT, preferred_element_type=F32)
            upd = lax.dot_general(v, k_rem, TN, preferred_element_type=F32)
            first.append((att, carry, upd))
        for n, (rs, vs, q_dec, k_bf, k_rem, e_end, v) in enumerate(prepared):
            att, carry, upd = first[n]
            reverse = chains[n][7]
            att = jnp.where((c >= r) if reverse else (c <= r), att, 0.0).astype(BF16)
            o = jnp.dot(att, v, preferred_element_type=F32) + carry
            chains[n][3][rs, vs] = o.astype(chains[n][3].dtype)
            states[n] = states[n] * e_end + upd
    for n, (_, _, _, _, _, st_ref, h, _) in enumerate(chains):
        st_ref[h] = states[n]

    @pl.when(i == nb - 1)
    def _emit():
        sf_ref[...] = stf_ref[...]
        sb_ref[...] = stb_ref[...]


def _gla(p, lr, wgf, wgb, bg, s0f, s0b, batch, seq, dk, dv):
    tc = min(seq, GLA_ROWS_PER_STEP)
    nb = seq // tc
    hk, hv = GLA_HEADS * dk, GLA_HEADS * dv
    fwd = lambda width, col: pl.BlockSpec((tc, width), lambda b, i: (b * nb + i, col))
    bwd = lambda width, col: pl.BlockSpec((tc, width), lambda b, i: (b * nb + nb - 1 - i, col))
    st_spec = pl.BlockSpec((None, GLA_HEADS, dv, dk), lambda b, i: (b, 0, 0, 0))
    v_col = 2 * hk // hv
    idx = np.arange(tc)
    same = (idx[:, None] // GLA_CHUNK) == (idx[None, :] // GLA_CHUNK)
    tri_f = jnp.asarray(same & (idx[None, :] <= idx[:, None]), BF16)
    tri_b = jnp.asarray(same & (idx[None, :] >= idx[:, None]), BF16)
    return pl.pallas_call(
        functools.partial(_gla_kernel, dk=dk, dv=dv),
        grid=(batch, nb),
        in_specs=[fwd(hk, 0), fwd(hk, 1), fwd(hv, v_col), fwd(LANES, 0),
                  bwd(hk, 0), bwd(hk, 1), bwd(hv, v_col), bwd(LANES, 0),
                  _const_spec(wgf.shape), _const_spec(wgb.shape), _const_spec(bg.shape),
                  _const_spec(tri_f.shape), _const_spec(tri_b.shape), st_spec, st_spec],
        out_specs=[fwd(hv, 0), bwd(hv, 0), st_spec, st_spec],
        out_shape=[jax.ShapeDtypeStruct((batch * seq, hv), BF16)] * 2
        + [jax.ShapeDtypeStruct((batch, GLA_HEADS, dv, dk), F32)] * 2,
        scratch_shapes=[pltpu.VMEM((GLA_HEADS, dv, dk), F32)] * 2,
        compiler_params=_params(("parallel", "arbitrary")),
        name="gla_scan",
    )(p, p, p, lr, p, p, p, lr, wgf, wgb, bg, tri_f, tri_b, s0f, s0b)


def _outproj1_kernel(x_ref, of_ref, ob_ref, go_ref, g_ref, nw_ref, w_ref, out_ref, *, dv):
    o = of_ref[...].astype(F32) + ob_ref[...].astype(F32)
    parts = []
    for h in range(GLA_HEADS):
        oh = o[:, h * dv:(h + 1) * dv]
        ms = jnp.mean(oh * oh, axis=1, keepdims=True)
        parts.append(oh * lax.rsqrt(ms + EPS))
    o = jnp.concatenate(parts, axis=1) * nw_ref[...] * _silu(go_ref[...].astype(F32))
    out_ref[...] = x_ref[...] + g_ref[...] * jnp.dot(o.astype(BF16), w_ref[...], preferred_element_type=F32)


def _outproj1(x, o_f, o_b, p, g, out_norm_w, w_out, seq, row_fn, dv):
    t, d = x.shape
    hv = GLA_HEADS * dv
    tm = min(seq, TOKEN_TILE)
    per_seq = seq // tm
    tok = lambda n, col=0: pl.BlockSpec((tm, n), lambda i: (i, col))
    go_col = (p.shape[1] - hv) // hv
    nw = jnp.tile(out_norm_w, GLA_HEADS).reshape(1, hv)
    wb = w_out.astype(BF16)
    return pl.pallas_call(
        functools.partial(_outproj1_kernel, dv=dv),
        grid=(t // tm,),
        in_specs=[tok(d), tok(hv), tok(hv), tok(hv, go_col),
                  pl.BlockSpec((None, 1, d), lambda i: (row_fn(i // per_seq), 0, 0)),
                  _const_spec((1, hv)), _const_spec(wb.shape)],
        out_specs=tok(d),
        out_shape=jax.ShapeDtypeStruct((t, d), F32),
        compiler_params=_params(("parallel",)),
        name="outproj1",
    )(x, o_f, o_b, p, g, nw, wb)


def _mod_rows(mod_l, d):
    return [mod_l[:, j * d:(j + 1) * d].reshape(mod_l.shape[0], 1, d) for j in range(6)]


def _trunk(x, batch, seq, mods, row_fn, shared_mod, norm_w, ab, cp, moe_w, cache, s0, emit_cache):
    d = x.shape[-1]
    t = batch * seq
    x = x.reshape(t, d)
    (w_in0, conv_w, conv_b, filt, hy_skip, qn_w, kn_w, lam_p, subln_w, w_out0) = ab
    (w_in1, gate_up_w, gate_up_b, out_norm_w, w_out1) = cp
    router_w, router_b, wg, wu, wd = moe_w
    width = d // 2
    hy_in = 3 * width

    sh1, sc1, g1, sh2, sc2, g2 = mods[0]
    nseg = DA_HEADS * 2
    qkw = jnp.stack([jnp.tile(qn_w, nseg), jnp.tile(kn_w, nseg)])
    seg = np.arange(DA_HEADS * DA_VD) // DA_DH
    ones = jnp.asarray(seg[:, None] == seg[None, :], BF16)
    v_cols = w_in0[:, hy_in + 2 * DA_HEADS * DA_VD:]
    w_chan = jnp.concatenate([w_in0[:, :hy_in], v_cols], axis=1).T.astype(BF16)
    outs = _proj0(x, sc1, sh1, norm_w[0, 0].reshape(1, d), w_in0[:, hy_in:].astype(BF16), w_chan, qkw, ones,
                  batch, seq, row_fn, rope=cache is not None, emit_cache=emit_cache)
    chan, q, k = outs[:3]
    y_hy = _hyena(chan, conv_w, conv_b, filt, hy_skip, seq, width)
    lam_init = 0.8 - 0.6 * math.exp(-0.3 * 0)
    o = _diff_attention(q, k, chan, hy_in, cache, lam_p, subln_w, batch, seq, lam_init)
    x = _outproj0(x, y_hy, o, g1, w_out0, seq, row_fn)
    x = _moe(x, sc2, sh2, g2, norm_w[0, 1].reshape(1, d), router_w, router_b, wg, wu, wd, 0, seq, row_fn,
             shared_mod)

    sh1, sc1, g1, sh2, sc2, g2 = mods[1]
    dk = d // 2 // GLA_HEADS
    dv = d // GLA_HEADS
    n_main = GLA_HEADS * (2 * dk + 2 * dv)
    w_lr = jnp.pad(w_in1[:, n_main:], ((0, 0), (0, LANES - 2 * GLA_RANK))).astype(BF16)
    p, lr = _proj1(x, sc1, sh1, norm_w[1, 0].reshape(1, d), w_in1[:, :n_main].astype(BF16), w_lr, seq, row_fn)
    wgf = jnp.pad(gate_up_w[0], ((0, LANES - GLA_RANK), (0, 0)))
    wgb = jnp.pad(gate_up_w[1], ((GLA_RANK, LANES - 2 * GLA_RANK), (0, 0)))
    o_f, o_b, s_f, s_b = _gla(p, lr, wgf, wgb, gate_up_b, jnp.swapaxes(s0[0], 2, 3), jnp.swapaxes(s0[1], 2, 3),
                              batch, seq, dk, dv)
    s_f, s_b = jnp.swapaxes(s_f, 2, 3), jnp.swapaxes(s_b, 2, 3)
    x = _outproj1(x, o_f, o_b, p, g1, out_norm_w, w_out1, seq, row_fn, dv)
    x = _moe(x, sc2, sh2, g2, norm_w[1, 1].reshape(1, d), router_w, router_b, wg, wu, wd, 1, seq, row_fn,
             shared_mod)
    return x.reshape(batch, seq, d), outs[3:], (s_f, s_b)


def kernel(x_prompt, x_sample, cache_l0_k, cache_l0_v, state_l1_fwd, state_l1_bwd, c, c_ctx, w_mod, b_mod, norm_w,
           l0_w_in, l0_conv_w, l0_conv_b, l0_filt_w1, l0_filt_b1, l0_filt_freq, l0_filt_w2, l0_filt_b2, l0_filt_w3,
           l0_hy_skip, l0_qn_w, l0_kn_w, l0_lambda, l0_subln_w, l0_w_out, l1_w_in, l1_gate_up_w, l1_gate_up_b,
           l1_out_norm_w, l1_w_out, moe_router_w, moe_router_b, moe_w_gate, moe_w_up, moe_w_down):
    d = x_prompt.shape[-1]
    n_lat = c.shape[0]
    ctx_row = n_lat
    rows = 2 * SUBLANES
    cond = jnp.zeros((rows, d), F32).at[:n_lat].set(c).at[ctx_row].set(c_ctx)
    mod = _modulation(cond, w_mod, b_mod)
    mods = [_mod_rows(mod[l], d) for l in range(mod.shape[0])]
    filt = (l0_filt_w1, l0_filt_b1, l0_filt_freq, l0_filt_w2, l0_filt_b2, l0_filt_w3)
    ab = (l0_w_in, l0_conv_w, l0_conv_b, filt, l0_hy_skip, l0_qn_w, l0_kn_w, l0_lambda, l0_subln_w, l0_w_out)
    cp = (l1_w_in, l1_gate_up_w, l1_gate_up_b, l1_out_norm_w, l1_w_out)
    moe_w = (moe_router_w, moe_router_b, moe_w_gate.astype(BF16), moe_w_up.astype(BF16), moe_w_down.astype(BF16))

    b_ctx, l_ctx = x_prompt.shape[:2]
    zero_state = jnp.zeros((b_ctx,) + state_l1_fwd.shape[1:], F32)
    y_prompt, (new_k, new_v), (new_sf, new_sb) = _trunk(
        x_prompt, b_ctx, l_ctx, mods, lambda b: ctx_row, True, norm_w, ab, cp, moe_w, None,
        (zero_state, zero_state), True)
    b_lat, l_lat = x_sample.shape[:2]
    y_sample, _, _ = _trunk(
        x_sample, b_lat, l_lat, mods, lambda b: b, False, norm_w, ab, cp, moe_w, (cache_l0_k, cache_l0_v),
        (state_l1_fwd, state_l1_bwd), False)
    return (y_prompt, y_sample, new_k, new_v, new_sf, new_sb)
```

```python
import cmath
import functools
import math

import numpy as np
import jax
import jax.numpy as jnp
from jax import lax
from jax.experimental import pallas as pl
from jax.experimental.pallas import tpu as pltpu

F32 = jnp.float32
BF16 = jnp.bfloat16
HI = lax.Precision.HIGHEST
EPS = 1e-6

LANES = 128
SUBLANES = 8
VMEM_LIMIT = 56 << 20

DA_HEADS = 4
DA_DH = 64
DA_VD = 2 * DA_DH
ROPE_NF = DA_DH // 4
ROPE_BASE = 10000.0
GRID_W = 64
HY_BANDS = 16
HY_TARGET = 1e-2
HY_MAX_DECAY = math.log(HY_TARGET) / 0.3
HY_MIN_DECAY = math.log(HY_TARGET) / 1.5
GLA_HEADS = 4
GLA_RANK = 16
GLA_GATE_NORM = 16.0
GLA_CHUNK = 64
N_EXPERTS = 16
N_GROUPS = 4
EPG = N_EXPERTS // N_GROUPS
TOKEN_TILE = 1024
MOD_COL_TILE = 1536
FFT_BLOCK = 256
HY_TILE_ELEMS = 128 * 4096
ATTN_Q_TILE = 1024
ATTN_KEY_CHUNK = 1024
ATTN_BOUND_PAD = 1.02
ATTN_SUM_FLOOR = 2.0 ** -60
GLA_ROWS_PER_STEP = 512
MOE_BLOCK = 1024
MOE_TILE = 160
MOE_EXPERTS_PER_STEP = 4
NT = (((1,), (1,)), ((), ()))
TN = (((0,), (0,)), ((), ()))


def _params(sem):
    return pltpu.CompilerParams(dimension_semantics=sem, vmem_limit_bytes=VMEM_LIMIT)


def _const_spec(shape):
    nd = len(shape)
    return pl.BlockSpec(shape, lambda *_: (0,) * nd)


def _silu(x):
    return x * (1.0 / (1.0 + jnp.exp(-x)))


def _norm_mod(x, nw, sc, sh):
    ms = jnp.mean(x * x, axis=-1, keepdims=True)
    return x * lax.rsqrt(ms + EPS) * nw * (1.0 + sc) + sh


def _mod_kernel(c_ref, w_ref, b_ref, o_ref):
    s = _silu(c_ref[...])
    o_ref[...] = jnp.dot(s, w_ref[...], precision=HI, preferred_element_type=F32) + b_ref[...]


def _modulation(cond, w_mod, b_mod):
    depth, d, d6 = w_mod.shape
    r = cond.shape[0]
    tn = MOD_COL_TILE
    return pl.pallas_call(
        _mod_kernel,
        grid=(depth, d6 // tn),
        in_specs=[
            pl.BlockSpec((r, d), lambda l, j: (0, 0)),
            pl.BlockSpec((None, d, tn), lambda l, j: (l, 0, j)),
            pl.BlockSpec((None, 1, tn), lambda l, j: (l, 0, j)),
        ],
        out_specs=pl.BlockSpec((None, r, tn), lambda l, j: (l, 0, j)),
        out_shape=jax.ShapeDtypeStruct((depth, r, d6), F32),
        compiler_params=_params(("parallel", "parallel")),
        name="modulation",
    )(cond, w_mod, b_mod.reshape(depth, 1, d6))


def _proj0_kernel(x_ref, sc_ref, sh_ref, nw_ref, wtok_ref, wchan_ref, qkw_ref, ones_ref, cos_ref, sin_ref,
                  chan_ref, q_ref, k_ref, *cache_refs, rope):
    h = _norm_mod(x_ref[...], nw_ref[...], sc_ref[...], sh_ref[...]).astype(BF16)
    chan_ref[...] = lax.dot_general(wchan_ref[...], h, NT, preferred_element_type=F32).astype(chan_ref.dtype)
    p = jnp.dot(h, wtok_ref[...], preferred_element_type=F32)
    w = DA_HEADS * DA_VD
    ones = ones_ref[...]

    def segnorm(z, gain):
        ss = jnp.dot((z * z).astype(BF16), ones, preferred_element_type=F32)
        return z * lax.rsqrt(ss * (1.0 / DA_DH) + EPS) * gain

    q = segnorm(p[:, :w], qkw_ref[0:1, :])
    k = segnorm(p[:, w:2 * w], qkw_ref[1:2, :])
    if cache_refs:
        kn_ref, vn_ref = cache_refs
        v = p[:, 2 * w:]
        for hh in range(DA_HEADS):
            kn_ref[hh] = k[:, hh * DA_VD:(hh + 1) * DA_VD]
            vn_ref[hh] = v[:, hh * DA_VD:(hh + 1) * DA_VD]
    if rope:
        cos = jnp.concatenate([cos_ref[...]] * (w // LANES), axis=1)
        sin = jnp.concatenate([sin_ref[...]] * (w // LANES), axis=1)
        lane = lax.broadcasted_iota(jnp.int32, q.shape, 1)
        first = (lane % (2 * ROPE_NF)) < ROPE_NF

        def rot(z):
            partner = jnp.where(first, pltpu.roll(z, w - ROPE_NF, 1), pltpu.roll(z, ROPE_NF, 1))
            return z * cos + partner * sin

        q = rot(q)
        k = rot(k)
    q_ref[...] = (q * (DA_DH ** -0.5 * math.log2(math.e))).astype(q_ref.dtype)
    k_ref[...] = k.astype(k_ref.dtype)


def _rope_tables(seq):
    t = np.arange(seq)
    lane = np.arange(LANES)
    d = lane % DA_DH
    axis = d // (2 * ROPE_NF)
    part = (d % (2 * ROPE_NF)) // ROPE_NF
    f = d % ROPE_NF
    pos = jnp.where(axis[None, :] == 0, (t // GRID_W)[:, None], (t % GRID_W)[:, None]).astype(F32)
    inv = ROPE_BASE ** (-jnp.arange(ROPE_NF, dtype=F32) / ROPE_NF)
    ang = pos * inv[f][None, :]
    sign = jnp.asarray(np.where(part == 0, -1.0, 1.0), F32)[None, :]
    return jnp.cos(ang), jnp.sin(ang) * sign


def _proj0(x, sc, sh, nw, wqkv, whyt, qkw, ones, batch, seq, row_fn, rope, emit_cache):
    t, d = x.shape
    tm = min(seq, TOKEN_TILE)
    per_seq = seq // tm
    w = DA_HEADS * DA_VD
    hy_in = whyt.shape[0]
    if not emit_cache:
        wqkv = wqkv[:, :2 * w]
    if rope:
        cos, sin = _rope_tables(seq)
    else:
        cos = jnp.ones((seq, LANES), F32)
        sin = jnp.zeros((seq, LANES), F32)
    mod_spec = pl.BlockSpec((None, 1, d), lambda i: (row_fn(i // per_seq), 0, 0))
    tab_spec = pl.BlockSpec((tm, LANES), lambda i: (i % per_seq, 0))
    tok = lambda n: pl.BlockSpec((tm, n), lambda i: (i, 0))
    out_specs = [pl.BlockSpec((None, hy_in, tm), lambda i: (i // per_seq, 0, i % per_seq)), tok(w), tok(w)]
    out_shape = [jax.ShapeDtypeStruct((batch, hy_in, seq), BF16)] + [jax.ShapeDtypeStruct((t, w), BF16)] * 2
    if emit_cache:
        cspec = pl.BlockSpec((None, DA_HEADS, tm, DA_VD), lambda i: (i // per_seq, 0, i % per_seq, 0))
        out_specs += [cspec, cspec]
        out_shape += [jax.ShapeDtypeStruct((batch, DA_HEADS, seq, DA_VD), F32)] * 2
    return pl.pallas_call(
        functools.partial(_proj0_kernel, rope=rope),
        grid=(t // tm,),
        in_specs=[tok(d), mod_spec, mod_spec, _const_spec((1, d)), _const_spec(wqkv.shape), _const_spec(whyt.shape),
                  _const_spec(qkw.shape), _const_spec(ones.shape), tab_spec, tab_spec],
        out_specs=out_specs,
        out_shape=out_shape,
        compiler_params=_params(("parallel",)),
        name="proj0",
    )(x, sc, sh, nw, wqkv, whyt, qkw, ones, cos, sin)


def _c_add(a, b, sign=1.0):
    if b is None:
        return a
    if a is None:
        return b if sign > 0 else tuple(None if p is None else -p for p in b)
    out = []
    for pa, pb in zip(a, b):
        if pb is None:
            out.append(pa)
        elif pa is None:
            out.append(pb if sign > 0 else -pb)
        else:
            out.append(pa + pb if sign > 0 else pa - pb)
    return tuple(out)


def _c_mul_const(w, a):
    if a is None:
        return None
    wr = 0.0 if abs(w.real) < 1e-12 else w.real
    wi = 0.0 if abs(w.imag) < 1e-12 else w.imag
    ar, ai = a

    def scaled(c, p):
        if p is None or c == 0.0:
            return None
        if c == 1.0:
            return p
        if c == -1.0:
            return -p
        return c * p

    re = _c_add((scaled(wr, ar),), (scaled(wi, ai),), -1.0)[0]
    im = _c_add((scaled(wr, ai),), (scaled(wi, ar),), 1.0)[0]
    return (re, im)


def _fft_list(vals, sign, first_half_only=False):
    n = len(vals)
    if n == 1:
        return list(vals)
    ev = _fft_list(vals[0::2], sign)
    od = _fft_list(vals[1::2], sign)
    out = [None] * n
    for k in range(n // 2):
        tw = _c_mul_const(cmath.exp(sign * 2j * math.pi * k / n), od[k])
        out[k] = _c_add(ev[k], tw, 1.0)
        if not first_half_only:
            out[k + n // 2] = _c_add(ev[k], tw, -1.0)
    return out


def _unit_circle(rows, cols, period):
    r = lax.broadcasted_iota(jnp.int32, (rows, cols), 0)
    c = lax.broadcasted_iota(jnp.int32, (rows, cols), 1)
    ang = ((r * c) % period).astype(F32) * (2.0 * math.pi / period)
    return jnp.cos(ang), -jnp.sin(ang)


def _dft_mats(n2, total):
    fr, fi = _unit_circle(n2, n2, n2)
    fwd = jnp.concatenate([jnp.concatenate([fr, fi], axis=1), jnp.concatenate([-fi, fr], axis=1)], axis=0)
    inv = jnp.concatenate([jnp.concatenate([fr, -fi], axis=1), jnp.concatenate([fi, fr], axis=1)], axis=0) / total
    return fwd.astype(BF16), inv.astype(BF16)


def _twiddles(n1, n2):
    return _unit_circle(n1, n2, n1 * n2)


def _fft_split(seq):
    n = 2 * seq
    n2 = min(FFT_BLOCK, n // 2)
    return n // n2, n2


def _across_fwd(load, n_in, z_ref, twr_ref, twi_ref, n1, n2, rows):
    per_row = n2 // LANES

    def body(it, carry):
        r0 = pl.multiple_of((it // per_row) * SUBLANES, SUBLANES)
        l0 = pl.multiple_of((it % per_row) * LANES, LANES)
        vals = [load(j, r0, l0) for j in range(n_in)] + [None] * (n1 - n_in)
        outs = _fft_list(vals, -1.0)
        for k1 in range(n1):
            twr = twr_ref[pl.ds(k1, 1), pl.ds(l0, LANES)]
            twi = twi_ref[pl.ds(k1, 1), pl.ds(l0, LANES)]
            re, im = outs[k1]
            zero = jnp.zeros((SUBLANES, LANES), F32)
            re = zero if re is None else re
            im = zero if im is None else im
            z_ref[k1, pl.ds(r0, SUBLANES), pl.ds(l0, LANES)] = re * twr - im * twi
            z_ref[k1, pl.ds(r0, SUBLANES), pl.ds(n2 + l0, LANES)] = re * twi + im * twr
        return carry

    lax.fori_loop(0, (rows // SUBLANES) * per_row, body, 0)


def _hy_hidden_kernel(w1_ref, b1_ref, freq_ref, w2_ref, b2_ref, bands_ref, o_ref, *, seq):
    n = 2 * seq
    k = lax.broadcasted_iota(jnp.int32, (1, n), 1)
    pos = jnp.where(k < seq, k, n - k).astype(F32)
    t = pos / (seq - 1)
    ang = 2.0 * math.pi * pos / seq * bands_ref[...]
    z = (w1_ref[:, 0:1] * t
         + jnp.dot(w1_ref[:, 1:1 + HY_BANDS], jnp.cos(ang), precision=HI, preferred_element_type=F32)
         + jnp.dot(w1_ref[:, 1 + HY_BANDS:], -jnp.sin(ang), precision=HI, preferred_element_type=F32)
         + b1_ref[...])
    hid = jnp.sin(freq_ref[:, 0:1] * z)
    hid = jnp.sin(freq_ref[:, 1:2] * (jnp.dot(w2_ref[...], hid, precision=HI, preferred_element_type=F32) + b2_ref[...]))
    o_ref[...] = hid


def _hy_hidden(seq, f_w1, f_b1, f_freq, f_w2, f_b2):
    ffn = f_w1.shape[1]
    bands = jnp.linspace(1e-4, HY_BANDS - 1, HY_BANDS, dtype=F32).reshape(HY_BANDS, 1)
    args = (f_w1.T, f_b1.reshape(ffn, 1), f_freq.T, f_w2.T, f_b2.reshape(ffn, 1), bands)
    return pl.pallas_call(
        functools.partial(_hy_hidden_kernel, seq=seq),
        grid=(1,),
        in_specs=[_const_spec(a.shape) for a in args],
        out_specs=_const_spec((ffn, 2 * seq)),
        out_shape=jax.ShapeDtypeStruct((ffn, 2 * seq), F32),
        compiler_params=_params(("arbitrary",)),
        name="hyena_filter_hidden",
    )(*args)


def _hy_spec_kernel(hid_ref, wb_ref, wa_ref, dec_ref, twr_ref, twi_ref, gf_ref, o_ref, taps_ref, z_ref, *, seq, n1, n2):
    ct = wb_ref.shape[0]
    hid = hid_ref[...]
    back = jnp.dot(wb_ref[...], hid[:, :seq], precision=HI, preferred_element_type=F32)
    ahead = jnp.dot(wa_ref[...], hid[:, seq:], precision=HI, preferred_element_type=F32)
    lane = lax.broadcasted_iota(jnp.int32, (1, seq), 1)
    dec = dec_ref[...]
    t_back = lane.astype(F32) / (seq - 1)
    t_ahead = (seq - lane).astype(F32) / (seq - 1)
    back = back * jnp.exp(-t_back * dec)
    ahead = jnp.where(lane == 0, 0.0, ahead * jnp.exp(-t_ahead * dec))
    norm = (jnp.sum(jnp.abs(back), axis=1, keepdims=True) + jnp.sum(jnp.abs(ahead), axis=1, keepdims=True)) + EPS
    taps_ref[:, :seq] = back / norm
    taps_ref[:, seq:] = ahead / norm

    def load(j, r0, l0):
        return (taps_ref[pl.ds(r0, SUBLANES), pl.ds(j * n2 + l0, LANES)], None)

    _across_fwd(load, n1, z_ref, twr_ref, twi_ref, n1, n2, ct)
    for k1 in range(n1):
        o_ref[k1] = jnp.dot(z_ref[k1].astype(BF16), gf_ref[...], preferred_element_type=F32)


def _hy_filter_spectra(hid, f_w3, seq, width, ct):
    n1, n2 = _fft_split(seq)
    ffn = f_w3.shape[0]
    w3t = f_w3.T.reshape(4, width, ffn)
    decay = jnp.abs(jnp.linspace(HY_MIN_DECAY, HY_MAX_DECAY, width, dtype=F32)).reshape(width, 1)
    twr, twi = _twiddles(n1, n2)
    gf, _ = _dft_mats(n2, 2 * seq)
    return pl.pallas_call(
        functools.partial(_hy_spec_kernel, seq=seq, n1=n1, n2=n2),
        grid=(2, width // ct),
        in_specs=[
            _const_spec(hid.shape),
            pl.BlockSpec((None, ct, ffn), lambda i, c: (2 * i, c, 0)),
            pl.BlockSpec((None, ct, ffn), lambda i, c: (2 * i + 1, c, 0)),
            pl.BlockSpec((ct, 1), lambda i, c: (c, 0)),
            _const_spec(twr.shape), _const_spec(twi.shape), _const_spec(gf.shape),
        ],
        out_specs=pl.BlockSpec((None, n1, ct, 2 * n2), lambda i, c: (i, 0, c, 0)),
        out_shape=jax.ShapeDtypeStruct((2, n1, width, 2 * n2), F32),
        scratch_shapes=[pltpu.VMEM((ct, 2 * seq), F32), pltpu.VMEM((n1, ct, 2 * n2), F32)],
        compiler_params=_params(("parallel", "parallel")),
        name="hyena_filter_spectra",
    )(hid, w3t, w3t, decay, twr, twi, gf)


def _short_conv_rows(ref, cw, r0, nrows, seq):
    x = ref[pl.ds(r0, nrows), :].astype(F32)
    lane = lax.broadcasted_iota(jnp.int32, (nrows, LANES), 1)
    left = pltpu.roll(x, 1, 1)
    right = pltpu.roll(x, seq - 1, 1)
    left = jnp.concatenate([jnp.where(lane == 0, 0.0, left[:, :LANES]), left[:, LANES:]], axis=1)
    right = jnp.concatenate([right[:, :seq - LANES], jnp.where(lane == LANES - 1, 0.0, right[:, seq - LANES:])], axis=1)
    return left * cw[:, 0:1] + x * cw[:, 1:2] + right * cw[:, 2:3] + cw[:, 3:4]


def _hy_conv_kernel(uin_ref, gin_ref, cwu_ref, cwg_ref, d_ref, h_ref, twr_ref, twi_ref, gf_ref, gi_ref,
                    o_ref, u_ref, z_ref, *, seq, n1, n2, conv_u):
    ct = uin_ref.shape[1]
    rows16 = 2 * SUBLANES

    def prep(i, carry):
        r0 = pl.multiple_of(i * rows16, rows16)
        cwg = cwg_ref[pl.ds(r0, rows16), :]
        cwu = cwu_ref[pl.ds(r0, rows16), :]
        for e in range(2):
            o_ref[e, pl.ds(r0, rows16), :] = _short_conv_rows(gin_ref.at[e], cwg, r0, rows16, seq)
            if conv_u:
                u_ref[e, pl.ds(r0, rows16), :] = _short_conv_rows(uin_ref.at[e], cwu, r0, rows16, seq)
            else:
                u_ref[e, pl.ds(r0, rows16), :] = uin_ref[e, pl.ds(r0, rows16), :].astype(F32)
        return carry

    lax.fori_loop(0, ct // rows16, prep, 0)

    def load(j, r0, l0):
        return (u_ref[0, pl.ds(r0, SUBLANES), pl.ds(j * n2 + l0, LANES)],
                u_ref[1, pl.ds(r0, SUBLANES), pl.ds(j * n2 + l0, LANES)])

    _across_fwd(load, n1 // 2, z_ref, twr_ref, twi_ref, n1, n2, ct)

    group = 4 if n1 % 4 == 0 else 2

    def within(g, carry):
        ks = [group * g + j for j in range(group)]
        zz = jnp.concatenate([z_ref[k1] for k1 in ks], axis=0)
        y = jnp.dot(zz.astype(BF16), gf_ref[...], preferred_element_type=F32)
        hh = jnp.concatenate([h_ref[k1] for k1 in ks], axis=0)
        yr, yi = y[:, :n2], y[:, n2:]
        hr, hi = hh[:, :n2], hh[:, n2:]
        p = jnp.concatenate([yr * hr - yi * hi, yr * hi + yi * hr], axis=1)
        q = jnp.dot(p.astype(BF16), gi_ref[...], preferred_element_type=F32)
        for j, k1 in enumerate(ks):
            z_ref[k1] = q[j * ct:(j + 1) * ct]
        return carry

    lax.fori_loop(0, n1 // group, within, 0)

    per_row = n2 // LANES

    def finish(it, carry):
        r0 = pl.multiple_of((it // per_row) * SUBLANES, SUBLANES)
        l0 = pl.multiple_of((it % per_row) * LANES, LANES)
        vals = []
        for k1 in range(n1):
            twr = twr_ref[pl.ds(k1, 1), pl.ds(l0, LANES)]
            twi = twi_ref[pl.ds(k1, 1), pl.ds(l0, LANES)]
            re = z_ref[k1, pl.ds(r0, SUBLANES), pl.ds(l0, LANES)]
            im = z_ref[k1, pl.ds(r0, SUBLANES), pl.ds(n2 + l0, LANES)]
            vals.append((re * twr + im * twi, im * twr - re * twi))
        outs = _fft_list(vals, 1.0, first_half_only=True)
        d = d_ref[pl.ds(r0, SUBLANES), :]
        for j in range(n1 // 2):
            sl = (pl.ds(r0, SUBLANES), pl.ds(j * n2 + l0, LANES))
            for e in range(2):
                o_ref[(e,) + sl] = o_ref[(e,) + sl] * (outs[j][e] + u_ref[(e,) + sl] * d)
        return carry

    lax.fori_loop(0, (ct // SUBLANES) * per_row, finish, 0, unroll=2 if n1 <= 16 else 1)


def _hy_conv(u, u_off, g, g_off, cw, spectra, conv_idx, d_skip, seq, width, ct, conv_u):
    batch = u.shape[0]
    n1, n2 = _fft_split(seq)
    twr, twi = _twiddles(n1, n2)
    gf, gi = _dft_mats(n2, 2 * seq)
    uo, go = u_off // ct, g_off // ct
    return pl.pallas_call(
        functools.partial(_hy_conv_kernel, seq=seq, n1=n1, n2=n2, conv_u=conv_u),
        grid=(width // ct, batch // 2),
        in_specs=[
            pl.BlockSpec((2, ct, seq), lambda c, p: (p, uo + c, 0)),
            pl.BlockSpec((2, ct, seq), lambda c, p: (p, go + c, 0)),
            pl.BlockSpec((ct, 4), lambda c, p: (uo + c, 0)),
            pl.BlockSpec((ct, 4), lambda c, p: (go + c, 0)),
            pl.BlockSpec((ct, 1), lambda c, p: (c, 0)),
            pl.BlockSpec((None, n1, ct, 2 * n2), lambda c, p: (conv_idx, 0, c, 0), pipeline_mode=pl.Buffered(1)),
            _const_spec(twr.shape), _const_spec(twi.shape), _const_spec(gf.shape), _const_spec(gi.shape),
        ],
        out_specs=pl.BlockSpec((2, ct, seq), lambda c, p: (p, c, 0)),
        out_shape=jax.ShapeDtypeStruct((batch, width, seq), F32),
        scratch_shapes=[pltpu.VMEM((2, ct, seq), F32), pltpu.VMEM((n1, ct, 2 * n2), F32)],
        compiler_params=_params(("arbitrary", "arbitrary")),
        name="hyena_conv",
    )(u, g, cw, cw, d_skip, spectra, twr, twi, gf, gi)


def _hyena(hyt, conv_w, conv_b, filt, hy_skip, seq, width):
    f_w1, f_b1, f_freq, f_w2, f_b2, f_w3 = filt
    ct = min(width, LANES * max(1, HY_TILE_ELEMS // (LANES * seq)))
    hid = _hy_hidden(seq, f_w1, f_b1, f_freq, f_w2, f_b2)
    spectra = _hy_filter_spectra(hid, f_w3, seq, width, ct)
    cw = jnp.concatenate([conv_w.T, conv_b[:, None]], axis=1)
    y1 = _hy_conv(hyt, 0, hyt, width, cw, spectra, 0, hy_skip[0].reshape(width, 1), seq, width, ct, True)
    return _hy_conv(y1, 0, hyt, 2 * width, cw, spectra, 1, hy_skip[1].reshape(width, 1), seq, width, ct, False)


def _attn_kernel(q_ref, k_ref, vt_ref, *rest, lam_init, with_cache, kc):
    if with_cache:
        kc_ref, vct_ref, lam_ref, sw_ref, o_ref, kmax_ref = rest
    else:
        lam_ref, sw_ref, o_ref, kmax_ref = rest
    lp = lam_ref[...]
    lam = (jnp.exp(jnp.sum(lp[0:1] * lp[1:2], axis=1, keepdims=True))
           - jnp.exp(jnp.sum(lp[2:3] * lp[3:4], axis=1, keepdims=True)) + lam_init)
    q = q_ref[...]
    lane = lax.broadcasted_iota(jnp.int32, q.shape, 1)
    zero = jnp.zeros_like(q)
    qs = (jnp.where(lane < DA_DH, q, zero), jnp.where(lane >= DA_DH, q, zero))
    seq = k_ref.shape[0]
    chunks = [(k_ref[c * kc:(c + 1) * kc, :], vt_ref[:, c * kc:(c + 1) * kc]) for c in range(seq // kc)]
    if with_cache:
        chunks.append((kc_ref[...].astype(BF16), vct_ref[...].astype(BF16)))
    tq = q.shape[0]
    sub = lax.broadcasted_iota(jnp.int32, (SUBLANES, DA_VD), 0)
    ln = lax.broadcasted_iota(jnp.int32, (SUBLANES, DA_VD), 1)
    pick = jnp.where((ln // DA_DH) == sub, 1.0, 0.0).astype(BF16)

    def sq_norms(x):
        xf = x.astype(F32)
        return lax.dot_general(pick, (xf * xf).astype(BF16), NT, preferred_element_type=F32)

    @pl.when(pl.program_id(2) == 0)
    def _key_norms():
        best = None
        for kk, _ in chunks:
            cur = jnp.max(sq_norms(kk), axis=1, keepdims=True)
            best = cur if best is None else jnp.maximum(best, cur)
        kmax_ref[...] = jnp.broadcast_to(best, kmax_ref.shape)

    def scores(ci):
        return [lax.dot_general(chunks[ci][0], qm, NT, preferred_element_type=F32) for qm in qs]

    def finish(acc1, l1, acc2, l2):
        o = acc1 * (1.0 / l1) - acc2 * (lam / l2)
        ms = jnp.mean(o * o, axis=0, keepdims=True)
        o_ref[...] = (o * lax.rsqrt(ms + EPS) * sw_ref[...] * (1.0 - lam_init)).astype(o_ref.dtype)

    bound = jnp.sqrt(sq_norms(q) * kmax_ref[:, 0:1]) * ATTN_BOUND_PAD
    lsum = [jnp.zeros((SUBLANES, tq), F32), jnp.zeros((SUBLANES, tq), F32)]
    acc = [None, None]
    s_next = scores(0)
    for ci, (kk, vv) in enumerate(chunks):
        s_cur = s_next
        if ci + 1 < len(chunks):
            s_next = scores(ci + 1)
        for mi in range(2):
            p = jnp.exp2(s_cur[mi] - bound[mi:mi + 1, :])
            lsum[mi] = lsum[mi] + jnp.sum(p.reshape(-1, SUBLANES, tq), axis=0)
            part = jnp.dot(vv, p.astype(BF16), preferred_element_type=F32)
            acc[mi] = part if acc[mi] is None else acc[mi] + part
    l1 = jnp.sum(lsum[0], axis=0, keepdims=True)
    l2 = jnp.sum(lsum[1], axis=0, keepdims=True)
    finish(acc[0], l1, acc[1], l2)
    safe = jnp.min(jnp.minimum(l1, l2)) > ATTN_SUM_FLOOR

    @pl.when(jnp.logical_not(safe))
    def _online():
        state = [None, None]
        for ci, (kk, vv) in enumerate(chunks):
            s_cur = scores(ci)
            for mi in range(2):
                s = s_cur[mi]
                cmax = jnp.max(s, axis=0, keepdims=True)
                if state[mi] is None:
                    m = cmax
                    p = jnp.exp2(s - m)
                    l = jnp.sum(p, axis=0, keepdims=True)
                    a = jnp.dot(vv, p.astype(BF16), preferred_element_type=F32)
                else:
                    m_old, l, a = state[mi]
                    m = jnp.maximum(m_old, cmax)
                    alpha = jnp.exp2(m_old - m)
                    p = jnp.exp2(s - m)
                    l = l * alpha + jnp.sum(p, axis=0, keepdims=True)
                    a = a * alpha + jnp.dot(vv, p.astype(BF16), preferred_element_type=F32)
                state[mi] = (m, l, a)
        finish(state[0][2], state[0][1], state[1][2], state[1][1])


def _diff_attention(q, k, chan, v_row0, cache, lam_p, subln_w, batch, seq, lam_init):
    tq = min(seq, ATTN_Q_TILE)
    kc = min(seq, ATTN_KEY_CHUNK)
    nq = seq // tq
    vb = v_row0 // DA_VD
    grid = (batch, DA_HEADS, nq)
    in_specs = [pl.BlockSpec((tq, DA_VD), lambda b, h, i: (b * nq + i, h)),
                pl.BlockSpec((seq, DA_VD), lambda b, h, i: (b, h)),
                pl.BlockSpec((None, DA_VD, seq), lambda b, h, i: (b, vb + h, 0))]
    args = [q, k, chan]
    if cache is not None:
        past = cache[0].shape[2]
        in_specs += [pl.BlockSpec((None, None, past, DA_VD), lambda b, h, i: (b, h, 0, 0)),
                     pl.BlockSpec((None, None, DA_VD, past), lambda b, h, i: (b, h, 0, 0))]
        args += [cache[0], jnp.swapaxes(cache[1], 2, 3)]
    in_specs += [_const_spec(lam_p.shape), _const_spec((DA_VD, 1))]
    args += [lam_p, subln_w.reshape(DA_VD, 1)]
    return pl.pallas_call(
        functools.partial(_attn_kernel, lam_init=lam_init, with_cache=cache is not None, kc=kc),
        grid=grid,
        in_specs=in_specs,
        out_specs=pl.BlockSpec((None, DA_VD, tq), lambda b, h, i: (b, h, i)),
        out_shape=jax.ShapeDtypeStruct((batch, DA_HEADS * DA_VD, seq), BF16),
        scratch_shapes=[pltpu.VMEM((SUBLANES, LANES), F32)],
        compiler_params=_params(("parallel", "parallel", "arbitrary")),
        name="diff_attention",
    )(*args)


def _outproj0_kernel(x_ref, yt_ref, ot_ref, g_ref, wy_ref, wo_ref, out_ref):
    mix = lax.dot_general(yt_ref[...].astype(BF16), wy_ref[...], TN, preferred_element_type=F32)
    mix = mix + lax.dot_general(ot_ref[...], wo_ref[...], TN, preferred_element_type=F32)
    out_ref[...] = x_ref[...] + g_ref[...] * mix


def _outproj0(x, yt, ot, g, w_out, seq, row_fn):
    t, d = x.shape
    width = yt.shape[1]
    tm = min(seq, TOKEN_TILE)
    per_seq = seq // tm
    wy = w_out[:width].astype(BF16)
    wo = w_out[width:].astype(BF16)
    tok = pl.BlockSpec((tm, d), lambda i: (i, 0))
    chan = lambda n: pl.BlockSpec((None, n, tm), lambda i: (i // per_seq, 0, i % per_seq))
    return pl.pallas_call(
        _outproj0_kernel,
        grid=(t // tm,),
        in_specs=[tok, chan(width), chan(ot.shape[1]),
                  pl.BlockSpec((None, 1, d), lambda i: (row_fn(i // per_seq), 0, 0)),
                  _const_spec(wy.shape), _const_spec(wo.shape)],
        out_specs=tok,
        out_shape=jax.ShapeDtypeStruct((t, d), F32),
        compiler_params=_params(("parallel",)),
        name="outproj0",
    )(x, yt, ot, g, wy, wo)


def _top2_of_rows(rows):
    n = len(rows)
    v1 = functools.reduce(jnp.maximum, rows)
    i1 = jnp.full(rows[0].shape, n - 1, jnp.int32)
    for j in range(n - 2, -1, -1):
        i1 = jnp.where(rows[j] == v1, j, i1)
    masked = [jnp.where(i1 == j, -jnp.inf, rows[j]) for j in range(n)]
    v2 = functools.reduce(jnp.maximum, masked)
    i2 = jnp.full(rows[0].shape, n - 1, jnp.int32)
    for j in range(n - 2, -1, -1):
        i2 = jnp.where(masked[j] == v2, j, i2)
    return v1, i1, v2, i2


def _select_rows(idx, rows):
    out = rows[-1]
    for j in range(len(rows) - 2, -1, -1):
        out = jnp.where(idx == j, rows[j], out)
    return out


def _moe_kernel(x_ref, sc_ref, sh_ref, g_ref, nw_ref, rwt_ref, rb_ref, upper_ref, wg_ref, wu_ref, wd_ref,
                out_ref, hb_ref, rank_ref, comb_ref, acc_ref, sel_ref, y_ref, cnt_ref):
    step = pl.program_id(1)
    tb = x_ref.shape[0]

    @pl.when(step == 0)
    def _route():
        h = _norm_mod(x_ref[...], nw_ref[...], sc_ref[...], sh_ref[...])
        hb = h.astype(BF16)
        hb_ref[...] = hb
        h_lo = (h - hb.astype(F32)).astype(BF16)
        logits = (lax.dot_general(rwt_ref[0], hb, NT, preferred_element_type=F32)
                  + lax.dot_general(rwt_ref[0], h_lo, NT, preferred_element_type=F32)
                  + lax.dot_general(rwt_ref[1], hb, NT, preferred_element_type=F32))
        s = 1.0 / (1.0 + jnp.exp(-logits))
        s_sel = s + rb_ref[...]
        groups = tb // LANES

        def pack(row):
            return jnp.concatenate([row[:, c * LANES:(c + 1) * LANES] for c in range(groups)], axis=0)

        def unpack(tile):
            return jnp.concatenate([tile[c:c + 1, :] for c in range(groups)], axis=1)

        s_rows = [pack(s[j:j + 1, :]) for j in range(N_EXPERTS)]
        sel_rows = [pack(s_sel[j:j + 1, :]) for j in range(N_EXPERTS)]
        tops = [_top2_of_rows(sel_rows[g * EPG:(g + 1) * EPG]) for g in range(N_GROUPS)]
        scores = [tp[0] + tp[2] for tp in tops]
        best = functools.reduce(jnp.maximum, scores)
        g_idx = jnp.full(best.shape, N_GROUPS - 1, jnp.int32)
        for g in range(N_GROUPS - 2, -1, -1):
            g_idx = jnp.where(scores[g] == best, g, g_idx)
        e1 = g_idx * EPG + _select_rows(g_idx, [tp[1] for tp in tops])
        e2 = g_idx * EPG + _select_rows(g_idx, [tp[3] for tp in tops])
        w1 = _select_rows(e1, s_rows)
        w2 = _select_rows(e2, s_rows)
        tot = w1 + w2
        w1, w2 = w1 / tot, w2 / tot
        masks = [jnp.where((e1 == j) | (e2 == j), 1.0, 0.0) for j in range(N_EXPERTS)]
        comb_ref[...] = jnp.concatenate(
            [unpack(jnp.where(e1 == j, w1, 0.0) + jnp.where(e2 == j, w2, 0.0)) for j in range(N_EXPERTS)], axis=0)
        mask = jnp.concatenate([unpack(m) for m in masks], axis=0)
        rank = jnp.dot(mask.astype(BF16), upper_ref[...], preferred_element_type=F32)
        rank_ref[...] = jnp.where(mask > 0.0, rank, -1.0)
        for j in range(N_EXPERTS):
            cnt_ref[j] = jnp.sum(masks[j]).astype(jnp.int32)
        acc_ref[...] = jnp.zeros_like(acc_ref)

    def expert_tile(j, tile_idx):
        e = step * MOE_EXPERTS_PER_STEP + j
        rank = rank_ref[pl.ds(e, 1), :]
        comb = comb_ref[pl.ds(e, 1), :]
        slot = (lax.broadcasted_iota(jnp.int32, (MOE_TILE, tb), 0) + tile_idx * MOE_TILE).astype(F32)
        hit = rank == slot
        onehot = jnp.where(hit, 1.0, 0.0).astype(BF16)
        xe = jnp.dot(onehot, hb_ref[...], preferred_element_type=F32).astype(BF16)
        gate = jnp.dot(xe, wg_ref[j], preferred_element_type=F32)
        up = jnp.dot(xe, wu_ref[j], preferred_element_type=F32)
        y = jnp.dot((_silu(gate) * up).astype(BF16), wd_ref[j], preferred_element_type=F32)
        w_slot = jnp.sum(jnp.where(hit, comb, 0.0), axis=1, keepdims=True)
        return onehot, (y * w_slot).astype(BF16)

    experts = range(MOE_EXPERTS_PER_STEP)
    slot0 = lax.broadcasted_iota(jnp.int32, (MOE_TILE, tb), 0).astype(F32)
    hits = [rank_ref[pl.ds(step * MOE_EXPERTS_PER_STEP + j, 1), :] == slot0 for j in experts]
    for j in experts:
        sel_ref[j * MOE_TILE:(j + 1) * MOE_TILE, :] = jnp.where(hits[j], 1.0, 0.0).astype(BF16)
    xes = [jnp.dot(sel_ref[j * MOE_TILE:(j + 1) * MOE_TILE, :], hb_ref[...],
                   preferred_element_type=F32).astype(BF16) for j in experts]
    hidden = [(jnp.dot(xes[j], wg_ref[j], preferred_element_type=F32),
               jnp.dot(xes[j], wu_ref[j], preferred_element_type=F32)) for j in experts]
    acts = [(_silu(gate) * up).astype(BF16) for gate, up in hidden]
    ys = [jnp.dot(acts[j], wd_ref[j], preferred_element_type=F32) for j in experts]
    for j in experts:
        comb = comb_ref[pl.ds(step * MOE_EXPERTS_PER_STEP + j, 1), :]
        w_slot = jnp.sum(jnp.where(hits[j], comb, 0.0), axis=1, keepdims=True)
        y_ref[j * MOE_TILE:(j + 1) * MOE_TILE, :] = (ys[j] * w_slot).astype(BF16)

    acc_ref[...] += lax.dot_general(sel_ref[...], y_ref[...], TN, preferred_element_type=F32)

    for j in experts:
        def overflow(tile_idx, carry, j=j):
            onehot, yw = expert_tile(j, tile_idx)
            acc_ref[...] += lax.dot_general(onehot, yw, TN, preferred_element_type=F32)
            return carry

        n_tiles = (cnt_ref[step * MOE_EXPERTS_PER_STEP + j] + MOE_TILE - 1) // MOE_TILE
        lax.fori_loop(1, n_tiles, overflow, 0)

    @pl.when(step == pl.num_programs(1) - 1)
    def _finish():
        out_ref[...] = x_ref[...] + g_ref[...] * acc_ref[...]


def _moe(x, sc, sh, g, nw, router_w, router_b, w_gate, w_up, w_down, layer, seq, row_fn, shared_mod):
    t, d = x.shape
    tb = min(t if shared_mod else seq, MOE_BLOCK)
    per_seq = t if shared_mod else seq // tb
    dff = w_gate.shape[-1]
    upper = jnp.asarray(np.triu(np.ones((tb, tb), np.float32), 1), BF16)
    mod_spec = pl.BlockSpec((None, 1, d), lambda i, e: (row_fn(i // per_seq), 0, 0))
    tok = pl.BlockSpec((tb, d), lambda i, e: (i, 0))
    per = MOE_EXPERTS_PER_STEP
    stacked = per * MOE_TILE
    rw_hi = router_w.T.astype(BF16)
    rwt = jnp.stack([rw_hi, (router_w.T - rw_hi.astype(F32)).astype(BF16)])
    return pl.pallas_call(
        _moe_kernel,
        grid=(t // tb, N_EXPERTS // per),
        in_specs=[tok, mod_spec, mod_spec, mod_spec, _const_spec((1, d)), _const_spec((2, N_EXPERTS, d)),
                  _const_spec((N_EXPERTS, 1)), _const_spec(upper.shape),
                  pl.BlockSpec((None, per, d, dff), lambda i, e: (layer, e, 0, 0)),
                  pl.BlockSpec((None, per, d, dff), lambda i, e: (layer, e, 0, 0)),
                  pl.BlockSpec((None, per, dff, d), lambda i, e: (layer, e, 0, 0))],
        out_specs=tok,
        out_shape=jax.ShapeDtypeStruct((t, d), F32),
        scratch_shapes=[pltpu.VMEM((tb, d), BF16), pltpu.VMEM((N_EXPERTS, tb), F32), pltpu.VMEM((N_EXPERTS, tb), F32),
                        pltpu.VMEM((tb, d), F32), pltpu.VMEM((stacked, tb), BF16), pltpu.VMEM((stacked, d), BF16),
                        pltpu.SMEM((N_EXPERTS,), jnp.int32)],
        compiler_params=_params(("parallel", "arbitrary")),
        name="moe",
    )(x, sc, sh, g, nw, rwt, router_b.reshape(N_EXPERTS, 1), upper, w_gate, w_up, w_down)


def _proj1_kernel(x_ref, sc_ref, sh_ref, nw_ref, w_ref, wlr_ref, p_ref, lr_ref):
    h = _norm_mod(x_ref[...], nw_ref[...], sc_ref[...], sh_ref[...]).astype(BF16)
    p_ref[...] = jnp.dot(h, w_ref[...], preferred_element_type=F32).astype(p_ref.dtype)
    lr_ref[...] = jnp.dot(h, wlr_ref[...], preferred_element_type=F32)


def _proj1(x, sc, sh, nw, w_main, w_lr, seq, row_fn):
    t, d = x.shape
    tm = min(seq, TOKEN_TILE)
    per_seq = seq // tm
    mod_spec = pl.BlockSpec((None, 1, d), lambda i: (row_fn(i // per_seq), 0, 0))
    tok = lambda n: pl.BlockSpec((tm, n), lambda i: (i, 0))
    return pl.pallas_call(
        _proj1_kernel,
        grid=(t // tm,),
        in_specs=[tok(d), mod_spec, mod_spec, _const_spec((1, d)), _const_spec(w_main.shape), _const_spec(w_lr.shape)],
        out_specs=[tok(w_main.shape[1]), tok(LANES)],
        out_shape=[jax.ShapeDtypeStruct((t, w_main.shape[1]), BF16), jax.ShapeDtypeStruct((t, LANES), F32)],
        compiler_params=_params(("parallel",)),
        name="proj1",
    )(x, sc, sh, nw, w_main, w_lr)


def _log_sigmoid(x):
    return jnp.minimum(x, 0.0) - jnp.log(1.0 + jnp.exp(-jnp.abs(x)))


def _gla_kernel(qf_ref, kf_ref, vf_ref, lf_ref, qb_ref, kb_ref, vb_ref, lb_ref, wgf_ref, wgb_ref, bg_ref,
                trif_ref, trib_ref, s0f_ref, s0b_ref, of_ref, ob_ref, sf_ref, sb_ref, stf_ref, stb_ref, *, dk, dv):
    i = pl.program_id(1)
    nb = pl.num_programs(1)
    tc = qf_ref.shape[0]

    @pl.when(i == 0)
    def _init():
        stf_ref[...] = s0f_ref[...].astype(F32)
        stb_ref[...] = s0b_ref[...].astype(F32)

    r = lax.broadcasted_iota(jnp.int32, (GLA_CHUNK, GLA_CHUNK), 0)
    c = lax.broadcasted_iota(jnp.int32, (GLA_CHUNK, GLA_CHUNK), 1)
    scale = dk ** -0.5

    def decay_sums(l_ref, wg_ref, bias, tri_ref):
        gate = _log_sigmoid(jnp.dot(l_ref[...], wg_ref[...], precision=HI, preferred_element_type=F32)
                            + bias) / GLA_GATE_NORM
        total = None
        rest = gate
        for _ in range(3):
            part = rest.astype(BF16)
            rest = rest - part.astype(F32)
            term = jnp.dot(tri_ref[...], part, preferred_element_type=F32)
            total = term if total is None else total + term
        return total

    b_f = decay_sums(lf_ref, wgf_ref, bg_ref[0:1, :], trif_ref)
    b_b = decay_sums(lb_ref, wgb_ref, bg_ref[1:2, :], trib_ref)
    n_chunks = tc // GLA_CHUNK
    chains = []
    for h in range(GLA_HEADS):
        chains.append((qf_ref, kf_ref, vf_ref, of_ref, b_f, stf_ref, h, False))
        chains.append((qb_ref, kb_ref, vb_ref, ob_ref, b_b, stb_ref, h, True))
    states = [st_ref[h] for (_, _, _, _, _, st_ref, h, _) in chains]
    for step in range(n_chunks):
        prepared = []
        for (q_ref, k_ref, v_ref, _, b_all, _, h, reverse) in chains:
            ci = n_chunks - 1 - step if reverse else step
            rs = slice(ci * GLA_CHUNK, (ci + 1) * GLA_CHUNK)
            ks, vs = slice(h * dk, (h + 1) * dk), slice(h * dv, (h + 1) * dv)
            b = b_all[rs, ks]
            b_end = b[0:1, :] if reverse else b[GLA_CHUNK - 1:GLA_CHUNK, :]
            q_dec = (q_ref[rs, ks].astype(F32) * scale * jnp.exp(b)).astype(BF16)
            k_dec = k_ref[rs, ks].astype(F32) * jnp.exp(-b)
            e_end = jnp.exp(b_end)
            prepared.append((rs, vs, q_dec, k_dec.astype(BF16), (k_dec * e_end).astype(BF16), e_end, v_ref[rs, vs]))
        first = []
        for n, (rs, vs, q_dec, k_bf, k_rem, e_end, v) in enumerate(prepared):
            att = lax.dot_general(q_dec, k_bf, NT, preferred_element_type=F32)
            carry = lax.dot_general(q_dec, states[n].astype(BF16), NT, preferred_element_type=F32)
            upd = lax.dot_general(v, k_rem, TN, preferred_element_type=F32)
            first.append((att, carry, upd))
        for n, (rs, vs, q_dec, k_bf, k_rem, e_end, v) in enumerate(prepared):
            att, carry, upd = first[n]
            reverse = chains[n][7]
            att = jnp.where((c >= r) if reverse else (c <= r), att, 0.0).astype(BF16)
            o = jnp.dot(att, v, preferred_element_type=F32) + carry
            chains[n][3][rs, vs] = o.astype(chains[n][3].dtype)
            states[n] = states[n] * e_end + upd
    for n, (_, _, _, _, _, st_ref, h, _) in enumerate(chains):
        st_ref[h] = states[n]

    @pl.when(i == nb - 1)
    def _emit():
        sf_ref[...] = stf_ref[...]
        sb_ref[...] = stb_ref[...]


def _gla(p, lr, wgf, wgb, bg, s0f, s0b, batch, seq, dk, dv):
    tc = min(seq, GLA_ROWS_PER_STEP)
    nb = seq // tc
    hk, hv = GLA_HEADS * dk, GLA_HEADS * dv
    fwd = lambda width, col: pl.BlockSpec((tc, width), lambda b, i: (b * nb + i, col))
    bwd = lambda width, col: pl.BlockSpec((tc, width), lambda b, i: (b * nb + nb - 1 - i, col))
    st_spec = pl.BlockSpec((None, GLA_HEADS, dv, dk), lambda b, i: (b, 0, 0, 0))
    v_col = 2 * hk // hv
    idx = np.arange(tc)
    same = (idx[:, None] // GLA_CHUNK) == (idx[None, :] // GLA_CHUNK)
    tri_f = jnp.asarray(same & (idx[None, :] <= idx[:, None]), BF16)
    tri_b = jnp.asarray(same & (idx[None, :] >= idx[:, None]), BF16)
    return pl.pallas_call(
        functools.partial(_gla_kernel, dk=dk, dv=dv),
        grid=(batch, nb),
        in_specs=[fwd(hk, 0), fwd(hk, 1), fwd(hv, v_col), fwd(LANES, 0),
                  bwd(hk, 0), bwd(hk, 1), bwd(hv, v_col), bwd(LANES, 0),
                  _const_spec(wgf.shape), _const_spec(wgb.shape), _const_spec(bg.shape),
                  _const_spec(tri_f.shape), _const_spec(tri_b.shape), st_spec, st_spec],
        out_specs=[fwd(hv, 0), bwd(hv, 0), st_spec, st_spec],
        out_shape=[jax.ShapeDtypeStruct((batch * seq, hv), BF16)] * 2
        + [jax.ShapeDtypeStruct((batch, GLA_HEADS, dv, dk), F32)] * 2,
        scratch_shapes=[pltpu.VMEM((GLA_HEADS, dv, dk), F32)] * 2,
        compiler_params=_params(("parallel", "arbitrary")),
        name="gla_scan",
    )(p, p, p, lr, p, p, p, lr, wgf, wgb, bg, tri_f, tri_b, s0f, s0b)


def _outproj1_kernel(x_ref, of_ref, ob_ref, go_ref, g_ref, nw_ref, w_ref, out_ref, *, dv):
    o = of_ref[...].astype(F32) + ob_ref[...].astype(F32)
    parts = []
    for h in range(GLA_HEADS):
        oh = o[:, h * dv:(h + 1) * dv]
        ms = jnp.mean(oh * oh, axis=1, keepdims=True)
        parts.append(oh * lax.rsqrt(ms + EPS))
    o = jnp.concatenate(parts, axis=1) * nw_ref[...] * _silu(go_ref[...].astype(F32))
    out_ref[...] = x_ref[...] + g_ref[...] * jnp.dot(o.astype(BF16), w_ref[...], preferred_element_type=F32)


def _outproj1(x, o_f, o_b, p, g, out_norm_w, w_out, seq, row_fn, dv):
    t, d = x.shape
    hv = GLA_HEADS * dv
    tm = min(seq, TOKEN_TILE)
    per_seq = seq // tm
    tok = lambda n, col=0: pl.BlockSpec((tm, n), lambda i: (i, col))
    go_col = (p.shape[1] - hv) // hv
    nw = jnp.tile(out_norm_w, GLA_HEADS).reshape(1, hv)
    wb = w_out.astype(BF16)
    return pl.pallas_call(
        functools.partial(_outproj1_kernel, dv=dv),
        grid=(t // tm,),
        in_specs=[tok(d), tok(hv), tok(hv), tok(hv, go_col),
                  pl.BlockSpec((None, 1, d), lambda i: (row_fn(i // per_seq), 0, 0)),
                  _const_spec((1, hv)), _const_spec(wb.shape)],
        out_specs=tok(d),
        out_shape=jax.ShapeDtypeStruct((t, d), F32),
        compiler_params=_params(("parallel",)),
        name="outproj1",
    )(x, o_f, o_b, p, g, nw, wb)


def _mod_rows(mod_l, d):
    return [mod_l[:, j * d:(j + 1) * d].reshape(mod_l.shape[0], 1, d) for j in range(6)]


def _trunk(x, batch, seq, mods, row_fn, shared_mod, norm_w, ab, cp, moe_w, cache, s0, emit_cache):
    d = x.shape[-1]
    t = batch * seq
    x = x.reshape(t, d)
    (w_in0, conv_w, conv_b, filt, hy_skip, qn_w, kn_w, lam_p, subln_w, w_out0) = ab
    (w_in1, gate_up_w, gate_up_b, out_norm_w, w_out1) = cp
    router_w, router_b, wg, wu, wd = moe_w
    width = d // 2
    hy_in = 3 * width

    sh1, sc1, g1, sh2, sc2, g2 = mods[0]
    nseg = DA_HEADS * 2
    qkw = jnp.stack([jnp.tile(qn_w, nseg), jnp.tile(kn_w, nseg)])
    seg = np.arange(DA_HEADS * DA_VD) // DA_DH
    ones = jnp.asarray(seg[:, None] == seg[None, :], BF16)
    v_cols = w_in0[:, hy_in + 2 * DA_HEADS * DA_VD:]
    w_chan = jnp.concatenate([w_in0[:, :hy_in], v_cols], axis=1).T.astype(BF16)
    outs = _proj0(x, sc1, sh1, norm_w[0, 0].reshape(1, d), w_in0[:, hy_in:].astype(BF16), w_chan, qkw, ones,
                  batch, seq, row_fn, rope=cache is not None, emit_cache=emit_cache)
    chan, q, k = outs[:3]
    y_hy = _hyena(chan, conv_w, conv_b, filt, hy_skip, seq, width)
    lam_init = 0.8 - 0.6 * math.exp(-0.3 * 0)
    o = _diff_attention(q, k, chan, hy_in, cache, lam_p, subln_w, batch, seq, lam_init)
    x = _outproj0(x, y_hy, o, g1, w_out0, seq, row_fn)
    x = _moe(x, sc2, sh2, g2, norm_w[0, 1].reshape(1, d), router_w, router_b, wg, wu, wd, 0, seq, row_fn,
             shared_mod)

    sh1, sc1, g1, sh2, sc2, g2 = mods[1]
    dk = d // 2 // GLA_HEADS
    dv = d // GLA_HEADS
    n_main = GLA_HEADS * (2 * dk + 2 * dv)
    w_lr = jnp.pad(w_in1[:, n_main:], ((0, 0), (0, LANES - 2 * GLA_RANK))).astype(BF16)
    p, lr = _proj1(x, sc1, sh1, norm_w[1, 0].reshape(1, d), w_in1[:, :n_main].astype(BF16), w_lr, seq, row_fn)
    wgf = jnp.pad(gate_up_w[0], ((0, LANES - GLA_RANK), (0, 0)))
    wgb = jnp.pad(gate_up_w[1], ((GLA_RANK, LANES - 2 * GLA_RANK), (0, 0)))
    o_f, o_b, s_f, s_b = _gla(p, lr, wgf, wgb, gate_up_b, jnp.swapaxes(s0[0], 2, 3), jnp.swapaxes(s0[1], 2, 3),
                              batch, seq, dk, dv)
    s_f, s_b = jnp.swapaxes(s_f, 2, 3), jnp.swapaxes(s_b, 2, 3)
    x = _outproj1(x, o_f, o_b, p, g1, out_norm_w, w_out1, seq, row_fn, dv)
    x = _moe(x, sc2, sh2, g2, norm_w[1, 1].reshape(1, d), router_w, router_b, wg, wu, wd, 1, seq, row_fn,
             shared_mod)
    return x.reshape(batch, seq, d), outs[3:], (s_f, s_b)


def kernel(x_prompt, x_sample, cache_l0_k, cache_l0_v, state_l1_fwd, state_l1_bwd, c, c_ctx, w_mod, b_mod, norm_w,
           l0_w_in, l0_conv_w, l0_conv_b, l0_filt_w1, l0_filt_b1, l0_filt_freq, l0_filt_w2, l0_filt_b2, l0_filt_w3,
           l0_hy_skip, l0_qn_w, l0_kn_w, l0_lambda, l0_subln_w, l0_w_out, l1_w_in, l1_gate_up_w, l1_gate_up_b,
           l1_out_norm_w, l1_w_out, moe_router_w, moe_router_b, moe_w_gate, moe_w_up, moe_w_down):
    d = x_prompt.shape[-1]
    n_lat = c.shape[0]
    ctx_row = n_lat
    rows = 2 * SUBLANES
    cond = jnp.zeros((rows, d), F32).at[:n_lat].set(c).at[ctx_row].set(c_ctx)
    mod = _modulation(cond, w_mod, b_mod)
    mods = [_mod_rows(mod[l], d) for l in range(mod.shape[0])]
    filt = (l0_filt_w1, l0_filt_b1, l0_filt_freq, l0_filt_w2, l0_filt_b2, l0_filt_w3)
    ab = (l0_w_in, l0_conv_w, l0_conv_b, filt, l0_hy_skip, l0_qn_w, l0_kn_w, l0_lambda, l0_subln_w, l0_w_out)
    cp = (l1_w_in, l1_gate_up_w, l1_gate_up_b, l1_out_norm_w, l1_w_out)
    moe_w = (moe_router_w, moe_router_b, moe_w_gate.astype(BF16), moe_w_up.astype(BF16), moe_w_down.astype(BF16))

    b_ctx, l_ctx = x_prompt.shape[:2]
    zero_state = jnp.zeros((b_ctx,) + state_l1_fwd.shape[1:], F32)
    y_prompt, (new_k, new_v), (new_sf, new_sb) = _trunk(
        x_prompt, b_ctx, l_ctx, mods, lambda b: ctx_row, True, norm_w, ab, cp, moe_w, None,
        (zero_state, zero_state), True)
    b_lat, l_lat = x_sample.shape[:2]
    y_sample, _, _ = _trunk(
        x_sample, b_lat, l_lat, mods, lambda b: b, False, norm_w, ab, cp, moe_w, (cache_l0_k, cache_l0_v),
        (state_l1_fwd, state_l1_bwd), False)
    return (y_prompt, y_sample, new_k, new_v, new_sf, new_sb)
```
